```python
import math
import jax, jax.numpy as jnp
from jax import lax
import numpy as np

D_MODEL = 1024
BATCH = 8
SEQ = 4096
DEPTH = 2

HEAD_DIM = 64
N_HEADS_A = 4
N_HEADS_B = 4
N_HEADS_C = 4
N_HEADS_D = 4
MIX_WIDTH = HEAD_DIM * (N_HEADS_A + N_HEADS_B + N_HEADS_C + N_HEADS_D)
DIFF_QK_DIM = HEAD_DIM // 2
DILATED_CONFIGS = ((128, 1), (512, 4), (2048, 16))
IDX_HEADS = 16
IDX_DIM = 64
INDEX_TOPK_MAX = 256
N_EXPERTS = 32
TOP_K = 4
D_FF = D_MODEL
SWIGLU_ALPHA = 1.702
SWIGLU_LIMIT = 7.0
N_BUCKETS = 32
MAX_DISTANCE = 128
N_BIAS_HEADS = N_HEADS_B + N_HEADS_C + N_HEADS_D
Q_BLOCK = 128
MOE_BLOCK = 512
LN_EPS = 1e-5
DEEPNORM_ALPHA = (2 * DEPTH) ** 0.25
DEEPNORM_BETA = (8 * DEPTH) ** -0.25

SEG_SIZES = (
    N_HEADS_A * HEAD_DIM, N_HEADS_A * HEAD_DIM, N_HEADS_A * HEAD_DIM,
    N_HEADS_B * HEAD_DIM, N_HEADS_B * HEAD_DIM, N_HEADS_B * HEAD_DIM,
    N_HEADS_C * HEAD_DIM, N_HEADS_C * HEAD_DIM, N_HEADS_C * HEAD_DIM,
    IDX_HEADS * IDX_DIM, IDX_DIM, IDX_HEADS,
    N_HEADS_D * 2 * DIFF_QK_DIM, N_HEADS_D * 2 * DIFF_QK_DIM, N_HEADS_D * HEAD_DIM,
)
IN_COLS = sum(SEG_SIZES)
SPLIT_POINTS = tuple(int(p) for p in np.cumsum(SEG_SIZES)[:-1])

kernel_name = "hybrid_sb_dilated_dsa_diff_moe_deepnorm"


def layer_norm(x, g=None, b=None):
    xf = x.astype(jnp.float32)
    xc = xf - jnp.mean(xf, -1, keepdims=True)
    y = xc * lax.rsqrt(jnp.mean(xc * xc, -1, keepdims=True) + LN_EPS)
    if g is not None:
        y = y * g.astype(jnp.float32) + b.astype(jnp.float32)
    return y.astype(x.dtype)


def rel_bucket(dist):
    n = jnp.maximum(dist, 0)
    max_exact = N_BUCKETS // 2
    nf = jnp.maximum(n, 1).astype(jnp.float32)
    large = max_exact + (jnp.log(nf / max_exact) / math.log(MAX_DISTANCE / max_exact)
                         * (N_BUCKETS - max_exact)).astype(jnp.int32)
    large = jnp.minimum(large, N_BUCKETS - 1)
    return jnp.where(n < max_exact, n, large)


def seq_blocks(a, axis):
    t = a.shape[axis]
    a = a.reshape(a.shape[:axis] + (t // Q_BLOCK, Q_BLOCK) + a.shape[axis + 1:])
    return jnp.moveaxis(a, axis, 0)


def from_blocks(o):
    o = jnp.moveaxis(o, 0, 2)
    return o.reshape(o.shape[:2] + (-1,) + o.shape[4:])


def stick_breaking_attention(q, k, v):
    t, d = q.shape[2], q.shape[3]
    scale = d ** -0.5
    kpos = jnp.arange(t)

    def block(args):
        qi, s0 = args
        qpos = s0 + jnp.arange(Q_BLOCK)
        valid = kpos[None, :] < qpos[:, None]
        z = jnp.einsum('bhqd,bhsd->bhqs', qi, k).astype(jnp.float32) * scale
        log_fail = jnp.where(valid, jax.nn.log_sigmoid(-z), 0.0)
        after = lax.cumsum(log_fail, axis=3, reverse=True) - log_fail
        w = jnp.where(valid, jnp.exp(jax.nn.log_sigmoid(z) + after), 0.0)
        return jnp.einsum('bhqs,bhsd->bhqd', w.astype(v.dtype), v)

    starts = jnp.arange(t // Q_BLOCK) * Q_BLOCK
    return from_blocks(lax.map(block, (seq_blocks(q, 2), starts)))


def dilated_window_attention(q, k, v, bias_tab):
    t, d = q.shape[2], q.shape[3]
    scale = d ** -0.5

    def block(args):
        qi, s0 = args
        qpos = s0 + jnp.arange(Q_BLOCK)
        lses, outs = [], []
        for win, dil in DILATED_CONFIGS:
            dist = jnp.arange(win // dil + 1) * dil
            kidx = qpos[:, None] - dist[None, :]
            valid = kidx >= 0
            kidx = jnp.maximum(kidx, 0)
            kg = jnp.take(k, kidx, axis=2)
            vg = jnp.take(v, kidx, axis=2)
            bias = bias_tab[rel_bucket(dist)].T.astype(jnp.float32)
            z = jnp.einsum('bhqd,bhqmd->bhqm', qi, kg).astype(jnp.float32) * scale + bias[None, :, None, :]
            z = jnp.where(valid[None, None], z, -jnp.inf)
            zmax = jnp.max(z, -1, keepdims=True)
            p = jnp.exp(z - zmax)
            den = jnp.sum(p, -1, keepdims=True)
            outs.append(jnp.einsum('bhqm,bhqmd->bhqd', (p / den).astype(v.dtype), vg))
            lses.append(zmax + jnp.log(den))
        mix = jax.nn.softmax(jnp.concatenate(lses, -1), axis=-1)
        return jnp.einsum('bhqc,cbhqd->bhqd', mix.astype(v.dtype), jnp.stack(outs))

    starts = jnp.arange(t // Q_BLOCK) * Q_BLOCK
    return from_blocks(lax.map(block, (seq_blocks(q, 2), starts)))


def dsa_attention(q, k, v, q_idx, k_idx, w_idx, bias_tab):
    t, d = q.shape[2], q.shape[3]
    topk = min(INDEX_TOPK_MAX, t // 4)
    scale = d ** -0.5
    kpos = jnp.arange(t)

    def block(args):
        qi, qxi, wxi, s0 = args
        qpos = s0 + jnp.arange(Q_BLOCK)
        dots = jnp.einsum('bqhd,bsd->bqhs', qxi, k_idx).astype(jnp.float32) * IDX_DIM ** -0.5
        score = jnp.einsum('bqh,bqhs->bqs', wxi.astype(jnp.float32) * IDX_HEADS ** -0.5, jax.nn.relu(dots))
        score = jnp.where((kpos[None, :] <= qpos[:, None])[None], score, -jnp.inf)
        _, sel = lax.top_k(score, topk)
        kg = jax.vmap(lambda kk, ss: kk[:, ss])(k, sel)
        vg = jax.vmap(lambda vv, ss: vv[:, ss])(v, sel)
        dist = qpos[None, :, None] - sel
        bias = jnp.moveaxis(bias_tab[rel_bucket(dist)], -1, 1).astype(jnp.float32)
        z = jnp.einsum('bhqd,bhqkd->bhqk', qi, kg).astype(jnp.float32) * scale + bias
        z = jnp.where((dist >= 0)[:, None], z, -jnp.inf)
        p = jax.nn.softmax(z, axis=-1)
        return jnp.einsum('bhqk,bhqkd->bhqd', p.astype(v.dtype), vg)

    starts = jnp.arange(t // Q_BLOCK) * Q_BLOCK
    xs = (seq_blocks(q, 2), seq_blocks(q_idx, 1), seq_blocks(w_idx, 1), starts)
    return from_blocks(lax.map(block, xs))


def differential_attention(q1, q2, k1, k2, v, lam, bias_tab):
    t = q1.shape[2]
    scale = q1.shape[-1] ** -0.5
    kpos = jnp.arange(t)

    def block(args):
        q1i, q2i, s0 = args
        qpos = s0 + jnp.arange(Q_BLOCK)
        dist = qpos[:, None] - kpos[None, :]
        causal = dist >= 0
        bias = jnp.moveaxis(bias_tab[rel_bucket(dist)], -1, 0).astype(jnp.float32)

        def probs(qi, kk):
            z = jnp.einsum('bhqd,bhsd->bhqs', qi, kk).astype(jnp.float32) * scale + bias[None]
            return jax.nn.softmax(jnp.where(causal, z, -jnp.inf), axis=-1)

        a = probs(q1i, k1) - lam * probs(q2i, k2)
        return jnp.einsum('bhqs,bhsd->bhqd', a.astype(v.dtype), v)

    starts = jnp.arange(t // Q_BLOCK) * Q_BLOCK
    return from_blocks(lax.map(block, (seq_blocks(q1, 2), seq_blocks(q2, 2), starts)))


def hybrid_mixer(h, w_in, w_out, diff_lam, diff_g, rel_bias, layer_idx):
    b, t, _ = h.shape
    (qa, ka, va, qb, kb, vb, qc, kc, vc, qx, kx, wx, qd, kd, vd) = jnp.split(h @ w_in, SPLIT_POINTS, axis=-1)

    def heads(a, n):
        return a.reshape(b, t, n, -1).transpose(0, 2, 1, 3)

    def merge(o):
        return o.transpose(0, 2, 1, 3).reshape(b, t, -1)

    bias_b = rel_bias[:, :N_HEADS_B]
    bias_c = rel_bias[:, N_HEADS_B:N_HEADS_B + N_HEADS_C]
    bias_d = rel_bias[:, N_HEADS_B + N_HEADS_C:]

    o_a = stick_breaking_attention(heads(qa, N_HEADS_A), heads(ka, N_HEADS_A), heads(va, N_HEADS_A))
    o_b = dilated_window_attention(heads(qb, N_HEADS_B), heads(kb, N_HEADS_B), heads(vb, N_HEADS_B), bias_b)
    o_c = dsa_attention(heads(qc, N_HEADS_C), heads(kc, N_HEADS_C), heads(vc, N_HEADS_C),
                        qx.reshape(b, t, IDX_HEADS, IDX_DIM), kx, wx, bias_c)

    qd = qd.reshape(b, t, N_HEADS_D, 2, DIFF_QK_DIM)
    kd = kd.reshape(b, t, N_HEADS_D, 2, DIFF_QK_DIM)
    q1, q2 = qd[..., 0, :].transpose(0, 2, 1, 3), qd[..., 1, :].transpose(0, 2, 1, 3)
    k1, k2 = kd[..., 0, :].transpose(0, 2, 1, 3), kd[..., 1, :].transpose(0, 2, 1, 3)
    lamp = diff_lam.astype(jnp.float32)
    lambda_init = 0.8 - 0.6 * math.exp(-0.3 * layer_idx)
    lam = jnp.exp(jnp.sum(lamp[0] * lamp[1])) - jnp.exp(jnp.sum(lamp[2] * lamp[3])) + lambda_init
    o_d = differential_attention(q1, q2, k1, k2, heads(vd, N_HEADS_D), lam, bias_d)
    of = o_d.astype(jnp.float32)
    of = of * lax.rsqrt(jnp.mean(of * of, -1, keepdims=True) + LN_EPS)
    o_d = (of * diff_g.astype(jnp.float32) * (1.0 - lambda_init)).astype(h.dtype)

    merged = jnp.concatenate([merge(o_a), merge(o_b), merge(o_c), merge(o_d)], axis=-1)
    return merged @ w_out


def clamped_swiglu(hh):
    glu, lin = hh[..., ::2], hh[..., 1::2]
    glu = jnp.minimum(glu, SWIGLU_LIMIT)
    lin = jnp.clip(lin, -SWIGLU_LIMIT, SWIGLU_LIMIT)
    return glu * jax.nn.sigmoid(SWIGLU_ALPHA * glu) * (lin + 1.0)


def moe_ffn(h, w_router, b_router, w1, b1, w2, b2):
    b, t, d = h.shape
    hf = h.reshape(-1, d)
    n_tok = hf.shape[0]
    logits = (hf @ w_router + b_router).astype(jnp.float32)
    top_val, top_idx = lax.top_k(logits, TOP_K)
    gate = jax.nn.softmax(top_val, axis=-1)
    e_flat = top_idx.reshape(-1)
    tok_flat = jnp.repeat(jnp.arange(n_tok), TOP_K)
    g_flat = gate.reshape(-1)
    m = e_flat.shape[0]
    order = jnp.argsort(e_flat)
    e_s, tok_s, g_s = e_flat[order], tok_flat[order], g_flat[order]
    counts = jnp.bincount(e_flat, length=N_EXPERTS)
    padded = (counts + MOE_BLOCK - 1) // MOE_BLOCK * MOE_BLOCK
    start = jnp.cumsum(counts) - counts
    pend = jnp.cumsum(padded)
    pstart = pend - padded
    dest = pstart[e_s] + (jnp.arange(m) - start[e_s])
    cap = ((m + MOE_BLOCK - 1) // MOE_BLOCK + N_EXPERTS) * MOE_BLOCK
    n_blocks = cap // MOE_BLOCK
    tok_buf = jnp.zeros((cap,), jnp.int32).at[dest].set(tok_s)
    g_buf = jnp.zeros((cap,), jnp.float32).at[dest].set(g_s)
    blk_expert = jnp.minimum(jnp.searchsorted(pend, jnp.arange(n_blocks) * MOE_BLOCK, side='right'),
                             N_EXPERTS - 1)

    def expert_block(args):
        tok, g, e = args
        xb = hf[tok]
        y = clamped_swiglu(xb @ w1[e] + b1[e]) @ w2[e] + b2[e]
        return y * g[:, None].astype(y.dtype)

    y = lax.map(expert_block, (tok_buf.reshape(n_blocks, MOE_BLOCK), g_buf.reshape(n_blocks, MOE_BLOCK), blk_expert))
    out = jnp.zeros_like(hf).at[tok_buf].add(y.reshape(cap, d))
    return out.reshape(b, t, d)


def setup_inputs(seed: int = 0) -> dict:
    key = jax.random.key(seed)
    ks = jax.random.split(key, 17)

    def nrm(k, shape, scale):
        return jax.random.normal(k, shape, jnp.float32) * scale

    return {
        "x": nrm(ks[0], (BATCH, SEQ, D_MODEL), 1.0),
        "c": nrm(ks[1], (BATCH, D_MODEL), 1.0),
        "w_ada": nrm(ks[2], (DEPTH, D_MODEL, 6 * D_MODEL), 0.1 * D_MODEL ** -0.5),
        "b_ada": nrm(ks[3], (DEPTH, 6 * D_MODEL), 0.02),
        "w_in": nrm(ks[4], (DEPTH, D_MODEL, IN_COLS), D_MODEL ** -0.5),
        "w_out": nrm(ks[5], (DEPTH, MIX_WIDTH, D_MODEL), DEEPNORM_BETA * MIX_WIDTH ** -0.5),
        "diff_lam": nrm(ks[6], (DEPTH, 4, DIFF_QK_DIM), 0.1),
        "diff_g": 1.0 + nrm(ks[7], (DEPTH, HEAD_DIM), 0.02),
        "ln_g": 1.0 + nrm(ks[8], (DEPTH, 2, D_MODEL), 0.02),
        "ln_b": nrm(ks[9], (DEPTH, 2, D_MODEL), 0.02),
        "w_router": nrm(ks[10], (DEPTH, D_MODEL, N_EXPERTS), D_MODEL ** -0.5),
        "b_router": nrm(ks[11], (DEPTH, N_EXPERTS), 0.01),
        "w1": nrm(ks[12], (DEPTH, N_EXPERTS, D_MODEL, 2 * D_FF), D_MODEL ** -0.5),
        "b1": nrm(ks[13], (DEPTH, N_EXPERTS, 2 * D_FF), 0.01),
        "w2": nrm(ks[14], (DEPTH, N_EXPERTS, D_FF, D_MODEL), DEEPNORM_BETA * D_FF ** -0.5),
        "b2": nrm(ks[15], (DEPTH, N_EXPERTS, D_MODEL), 0.01),
        "rel_bias": nrm(ks[16], (N_BUCKETS, N_BIAS_HEADS), 0.5),
    }


def reference(x, c, w_ada, b_ada, w_in, w_out, diff_lam, diff_g, ln_g, ln_b,
              w_router, b_router, w1, b1, w2, b2, rel_bias):
    for l in range(DEPTH):
        mod = (c @ w_ada[l] + b_ada[l])[:, None, :]
        sh1, sc1, g1, sh2, sc2, g2 = jnp.split(mod, 6, axis=-1)
        h = layer_norm(x) * (1.0 + sc1) + sh1
        y = hybrid_mixer(h, w_in[l], w_out[l], diff_lam[l], diff_g[l], rel_bias, l)
        x = layer_norm(DEEPNORM_ALPHA * x + (1.0 + g1) * y, ln_g[l, 0], ln_b[l, 0])
        h = layer_norm(x) * (1.0 + sc2) + sh2
        y = moe_ffn(h, w_router[l], b_router[l], w1[l], b1[l], w2[l], b2[l])
        x = layer_norm(DEEPNORM_ALPHA * x + (1.0 + g2) * y, ln_g[l, 1], ln_b[l, 1])
    return x
```

```python
import functools
import math

import numpy as np
import jax
import jax.numpy as jnp
from jax import lax
from jax.experimental import pallas as pl
from jax.experimental.pallas import tpu as pltpu

F32 = jnp.float32
BF16 = jnp.bfloat16
I32 = jnp.int32

HEAD_DIM = 64
GROUP_HEADS = 4
GROUP_W = GROUP_HEADS * HEAD_DIM
DIFF_QK_DIM = HEAD_DIM // 2
DILATED_CONFIGS = ((128, 1), (512, 4), (2048, 16))
IDX_HEADS = 16
IDX_DIM = 64
INDEX_TOPK_MAX = 256
N_EXPERTS = 32
TOP_K = 4
SWIGLU_ALPHA = 1.702
SWIGLU_LIMIT = 7.0
N_BUCKETS = 32
MAX_DISTANCE = 128
LN_EPS = 1e-5
MOE_BLOCK = 512

LANES = 128
VMEM_LIMIT_BYTES = 56 * 1024 * 1024
NEG_BIG = -1e30
SB_SKIP_LOG = -100.0
INT_MIN = -2 ** 31

COL_QX = 0
COL_A = 1024
COL_B = COL_A + 3 * GROUP_W
COL_C = COL_B + 3 * GROUP_W
COL_D = COL_C + 3 * GROUP_W
COL_TAIL = COL_D + 3 * GROUP_W
PROJ_COLS = COL_TAIL + LANES


def _cparams(*sem):
    return pltpu.CompilerParams(dimension_semantics=sem, vmem_limit_bytes=VMEM_LIMIT_BYTES)


def _ln(x):
    mu = jnp.mean(x, axis=-1, keepdims=True)
    xc = x - mu
    return xc * lax.rsqrt(jnp.mean(xc * xc, axis=-1, keepdims=True) + LN_EPS)


def _dot_nt(a, b):
    return lax.dot_general(a, b, (((1,), (1,)), ((), ())), preferred_element_type=F32)


def _dot(a, b):
    return jnp.dot(a, b, preferred_element_type=F32)


def _ada_kernel(c_ref, w_ref, b_ref, o_ref):
    o_ref[...] = jnp.dot(c_ref[...], w_ref[...], precision=lax.Precision.HIGHEST,
                         preferred_element_type=F32) + b_ref[...]


def _ada_mod(c, w, b):
    bsz, d = c.shape
    n = w.shape[1]
    return pl.pallas_call(
        _ada_kernel,
        grid=(n // d,),
        in_specs=[pl.BlockSpec((bsz, d), lambda j: (0, 0)),
                  pl.BlockSpec((d, d), lambda j: (0, j)),
                  pl.BlockSpec((1, d), lambda j: (0, j))],
        out_specs=pl.BlockSpec((bsz, d), lambda j: (0, j)),
        out_shape=jax.ShapeDtypeStruct((bsz, n), F32),
        compiler_params=_cparams("arbitrary"),
        name="ada",
    )(c, w, b.reshape(1, n))


def _proj_kernel(x_ref, mod_ref, w_ref, o_ref, *, col_chunk):
    h = (_ln(x_ref[...]) * (1.0 + mod_ref[1:2, :]) + mod_ref[0:1, :]).astype(BF16)
    for c0 in range(0, o_ref.shape[-1], col_chunk):
        o_ref[:, c0:c0 + col_chunk] = _dot(h, w_ref[:, c0:c0 + col_chunk]).astype(BF16)


def _ln_mod_proj(x, mod, w):
    bsz, t, d = x.shape
    ncol = w.shape[1]
    tm = min(512, t)
    return pl.pallas_call(
        functools.partial(_proj_kernel, col_chunk=3 * LANES),
        grid=(bsz, t // tm),
        in_specs=[pl.BlockSpec((None, tm, d), lambda b, i: (b, i, 0)),
                  pl.BlockSpec((None, 8, d), lambda b, i: (b, 0, 0)),
                  pl.BlockSpec((d, ncol), lambda b, i: (0, 0))],
        out_specs=pl.BlockSpec((None, tm, ncol), lambda b, i: (b, i, 0)),
        out_shape=jax.ShapeDtypeStruct((bsz, t, ncol), BF16),
        compiler_params=_cparams("parallel", "parallel"),
        name="proj",
    )(x, mod, w)


def _sb_kernel(q_ref, k_ref, v_ref, o_ref, *, tq, scale):
    qi = pl.program_id(1)
    r = lax.broadcasted_iota(I32, (tq, tq), 0)
    c = lax.broadcasted_iota(I32, (tq, tq), 1)
    strict_lower = c < r
    upper = jnp.where(r > c, 1.0, 0.0).astype(BF16)

    outs = []
    for h in range(GROUP_HEADS):
        hs = slice(h * HEAD_DIM, (h + 1) * HEAD_DIM)
        q = q_ref[:, hs]

        def step(kb, carry, acc, masked, hs=hs, q=q):
            start = pl.multiple_of(kb * tq, tq)
            kblk = k_ref[pl.ds(start, tq), hs]
            vblk = v_ref[pl.ds(start, tq), hs]
            z = _dot_nt(q, kblk) * scale
            log_sig = jnp.minimum(z, 0.0) - jnp.log(1.0 + jnp.exp(-jnp.abs(z)))
            log_fail = log_sig - z
            if masked:
                log_fail = jnp.where(strict_lower, log_fail, 0.0)
            lf_hi = log_fail.astype(BF16)
            lf_lo = (log_fail - lf_hi.astype(F32)).astype(BF16)
            after = _dot(lf_hi, upper) + _dot(lf_lo, upper) + carry
            w = jnp.exp(log_sig + after)
            if masked:
                w = jnp.where(strict_lower, w, 0.0)
            acc = acc + _dot(w.astype(BF16), vblk)
            carry = carry + jnp.sum(log_fail, axis=1, keepdims=True)
            return carry, acc

        carry, acc = step(qi, jnp.zeros((tq, 1), F32), jnp.zeros((tq, HEAD_DIM), F32), True)

        def cond(s):
            return jnp.logical_and(s[0] >= 0, s[3] > 0)

        def body(s, step=step):
            kb, carry, acc, _ = s
            carry, acc = step(kb, carry, acc, False)
            go = (jnp.max(carry) > SB_SKIP_LOG).astype(I32)
            return kb - 1, carry, acc, go

        go0 = (jnp.max(carry) > SB_SKIP_LOG).astype(I32)
        _, _, acc, _ = lax.while_loop(cond, body, (qi - 1, carry, acc, go0))
        outs.append(acc)
    o_ref[...] = jnp.concatenate(outs, axis=-1).astype(BF16)


def _stick_breaking(proj):
    bsz, t, _ = proj.shape
    tq = min(256, t)
    cb = COL_A // GROUP_W
    return pl.pallas_call(
        functools.partial(_sb_kernel, tq=tq, scale=HEAD_DIM ** -0.5),
        grid=(bsz, t // tq),
        in_specs=[pl.BlockSpec((None, tq, GROUP_W), lambda b, i: (b, i, cb)),
                  pl.BlockSpec((None, t, GROUP_W), lambda b, i: (b, 0, cb + 1)),
                  pl.BlockSpec((None, t, GROUP_W), lambda b, i: (b, 0, cb + 2))],
        out_specs=pl.BlockSpec((None, tq, GROUP_W), lambda b, i: (b, i, 0)),
        out_shape=jax.ShapeDtypeStruct((bsz, t, GROUP_W), BF16),
        compiler_params=_cparams("parallel", "arbitrary"),
        name="sb",
    )(proj, proj, proj)


def _softmax_step(z, m, l, acc, vblk):
    m_new = jnp.maximum(m, jnp.max(z, axis=1, keepdims=True))
    alpha = jnp.exp(m - m_new)
    p = jnp.exp(z - m_new)
    l = alpha * l + jnp.sum(p, axis=1, keepdims=True)
    acc = alpha * acc + _dot(p.astype(BF16), vblk)
    return m_new, l, acc


def _diff_kernel(lam_ref, q_ref, k_ref, v_ref, bias_ref, g_ref, o_ref, *, tq, scale, out_scale):
    qi = pl.program_id(1)
    r = lax.broadcasted_iota(I32, (tq, tq), 0)
    c = lax.broadcasted_iota(I32, (tq, tq), 1)
    causal = c <= r
    lam = lam_ref[0]
    d2 = DIFF_QK_DIM

    outs = []
    for h in range(GROUP_HEADS):
        hs = slice(h * HEAD_DIM, (h + 1) * HEAD_DIM)
        s1 = slice(h * HEAD_DIM, h * HEAD_DIM + d2)
        s2 = slice(h * HEAD_DIM + d2, (h + 1) * HEAD_DIM)
        q1 = q_ref[:, s1]
        q2 = q_ref[:, s2]

        def step(kb, st, bias, masked, hs=hs, s1=s1, s2=s2, q1=q1, q2=q2):
            start = pl.multiple_of(kb * tq, tq)
            vblk = v_ref[pl.ds(start, tq), hs]
            z1 = _dot_nt(q1, k_ref[pl.ds(start, tq), s1]) * scale
            z2 = _dot_nt(q2, k_ref[pl.ds(start, tq), s2]) * scale
            if bias is not None:
                z1 = z1 + bias
                z2 = z2 + bias
            if masked:
                z1 = jnp.where(causal, z1, NEG_BIG)
                z2 = jnp.where(causal, z2, NEG_BIG)
            m1, l1, a1, m2, l2, a2 = st
            m1, l1, a1 = _softmax_step(z1, m1, l1, a1, vblk)
            m2, l2, a2 = _softmax_step(z2, m2, l2, a2, vblk)
            return m1, l1, a1, m2, l2, a2

        def init():
            return (jnp.full((tq, 1), NEG_BIG, F32), jnp.zeros((tq, 1), F32), jnp.zeros((tq, HEAD_DIM), F32))

        st = init() + init()
        st = lax.fori_loop(0, jnp.maximum(qi - 1, 0), lambda kb, s, step=step: step(kb, s, None, False), st)
        st = lax.cond(qi >= 1,
                      lambda s, step=step, h=h: step(qi - 1, s, bias_ref[h, 1], False),
                      lambda s: s, st)
        m1, l1, a1, m2, l2, a2 = step(qi, st, bias_ref[h, 0], True)
        o = a1 / l1 - lam * (a2 / l2)
        o = o * lax.rsqrt(jnp.mean(o * o, axis=-1, keepdims=True) + LN_EPS)
        outs.append(o * g_ref[...] * out_scale)
    o_ref[...] = jnp.concatenate(outs, axis=-1).astype(BF16)


def _differential(proj, lam, bias_tiles, diff_g, lambda_init):
    bsz, t, _ = proj.shape
    tq = bias_tiles.shape[-1]
    cb = COL_D // GROUP_W
    grid_spec = pltpu.PrefetchScalarGridSpec(
        num_scalar_prefetch=1,
        grid=(bsz, t // tq),
        in_specs=[pl.BlockSpec((None, tq, GROUP_W), lambda b, i, lam: (b, i, cb)),
                  pl.BlockSpec((None, t, GROUP_W), lambda b, i, lam: (b, 0, cb + 1)),
                  pl.BlockSpec((None, t, GROUP_W), lambda b, i, lam: (b, 0, cb + 2)),
                  pl.BlockSpec(bias_tiles.shape, lambda b, i, lam: (0, 0, 0, 0)),
                  pl.BlockSpec((1, HEAD_DIM), lambda b, i, lam: (0, 0))],
        out_specs=pl.BlockSpec((None, tq, GROUP_W), lambda b, i, lam: (b, i, 0)),
    )
    return pl.pallas_call(
        functools.partial(_diff_kernel, tq=tq, scale=DIFF_QK_DIM ** -0.5, out_scale=1.0 - lambda_init),
        grid_spec=grid_spec,
        out_shape=jax.ShapeDtypeStruct((bsz, t, GROUP_W), BF16),
        compiler_params=_cparams("parallel", "arbitrary"),
        name="diff",
    )(lam.reshape(1).astype(F32), proj, proj, proj, bias_tiles, diff_g.reshape(1, HEAD_DIM).astype(F32))


def _dsa_kernel(qx_ref, tq_ref, tk_ref, q_ref, k_ref, v_ref, bias_ref, o_ref, key_scr, cut_scr,
                *, tq, topk, scale, col_bits):
    qi = pl.program_id(1)
    nkb = qi + 1
    r = lax.broadcasted_iota(I32, (tq, tq), 0)
    c = lax.broadcasted_iota(I32, (tq, tq), 1)
    wx = tq_ref[:, IDX_DIM:IDX_DIM + IDX_HEADS].astype(F32) * IDX_HEADS ** -0.5

    def score_block(kb, _):
        start = pl.multiple_of(kb * tq, tq)
        kx = tk_ref[pl.ds(start, tq), 0:IDX_DIM]
        s = jnp.zeros((tq, tq), F32)
        for h in range(IDX_HEADS):
            d = _dot_nt(qx_ref[:, h * IDX_DIM:(h + 1) * IDX_DIM], kx) * IDX_DIM ** -0.5
            s = s + wx[:, h:h + 1] * jnp.maximum(d, 0.0)
        s = jnp.where(s == 0.0, 0.0, s)
        s = jnp.where(c + kb * tq <= r + qi * tq, s, -jnp.inf)
        bits = pltpu.bitcast(s, I32)
        key_scr[kb] = bits ^ ((bits >> 31) & 0x7FFFFFFF)
        return 0

    lax.fori_loop(0, nkb, score_block, 0)

    def count(pred):
        def body(kb, acc):
            return acc + jnp.where(pred(key_scr[kb], kb), 1.0, 0.0)
        acc = lax.fori_loop(0, nkb, body, jnp.zeros((tq, tq), F32))
        return jnp.sum(acc, axis=1, keepdims=True)

    def bit_step(i, thr):
        cand = thr + lax.shift_left(jnp.int32(1), 31 - i)
        cnt = count(lambda key, kb: key >= cand)
        return jnp.where(cnt >= topk, cand, thr)

    thr = lax.fori_loop(0, 32, bit_step, jnp.full((tq, 1), INT_MIN, I32))
    n_gt = count(lambda key, kb: key > thr)
    n_eq = count(lambda key, kb: key == thr)
    need = topk - n_gt
    cut_scr[...] = jnp.full((tq, LANES), 2 ** 30, I32)

    @pl.when(jnp.max(n_eq - need) > 0.0)
    def _():
        def col_step(i, lo):
            cand = lo + lax.shift_left(jnp.int32(1), col_bits - 1 - i)
            cnt = count(lambda key, kb: jnp.logical_and(key == thr, c + kb * tq < cand))
            return jnp.where(cnt < need, cand, lo)
        lo = lax.fori_loop(0, col_bits, col_step, jnp.zeros((tq, 1), I32))
        cut_scr[...] = jnp.broadcast_to(lo, (tq, LANES))

    cut = cut_scr[:, 0:1]

    def select_block(kb, _):
        key = key_scr[kb]
        col = c + kb * tq
        sel = jnp.logical_or(key > thr, jnp.logical_and(key == thr, col <= cut))
        sel = jnp.logical_and(sel, col <= r + qi * tq)
        key_scr[kb] = jnp.where(sel, 1, 0)
        return 0

    lax.fori_loop(0, nkb, select_block, 0)

    outs = []
    for h in range(GROUP_HEADS):
        hs = slice(h * HEAD_DIM, (h + 1) * HEAD_DIM)
        q = q_ref[:, hs]

        def step(kb, st, bias, hs=hs, q=q):
            start = pl.multiple_of(kb * tq, tq)
            z = _dot_nt(q, k_ref[pl.ds(start, tq), hs]) * scale
            if bias is not None:
                z = z + bias
            z = jnp.where(key_scr[kb] != 0, z, NEG_BIG)
            return _softmax_step(z, *st, v_ref[pl.ds(start, tq), hs])

        st = (jnp.full((tq, 1), NEG_BIG, F32), jnp.zeros((tq, 1), F32), jnp.zeros((tq, HEAD_DIM), F32))
        st = lax.fori_loop(0, jnp.maximum(qi - 1, 0), lambda kb, s, step=step: step(kb, s, None), st)
        st = lax.cond(qi >= 1, lambda s, step=step, h=h: step(qi - 1, s, bias_ref[h, 1]), lambda s: s, st)
        _, l, acc = step(qi, st, bias_ref[h, 0])
        outs.append(acc / l)
    o_ref[...] = jnp.concatenate(outs, axis=-1).astype(BF16)


def _dsa(proj, bias_tiles):
    bsz, t, _ = proj.shape
    tq = bias_tiles.shape[-1]
    topk = min(INDEX_TOPK_MAX, t // 4)
    assert tq >= topk, "the threshold search needs at least topk columns in the first block"
    cb = COL_C // GROUP_W
    tail = COL_TAIL // LANES
    return pl.pallas_call(
        functools.partial(_dsa_kernel, tq=tq, topk=float(topk), scale=HEAD_DIM ** -0.5,
                          col_bits=max(1, (t - 1).bit_length())),
        grid=(bsz, t // tq),
        in_specs=[pl.BlockSpec((None, tq, IDX_HEADS * IDX_DIM), lambda b, i: (b, i, 0)),
                  pl.BlockSpec((None, tq, LANES), lambda b, i: (b, i, tail)),
                  pl.BlockSpec((None, t, LANES), lambda b, i: (b, 0, tail)),
                  pl.BlockSpec((None, tq, GROUP_W), lambda b, i: (b, i, cb)),
                  pl.BlockSpec((None, t, GROUP_W), lambda b, i: (b, 0, cb + 1)),
                  pl.BlockSpec((None, t, GROUP_W), lambda b, i: (b, 0, cb + 2)),
                  pl.BlockSpec(bias_tiles.shape, lambda b, i: (0, 0, 0, 0))],
        out_specs=pl.BlockSpec((None, tq, GROUP_W), lambda b, i: (b, i, 0)),
        out_shape=jax.ShapeDtypeStruct((bsz, t, GROUP_W), BF16),
        scratch_shapes=[pltpu.VMEM((t // tq, tq, tq), I32), pltpu.VMEM((tq, LANES), I32)],
        compiler_params=_cparams("parallel", "arbitrary"),
        name="dsa",
    )(proj, proj, proj, proj, proj, proj, bias_tiles)


def _dil_kernel(q_ref, kp_ref, kd_ref, vp_ref, vd_ref, bias_ref, o_ref, lse_ref, *, tq, scale):
    qi = pl.program_id(1)
    r = lax.broadcasted_iota(I32, (tq, tq), 0)
    c = lax.broadcasted_iota(I32, (tq, tq), 1)
    prev_ok = jnp.logical_and(r <= c, qi > 0)
    diag_ok = c <= r
    outs, lses = [], []
    for h in range(GROUP_HEADS):
        hs = slice(h * HEAD_DIM, (h + 1) * HEAD_DIM)
        q = q_ref[:, hs]
        zp = jnp.where(prev_ok, _dot_nt(q, kp_ref[:, hs]) * scale + bias_ref[h, 1], NEG_BIG)
        zd = jnp.where(diag_ok, _dot_nt(q, kd_ref[:, hs]) * scale + bias_ref[h, 0], NEG_BIG)
        m = jnp.maximum(jnp.max(zp, axis=1, keepdims=True), jnp.max(zd, axis=1, keepdims=True))
        pp = jnp.exp(zp - m)
        pd = jnp.exp(zd - m)
        den = jnp.sum(pp, axis=1, keepdims=True) + jnp.sum(pd, axis=1, keepdims=True)
        o = (_dot(pp.astype(BF16), vp_ref[:, hs]) + _dot(pd.astype(BF16), vd_ref[:, hs])) / den
        outs.append(o)
        lses.append(jnp.broadcast_to(m + jnp.log(den), (tq, HEAD_DIM)))
    o_ref[...] = jnp.concatenate(outs, axis=-1)
    lse_ref[...] = jnp.concatenate(lses, axis=-1)


def _dilated_one(qp, kp, vp, bias_tiles):
    n, length, _ = qp.shape
    tq = bias_tiles.shape[-1]
    blk = lambda f: pl.BlockSpec((None, tq, GROUP_W), f)
    prev = lambda b, i: (b, jnp.maximum(i - 1, 0), 0)
    cur = lambda b, i: (b, i, 0)
    return pl.pallas_call(
        functools.partial(_dil_kernel, tq=tq, scale=HEAD_DIM ** -0.5),
        grid=(n, length // tq),
        in_specs=[blk(cur), blk(prev), blk(cur), blk(prev), blk(cur),
                  pl.BlockSpec(bias_tiles.shape, lambda b, i: (0, 0, 0, 0))],
        out_specs=[blk(cur), blk(cur)],
        out_shape=[jax.ShapeDtypeStruct((n, length, GROUP_W), F32)] * 2,
        compiler_params=_cparams("parallel", "arbitrary"),
        name="dil",
    )(qp, kp, kp, vp, vp, bias_tiles)


def _dilmix_kernel(o0, o1, o2, l0, l1, l2, out_ref):
    a0, a1, a2 = l0[...], l1[...], l2[...]
    m = jnp.maximum(jnp.maximum(a0, a1), a2)
    e0, e1, e2 = jnp.exp(a0 - m), jnp.exp(a1 - m), jnp.exp(a2 - m)
    out_ref[...] = ((e0 * o0[...] + e1 * o1[...] + e2 * o2[...]) / (e0 + e1 + e2)).astype(BF16)


def _dilated_mix(outs, lses):
    bsz, t, w = outs[0].shape
    tm = min(512, t)
    spec = pl.BlockSpec((None, tm, w), lambda b, i: (b, i, 0))
    return pl.pallas_call(
        _dilmix_kernel,
        grid=(bsz, t // tm),
        in_specs=[spec] * 6,
        out_specs=spec,
        out_shape=jax.ShapeDtypeStruct((bsz, t, w), BF16),
        compiler_params=_cparams("parallel", "parallel"),
        name="dilmix",
    )(*outs, *lses)


def _dilated(proj, bias_tiles_per_cfg):
    bsz, t, _ = proj.shape
    q, k, v = (proj[:, :, COL_B + j * GROUP_W:COL_B + (j + 1) * GROUP_W] for j in range(3))
    outs, lses = [], []
    for (_, dil), tiles in zip(DILATED_CONFIGS, bias_tiles_per_cfg):
        def perm(a, dil=dil):
            return a.reshape(bsz, t // dil, dil, GROUP_W).transpose(0, 2, 1, 3).reshape(bsz * dil, t // dil, GROUP_W)

        def unperm(a, dil=dil):
            return a.reshape(bsz, dil, t // dil, GROUP_W).transpose(0, 2, 1, 3).reshape(bsz, t, GROUP_W)

        o, lse = _dilated_one(perm(q), perm(k), perm(v), tiles)
        outs.append(unperm(o))
        lses.append(unperm(lse))
    return _dilated_mix(outs, lses)


def _post_kernel(oa_ref, ob_ref, oc_ref, od_ref, wo_ref, x_ref, mod_ref, ln_ref, wr_ref, br_ref,
                 x1_ref, h2_ref, idx_ref, gate_ref, *, alpha):
    y = jnp.zeros(x_ref.shape, F32)
    for g, o_ref in enumerate((oa_ref, ob_ref, oc_ref, od_ref)):
        y = y + _dot(o_ref[...], wo_ref[g * GROUP_W:(g + 1) * GROUP_W, :])
    u = alpha * x_ref[...] + (1.0 + mod_ref[2:3, :]) * y
    x1 = _ln(u) * ln_ref[0:1, :] + ln_ref[1:2, :]
    x1_ref[...] = x1
    h2 = _ln(x1) * (1.0 + mod_ref[4:5, :]) + mod_ref[3:4, :]
    h2_ref[...] = h2
    logits = lax.dot_general(wr_ref[...], h2, (((1,), (1,)), ((), ())), precision=lax.Precision.HIGHEST,
                             preferred_element_type=F32) + br_ref[...]
    n_exp, tm = logits.shape
    eid = lax.broadcasted_iota(I32, (n_exp, tm), 0)
    vals, ids = [], []
    for _ in range(TOP_K):
        m = jnp.max(logits, axis=0, keepdims=True)
        first = jnp.min(jnp.where(logits == m, eid, n_exp), axis=0, keepdims=True)
        vals.append(m)
        ids.append(first)
        logits = jnp.where(eid == first, -jnp.inf, logits)
    ex = [jnp.exp(v - vals[0]) for v in vals]
    den = ex[0] + ex[1] + ex[2] + ex[3]
    zero_f = jnp.zeros((8 - TOP_K, tm), F32)
    gate_ref[...] = jnp.concatenate([e / den for e in ex] + [zero_f], axis=0)
    idx_ref[...] = jnp.concatenate(ids + [zero_f.astype(I32)], axis=0)


def _post_mixer(o_groups, w_out, x, mod, ln_rows, w_router_t, b_router, alpha):
    bsz, t, d = x.shape
    tm = min(256, t)
    n_exp = w_router_t.shape[0]
    og = pl.BlockSpec((None, tm, GROUP_W), lambda b, i: (b, i, 0))
    row = pl.BlockSpec((None, tm, d), lambda b, i: (b, i, 0))
    small = pl.BlockSpec((None, 8, tm), lambda b, i: (b, 0, i))
    return pl.pallas_call(
        functools.partial(_post_kernel, alpha=alpha),
        grid=(bsz, t // tm),
        in_specs=[og, og, og, og,
                  pl.BlockSpec(w_out.shape, lambda b, i: (0, 0)),
                  row,
                  pl.BlockSpec((None, 8, d), lambda b, i: (b, 0, 0)),
                  pl.BlockSpec((4, d), lambda b, i: (0, 0)),
                  pl.BlockSpec((n_exp, d), lambda b, i: (0, 0)),
                  pl.BlockSpec((n_exp, 1), lambda b, i: (0, 0))],
        out_specs=[row, row, small, small],
        out_shape=[jax.ShapeDtypeStruct((bsz, t, d), F32), jax.ShapeDtypeStruct((bsz, t, d), F32),
                   jax.ShapeDtypeStruct((bsz, 8, t), I32), jax.ShapeDtypeStruct((bsz, 8, t), F32)],
        compiler_params=_cparams("parallel", "parallel"),
        name="post",
    )(*o_groups, w_out, x, mod, ln_rows, w_router_t, b_router.reshape(n_exp, 1))


def _moe_kernel(be_ref, nval_ref, nblk_ref, cur_ref, nxt_ref, h_hbm, w1g_ref, w1l_ref, b1g_ref, b1l_ref,
                w2_ref, b2_ref, out_hbm, xbuf, ybuf, gsem, ssem, *, bm, f_chunk):
    i = pl.program_id(0)
    nblk = nblk_ref[0]
    slot = i % 2

    def gather_row_copy(tok, r, s):
        return pltpu.make_async_copy(h_hbm.at[pl.ds(tok, 1)], xbuf.at[s, pl.ds(r, 1)], gsem.at[s])

    def scatter_row_copy(row, r, s):
        return pltpu.make_async_copy(ybuf.at[s, pl.ds(r, 1)], out_hbm.at[pl.ds(row, 1)], ssem.at[s])

    def start_gather(idx_ref, s):
        def body(r, _):
            gather_row_copy(idx_ref[0, r], r, s).start()
            return 0
        lax.fori_loop(0, bm, body, 0, unroll=8)

    def wait_gather(s):
        pltpu.make_async_copy(h_hbm.at[pl.ds(0, bm)], xbuf.at[s], gsem.at[s]).wait()

    def start_scatter(s, n):
        def body(r, _):
            scatter_row_copy(cur_ref[0, bm + r], r, s).start()
            return 0

        @pl.when(n == bm)
        def _():
            lax.fori_loop(0, bm, body, 0, unroll=8)

        @pl.when(n < bm)
        def _():
            lax.fori_loop(0, n, body, 0)

    def wait_scatter(s, n):
        @pl.when(n == bm)
        def _():
            pltpu.make_async_copy(ybuf.at[s], out_hbm.at[pl.ds(0, bm)], ssem.at[s]).wait()

        @pl.when(n < bm)
        def _():
            def body(r, _):
                scatter_row_copy(0, r, s).wait()
                return 0
            lax.fori_loop(0, n, body, 0)

    @pl.when(jnp.logical_and(i == 0, nblk > 0))
    def _():
        start_gather(cur_ref, 0)

    @pl.when(i + 1 < nblk)
    def _():
        start_gather(nxt_ref, 1 - slot)

    @pl.when(i < nblk)
    def _():
        wait_gather(slot)

        @pl.when(i >= 2)
        def _():
            wait_scatter(slot, nval_ref[jnp.maximum(i - 2, 0)])

        x = xbuf[slot].astype(BF16)
        n_f = w1g_ref.shape[1]
        y = jnp.zeros((bm, w2_ref.shape[1]), F32) + b2_ref[...]
        for f0 in range(0, n_f, f_chunk):
            fs = slice(f0, f0 + f_chunk)
            glu = jnp.minimum(_dot(x, w1g_ref[:, fs]) + b1g_ref[:, fs], SWIGLU_LIMIT)
            lin = jnp.clip(_dot(x, w1l_ref[:, fs]) + b1l_ref[:, fs], -SWIGLU_LIMIT, SWIGLU_LIMIT)
            act = glu * jax.nn.sigmoid(SWIGLU_ALPHA * glu) * (lin + 1.0)
            y = y + _dot(act.astype(BF16), w2_ref[fs, :])
        ybuf[slot] = y
        start_scatter(slot, nval_ref[i])

    last = pl.num_programs(0) - 1

    @pl.when(jnp.logical_and(i == last, nblk >= 2))
    def _():
        wait_scatter(nblk % 2, nval_ref[jnp.maximum(nblk - 2, 0)])

    @pl.when(jnp.logical_and(i == last, nblk >= 1))
    def _():
        wait_scatter((nblk + 1) % 2, nval_ref[jnp.maximum(nblk - 1, 0)])


def _moe_experts(h2, blk_expert, blk_valid, n_used, slot_idx, w1g, w1l, b1g, b1l, w2, b2):
    n_tok, d = h2.shape
    n_blocks, _, two_bm = slot_idx.shape
    bm = two_bm // 2
    n_exp, _, f = w1g.shape
    idx_spec = lambda f_: pl.BlockSpec((None, 1, two_bm), f_, memory_space=pltpu.SMEM)
    wspec = lambda shp: pl.BlockSpec((None,) + shp, lambda i, be, nv, nb: (be[i], 0, 0))
    grid_spec = pltpu.PrefetchScalarGridSpec(
        num_scalar_prefetch=3,
        grid=(n_blocks,),
        in_specs=[idx_spec(lambda i, be, nv, nb: (i, 0, 0)),
                  idx_spec(lambda i, be, nv, nb: (jnp.minimum(i + 1, n_blocks - 1), 0, 0)),
                  pl.BlockSpec(memory_space=pl.ANY),
                  wspec((d, f)), wspec((d, f)), wspec((1, f)), wspec((1, f)), wspec((f, d)), wspec((1, d))],
        out_specs=pl.BlockSpec(memory_space=pl.ANY),
        scratch_shapes=[pltpu.VMEM((2, bm, d), F32), pltpu.VMEM((2, bm, d), F32),
                        pltpu.SemaphoreType.DMA((2,)), pltpu.SemaphoreType.DMA((2,))],
    )
    return pl.pallas_call(
        functools.partial(_moe_kernel, bm=bm, f_chunk=min(512, f)),
        grid_spec=grid_spec,
        out_shape=jax.ShapeDtypeStruct((n_tok * TOP_K, d), F32),
        compiler_params=_cparams("arbitrary"),
        name="moe",
    )(blk_expert, blk_valid, n_used, slot_idx, slot_idx, h2, w1g, w1l, b1g, b1l, w2, b2)


def _moe_dispatch(top_idx, bm):
    n_tok = top_idx.shape[0]
    m = n_tok * TOP_K
    e_flat = top_idx.reshape(-1)
    order = jnp.argsort(e_flat, stable=True).astype(I32)
    e_s = e_flat[order]
    counts = jnp.sum((e_flat[:, None] == jnp.arange(N_EXPERTS, dtype=I32)[None, :]).astype(I32), axis=0)
    padded = (counts + bm - 1) // bm * bm
    start = jnp.cumsum(counts) - counts
    pend = jnp.cumsum(padded)
    pstart = pend - padded
    dest = pstart[e_s] + (jnp.arange(m, dtype=I32) - start[e_s])
    n_blocks = (m + bm - 1) // bm + N_EXPERTS
    cap = n_blocks * bm
    asg = jnp.full((cap,), -1, I32).at[dest].set(order)
    valid = asg >= 0
    src_tok = jnp.where(valid, asg // TOP_K, 0)
    dst_row = jnp.maximum(asg, 0)
    slot_idx = jnp.concatenate([src_tok.reshape(n_blocks, bm), dst_row.reshape(n_blocks, bm)], axis=1)
    blk_valid = jnp.sum(valid.reshape(n_blocks, bm).astype(I32), axis=1)
    blk_expert = jnp.minimum(jnp.searchsorted(pend, jnp.arange(n_blocks, dtype=I32) * bm, side='right'),
                             N_EXPERTS - 1).astype(I32)
    n_used = (pend[-1] // bm).astype(I32).reshape(1)
    return blk_expert, blk_valid, n_used, slot_idx.reshape(n_blocks, 1, 2 * bm)


def _comb_kernel(y4_ref, gate_ref, x_ref, mod_ref, ln_ref, o_ref, *, alpha):
    d = x_ref.shape[-1]
    y = jnp.zeros(x_ref.shape, F32)
    for k in range(TOP_K):
        y = y + gate_ref[:, k:k + 1] * y4_ref[:, k * d:(k + 1) * d]
    u = alpha * x_ref[...] + (1.0 + mod_ref[5:6, :]) * y
    o_ref[...] = _ln(u) * ln_ref[2:3, :] + ln_ref[3:4, :]


def _combine(y4, gate, x1, mod, ln_rows, alpha):
    bsz, t, d = x1.shape
    tm = min(256, t)
    nt = t // tm
    return pl.pallas_call(
        functools.partial(_comb_kernel, alpha=alpha),
        grid=(bsz, nt),
        in_specs=[pl.BlockSpec((tm, TOP_K * d), lambda b, i: (b * nt + i, 0)),
                  pl.BlockSpec((None, tm, 8), lambda b, i: (b, i, 0)),
                  pl.BlockSpec((None, tm, d), lambda b, i: (b, i, 0)),
                  pl.BlockSpec((None, 8, d), lambda b, i: (b, 0, 0)),
                  pl.BlockSpec((4, d), lambda b, i: (0, 0))],
        out_specs=pl.BlockSpec((None, tm, d), lambda b, i: (b, i, 0)),
        out_shape=jax.ShapeDtypeStruct((bsz, t, d), F32),
        compiler_params=_cparams("parallel", "parallel"),
        name="comb",
    )(y4, gate, x1, mod, ln_rows)


def _rel_bucket(dist):
    n = jnp.maximum(dist, 0)
    max_exact = N_BUCKETS // 2
    nf = jnp.maximum(n, 1).astype(F32)
    large = max_exact + (jnp.log(nf / max_exact) / math.log(MAX_DISTANCE / max_exact)
                         * (N_BUCKETS - max_exact)).astype(I32)
    large = jnp.minimum(large, N_BUCKETS - 1)
    return jnp.where(n < max_exact, n, large)


def _bias_tiles(bias_tab, tq, dil, fold_far):
    i = np.arange(tq)[:, None]
    j = np.arange(tq)[None, :]
    dist = np.stack([np.maximum(i - j, 0), i - j + tq]) * dil
    tiles = bias_tab.astype(F32)[_rel_bucket(jnp.asarray(dist, I32))]
    tiles = jnp.moveaxis(tiles, -1, 0)
    if fold_far:
        tiles = tiles - bias_tab.astype(F32)[N_BUCKETS - 1][:, None, None, None]
    return tiles


def _reorder_w_in(w_in):
    d = w_in.shape[0]
    g3 = 3 * GROUP_W
    a = w_in[:, 0:g3]
    b = w_in[:, g3:2 * g3]
    c = w_in[:, 2 * g3:3 * g3]
    o = 3 * g3
    qx = w_in[:, o:o + IDX_HEADS * IDX_DIM]
    o += IDX_HEADS * IDX_DIM
    kx = w_in[:, o:o + IDX_DIM]
    o += IDX_DIM
    wx = w_in[:, o:o + IDX_HEADS]
    o += IDX_HEADS
    dd = w_in[:, o:o + g3]
    pad = jnp.zeros((d, LANES - IDX_DIM - IDX_HEADS), w_in.dtype)
    return jnp.concatenate([qx, a, b, c, dd, kx, wx, pad], axis=1).astype(BF16)


def _layer(x, c, layer, depth, p):
    bsz, t, d = x.shape
    alpha = (2 * depth) ** 0.25
    tq = min(256, t)
    mod = _ada_mod(c, p["w_ada"], p["b_ada"]).reshape(bsz, 6, d)
    mod = jnp.concatenate([mod, jnp.zeros((bsz, 2, d), F32)], axis=1)
    ln_rows = jnp.concatenate([p["ln_g"][0:1], p["ln_b"][0:1], p["ln_g"][1:2], p["ln_b"][1:2]], axis=0)

    proj = _ln_mod_proj(x, mod, _reorder_w_in(p["w_in"]))

    rel_bias = p["rel_bias"]
    bias_b = rel_bias[:, :GROUP_HEADS]
    bias_c = rel_bias[:, GROUP_HEADS:2 * GROUP_HEADS]
    bias_d = rel_bias[:, 2 * GROUP_HEADS:]

    o_a = _stick_breaking(proj)
    o_b = _dilated(proj, [_bias_tiles(bias_b, min(128, t // dil), dil, False) for _, dil in DILATED_CONFIGS])
    o_c = _dsa(proj, _bias_tiles(bias_c, tq, 1, True))
    lamp = p["diff_lam"].astype(F32)
    lambda_init = 0.8 - 0.6 * math.exp(-0.3 * layer)
    lam = jnp.exp(jnp.sum(lamp[0] * lamp[1])) - jnp.exp(jnp.sum(lamp[2] * lamp[3])) + lambda_init
    o_d = _differential(proj, lam, _bias_tiles(bias_d, tq, 1, True), p["diff_g"], lambda_init)

    x1, h2, top_idx, gate = _post_mixer((o_a, o_b, o_c, o_d), p["w_out"].astype(BF16), x, mod, ln_rows,
                                        p["w_router"].T, p["b_router"], alpha)

    top_idx = top_idx[:, :TOP_K, :].transpose(0, 2, 1).reshape(bsz * t, TOP_K)
    blk_expert, blk_valid, n_used, slot_idx = _moe_dispatch(top_idx, MOE_BLOCK)
    w1 = p["w1"]
    y_rows = _moe_experts(h2.reshape(bsz * t, d), blk_expert, blk_valid, n_used, slot_idx,
                          w1[:, :, 0::2].astype(BF16), w1[:, :, 1::2].astype(BF16),
                          p["b1"][:, None, 0::2], p["b1"][:, None, 1::2],
                          p["w2"].astype(BF16), p["b2"][:, None, :])
    y4 = y_rows.reshape(-1, TOP_K * d)
    return _combine(y4, gate.transpose(0, 2, 1), x1, mod, ln_rows, alpha)


def kernel(x, c, w_ada, b_ada, w_in, w_out, diff_lam, diff_g, ln_g, ln_b, w_router, b_router, w1, b1, w2, b2,
           rel_bias):
    depth = w_in.shape[0]
    for layer in range(depth):
        p = dict(w_ada=w_ada[layer], b_ada=b_ada[layer], w_in=w_in[layer], w_out=w_out[layer],
                 diff_lam=diff_lam[layer], diff_g=diff_g[layer], ln_g=ln_g[layer], ln_b=ln_b[layer],
                 w_router=w_router[layer], b_router=b_router[layer], w1=w1[layer], b1=b1[layer],
                 w2=w2[layer], b2=b2[layer], rel_bias=rel_bias)
        x = _layer(x, c, layer, depth, p)
    return x
```

```python
import functools
import math

import numpy as np
import jax
import jax.numpy as jnp
from jax import lax
from jax.experimental import pallas as pl
from jax.experimental.pallas import tpu as pltpu

F32 = jnp.float32
BF16 = jnp.bfloat16
I32 = jnp.int32

HEAD_DIM = 64
GROUP_HEADS = 4
GROUP_W = GROUP_HEADS * HEAD_DIM
DIFF_QK_DIM = HEAD_DIM // 2
DILATED_CONFIGS = ((128, 1), (512, 4), (2048, 16))
IDX_HEADS = 16
IDX_DIM = 64
INDEX_TOPK_MAX = 256
N_EXPERTS = 32
TOP_K = 4
SWIGLU_ALPHA = 1.702
SWIGLU_LIMIT = 7.0
N_BUCKETS = 32
MAX_DISTANCE = 128
LN_EPS = 1e-5
MOE_BLOCK = 512

LANES = 128
SUBLANES = 8
VMEM_LIMIT_BYTES = 56 * 1024 * 1024
NEG_BIG = -1e30
SB_SKIP_LOG = -100.0
INT_MIN = -2 ** 31
LOG2E = math.log2(math.e)
ATT_BLOCK = 256
FF_GROUP = 256

COL_A = 0
COL_B = COL_A + 3 * GROUP_W
COL_KC = COL_B + 3 * GROUP_W
COL_KD = COL_KC + GROUP_W
COL_TAIL = COL_KD + GROUP_W
PROJ_COLS = COL_TAIL + LANES
ROW_QX = 0
ROW_QC = ROW_QX + IDX_HEADS * IDX_DIM
ROW_VC = ROW_QC + GROUP_W
ROW_QD = ROW_VC + GROUP_W
ROW_VD = ROW_QD + GROUP_W
ROW_WX = ROW_VD + GROUP_W
PROJ_ROWS = ROW_WX + IDX_HEADS


def _cparams(*sem):
    return pltpu.CompilerParams(dimension_semantics=sem, vmem_limit_bytes=VMEM_LIMIT_BYTES)


def _ln(x):
    mu = jnp.mean(x, axis=-1, keepdims=True)
    xc = x - mu
    return xc * lax.rsqrt(jnp.mean(xc * xc, axis=-1, keepdims=True) + LN_EPS)


def _dot_nt(a, b):
    return lax.dot_general(a, b, (((1,), (1,)), ((), ())), preferred_element_type=F32)


def _dot(a, b):
    return jnp.dot(a, b, preferred_element_type=F32)


def _ada_kernel(c_ref, w_ref, b_ref, o_ref):
    o_ref[...] = jnp.dot(c_ref[...], w_ref[...], precision=lax.Precision.HIGHEST,
                         preferred_element_type=F32) + b_ref[...]


def _ada_mod(c, w, b):
    bsz, d = c.shape
    n = w.shape[1]
    return pl.pallas_call(
        _ada_kernel,
        grid=(n // d,),
        in_specs=[pl.BlockSpec((bsz, d), lambda j: (0, 0)),
                  pl.BlockSpec((d, d), lambda j: (0, j)),
                  pl.BlockSpec((1, d), lambda j: (0, j))],
        out_specs=pl.BlockSpec((bsz, d), lambda j: (0, j)),
        out_shape=jax.ShapeDtypeStruct((bsz, n), F32),
        compiler_params=_cparams("arbitrary"),
        name="ada",
    )(c, w, b.reshape(1, n))


def _proj_kernel(x_ref, mod_ref, w_ref, wt_ref, o_ref, ot_ref, *, chunk):
    h = (_ln(x_ref[...]) * (1.0 + mod_ref[1:2, :]) + mod_ref[0:1, :]).astype(BF16)
    ncol = o_ref.shape[-1]
    for c0 in range(0, ncol, chunk):
        c1 = min(c0 + chunk, ncol)
        o_ref[:, c0:c1] = _dot(h, w_ref[:, c0:c1]).astype(BF16)
    n_sub, nrow, tq = ot_ref.shape
    for r0 in range(0, nrow, chunk):
        r1 = min(r0 + chunk, nrow)
        res = _dot_nt(wt_ref[r0:r1, :], h).astype(BF16)
        for j in range(n_sub):
            ot_ref[j, r0:r1, :] = res[:, j * tq:(j + 1) * tq]


def _ln_mod_proj(x, mod, w, wt):
    bsz, t, d = x.shape
    ncol, nrow = w.shape[1], wt.shape[0]
    tq = min(ATT_BLOCK, t)
    tm = min(2 * tq, t)
    return pl.pallas_call(
        functools.partial(_proj_kernel, chunk=2 * LANES),
        grid=(bsz, t // tm),
        in_specs=[pl.BlockSpec((None, tm, d), lambda b, i: (b, i, 0)),
                  pl.BlockSpec((None, 8, d), lambda b, i: (b, 0, 0)),
                  pl.BlockSpec((d, ncol), lambda b, i: (0, 0)),
                  pl.BlockSpec((nrow, d), lambda b, i: (0, 0))],
        out_specs=[pl.BlockSpec((None, tm, ncol), lambda b, i: (b, i, 0)),
                   pl.BlockSpec((None, tm // tq, nrow, tq), lambda b, i: (b, i, 0, 0))],
        out_shape=[jax.ShapeDtypeStruct((bsz, t, ncol), BF16),
                   jax.ShapeDtypeStruct((bsz, t // tq, nrow, tq), BF16)],
        compiler_params=_cparams("parallel", "parallel"),
        name="proj",
    )(x, mod, w, wt)


def _sb_kernel(q_ref, k_ref, v_ref, o_ref, *, tq, scale):
    qi = pl.program_id(1)
    r = lax.broadcasted_iota(I32, (tq, tq), 0)
    c = lax.broadcasted_iota(I32, (tq, tq), 1)
    strict_lower = c < r
    upper = jnp.where(r > c, 1.0, 0.0).astype(BF16)

    outs = []
    for h in range(GROUP_HEADS):
        hs = slice(h * HEAD_DIM, (h + 1) * HEAD_DIM)
        q = q_ref[:, hs]

        def step(kb, carry, acc, masked, hs=hs, q=q):
            start = pl.multiple_of(kb * tq, tq)
            kblk = k_ref[pl.ds(start, tq), hs]
            vblk = v_ref[pl.ds(start, tq), hs]
            z = _dot_nt(q, kblk) * scale
            log_sig = jnp.minimum(z, 0.0) - jnp.log(1.0 + jnp.exp(-jnp.abs(z)))
            log_fail = log_sig - z
            if masked:
                log_fail = jnp.where(strict_lower, log_fail, 0.0)
            lf_hi = log_fail.astype(BF16)
            lf_lo = (log_fail - lf_hi.astype(F32)).astype(BF16)
            after = _dot(lf_hi, upper) + _dot(lf_lo, upper) + carry
            w = jnp.exp(log_sig + after)
            if masked:
                w = jnp.where(strict_lower, w, 0.0)
            acc = acc + _dot(w.astype(BF16), vblk)
            carry = carry + jnp.sum(log_fail, axis=1, keepdims=True)
            return carry, acc

        carry, acc = step(qi, jnp.zeros((tq, 1), F32), jnp.zeros((tq, HEAD_DIM), F32), True)

        def cond(s):
            return jnp.logical_and(s[0] >= 0, s[3] > 0)

        def body(s, step=step):
            kb, carry, acc, _ = s
            carry, acc = step(kb, carry, acc, False)
            go = (jnp.max(carry) > SB_SKIP_LOG).astype(I32)
            return kb - 1, carry, acc, go

        go0 = (jnp.max(carry) > SB_SKIP_LOG).astype(I32)
        _, _, acc, _ = lax.while_loop(cond, body, (qi - 1, carry, acc, go0))
        outs.append(acc)
    o_ref[...] = jnp.concatenate(outs, axis=-1).astype(BF16)


def _stick_breaking(proj):
    bsz, t, _ = proj.shape
    tq = min(ATT_BLOCK, t)
    cb = COL_A // GROUP_W
    return pl.pallas_call(
        functools.partial(_sb_kernel, tq=tq, scale=HEAD_DIM ** -0.5),
        grid=(bsz, t // tq),
        in_specs=[pl.BlockSpec((None, tq, GROUP_W), lambda b, i: (b, i, cb)),
                  pl.BlockSpec((None, t, GROUP_W), lambda b, i: (b, 0, cb + 1)),
                  pl.BlockSpec((None, t, GROUP_W), lambda b, i: (b, 0, cb + 2))],
        out_specs=pl.BlockSpec((None, tq, GROUP_W), lambda b, i: (b, i, 0)),
        out_shape=jax.ShapeDtypeStruct((bsz, t, GROUP_W), BF16),
        compiler_params=_cparams("parallel", "arbitrary"),
        name="sb",
    )(proj, proj, proj)


def _softmax_step_t(z, m, l, acc, vt):
    m_new = jnp.maximum(m, jnp.max(z, axis=0, keepdims=True))
    alpha = jnp.exp2(m - m_new)
    p = jnp.exp2(z - m_new)
    l = alpha * l + jnp.sum(p, axis=0, keepdims=True)
    acc = alpha * acc + _dot(vt, p.astype(BF16))
    return m_new, l, acc


def _softmax_init_t(tq):
    return (jnp.full((1, tq), NEG_BIG, F32), jnp.zeros((1, tq), F32), jnp.zeros((HEAD_DIM, tq), F32))


def _head_rows(qt, lo, hi):
    half = lo // LANES
    assert (hi - 1) // LANES == half
    part = qt[half * LANES:(half + 1) * LANES, :]
    row = lax.broadcasted_iota(I32, part.shape, 0) + half * LANES
    return jnp.where(jnp.logical_and(row >= lo, row < hi), part, jnp.zeros_like(part))


def _key_halves(k_ref, start, tk):
    return [k_ref[pl.ds(start, tk), j * LANES:(j + 1) * LANES] for j in range(GROUP_W // LANES)]


def _diff_kernel(lam_ref, qt_ref, k_ref, vt_ref, bias_ref, g_ref, o_ref, *, tq, c_scale, out_scale):
    qi = pl.program_id(1)
    r = lax.broadcasted_iota(I32, (tq, tq), 0)
    c = lax.broadcasted_iota(I32, (tq, tq), 1)
    causal = r <= c
    lam = lam_ref[0]
    qt = qt_ref[...]
    q1 = [_head_rows(qt, h * HEAD_DIM, h * HEAD_DIM + DIFF_QK_DIM) for h in range(GROUP_HEADS)]
    q2 = [_head_rows(qt, h * HEAD_DIM + DIFF_QK_DIM, (h + 1) * HEAD_DIM) for h in range(GROUP_HEADS)]

    def step(kb, st, which, masked):
        start = pl.multiple_of(kb * tq, tq)
        khalf = _key_halves(k_ref, start, tq)
        vblk = vt_ref[kb]
        out = []
        for h in range(GROUP_HEADS):
            kblk = khalf[h * HEAD_DIM // LANES]
            z1 = _dot(kblk, q1[h]) * c_scale
            z2 = _dot(kblk, q2[h]) * c_scale
            if which is not None:
                bias = bias_ref[h, which]
                z1 = z1 + bias
                z2 = z2 + bias
            if masked:
                z1 = jnp.where(causal, z1, NEG_BIG)
                z2 = jnp.where(causal, z2, NEG_BIG)
            vt = vblk[h * HEAD_DIM:(h + 1) * HEAD_DIM, :]
            out.append(_softmax_step_t(z1, *st[h][0:3], vt) + _softmax_step_t(z2, *st[h][3:6], vt))
        return tuple(out)

    st = tuple(_softmax_init_t(tq) + _softmax_init_t(tq) for _ in range(GROUP_HEADS))
    st = lax.fori_loop(0, jnp.maximum(qi - 1, 0), lambda kb, s: step(kb, s, None, False), st)
    st = lax.cond(qi >= 1, lambda s: step(qi - 1, s, 1, False), lambda s: s, st)
    st = step(qi, st, 0, True)
    outs = []
    for h in range(GROUP_HEADS):
        _, l1, a1, _, l2, a2 = st[h]
        o = a1 / l1 - lam * (a2 / l2)
        o = o * lax.rsqrt(jnp.mean(o * o, axis=0, keepdims=True) + LN_EPS)
        outs.append(o * g_ref[...] * out_scale)
    o_ref[...] = jnp.concatenate(outs, axis=0).T.astype(BF16)


def _differential(proj, proj_t, lam, bias_tiles, diff_g, lambda_init):
    bsz, t, _ = proj.shape
    nq, tq = proj_t.shape[1], proj_t.shape[3]
    grid_spec = pltpu.PrefetchScalarGridSpec(
        num_scalar_prefetch=1,
        grid=(bsz, nq),
        in_specs=[pl.BlockSpec((None, None, GROUP_W, tq), lambda b, i, lam: (b, i, ROW_QD // GROUP_W, 0)),
                  pl.BlockSpec((None, t, GROUP_W), lambda b, i, lam: (b, 0, COL_KD // GROUP_W)),
                  pl.BlockSpec((None, nq, GROUP_W, tq), lambda b, i, lam: (b, 0, ROW_VD // GROUP_W, 0)),
                  pl.BlockSpec(bias_tiles.shape, lambda b, i, lam: (0, 0, 0, 0)),
                  pl.BlockSpec((HEAD_DIM, 1), lambda b, i, lam: (0, 0))],
        out_specs=pl.BlockSpec((None, tq, GROUP_W), lambda b, i, lam: (b, i, 0)),
    )
    return pl.pallas_call(
        functools.partial(_diff_kernel, tq=tq, c_scale=DIFF_QK_DIM ** -0.5 * LOG2E, out_scale=1.0 - lambda_init),
        grid_spec=grid_spec,
        out_shape=jax.ShapeDtypeStruct((bsz, t, GROUP_W), BF16),
        compiler_params=_cparams("parallel", "arbitrary"),
        name="diff",
    )(lam.reshape(1).astype(F32), proj_t, proj, proj_t, bias_tiles, diff_g.reshape(HEAD_DIM, 1).astype(F32))


def _dsa_kernel(qx_ref, wx_ref, tail_ref, qt_ref, k_ref, vt_ref, bias_ref, o_ref, key_scr, cut_scr,
                *, tq, topk, c_scale, row_bits):
    qi = pl.program_id(1)
    nkb = qi + 1
    r = lax.broadcasted_iota(I32, (tq, tq), 0)
    c = lax.broadcasted_iota(I32, (tq, tq), 1)
    wx = wx_ref[...].astype(F32) * (IDX_HEADS ** -0.5 * IDX_DIM ** -0.5)
    zpad = jnp.zeros((LANES - IDX_DIM, tq), BF16)
    qx = [jnp.concatenate([qx_ref[h * IDX_DIM:(h + 1) * IDX_DIM, :], zpad], axis=0) for h in range(IDX_HEADS)]

    def score_block(kb, _):
        start = pl.multiple_of(kb * tq, tq)
        kt = tail_ref[pl.ds(start, tq), :]
        s = jnp.zeros((tq, tq), F32)
        for h in range(IDX_HEADS):
            s = s + wx[h:h + 1, :] * jnp.maximum(_dot(kt, qx[h]), 0.0)
        s = jnp.where(s == 0.0, 0.0, s)
        s = jnp.where(r + kb * tq <= c + qi * tq, s, -jnp.inf)
        bits = pltpu.bitcast(s, I32)
        key_scr[kb] = bits ^ ((bits >> 31) & 0x7FFFFFFF)
        return 0

    lax.fori_loop(0, nkb, score_block, 0)

    def count(pred):
        def body(kb, acc):
            hit = jnp.where(pred(key_scr[kb], kb), 1.0, 0.0)
            return acc + jnp.sum(hit.reshape(tq // SUBLANES, SUBLANES, tq), axis=0)
        acc = lax.fori_loop(0, nkb, body, jnp.zeros((SUBLANES, tq), F32))
        return jnp.sum(acc, axis=0, keepdims=True)

    def bit_step(i, thr):
        cand = thr + lax.shift_left(jnp.int32(1), 31 - i)
        cnt = count(lambda key, kb: key >= cand)
        return jnp.where(cnt >= topk, cand, thr)

    thr = lax.fori_loop(0, 32, bit_step, jnp.full((1, tq), INT_MIN, I32))
    n_gt = count(lambda key, kb: key > thr)
    n_eq = count(lambda key, kb: key == thr)
    need = topk - n_gt
    cut_scr[...] = jnp.full((SUBLANES, tq), 2 ** 30, I32)

    @pl.when(jnp.max(n_eq - need) > 0.0)
    def _():
        def row_step(i, lo):
            cand = lo + lax.shift_left(jnp.int32(1), row_bits - 1 - i)
            cnt = count(lambda key, kb: jnp.logical_and(key == thr, r + kb * tq < cand))
            return jnp.where(cnt < need, cand, lo)
        lo = lax.fori_loop(0, row_bits, row_step, jnp.zeros((1, tq), I32))
        cut_scr[...] = jnp.broadcast_to(lo, (SUBLANES, tq))

    cut = cut_scr[0:1, :]

    def select_block(kb, _):
        key = key_scr[kb]
        pos = r + kb * tq
        sel = jnp.logical_or(key > thr, jnp.logical_and(key == thr, pos <= cut))
        sel = jnp.logical_and(sel, pos <= c + qi * tq)
        key_scr[kb] = jnp.where(sel, 1, 0)
        return 0

    lax.fori_loop(0, nkb, select_block, 0)

    qt = qt_ref[...]
    qh = [_head_rows(qt, h * HEAD_DIM, (h + 1) * HEAD_DIM) for h in range(GROUP_HEADS)]

    def step(kb, st, which):
        start = pl.multiple_of(kb * tq, tq)
        khalf = _key_halves(k_ref, start, tq)
        vblk = vt_ref[kb]
        sel = key_scr[kb] != 0
        out = []
        for h in range(GROUP_HEADS):
            z = _dot(khalf[h * HEAD_DIM // LANES], qh[h]) * c_scale
            if which is not None:
                z = z + bias_ref[h, which]
            z = jnp.where(sel, z, NEG_BIG)
            out.append(_softmax_step_t(z, *st[h], vblk[h * HEAD_DIM:(h + 1) * HEAD_DIM, :]))
        return tuple(out)

    st = tuple(_softmax_init_t(tq) for _ in range(GROUP_HEADS))
    st = lax.fori_loop(0, jnp.maximum(qi - 1, 0), lambda kb, s: step(kb, s, None), st)
    st = lax.cond(qi >= 1, lambda s: step(qi - 1, s, 1), lambda s: s, st)
    st = step(qi, st, 0)
    o_ref[...] = jnp.concatenate([acc / l for _, l, acc in st], axis=0).T.astype(BF16)


def _dsa(proj, proj_t, bias_tiles):
    bsz, t, _ = proj.shape
    nq, tq = proj_t.shape[1], proj_t.shape[3]
    topk = min(INDEX_TOPK_MAX, t // 4)
    assert tq >= topk, "the threshold search needs at least topk keys in the first block"
    nqx = IDX_HEADS * IDX_DIM
    return pl.pallas_call(
        functools.partial(_dsa_kernel, tq=tq, topk=float(topk), c_scale=HEAD_DIM ** -0.5 * LOG2E,
                          row_bits=max(1, (t - 1).bit_length())),
        grid=(bsz, nq),
        in_specs=[pl.BlockSpec((None, None, nqx, tq), lambda b, i: (b, i, ROW_QX // nqx, 0)),
                  pl.BlockSpec((None, None, IDX_HEADS, tq), lambda b, i: (b, i, ROW_WX // IDX_HEADS, 0)),
                  pl.BlockSpec((None, t, LANES), lambda b, i: (b, 0, COL_TAIL // LANES)),
                  pl.BlockSpec((None, None, GROUP_W, tq), lambda b, i: (b, i, ROW_QC // GROUP_W, 0)),
                  pl.BlockSpec((None, t, GROUP_W), lambda b, i: (b, 0, COL_KC // GROUP_W)),
                  pl.BlockSpec((None, nq, GROUP_W, tq), lambda b, i: (b, 0, ROW_VC // GROUP_W, 0)),
                  pl.BlockSpec(bias_tiles.shape, lambda b, i: (0, 0, 0, 0))],
        out_specs=pl.BlockSpec((None, tq, GROUP_W), lambda b, i: (b, i, 0)),
        out_shape=jax.ShapeDtypeStruct((bsz, t, GROUP_W), BF16),
        scratch_shapes=[pltpu.VMEM((nq, tq, tq), I32), pltpu.VMEM((SUBLANES, tq), I32)],
        compiler_params=_cparams("parallel", "arbitrary"),
        name="dsa",
    )(proj_t, proj_t, proj, proj_t, proj, proj_t, bias_tiles)


def _dil_kernel(q_ref, kp_ref, kd_ref, vp_ref, vd_ref, bias_ref, o_ref, lse_ref, *, tq, scale):
    qi = pl.program_id(1)
    r = lax.broadcasted_iota(I32, (tq, tq), 0)
    c = lax.broadcasted_iota(I32, (tq, tq), 1)
    prev_ok = jnp.logical_and(r <= c, qi > 0)
    diag_ok = c <= r
    outs, lses = [], []
    for h in range(GROUP_HEADS):
        hs = slice(h * HEAD_DIM, (h + 1) * HEAD_DIM)
        q = q_ref[:, hs]
        zp = jnp.where(prev_ok, _dot_nt(q, kp_ref[:, hs]) * scale + bias_ref[h, 1], NEG_BIG)
        zd = jnp.where(diag_ok, _dot_nt(q, kd_ref[:, hs]) * scale + bias_ref[h, 0], NEG_BIG)
        m = jnp.maximum(jnp.max(zp, axis=1, keepdims=True), jnp.max(zd, axis=1, keepdims=True))
        pp = jnp.exp(zp - m)
        pd = jnp.exp(zd - m)
        den = jnp.sum(pp, axis=1, keepdims=True) + jnp.sum(pd, axis=1, keepdims=True)
        o = (_dot(pp.astype(BF16), vp_ref[:, hs]) + _dot(pd.astype(BF16), vd_ref[:, hs])) / den
        outs.append(o)
        lses.append(jnp.broadcast_to(m + jnp.log(den), (tq, HEAD_DIM)))
    o_ref[...] = jnp.concatenate(outs, axis=-1)
    lse_ref[...] = jnp.concatenate(lses, axis=-1)


def _dilated_one(qp, kp, vp, bias_tiles):
    n, length, _ = qp.shape
    tq = bias_tiles.shape[-1]
    blk = lambda f: pl.BlockSpec((None, tq, GROUP_W), f)
    prev = lambda b, i: (b, jnp.maximum(i - 1, 0), 0)
    cur = lambda b, i: (b, i, 0)
    return pl.pallas_call(
        functools.partial(_dil_kernel, tq=tq, scale=HEAD_DIM ** -0.5),
        grid=(n, length // tq),
        in_specs=[blk(cur), blk(prev), blk(cur), blk(prev), blk(cur),
                  pl.BlockSpec(bias_tiles.shape, lambda b, i: (0, 0, 0, 0))],
        out_specs=[blk(cur), blk(cur)],
        out_shape=[jax.ShapeDtypeStruct((n, length, GROUP_W), F32)] * 2,
        compiler_params=_cparams("parallel", "arbitrary"),
        name="dil",
    )(qp, kp, kp, vp, vp, bias_tiles)


def _dilmix_kernel(o0, o1, o2, l0, l1, l2, out_ref):
    a0, a1, a2 = l0[...], l1[...], l2[...]
    m = jnp.maximum(jnp.maximum(a0, a1), a2)
    e0, e1, e2 = jnp.exp(a0 - m), jnp.exp(a1 - m), jnp.exp(a2 - m)
    out_ref[...] = ((e0 * o0[...] + e1 * o1[...] + e2 * o2[...]) / (e0 + e1 + e2)).astype(BF16)


def _dilated_mix(outs, lses):
    bsz, t, w = outs[0].shape
    tm = min(512, t)
    spec = pl.BlockSpec((None, tm, w), lambda b, i: (b, i, 0))
    return pl.pallas_call(
        _dilmix_kernel,
        grid=(bsz, t // tm),
        in_specs=[spec] * 6,
        out_specs=spec,
        out_shape=jax.ShapeDtypeStruct((bsz, t, w), BF16),
        compiler_params=_cparams("parallel", "parallel"),
        name="dilmix",
    )(*outs, *lses)


def _dilated(proj, bias_tiles_per_cfg):
    bsz, t, _ = proj.shape
    q, k, v = (proj[:, :, COL_B + j * GROUP_W:COL_B + (j + 1) * GROUP_W] for j in range(3))
    outs, lses = [], []
    for (_, dil), tiles in zip(DILATED_CONFIGS, bias_tiles_per_cfg):
        def perm(a, dil=dil):
            return a.reshape(bsz, t // dil, dil, GROUP_W).transpose(0, 2, 1, 3).reshape(bsz * dil, t // dil, GROUP_W)

        def unperm(a, dil=dil):
            return a.reshape(bsz, dil, t // dil, GROUP_W).transpose(0, 2, 1, 3).reshape(bsz, t, GROUP_W)

        o, lse = _dilated_one(perm(q), perm(k), perm(v), tiles)
        outs.append(unperm(o))
        lses.append(unperm(lse))
    return _dilated_mix(outs, lses)


def _post_kernel(oa_ref, ob_ref, oc_ref, od_ref, wo_ref, x_ref, mod_ref, ln_ref, wr_ref, br_ref,
                 x1_ref, h2_ref, idx_ref, gate_ref, *, alpha):
    y = jnp.zeros(x_ref.shape, F32)
    for g, o_ref in enumerate((oa_ref, ob_ref, oc_ref, od_ref)):
        y = y + _dot(o_ref[...], wo_ref[g * GROUP_W:(g + 1) * GROUP_W, :])
    u = alpha * x_ref[...] + (1.0 + mod_ref[2:3, :]) * y
    x1 = _ln(u) * ln_ref[0:1, :] + ln_ref[1:2, :]
    x1_ref[...] = x1
    h2 = _ln(x1) * (1.0 + mod_ref[4:5, :]) + mod_ref[3:4, :]
    h2_ref[...] = h2
    logits = lax.dot_general(wr_ref[...], h2, (((1,), (1,)), ((), ())), precision=lax.Precision.HIGHEST,
                             preferred_element_type=F32) + br_ref[...]
    n_exp, tm = logits.shape
    eid = lax.broadcasted_iota(I32, (n_exp, tm), 0)
    vals, ids = [], []
    for _ in range(TOP_K):
        m = jnp.max(logits, axis=0, keepdims=True)
        first = jnp.min(jnp.where(logits == m, eid, n_exp), axis=0, keepdims=True)
        vals.append(m)
        ids.append(first)
        logits = jnp.where(eid == first, -jnp.inf, logits)
    ex = [jnp.exp(v - vals[0]) for v in vals]
    den = ex[0] + ex[1] + ex[2] + ex[3]
    zero_f = jnp.zeros((8 - TOP_K, tm), F32)
    gate_ref[...] = jnp.concatenate([e / den for e in ex] + [zero_f], axis=0)
    idx_ref[...] = jnp.concatenate(ids + [zero_f.astype(I32)], axis=0)


def _post_mixer(o_groups, w_out, x, mod, ln_rows, w_router_t, b_router, alpha):
    bsz, t, d = x.shape
    tm = min(256, t)
    n_exp = w_router_t.shape[0]
    og = pl.BlockSpec((None, tm, GROUP_W), lambda b, i: (b, i, 0))
    row = pl.BlockSpec((None, tm, d), lambda b, i: (b, i, 0))
    small = pl.BlockSpec((None, 8, tm), lambda b, i: (b, 0, i))
    return pl.pallas_call(
        functools.partial(_post_kernel, alpha=alpha),
        grid=(bsz, t // tm),
        in_specs=[og, og, og, og,
                  pl.BlockSpec(w_out.shape, lambda b, i: (0, 0)),
                  row,
                  pl.BlockSpec((None, 8, d), lambda b, i: (b, 0, 0)),
                  pl.BlockSpec((4, d), lambda b, i: (0, 0)),
                  pl.BlockSpec((n_exp, d), lambda b, i: (0, 0)),
                  pl.BlockSpec((n_exp, 1), lambda b, i: (0, 0))],
        out_specs=[row, row, small, small],
        out_shape=[jax.ShapeDtypeStruct((bsz, t, d), F32), jax.ShapeDtypeStruct((bsz, t, d), F32),
                   jax.ShapeDtypeStruct((bsz, 8, t), I32), jax.ShapeDtypeStruct((bsz, 8, t), F32)],
        compiler_params=_cparams("parallel", "parallel"),
        name="post",
    )(*o_groups, w_out, x, mod, ln_rows, w_router_t, b_router.reshape(n_exp, 1))


def _deint_kernel(w_ref, p_ref, o_ref):
    o_ref[...] = _dot(w_ref[...].astype(BF16), p_ref[...]).astype(BF16)


def _deinterleave_w1(w1):
    n_exp, d, two_f = w1.shape
    grp = 2 * FF_GROUP
    j = np.arange(grp)
    src = np.where(j < FF_GROUP, 2 * j, 2 * (j - FF_GROUP) + 1)
    perm = jnp.asarray(np.arange(grp)[:, None] == src[None, :], BF16)
    return pl.pallas_call(
        _deint_kernel,
        grid=(n_exp, two_f // grp),
        in_specs=[pl.BlockSpec((None, d, grp), lambda e, g: (e, 0, g)),
                  pl.BlockSpec((grp, grp), lambda e, g: (0, 0))],
        out_specs=pl.BlockSpec((None, d, grp), lambda e, g: (e, 0, g)),
        out_shape=jax.ShapeDtypeStruct(w1.shape, BF16),
        compiler_params=_cparams("parallel", "parallel"),
        name="deint",
    )(w1, perm)


def _moe_kernel(be_ref, nval_ref, nblk_ref, cur_ref, nxt_ref, h_hbm, w1_ref, b1_ref, w2_ref, b2_ref,
                out_hbm, xbuf, ybuf, gsem, ssem, *, bm):
    i = pl.program_id(0)
    nblk = nblk_ref[0]
    slot = i % 2

    def gather_row_copy(tok, r, s):
        return pltpu.make_async_copy(h_hbm.at[pl.ds(tok, 1)], xbuf.at[s, pl.ds(r, 1)], gsem.at[s])

    def scatter_row_copy(row, r, s):
        return pltpu.make_async_copy(ybuf.at[s, pl.ds(r, 1)], out_hbm.at[pl.ds(row, 1)], ssem.at[s])

    def start_gather(idx_ref, s):
        def body(r, _):
            gather_row_copy(idx_ref[0, r], r, s).start()
            return 0
        lax.fori_loop(0, bm, body, 0, unroll=8)

    def wait_gather(s):
        pltpu.make_async_copy(h_hbm.at[pl.ds(0, bm)], xbuf.at[s], gsem.at[s]).wait()

    def start_scatter(s, n):
        def body(r, _):
            scatter_row_copy(cur_ref[0, bm + r], r, s).start()
            return 0

        @pl.when(n == bm)
        def _():
            lax.fori_loop(0, bm, body, 0, unroll=8)

        @pl.when(n < bm)
        def _():
            lax.fori_loop(0, n, body, 0)

    def wait_scatter(s, n):
        @pl.when(n == bm)
        def _():
            pltpu.make_async_copy(ybuf.at[s], out_hbm.at[pl.ds(0, bm)], ssem.at[s]).wait()

        @pl.when(n < bm)
        def _():
            def body(r, _):
                scatter_row_copy(0, r, s).wait()
                return 0
            lax.fori_loop(0, n, body, 0)

    @pl.when(jnp.logical_and(i == 0, nblk > 0))
    def _():
        start_gather(cur_ref, 0)

    @pl.when(i + 1 < nblk)
    def _():
        start_gather(nxt_ref, 1 - slot)

    @pl.when(i < nblk)
    def _():
        wait_gather(slot)

        @pl.when(i >= 2)
        def _():
            wait_scatter(slot, nval_ref[jnp.maximum(i - 2, 0)])

        x = xbuf[slot].astype(BF16)
        y = jnp.zeros((bm, w2_ref.shape[1]), F32) + b2_ref[...]
        for g in range(w2_ref.shape[0] // FF_GROUP):
            cols = slice(2 * g * FF_GROUP, 2 * (g + 1) * FF_GROUP)
            hh = _dot(x, w1_ref[:, cols]) + b1_ref[:, cols]
            glu = jnp.minimum(hh[:, :FF_GROUP], SWIGLU_LIMIT)
            lin = jnp.clip(hh[:, FF_GROUP:], -SWIGLU_LIMIT, SWIGLU_LIMIT)
            act = glu * jax.nn.sigmoid(SWIGLU_ALPHA * glu) * (lin + 1.0)
            y = y + _dot(act.astype(BF16), w2_ref[g * FF_GROUP:(g + 1) * FF_GROUP, :])
        ybuf[slot] = y
        start_scatter(slot, nval_ref[i])

    last = pl.num_programs(0) - 1

    @pl.when(jnp.logical_and(i == last, nblk >= 2))
    def _():
        wait_scatter(nblk % 2, nval_ref[jnp.maximum(nblk - 2, 0)])

    @pl.when(jnp.logical_and(i == last, nblk >= 1))
    def _():
        wait_scatter((nblk + 1) % 2, nval_ref[jnp.maximum(nblk - 1, 0)])


def _moe_experts(h2, blk_expert, blk_valid, n_used, slot_idx, w1p, b1p, w2, b2):
    n_tok, d = h2.shape
    n_blocks, _, two_bm = slot_idx.shape
    bm = two_bm // 2
    n_exp, _, two_f = w1p.shape
    f = two_f // 2
    idx_spec = lambda f_: pl.BlockSpec((None, 1, two_bm), f_, memory_space=pltpu.SMEM)
    wspec = lambda shp: pl.BlockSpec((None,) + shp, lambda i, be, nv, nb: (be[i], 0, 0))
    grid_spec = pltpu.PrefetchScalarGridSpec(
        num_scalar_prefetch=3,
        grid=(n_blocks,),
        in_specs=[idx_spec(lambda i, be, nv, nb: (i, 0, 0)),
                  idx_spec(lambda i, be, nv, nb: (jnp.minimum(i + 1, n_blocks - 1), 0, 0)),
                  pl.BlockSpec(memory_space=pl.ANY),
                  wspec((d, two_f)), wspec((1, two_f)), wspec((f, d)), wspec((1, d))],
        out_specs=pl.BlockSpec(memory_space=pl.ANY),
        scratch_shapes=[pltpu.VMEM((2, bm, d), F32), pltpu.VMEM((2, bm, d), F32),
                        pltpu.SemaphoreType.DMA((2,)), pltpu.SemaphoreType.DMA((2,))],
    )
    return pl.pallas_call(
        functools.partial(_moe_kernel, bm=bm),
        grid_spec=grid_spec,
        out_shape=jax.ShapeDtypeStruct((n_tok * TOP_K, d), F32),
        compiler_params=_cparams("arbitrary"),
        name="moe",
    )(blk_expert, blk_valid, n_used, slot_idx, slot_idx, h2, w1p, b1p, w2, b2)


def _moe_dispatch(top_idx, bm):
    n_tok = top_idx.shape[0]
    m = n_tok * TOP_K
    e_flat = top_idx.reshape(-1)
    order = jnp.argsort(e_flat, stable=True).astype(I32)
    e_s = e_flat[order]
    counts = jnp.sum((e_flat[:, None] == jnp.arange(N_EXPERTS, dtype=I32)[None, :]).astype(I32), axis=0)
    padded = (counts + bm - 1) // bm * bm
    start = jnp.cumsum(counts) - counts
    pend = jnp.cumsum(padded)
    pstart = pend - padded
    dest = pstart[e_s] + (jnp.arange(m, dtype=I32) - start[e_s])
    n_blocks = (m + bm - 1) // bm + N_EXPERTS
    cap = n_blocks * bm
    asg = jnp.full((cap,), -1, I32).at[dest].set(order)
    valid = asg >= 0
    src_tok = jnp.where(valid, asg // TOP_K, 0)
    dst_row = jnp.maximum(asg, 0)
    slot_idx = jnp.concatenate([src_tok.reshape(n_blocks, bm), dst_row.reshape(n_blocks, bm)], axis=1)
    blk_valid = jnp.sum(valid.reshape(n_blocks, bm).astype(I32), axis=1)
    blk_expert = jnp.minimum(jnp.searchsorted(pend, jnp.arange(n_blocks, dtype=I32) * bm, side='right'),
                             N_EXPERTS - 1).astype(I32)
    n_used = (pend[-1] // bm).astype(I32).reshape(1)
    return blk_expert, blk_valid, n_used, slot_idx.reshape(n_blocks, 1, 2 * bm)


def _comb_kernel(y4_ref, gate_ref, x_ref, mod_ref, ln_ref, o_ref, *, alpha):
    d = x_ref.shape[-1]
    y = jnp.zeros(x_ref.shape, F32)
    for k in range(TOP_K):
        y = y + gate_ref[:, k:k + 1] * y4_ref[:, k * d:(k + 1) * d]
    u = alpha * x_ref[...] + (1.0 + mod_ref[5:6, :]) * y
    o_ref[...] = _ln(u) * ln_ref[2:3, :] + ln_ref[3:4, :]


def _combine(y4, gate, x1, mod, ln_rows, alpha):
    bsz, t, d = x1.shape
    tm = min(256, t)
    nt = t // tm
    return pl.pallas_call(
        functools.partial(_comb_kernel, alpha=alpha),
        grid=(bsz, nt),
        in_specs=[pl.BlockSpec((tm, TOP_K * d), lambda b, i: (b * nt + i, 0)),
                  pl.BlockSpec((None, tm, 8), lambda b, i: (b, i, 0)),
                  pl.BlockSpec((None, tm, d), lambda b, i: (b, i, 0)),
                  pl.BlockSpec((None, 8, d), lambda b, i: (b, 0, 0)),
                  pl.BlockSpec((4, d), lambda b, i: (0, 0))],
        out_specs=pl.BlockSpec((None, tm, d), lambda b, i: (b, i, 0)),
        out_shape=jax.ShapeDtypeStruct((bsz, t, d), F32),
        compiler_params=_cparams("parallel", "parallel"),
        name="comb",
    )(y4, gate, x1, mod, ln_rows)


def _rel_bucket(dist):
    n = jnp.maximum(dist, 0)
    max_exact = N_BUCKETS // 2
    nf = jnp.maximum(n, 1).astype(F32)
    large = max_exact + (jnp.log(nf / max_exact) / math.log(MAX_DISTANCE / max_exact)
                         * (N_BUCKETS - max_exact)).astype(I32)
    large = jnp.minimum(large, N_BUCKETS - 1)
    return jnp.where(n < max_exact, n, large)


def _bias_tiles(bias_tab, tq, dil, key_major):
    i = np.arange(tq)[:, None]
    j = np.arange(tq)[None, :]
    dist = np.stack([np.maximum(i - j, 0), i - j + tq]) * dil
    tiles = bias_tab.astype(F32)[_rel_bucket(jnp.asarray(dist, I32))]
    tiles = jnp.moveaxis(tiles, -1, 0)
    if key_major:
        tiles = (tiles - bias_tab.astype(F32)[N_BUCKETS - 1][:, None, None, None]) * LOG2E
        tiles = jnp.swapaxes(tiles, -1, -2)
    return tiles


def _split_w_in(w_in):
    d = w_in.shape[0]
    g = GROUP_W
    segs = {}
    o = 0
    for name, width in (("qa", g), ("ka", g), ("va", g), ("qb", g), ("kb", g), ("vb", g),
                        ("qc", g), ("kc", g), ("vc", g), ("qx", IDX_HEADS * IDX_DIM), ("kx", IDX_DIM),
                        ("wx", IDX_HEADS), ("qd", g), ("kd", g), ("vd", g)):
        segs[name] = w_in[:, o:o + width]
        o += width
    pad = jnp.zeros((d, LANES - IDX_DIM - IDX_HEADS), w_in.dtype)
    w = jnp.concatenate([segs[n] for n in ("qa", "ka", "va", "qb", "kb", "vb", "kc", "kd", "kx", "wx")] + [pad],
                        axis=1)
    wt = jnp.concatenate([segs[n] for n in ("qx", "qc", "vc", "qd", "vd", "wx")], axis=1).T
    return w.astype(BF16), wt.astype(BF16)


def _layer(x, c, layer, depth, p):
    bsz, t, d = x.shape
    alpha = (2 * depth) ** 0.25
    tq = min(ATT_BLOCK, t)
    mod = _ada_mod(c, p["w_ada"], p["b_ada"]).reshape(bsz, 6, d)
    mod = jnp.concatenate([mod, jnp.zeros((bsz, 2, d), F32)], axis=1)
    ln_rows = jnp.concatenate([p["ln_g"][0:1], p["ln_b"][0:1], p["ln_g"][1:2], p["ln_b"][1:2]], axis=0)

    proj, proj_t = _ln_mod_proj(x, mod, *_split_w_in(p["w_in"]))

    rel_bias = p["rel_bias"]
    bias_b = rel_bias[:, :GROUP_HEADS]
    bias_c = rel_bias[:, GROUP_HEADS:2 * GROUP_HEADS]
    bias_d = rel_bias[:, 2 * GROUP_HEADS:]

    o_a = _stick_breaking(proj)
    o_b = _dilated(proj, [_bias_tiles(bias_b, min(128, t // dil), dil, False) for _, dil in DILATED_CONFIGS])
    o_c = _dsa(proj, proj_t, _bias_tiles(bias_c, tq, 1, True))
    lamp = p["diff_lam"].astype(F32)
    lambda_init = 0.8 - 0.6 * math.exp(-0.3 * layer)
    lam = jnp.exp(jnp.sum(lamp[0] * lamp[1])) - jnp.exp(jnp.sum(lamp[2] * lamp[3])) + lambda_init
    o_d = _differential(proj, proj_t, lam, _bias_tiles(bias_d, tq, 1, True), p["diff_g"], lambda_init)

    x1, h2, top_idx, gate = _post_mixer((o_a, o_b, o_c, o_d), p["w_out"].astype(BF16), x, mod, ln_rows,
                                        p["w_router"].T, p["b_router"], alpha)

    top_idx = top_idx[:, :TOP_K, :].transpose(0, 2, 1).reshape(bsz * t, TOP_K)
    blk_expert, blk_valid, n_used, slot_idx = _moe_dispatch(top_idx, MOE_BLOCK)
    n_grp = p["b1"].shape[-1] // (2 * FF_GROUP)
    b1p = p["b1"].reshape(N_EXPERTS, n_grp, FF_GROUP, 2).transpose(0, 1, 3, 2).reshape(N_EXPERTS, 1, -1)
    y_rows = _moe_experts(h2.reshape(bsz * t, d), blk_expert, blk_valid, n_used, slot_idx,
                          _deinterleave_w1(p["w1"]), b1p, p["w2"].astype(BF16), p["b2"][:, None, :])
    y4 = y_rows.reshape(-1, TOP_K * d)
    return _combine(y4, gate.transpose(0, 2, 1), x1, mod, ln_rows, alpha)


def kernel(x, c, w_ada, b_ada, w_in, w_out, diff_lam, diff_g, ln_g, ln_b, w_router, b_router, w1, b1, w2, b2,
           rel_bias):
    depth = w_in.shape[0]
    for layer in range(depth):
        p = dict(w_ada=w_ada[layer], b_ada=b_ada[layer], w_in=w_in[layer], w_out=w_out[layer],
                 diff_lam=diff_lam[layer], diff_g=diff_g[layer], ln_g=ln_g[layer], ln_b=ln_b[layer],
                 w_router=w_router[layer], b_router=b_router[layer], w1=w1[layer], b1=b1[layer],
                 w2=w2[layer], b2=b2[layer], rel_bias=rel_bias)
        x = _layer(x, c, layer, depth, p)
    return x
```

```python
import functools
import math

import numpy as np
import jax
import jax.numpy as jnp
from jax import lax
from jax.experimental import pallas as pl
from jax.experimental.pallas import tpu as pltpu

F32 = jnp.float32
BF16 = jnp.bfloat16
I32 = jnp.int32

HEAD_DIM = 64
GROUP_HEADS = 4
GROUP_W = GROUP_HEADS * HEAD_DIM
DIFF_QK_DIM = HEAD_DIM // 2
DILATED_CONFIGS = ((128, 1), (512, 4), (2048, 16))
IDX_HEADS = 16
IDX_DIM = 64
INDEX_TOPK_MAX = 256
N_EXPERTS = 32
TOP_K = 4
SWIGLU_ALPHA = 1.702
SWIGLU_LIMIT = 7.0
N_BUCKETS = 32
MAX_DISTANCE = 128
LN_EPS = 1e-5
MOE_BLOCK = 512

LANES = 128
SUBLANES = 8
VMEM_LIMIT_BYTES = 56 * 1024 * 1024
NEG_BIG = -1e30
SB_SKIP_LOG = -100.0
INT_MIN = -2 ** 31
LOG2E = math.log2(math.e)
ATT_BLOCK = 256
FF_GROUP = 256

COL_A = 0
COL_B = COL_A + 3 * GROUP_W
COL_KC = COL_B + 3 * GROUP_W
COL_KD = COL_KC + GROUP_W
COL_TAIL = COL_KD + GROUP_W
PROJ_COLS = COL_TAIL + LANES
ROW_QX = 0
ROW_QC = ROW_QX + IDX_HEADS * IDX_DIM
ROW_VC = ROW_QC + GROUP_W
ROW_QD = ROW_VC + GROUP_W
ROW_VD = ROW_QD + GROUP_W
ROW_WX = ROW_VD + GROUP_W
PROJ_ROWS = ROW_WX + IDX_HEADS


def _cparams(*sem):
    return pltpu.CompilerParams(dimension_semantics=sem, vmem_limit_bytes=VMEM_LIMIT_BYTES)


def _ln(x):
    mu = jnp.mean(x, axis=-1, keepdims=True)
    xc = x - mu
    return xc * lax.rsqrt(jnp.mean(xc * xc, axis=-1, keepdims=True) + LN_EPS)


def _dot_nt(a, b):
    return lax.dot_general(a, b, (((1,), (1,)), ((), ())), preferred_element_type=F32)


def _dot(a, b):
    return jnp.dot(a, b, preferred_element_type=F32)


def _load_token_tiles(ref, first_tok, n):
    base = first_tok * SUBLANES
    return jnp.concatenate([ref[pl.ds(base + c, n, stride=SUBLANES), :] for c in range(SUBLANES)], axis=1)


def _store_token_tiles(ref, first_tok, val):
    n = val.shape[0]
    base = first_tok * SUBLANES
    for c in range(SUBLANES):
        ref[pl.ds(base + c, n, stride=SUBLANES), :] = val[:, c * LANES:(c + 1) * LANES]


def _ada_kernel(c_ref, w_ref, b_ref, o_ref):
    o_ref[...] = jnp.dot(c_ref[...], w_ref[...], precision=lax.Precision.HIGHEST,
                         preferred_element_type=F32) + b_ref[...]


def _ada_mod(c, w_all, b_all, layer):
    bsz, d = c.shape
    depth, _, n = w_all.shape
    return pl.pallas_call(
        _ada_kernel,
        grid=(n // d,),
        in_specs=[pl.BlockSpec((bsz, d), lambda j: (0, 0)),
                  pl.BlockSpec((None, d, d), lambda j: (layer, 0, j)),
                  pl.BlockSpec((None, 1, d), lambda j: (layer, 0, j))],
        out_specs=pl.BlockSpec((bsz, d), lambda j: (0, j)),
        out_shape=jax.ShapeDtypeStruct((bsz, n), F32),
        compiler_params=_cparams("arbitrary"),
        name="ada",
    )(c, w_all, b_all.reshape(depth, 1, n))


def _proj_kernel(x_ref, mod_ref, w_ref, wt_ref, o_ref, ot_ref, *, chunk):
    h = (_ln(x_ref[...]) * (1.0 + mod_ref[1:2, :]) + mod_ref[0:1, :]).astype(BF16)
    ncol = o_ref.shape[-1]
    for c0 in range(0, ncol, chunk):
        c1 = min(c0 + chunk, ncol)
        o_ref[:, c0:c1] = _dot(h, w_ref[:, c0:c1]).astype(BF16)
    n_sub, nrow, tq = ot_ref.shape
    for r0 in range(0, nrow, chunk):
        r1 = min(r0 + chunk, nrow)
        res = _dot_nt(wt_ref[r0:r1, :], h).astype(BF16)
        for j in range(n_sub):
            ot_ref[j, r0:r1, :] = res[:, j * tq:(j + 1) * tq]


def _ln_mod_proj(x, mod, w, wt):
    bsz, t, d = x.shape
    ncol, nrow = w.shape[1], wt.shape[0]
    tq = min(ATT_BLOCK, t)
    tm = min(2 * tq, t)
    return pl.pallas_call(
        functools.partial(_proj_kernel, chunk=2 * LANES),
        grid=(bsz, t // tm),
        in_specs=[pl.BlockSpec((None, tm, d), lambda b, i: (b, i, 0)),
                  pl.BlockSpec((None, 8, d), lambda b, i: (b, 0, 0)),
                  pl.BlockSpec((d, ncol), lambda b, i: (0, 0)),
                  pl.BlockSpec((nrow, d), lambda b, i: (0, 0))],
        out_specs=[pl.BlockSpec((None, tm, ncol), lambda b, i: (b, i, 0)),
                   pl.BlockSpec((None, tm // tq, nrow, tq), lambda b, i: (b, i, 0, 0))],
        out_shape=[jax.ShapeDtypeStruct((bsz, t, ncol), BF16),
                   jax.ShapeDtypeStruct((bsz, t // tq, nrow, tq), BF16)],
        compiler_params=_cparams("parallel", "parallel"),
        name="proj",
    )(x, mod, w, wt)


def _sb_kernel(q_ref, k_ref, v_ref, o_ref, *, tq, scale):
    qi = pl.program_id(1)
    r = lax.broadcasted_iota(I32, (tq, tq), 0)
    c = lax.broadcasted_iota(I32, (tq, tq), 1)
    strict_lower = c < r
    upper = jnp.where(r > c, 1.0, 0.0).astype(BF16)

    outs = []
    for h in range(GROUP_HEADS):
        hs = slice(h * HEAD_DIM, (h + 1) * HEAD_DIM)
        q = q_ref[:, hs]

        def step(kb, carry, acc, masked, hs=hs, q=q):
            start = pl.multiple_of(kb * tq, tq)
            kblk = k_ref[pl.ds(start, tq), hs]
            vblk = v_ref[pl.ds(start, tq), hs]
            z = _dot_nt(q, kblk) * scale
            log_sig = jnp.minimum(z, 0.0) - jnp.log(1.0 + jnp.exp(-jnp.abs(z)))
            log_fail = log_sig - z
            if masked:
                log_fail = jnp.where(strict_lower, log_fail, 0.0)
            lf_hi = log_fail.astype(BF16)
            lf_lo = (log_fail - lf_hi.astype(F32)).astype(BF16)
            after = _dot(lf_hi, upper) + _dot(lf_lo, upper) + carry
            w = jnp.exp(log_sig + after)
            if masked:
                w = jnp.where(strict_lower, w, 0.0)
            acc = acc + _dot(w.astype(BF16), vblk)
            carry = carry + jnp.sum(log_fail, axis=1, keepdims=True)
            return carry, acc

        carry, acc = step(qi, jnp.zeros((tq, 1), F32), jnp.zeros((tq, HEAD_DIM), F32), True)

        def cond(s):
            return jnp.logical_and(s[0] >= 0, s[3] > 0)

        def body(s, step=step):
            kb, carry, acc, _ = s
            carry, acc = step(kb, carry, acc, False)
            go = (jnp.max(carry) > SB_SKIP_LOG).astype(I32)
            return kb - 1, carry, acc, go

        go0 = (jnp.max(carry) > SB_SKIP_LOG).astype(I32)
        _, _, acc, _ = lax.while_loop(cond, body, (qi - 1, carry, acc, go0))
        outs.append(acc)
    o_ref[...] = jnp.concatenate(outs, axis=-1).astype(BF16)


def _stick_breaking(proj):
    bsz, t, _ = proj.shape
    tq = min(ATT_BLOCK, t)
    cb = COL_A // GROUP_W
    return pl.pallas_call(
        functools.partial(_sb_kernel, tq=tq, scale=HEAD_DIM ** -0.5),
        grid=(bsz, t // tq),
        in_specs=[pl.BlockSpec((None, tq, GROUP_W), lambda b, i: (b, i, cb)),
                  pl.BlockSpec((None, t, GROUP_W), lambda b, i: (b, 0, cb + 1)),
                  pl.BlockSpec((None, t, GROUP_W), lambda b, i: (b, 0, cb + 2))],
        out_specs=pl.BlockSpec((None, tq, GROUP_W), lambda b, i: (b, i, 0)),
        out_shape=jax.ShapeDtypeStruct((bsz, t, GROUP_W), BF16),
        compiler_params=_cparams("parallel", "arbitrary"),
        name="sb",
    )(proj, proj, proj)


def _softmax_step_t(z, m, l, acc, vt):
    m_new = jnp.maximum(m, jnp.max(z, axis=0, keepdims=True))
    alpha = jnp.exp2(m - m_new)
    p = jnp.exp2(z - m_new)
    l = alpha * l + jnp.sum(p, axis=0, keepdims=True)
    acc = alpha * acc + _dot(vt, p.astype(BF16))
    return m_new, l, acc


def _softmax_init_t(tq):
    return (jnp.full((1, tq), NEG_BIG, F32), jnp.zeros((1, tq), F32), jnp.zeros((HEAD_DIM, tq), F32))


def _head_rows(qt, lo, hi):
    half = lo // LANES
    assert (hi - 1) // LANES == half
    part = qt[half * LANES:(half + 1) * LANES, :]
    row = lax.broadcasted_iota(I32, part.shape, 0) + half * LANES
    return jnp.where(jnp.logical_and(row >= lo, row < hi), part, jnp.zeros_like(part))


def _key_halves(k_ref, start, tk):
    return [k_ref[pl.ds(start, tk), j * LANES:(j + 1) * LANES] for j in range(GROUP_W // LANES)]


def _diff_kernel(lam_ref, qt_ref, k_ref, vt_ref, bias_ref, g_ref, o_ref, *, tq, c_scale, out_scale):
    qi = pl.program_id(1)
    r = lax.broadcasted_iota(I32, (tq, tq), 0)
    c = lax.broadcasted_iota(I32, (tq, tq), 1)
    causal = r <= c
    lam = lam_ref[0]
    qt = qt_ref[...]
    q1 = [_head_rows(qt, h * HEAD_DIM, h * HEAD_DIM + DIFF_QK_DIM) for h in range(GROUP_HEADS)]
    q2 = [_head_rows(qt, h * HEAD_DIM + DIFF_QK_DIM, (h + 1) * HEAD_DIM) for h in range(GROUP_HEADS)]

    def step(kb, st, which, masked):
        start = pl.multiple_of(kb * tq, tq)
        khalf = _key_halves(k_ref, start, tq)
        vblk = vt_ref[kb]
        out = []
        for h in range(GROUP_HEADS):
            kblk = khalf[h * HEAD_DIM // LANES]
            z1 = _dot(kblk, q1[h]) * c_scale
            z2 = _dot(kblk, q2[h]) * c_scale
            if which is not None:
                bias = bias_ref[h, which]
                z1 = z1 + bias
                z2 = z2 + bias
            if masked:
                z1 = jnp.where(causal, z1, NEG_BIG)
                z2 = jnp.where(causal, z2, NEG_BIG)
            vt = vblk[h * HEAD_DIM:(h + 1) * HEAD_DIM, :]
            out.append(_softmax_step_t(z1, *st[h][0:3], vt) + _softmax_step_t(z2, *st[h][3:6], vt))
        return tuple(out)

    st = tuple(_softmax_init_t(tq) + _softmax_init_t(tq) for _ in range(GROUP_HEADS))
    st = lax.fori_loop(0, jnp.maximum(qi - 1, 0), lambda kb, s: step(kb, s, None, False), st)
    st = lax.cond(qi >= 1, lambda s: step(qi - 1, s, 1, False), lambda s: s, st)
    st = step(qi, st, 0, True)
    outs = []
    for h in range(GROUP_HEADS):
        _, l1, a1, _, l2, a2 = st[h]
        o = a1 / l1 - lam * (a2 / l2)
        o = o * lax.rsqrt(jnp.mean(o * o, axis=0, keepdims=True) + LN_EPS)
        outs.append(o * g_ref[...] * out_scale)
    o_ref[...] = jnp.concatenate(outs, axis=0).T.astype(BF16)


def _differential(proj, proj_t, lam, bias_tiles, diff_g, lambda_init):
    bsz, t, _ = proj.shape
    nq, tq = proj_t.shape[1], proj_t.shape[3]
    grid_spec = pltpu.PrefetchScalarGridSpec(
        num_scalar_prefetch=1,
        grid=(bsz, nq),
        in_specs=[pl.BlockSpec((None, None, GROUP_W, tq), lambda b, i, lam: (b, i, ROW_QD // GROUP_W, 0)),
                  pl.BlockSpec((None, t, GROUP_W), lambda b, i, lam: (b, 0, COL_KD // GROUP_W)),
                  pl.BlockSpec((None, nq, GROUP_W, tq), lambda b, i, lam: (b, 0, ROW_VD // GROUP_W, 0)),
                  pl.BlockSpec(bias_tiles.shape, lambda b, i, lam: (0, 0, 0, 0)),
                  pl.BlockSpec((HEAD_DIM, 1), lambda b, i, lam: (0, 0))],
        out_specs=pl.BlockSpec((None, tq, GROUP_W), lambda b, i, lam: (b, i, 0)),
    )
    return pl.pallas_call(
        functools.partial(_diff_kernel, tq=tq, c_scale=DIFF_QK_DIM ** -0.5 * LOG2E, out_scale=1.0 - lambda_init),
        grid_spec=grid_spec,
        out_shape=jax.ShapeDtypeStruct((bsz, t, GROUP_W), BF16),
        compiler_params=_cparams("parallel", "arbitrary"),
        name="diff",
    )(lam.reshape(1).astype(F32), proj_t, proj, proj_t, bias_tiles, diff_g.reshape(HEAD_DIM, 1).astype(F32))


def _dsa_kernel(qx_ref, wx_ref, tail_ref, qt_ref, k_ref, vt_ref, bias_ref, o_ref, key_scr, cut_scr,
                *, tq, topk, c_scale, row_bits):
    qi = pl.program_id(1)
    nkb = qi + 1
    r = lax.broadcasted_iota(I32, (tq, tq), 0)
    c = lax.broadcasted_iota(I32, (tq, tq), 1)
    wx = wx_ref[...].astype(F32) * (IDX_HEADS ** -0.5 * IDX_DIM ** -0.5)
    zpad = jnp.zeros((LANES - IDX_DIM, tq), BF16)
    qx = [jnp.concatenate([qx_ref[h * IDX_DIM:(h + 1) * IDX_DIM, :], zpad], axis=0) for h in range(IDX_HEADS)]

    def score_block(kb, _):
        start = pl.multiple_of(kb * tq, tq)
        kt = tail_ref[pl.ds(start, tq), :]
        s = jnp.zeros((tq, tq), F32)
        for h in range(IDX_HEADS):
            s = s + wx[h:h + 1, :] * jnp.maximum(_dot(kt, qx[h]), 0.0)
        s = jnp.where(s == 0.0, 0.0, s)
        s = jnp.where(r + kb * tq <= c + qi * tq, s, -jnp.inf)
        bits = pltpu.bitcast(s, I32)
        key_scr[kb] = bits ^ ((bits >> 31) & 0x7FFFFFFF)
        return 0

    lax.fori_loop(0, nkb, score_block, 0)

    def count(pred):
        def body(kb, acc):
            hit = jnp.where(pred(key_scr[kb], kb), 1.0, 0.0)
            return acc + jnp.sum(hit.reshape(tq // SUBLANES, SUBLANES, tq), axis=0)
        acc = lax.fori_loop(0, nkb, body, jnp.zeros((SUBLANES, tq), F32))
        return jnp.sum(acc, axis=0, keepdims=True)

    def bit_step(i, thr):
        cand = thr + lax.shift_left(jnp.int32(1), 31 - i)
        cnt = count(lambda key, kb: key >= cand)
        return jnp.where(cnt >= topk, cand, thr)

    thr = lax.fori_loop(0, 32, bit_step, jnp.full((1, tq), INT_MIN, I32))
    n_gt = count(lambda key, kb: key > thr)
    n_eq = count(lambda key, kb: key == thr)
    need = topk - n_gt
    cut_scr[...] = jnp.full((SUBLANES, tq), 2 ** 30, I32)

    @pl.when(jnp.max(n_eq - need) > 0.0)
    def _():
        def row_step(i, lo):
            cand = lo + lax.shift_left(jnp.int32(1), row_bits - 1 - i)
            cnt = count(lambda key, kb: jnp.logical_and(key == thr, r + kb * tq < cand))
            return jnp.where(cnt < need, cand, lo)
        lo = lax.fori_loop(0, row_bits, row_step, jnp.zeros((1, tq), I32))
        cut_scr[...] = jnp.broadcast_to(lo, (SUBLANES, tq))

    cut = cut_scr[0:1, :]

    def select_block(kb, _):
        key = key_scr[kb]
        pos = r + kb * tq
        sel = jnp.logical_or(key > thr, jnp.logical_and(key == thr, pos <= cut))
        sel = jnp.logical_and(sel, pos <= c + qi * tq)
        key_scr[kb] = jnp.where(sel, 1, 0)
        return 0

    lax.fori_loop(0, nkb, select_block, 0)

    qt = qt_ref[...]
    qh = [_head_rows(qt, h * HEAD_DIM, (h + 1) * HEAD_DIM) for h in range(GROUP_HEADS)]

    def step(kb, st, which):
        start = pl.multiple_of(kb * tq, tq)
        khalf = _key_halves(k_ref, start, tq)
        vblk = vt_ref[kb]
        sel = key_scr[kb] != 0
        out = []
        for h in range(GROUP_HEADS):
            z = _dot(khalf[h * HEAD_DIM // LANES], qh[h]) * c_scale
            if which is not None:
                z = z + bias_ref[h, which]
            z = jnp.where(sel, z, NEG_BIG)
            out.append(_softmax_step_t(z, *st[h], vblk[h * HEAD_DIM:(h + 1) * HEAD_DIM, :]))
        return tuple(out)

    st = tuple(_softmax_init_t(tq) for _ in range(GROUP_HEADS))
    st = lax.fori_loop(0, jnp.maximum(qi - 1, 0), lambda kb, s: step(kb, s, None), st)
    st = lax.cond(qi >= 1, lambda s: step(qi - 1, s, 1), lambda s: s, st)
    st = step(qi, st, 0)
    o_ref[...] = jnp.concatenate([acc / l for _, l, acc in st], axis=0).T.astype(BF16)


def _dsa(proj, proj_t, bias_tiles):
    bsz, t, _ = proj.shape
    nq, tq = proj_t.shape[1], proj_t.shape[3]
    topk = min(INDEX_TOPK_MAX, t // 4)
    assert tq >= topk, "the threshold search needs at least topk keys in the first block"
    nqx = IDX_HEADS * IDX_DIM
    return pl.pallas_call(
        functools.partial(_dsa_kernel, tq=tq, topk=float(topk), c_scale=HEAD_DIM ** -0.5 * LOG2E,
                          row_bits=max(1, (t - 1).bit_length())),
        grid=(bsz, nq),
        in_specs=[pl.BlockSpec((None, None, nqx, tq), lambda b, i: (b, i, ROW_QX // nqx, 0)),
                  pl.BlockSpec((None, None, IDX_HEADS, tq), lambda b, i: (b, i, ROW_WX // IDX_HEADS, 0)),
                  pl.BlockSpec((None, t, LANES), lambda b, i: (b, 0, COL_TAIL // LANES)),
                  pl.BlockSpec((None, None, GROUP_W, tq), lambda b, i: (b, i, ROW_QC // GROUP_W, 0)),
                  pl.BlockSpec((None, t, GROUP_W), lambda b, i: (b, 0, COL_KC // GROUP_W)),
                  pl.BlockSpec((None, nq, GROUP_W, tq), lambda b, i: (b, 0, ROW_VC // GROUP_W, 0)),
                  pl.BlockSpec(bias_tiles.shape, lambda b, i: (0, 0, 0, 0))],
        out_specs=pl.BlockSpec((None, tq, GROUP_W), lambda b, i: (b, i, 0)),
        out_shape=jax.ShapeDtypeStruct((bsz, t, GROUP_W), BF16),
        scratch_shapes=[pltpu.VMEM((nq, tq, tq), I32), pltpu.VMEM((SUBLANES, tq), I32)],
        compiler_params=_cparams("parallel", "arbitrary"),
        name="dsa",
    )(proj_t, proj_t, proj, proj_t, proj, proj_t, bias_tiles)


def _dil_kernel(q_ref, kp_ref, kd_ref, vp_ref, vd_ref, bias_ref, o_ref, lse_ref, *, tq, scale):
    qi = pl.program_id(1)
    r = lax.broadcasted_iota(I32, (tq, tq), 0)
    c = lax.broadcasted_iota(I32, (tq, tq), 1)
    prev_ok = jnp.logical_and(r <= c, qi > 0)
    diag_ok = c <= r
    outs, lses = [], []
    for h in range(GROUP_HEADS):
        hs = slice(h * HEAD_DIM, (h + 1) * HEAD_DIM)
        q = q_ref[:, hs]
        zp = jnp.where(prev_ok, _dot_nt(q, kp_ref[:, hs]) * scale + bias_ref[h, 1], NEG_BIG)
        zd = jnp.where(diag_ok, _dot_nt(q, kd_ref[:, hs]) * scale + bias_ref[h, 0], NEG_BIG)
        m = jnp.maximum(jnp.max(zp, axis=1, keepdims=True), jnp.max(zd, axis=1, keepdims=True))
        pp = jnp.exp(zp - m)
        pd = jnp.exp(zd - m)
        den = jnp.sum(pp, axis=1, keepdims=True) + jnp.sum(pd, axis=1, keepdims=True)
        o = (_dot(pp.astype(BF16), vp_ref[:, hs]) + _dot(pd.astype(BF16), vd_ref[:, hs])) / den
        outs.append(o)
        lses.append(jnp.broadcast_to(m + jnp.log(den), (tq, HEAD_DIM)))
    o_ref[...] = jnp.concatenate(outs, axis=-1)
    lse_ref[...] = jnp.concatenate(lses, axis=-1)


def _dilated_one(qp, kp, vp, bias_tiles):
    n, length, _ = qp.shape
    tq = bias_tiles.shape[-1]
    blk = lambda f: pl.BlockSpec((None, tq, GROUP_W), f)
    prev = lambda b, i: (b, jnp.maximum(i - 1, 0), 0)
    cur = lambda b, i: (b, i, 0)
    return pl.pallas_call(
        functools.partial(_dil_kernel, tq=tq, scale=HEAD_DIM ** -0.5),
        grid=(n, length // tq),
        in_specs=[blk(cur), blk(prev), blk(cur), blk(prev), blk(cur),
                  pl.BlockSpec(bias_tiles.shape, lambda b, i: (0, 0, 0, 0))],
        out_specs=[blk(cur), blk(cur)],
        out_shape=[jax.ShapeDtypeStruct((n, length, GROUP_W), F32)] * 2,
        compiler_params=_cparams("parallel", "arbitrary"),
        name="dil",
    )(qp, kp, kp, vp, vp, bias_tiles)


def _dilmix_kernel(o0, o1, o2, l0, l1, l2, out_ref):
    a0, a1, a2 = l0[...], l1[...], l2[...]
    m = jnp.maximum(jnp.maximum(a0, a1), a2)
    e0, e1, e2 = jnp.exp(a0 - m), jnp.exp(a1 - m), jnp.exp(a2 - m)
    out_ref[...] = ((e0 * o0[...] + e1 * o1[...] + e2 * o2[...]) / (e0 + e1 + e2)).astype(BF16)


def _dilated_mix(outs, lses):
    bsz, t, w = outs[0].shape
    tm = min(512, t)
    spec = pl.BlockSpec((None, tm, w), lambda b, i: (b, i, 0))
    return pl.pallas_call(
        _dilmix_kernel,
        grid=(bsz, t // tm),
        in_specs=[spec] * 6,
        out_specs=spec,
        out_shape=jax.ShapeDtypeStruct((bsz, t, w), BF16),
        compiler_params=_cparams("parallel", "parallel"),
        name="dilmix",
    )(*outs, *lses)


def _dilated(proj, bias_tiles_per_cfg):
    bsz, t, _ = proj.shape
    q, k, v = (proj[:, :, COL_B + j * GROUP_W:COL_B + (j + 1) * GROUP_W] for j in range(3))
    outs, lses = [], []
    for (_, dil), tiles in zip(DILATED_CONFIGS, bias_tiles_per_cfg):
        def perm(a, dil=dil):
            return a.reshape(bsz, t // dil, dil, GROUP_W).transpose(0, 2, 1, 3).reshape(bsz * dil, t // dil, GROUP_W)

        def unperm(a, dil=dil):
            return a.reshape(bsz, dil, t // dil, GROUP_W).transpose(0, 2, 1, 3).reshape(bsz, t, GROUP_W)

        o, lse = _dilated_one(perm(q), perm(k), perm(v), tiles)
        outs.append(unperm(o))
        lses.append(unperm(lse))
    return _dilated_mix(outs, lses)


def _post_kernel(oa_ref, ob_ref, oc_ref, od_ref, wo_ref, x_ref, mod_ref, ln_ref, wr_ref, br_ref,
                 x1_ref, h2_ref, idx_ref, gate_ref, *, alpha):
    y = jnp.zeros(x_ref.shape, F32)
    for g, o_ref in enumerate((oa_ref, ob_ref, oc_ref, od_ref)):
        y = y + _dot(o_ref[...], wo_ref[g * GROUP_W:(g + 1) * GROUP_W, :])
    u = alpha * x_ref[...] + (1.0 + mod_ref[2:3, :]) * y
    x1 = _ln(u) * ln_ref[0:1, :] + ln_ref[1:2, :]
    x1_ref[...] = x1
    h2 = _ln(x1) * (1.0 + mod_ref[4:5, :]) + mod_ref[3:4, :]
    _store_token_tiles(h2_ref, 0, h2)
    logits = lax.dot_general(wr_ref[...], h2, (((1,), (1,)), ((), ())), precision=lax.Precision.HIGHEST,
                             preferred_element_type=F32) + br_ref[...]
    n_exp, tm = logits.shape
    eid = lax.broadcasted_iota(I32, (n_exp, tm), 0)
    vals, ids = [], []
    for _ in range(TOP_K):
        m = jnp.max(logits, axis=0, keepdims=True)
        first = jnp.min(jnp.where(logits == m, eid, n_exp), axis=0, keepdims=True)
        vals.append(m)
        ids.append(first)
        logits = jnp.where(eid == first, -jnp.inf, logits)
    ex = [jnp.exp(v - vals[0]) for v in vals]
    den = ex[0] + ex[1] + ex[2] + ex[3]
    zero_f = jnp.zeros((8 - TOP_K, tm), F32)
    gate_ref[...] = jnp.concatenate([e / den for e in ex] + [zero_f], axis=0)
    idx_ref[...] = jnp.concatenate(ids + [zero_f.astype(I32)], axis=0)


def _post_mixer(o_groups, w_out, x, mod, ln_rows, w_router_t, b_router, alpha):
    bsz, t, d = x.shape
    tm = min(256, t)
    n_exp = w_router_t.shape[0]
    og = pl.BlockSpec((None, tm, GROUP_W), lambda b, i: (b, i, 0))
    row = pl.BlockSpec((None, tm, d), lambda b, i: (b, i, 0))
    small = pl.BlockSpec((None, 8, tm), lambda b, i: (b, 0, i))
    assert d == SUBLANES * LANES, "token-tile layout needs one (8,128) tile per token"
    nt = t // tm
    tiles = pl.BlockSpec((tm * SUBLANES, LANES), lambda b, i: (b * nt + i, 0))
    return pl.pallas_call(
        functools.partial(_post_kernel, alpha=alpha),
        grid=(bsz, nt),
        in_specs=[og, og, og, og,
                  pl.BlockSpec(w_out.shape, lambda b, i: (0, 0)),
                  row,
                  pl.BlockSpec((None, 8, d), lambda b, i: (b, 0, 0)),
                  pl.BlockSpec((4, d), lambda b, i: (0, 0)),
                  pl.BlockSpec((n_exp, d), lambda b, i: (0, 0)),
                  pl.BlockSpec((n_exp, 1), lambda b, i: (0, 0))],
        out_specs=[row, tiles, small, small],
        out_shape=[jax.ShapeDtypeStruct((bsz, t, d), F32), jax.ShapeDtypeStruct((bsz * t * SUBLANES, LANES), F32),
                   jax.ShapeDtypeStruct((bsz, 8, t), I32), jax.ShapeDtypeStruct((bsz, 8, t), F32)],
        compiler_params=_cparams("parallel", "parallel"),
        name="post",
    )(*o_groups, w_out, x, mod, ln_rows, w_router_t, b_router.reshape(n_exp, 1))


def _deint_kernel(w_ref, p_ref, o_ref):
    o_ref[...] = _dot(w_ref[...].astype(BF16), p_ref[...]).astype(BF16)


def _deinterleave_w1(w1_all, layer):
    _, n_exp, d, two_f = w1_all.shape
    grp = 2 * FF_GROUP
    j = np.arange(grp)
    src = np.where(j < FF_GROUP, 2 * j, 2 * (j - FF_GROUP) + 1)
    perm = jnp.asarray(np.arange(grp)[:, None] == src[None, :], BF16)
    return pl.pallas_call(
        _deint_kernel,
        grid=(n_exp, two_f // grp),
        in_specs=[pl.BlockSpec((None, None, d, grp), lambda e, g: (layer, e, 0, g)),
                  pl.BlockSpec((grp, grp), lambda e, g: (0, 0))],
        out_specs=pl.BlockSpec((None, d, grp), lambda e, g: (e, 0, g)),
        out_shape=jax.ShapeDtypeStruct((n_exp, d, two_f), BF16),
        compiler_params=_cparams("parallel", "parallel"),
        name="deint",
    )(w1_all, perm)


def _moe_kernel(be_ref, nval_ref, nblk_ref, cur_ref, nxt_ref, h_hbm, w1_ref, b1_ref, w2_ref, b2_ref,
                out_hbm, xbuf, ybuf, gsem, ssem, *, bm):
    i = pl.program_id(0)
    nblk = nblk_ref[0]
    slot = i % 2

    def tile_rows(tok):
        return pl.ds(pl.multiple_of(tok * SUBLANES, SUBLANES), SUBLANES)

    def gather_row_copy(tok, r, s):
        return pltpu.make_async_copy(h_hbm.at[tile_rows(tok)], xbuf.at[tile_rows(s * bm + r)], gsem.at[s])

    def scatter_row_copy(row, r, s):
        return pltpu.make_async_copy(ybuf.at[tile_rows(s * bm + r)], out_hbm.at[tile_rows(row)], ssem.at[s])

    def block_rows(s):
        return pl.ds(pl.multiple_of(s * bm * SUBLANES, bm * SUBLANES), bm * SUBLANES)

    def start_gather(idx_ref, s):
        for r in range(bm):
            gather_row_copy(idx_ref[0, r], r, s).start()

    def wait_gather(s):
        pltpu.make_async_copy(h_hbm.at[pl.ds(0, bm * SUBLANES)], xbuf.at[block_rows(s)], gsem.at[s]).wait()

    def start_scatter(s, n):
        @pl.when(n == bm)
        def _():
            for r in range(bm):
                scatter_row_copy(cur_ref[0, bm + r], r, s).start()

        @pl.when(n < bm)
        def _():
            def body(r, _):
                scatter_row_copy(cur_ref[0, bm + r], r, s).start()
                return 0
            lax.fori_loop(0, n, body, 0)

    def wait_scatter(s, n):
        @pl.when(n == bm)
        def _():
            pltpu.make_async_copy(ybuf.at[block_rows(s)], out_hbm.at[pl.ds(0, bm * SUBLANES)], ssem.at[s]).wait()

        @pl.when(n < bm)
        def _():
            def body(r, _):
                scatter_row_copy(0, r, s).wait()
                return 0
            lax.fori_loop(0, n, body, 0)

    @pl.when(jnp.logical_and(i == 0, nblk > 0))
    def _():
        start_gather(cur_ref, 0)

    @pl.when(i + 1 < nblk)
    def _():
        start_gather(nxt_ref, 1 - slot)

    @pl.when(i < nblk)
    def _():
        wait_gather(slot)

        @pl.when(i >= 2)
        def _():
            wait_scatter(slot, nval_ref[jnp.maximum(i - 2, 0)])

        x = _load_token_tiles(xbuf, slot * bm, bm).astype(BF16)
        y = jnp.zeros((bm, w2_ref.shape[1]), F32) + b2_ref[...]
        for g in range(w2_ref.shape[0] // FF_GROUP):
            cols = slice(2 * g * FF_GROUP, 2 * (g + 1) * FF_GROUP)
            hh = _dot(x, w1_ref[:, cols]) + b1_ref[:, cols]
            glu = jnp.minimum(hh[:, :FF_GROUP], SWIGLU_LIMIT)
            lin = jnp.clip(hh[:, FF_GROUP:], -SWIGLU_LIMIT, SWIGLU_LIMIT)
            act = glu * jax.nn.sigmoid(SWIGLU_ALPHA * glu) * (lin + 1.0)
            y = y + _dot(act.astype(BF16), w2_ref[g * FF_GROUP:(g + 1) * FF_GROUP, :])
        _store_token_tiles(ybuf, slot * bm, y)
        start_scatter(slot, nval_ref[i])

    last = pl.num_programs(0) - 1

    @pl.when(jnp.logical_and(i == last, nblk >= 2))
    def _():
        wait_scatter(nblk % 2, nval_ref[jnp.maximum(nblk - 2, 0)])

    @pl.when(jnp.logical_and(i == last, nblk >= 1))
    def _():
        wait_scatter((nblk + 1) % 2, nval_ref[jnp.maximum(nblk - 1, 0)])


def _moe_experts(h2, blk_expert, blk_valid, n_used, slot_idx, w1p, b1p, w2_all, b2, layer):
    n_tok = h2.shape[0] // SUBLANES
    d = SUBLANES * LANES
    n_blocks, _, two_bm = slot_idx.shape
    bm = two_bm // 2
    n_exp, _, two_f = w1p.shape
    f = two_f // 2
    idx_spec = lambda f_: pl.BlockSpec((None, 1, two_bm), f_, memory_space=pltpu.SMEM)
    wspec = lambda shp: pl.BlockSpec((None,) + shp, lambda i, be, nv, nb: (be[i], 0, 0))
    grid_spec = pltpu.PrefetchScalarGridSpec(
        num_scalar_prefetch=3,
        grid=(n_blocks,),
        in_specs=[idx_spec(lambda i, be, nv, nb: (i, 0, 0)),
                  idx_spec(lambda i, be, nv, nb: (jnp.minimum(i + 1, n_blocks - 1), 0, 0)),
                  pl.BlockSpec(memory_space=pl.ANY),
                  wspec((d, two_f)), wspec((1, two_f)),
                  pl.BlockSpec((None, None, f, d), lambda i, be, nv, nb: (layer, be[i], 0, 0)),
                  wspec((1, d))],
        out_specs=pl.BlockSpec(memory_space=pl.ANY),
        scratch_shapes=[pltpu.VMEM((2 * bm * SUBLANES, LANES), F32), pltpu.VMEM((2 * bm * SUBLANES, LANES), F32),
                        pltpu.SemaphoreType.DMA((2,)), pltpu.SemaphoreType.DMA((2,))],
    )
    return pl.pallas_call(
        functools.partial(_moe_kernel, bm=bm),
        grid_spec=grid_spec,
        out_shape=jax.ShapeDtypeStruct((n_tok * TOP_K * SUBLANES, LANES), F32),
        compiler_params=_cparams("arbitrary"),
        name="moe",
    )(blk_expert, blk_valid, n_used, slot_idx, slot_idx, h2, w1p, b1p, w2_all, b2)


def _moe_dispatch(top_idx, bm):
    n_tok = top_idx.shape[0]
    m = n_tok * TOP_K
    assert m % bm == 0
    e_flat = top_idx.reshape(-1)
    experts = jnp.arange(N_EXPERTS, dtype=I32)
    counts = jnp.sum((e_flat[:, None] == experts[None, :]).astype(I32), axis=0)
    padded = (counts + bm - 1) // bm * bm
    pend = jnp.cumsum(padded)
    n_blocks = m // bm + N_EXPERTS
    pad_ok = jnp.arange(bm, dtype=I32)[None, :] < (padded - counts)[:, None]
    pad_key = jnp.where(pad_ok, 2 * experts[:, None] + 1, 2 * N_EXPERTS).reshape(-1)
    keys = jnp.concatenate([2 * e_flat, pad_key])
    vals = jnp.concatenate([jnp.arange(m, dtype=I32), jnp.full((N_EXPERTS * bm,), -1, I32)])
    _, asg = lax.sort((keys, vals), num_keys=1, is_stable=True)
    valid = asg >= 0
    tok = jnp.maximum(asg, 0) // TOP_K
    choice = jnp.maximum(asg, 0) % TOP_K
    slot_idx = jnp.concatenate([tok.reshape(n_blocks, bm), (choice * n_tok + tok).reshape(n_blocks, bm)], axis=1)
    blk_valid = jnp.sum(valid.reshape(n_blocks, bm).astype(I32), axis=1)
    blk_start = jnp.arange(n_blocks, dtype=I32) * bm
    blk_expert = jnp.minimum(jnp.sum((blk_start[:, None] >= pend[None, :]).astype(I32), axis=1), N_EXPERTS - 1)
    n_used = (pend[-1] // bm).astype(I32).reshape(1)
    return blk_expert, blk_valid, n_used, slot_idx.reshape(n_blocks, 1, 2 * bm)


def _comb_kernel(y4_ref, gate_ref, x_ref, mod_ref, ln_ref, o_ref, *, alpha):
    y = jnp.zeros(x_ref.shape, F32)
    for k in range(TOP_K):
        y = y + gate_ref[:, k:k + 1] * _load_token_tiles(y4_ref.at[k], 0, x_ref.shape[0])
    u = alpha * x_ref[...] + (1.0 + mod_ref[5:6, :]) * y
    o_ref[...] = _ln(u) * ln_ref[2:3, :] + ln_ref[3:4, :]


def _combine(y4, gate, x1, mod, ln_rows, alpha):
    bsz, t, d = x1.shape
    tm = min(256, t)
    nt = t // tm
    return pl.pallas_call(
        functools.partial(_comb_kernel, alpha=alpha),
        grid=(bsz, nt),
        in_specs=[pl.BlockSpec((TOP_K, tm * SUBLANES, LANES), lambda b, i: (0, b * nt + i, 0)),
                  pl.BlockSpec((None, tm, 8), lambda b, i: (b, i, 0)),
                  pl.BlockSpec((None, tm, d), lambda b, i: (b, i, 0)),
                  pl.BlockSpec((None, 8, d), lambda b, i: (b, 0, 0)),
                  pl.BlockSpec((4, d), lambda b, i: (0, 0))],
        out_specs=pl.BlockSpec((None, tm, d), lambda b, i: (b, i, 0)),
        out_shape=jax.ShapeDtypeStruct((bsz, t, d), F32),
        compiler_params=_cparams("parallel", "parallel"),
        name="comb",
    )(y4, gate, x1, mod, ln_rows)


def _rel_bucket(dist):
    n = jnp.maximum(dist, 0)
    max_exact = N_BUCKETS // 2
    nf = jnp.maximum(n, 1).astype(F32)
    large = max_exact + (jnp.log(nf / max_exact) / math.log(MAX_DISTANCE / max_exact)
                         * (N_BUCKETS - max_exact)).astype(I32)
    large = jnp.minimum(large, N_BUCKETS - 1)
    return jnp.where(n < max_exact, n, large)


def _bias_tiles(bias_tab, tq, dil, key_major):
    period = 2 * tq
    k = np.arange(period)
    d = np.where(k < tq, -k, period - k)
    dist = np.stack([np.maximum(d, 0), d + tq]) * dil
    line = bias_tab.astype(F32)[_rel_bucket(jnp.asarray(dist, I32))]
    line = jnp.moveaxis(line, -1, 0)
    flat = jnp.tile(line, (1, 1, tq))[..., :tq * (period - 1)]
    tiles = flat.reshape(line.shape[0], 2, tq, period - 1)[..., :tq]
    if key_major:
        tiles = (tiles - bias_tab.astype(F32)[N_BUCKETS - 1][:, None, None, None]) * LOG2E
        tiles = jnp.swapaxes(tiles, -1, -2)
    return tiles


def _split_w_in(w_in):
    d = w_in.shape[0]
    g = GROUP_W
    segs = {}
    o = 0
    for name, width in (("qa", g), ("ka", g), ("va", g), ("qb", g), ("kb", g), ("vb", g),
                        ("qc", g), ("kc", g), ("vc", g), ("qx", IDX_HEADS * IDX_DIM), ("kx", IDX_DIM),
                        ("wx", IDX_HEADS), ("qd", g), ("kd", g), ("vd", g)):
        segs[name] = w_in[:, o:o + width]
        o += width
    pad = jnp.zeros((d, LANES - IDX_DIM - IDX_HEADS), w_in.dtype)
    w = jnp.concatenate([segs[n] for n in ("qa", "ka", "va", "qb", "kb", "vb", "kc", "kd", "kx", "wx")] + [pad],
                        axis=1)
    wt = jnp.concatenate([segs[n] for n in ("qx", "qc", "vc", "qd", "vd", "wx")], axis=1).T
    return w.astype(BF16), wt.astype(BF16)


def _layer(x, c, layer, depth, p, tiles):
    bsz, t, d = x.shape
    alpha = (2 * depth) ** 0.25
    mod = _ada_mod(c, p["w_ada_all"], p["b_ada_all"], layer).reshape(bsz, 6, d)
    mod = jnp.concatenate([mod, jnp.zeros((bsz, 2, d), F32)], axis=1)
    ln_rows = jnp.concatenate([p["ln_g"][0:1], p["ln_b"][0:1], p["ln_g"][1:2], p["ln_b"][1:2]], axis=0)

    proj, proj_t = _ln_mod_proj(x, mod, *_split_w_in(p["w_in"]))

    o_a = _stick_breaking(proj)
    o_b = _dilated(proj, tiles["b"])
    o_c = _dsa(proj, proj_t, tiles["c"])
    lamp = p["diff_lam"].astype(F32)
    lambda_init = 0.8 - 0.6 * math.exp(-0.3 * layer)
    lam = jnp.exp(jnp.sum(lamp[0] * lamp[1])) - jnp.exp(jnp.sum(lamp[2] * lamp[3])) + lambda_init
    o_d = _differential(proj, proj_t, lam, tiles["d"], p["diff_g"], lambda_init)

    x1, h2, top_idx, gate = _post_mixer((o_a, o_b, o_c, o_d), p["w_out"].astype(BF16), x, mod, ln_rows,
                                        p["w_router"].T, p["b_router"], alpha)

    top_idx = top_idx[:, :TOP_K, :].transpose(0, 2, 1).reshape(bsz * t, TOP_K)
    blk_expert, blk_valid, n_used, slot_idx = _moe_dispatch(top_idx, MOE_BLOCK)
    n_grp = p["b1"].shape[-1] // (2 * FF_GROUP)
    b1p = p["b1"].reshape(N_EXPERTS, n_grp, FF_GROUP, 2).transpose(0, 1, 3, 2).reshape(N_EXPERTS, 1, -1)
    y_rows = _moe_experts(h2, blk_expert, blk_valid, n_used, slot_idx,
                          _deinterleave_w1(p["w1_all"], layer), b1p, p["w2_all"], p["b2"][:, None, :], layer)
    return _combine(y_rows.reshape(TOP_K, bsz * t * SUBLANES, LANES), gate.transpose(0, 2, 1), x1, mod, ln_rows,
                    alpha)


def kernel(x, c, w_ada, b_ada, w_in, w_out, diff_lam, diff_g, ln_g, ln_b, w_router, b_router, w1, b1, w2, b2,
           rel_bias):
    depth = w_in.shape[0]
    t = x.shape[1]
    tq = min(ATT_BLOCK, t)
    tiles = dict(
        b=[_bias_tiles(rel_bias[:, :GROUP_HEADS], min(128, t // dil), dil, False) for _, dil in DILATED_CONFIGS],
        c=_bias_tiles(rel_bias[:, GROUP_HEADS:2 * GROUP_HEADS], tq, 1, True),
        d=_bias_tiles(rel_bias[:, 2 * GROUP_HEADS:], tq, 1, True))
    w2_all = w2.astype(BF16)
    for layer in range(depth):
        p = dict(w_ada_all=w_ada, b_ada_all=b_ada, w_in=w_in[layer], w_out=w_out[layer],
                 diff_lam=diff_lam[layer], diff_g=diff_g[layer], ln_g=ln_g[layer], ln_b=ln_b[layer],
                 w_router=w_router[layer], b_router=b_router[layer], w1_all=w1, b1=b1[layer],
                 w2_all=w2_all, b2=b2[layer])
        x = _layer(x, c, layer, depth, p, tiles)
    return x
```

```python
import functools
import math

import numpy as np
import jax
import jax.numpy as jnp
from jax import lax
from jax.experimental import pallas as pl
from jax.experimental.pallas import tpu as pltpu

F32 = jnp.float32
BF16 = jnp.bfloat16
I32 = jnp.int32

HEAD_DIM = 64
GROUP_HEADS = 4
GROUP_W = GROUP_HEADS * HEAD_DIM
DIFF_QK_DIM = HEAD_DIM // 2
DILATED_CONFIGS = ((128, 1), (512, 4), (2048, 16))
IDX_HEADS = 16
IDX_DIM = 64
INDEX_TOPK_MAX = 256
N_EXPERTS = 32
TOP_K = 4
SWIGLU_ALPHA = 1.702
SWIGLU_LIMIT = 7.0
N_BUCKETS = 32
MAX_DISTANCE = 128
LN_EPS = 1e-5
MOE_BLOCK = 512

LANES = 128
SUBLANES = 8
VMEM_LIMIT_BYTES = 56 * 1024 * 1024
NEG_BIG = -1e30
SB_SKIP_LOG = -100.0
INT_MIN = -2 ** 31
NEG_BIG_BITS = int(np.float32(NEG_BIG).view(np.int32))
LOG2E = math.log2(math.e)
ATT_BLOCK = 256
FF_GROUP = 256

COL_A = 0
COL_B = COL_A + 3 * GROUP_W
COL_KC = COL_B + 3 * GROUP_W
COL_KD = COL_KC + GROUP_W
COL_TAIL = COL_KD + GROUP_W
PROJ_COLS = COL_TAIL + LANES
ROW_QX = 0
ROW_QC = ROW_QX + IDX_HEADS * IDX_DIM
ROW_VC = ROW_QC + GROUP_W
ROW_QD = ROW_VC + GROUP_W
ROW_VD = ROW_QD + GROUP_W
ROW_WX = ROW_VD + GROUP_W
PROJ_ROWS = ROW_WX + IDX_HEADS


def _cparams(*sem):
    return pltpu.CompilerParams(dimension_semantics=sem, vmem_limit_bytes=VMEM_LIMIT_BYTES)


def _ln(x):
    mu = jnp.mean(x, axis=-1, keepdims=True)
    xc = x - mu
    return xc * lax.rsqrt(jnp.mean(xc * xc, axis=-1, keepdims=True) + LN_EPS)


def _dot_nt(a, b):
    return lax.dot_general(a, b, (((1,), (1,)), ((), ())), preferred_element_type=F32)


def _dot(a, b):
    return jnp.dot(a, b, preferred_element_type=F32)


def _load_token_tiles(ref, first_tok, n):
    base = first_tok * SUBLANES
    return jnp.concatenate([ref[pl.ds(base + c, n, stride=SUBLANES), :] for c in range(SUBLANES)], axis=1)


def _store_token_tiles(ref, first_tok, val):
    n = val.shape[0]
    base = first_tok * SUBLANES
    for c in range(SUBLANES):
        ref[pl.ds(base + c, n, stride=SUBLANES), :] = val[:, c * LANES:(c + 1) * LANES]


def _ada_kernel(c_ref, w_ref, b_ref, o_ref):
    o_ref[...] = jnp.dot(c_ref[...], w_ref[...], precision=lax.Precision.HIGHEST,
                         preferred_element_type=F32) + b_ref[...]


def _ada_mod(c, w_all, b_all, layer):
    bsz, d = c.shape
    depth, _, n = w_all.shape
    return pl.pallas_call(
        _ada_kernel,
        grid=(n // d,),
        in_specs=[pl.BlockSpec((bsz, d), lambda j: (0, 0)),
                  pl.BlockSpec((None, d, d), lambda j: (layer, 0, j)),
                  pl.BlockSpec((None, 1, d), lambda j: (layer, 0, j))],
        out_specs=pl.BlockSpec((bsz, d), lambda j: (0, j)),
        out_shape=jax.ShapeDtypeStruct((bsz, n), F32),
        compiler_params=_cparams("arbitrary"),
        name="ada",
    )(c, w_all, b_all.reshape(depth, 1, n))


def _proj_kernel(x_ref, mod_ref, w_ref, wt_ref, o_ref, ot_ref, *, chunk):
    h = (_ln(x_ref[...]) * (1.0 + mod_ref[1:2, :]) + mod_ref[0:1, :]).astype(BF16)
    ncol = o_ref.shape[-1]
    for c0 in range(0, ncol, chunk):
        c1 = min(c0 + chunk, ncol)
        o_ref[:, c0:c1] = _dot(h, w_ref[:, c0:c1]).astype(BF16)
    n_sub, nrow, tq = ot_ref.shape
    for r0 in range(0, nrow, chunk):
        r1 = min(r0 + chunk, nrow)
        res = _dot_nt(wt_ref[r0:r1, :], h).astype(BF16)
        for j in range(n_sub):
            ot_ref[j, r0:r1, :] = res[:, j * tq:(j + 1) * tq]


def _ln_mod_proj(x, mod, w, wt):
    bsz, t, d = x.shape
    ncol, nrow = w.shape[1], wt.shape[0]
    tq = min(ATT_BLOCK, t)
    tm = min(2 * tq, t)
    return pl.pallas_call(
        functools.partial(_proj_kernel, chunk=2 * LANES),
        grid=(bsz, t // tm),
        in_specs=[pl.BlockSpec((None, tm, d), lambda b, i: (b, i, 0)),
                  pl.BlockSpec((None, 8, d), lambda b, i: (b, 0, 0)),
                  pl.BlockSpec((d, ncol), lambda b, i: (0, 0)),
                  pl.BlockSpec((nrow, d), lambda b, i: (0, 0))],
        out_specs=[pl.BlockSpec((None, tm, ncol), lambda b, i: (b, i, 0)),
                   pl.BlockSpec((None, tm // tq, nrow, tq), lambda b, i: (b, i, 0, 0))],
        out_shape=[jax.ShapeDtypeStruct((bsz, t, ncol), BF16),
                   jax.ShapeDtypeStruct((bsz, t // tq, nrow, tq), BF16)],
        compiler_params=_cparams("parallel", "parallel"),
        name="proj",
    )(x, mod, w, wt)


def _sb_kernel(q_ref, k_ref, v_ref, o_ref, *, tq, scale):
    qi = pl.program_id(1)
    r = lax.broadcasted_iota(I32, (tq, tq), 0)
    c = lax.broadcasted_iota(I32, (tq, tq), 1)
    strict_lower = c < r
    upper = jnp.where(r > c, 1.0, 0.0).astype(BF16)

    outs = []
    for h in range(GROUP_HEADS):
        hs = slice(h * HEAD_DIM, (h + 1) * HEAD_DIM)
        q = q_ref[:, hs]

        def step(kb, carry, acc, masked, hs=hs, q=q):
            start = pl.multiple_of(kb * tq, tq)
            kblk = k_ref[pl.ds(start, tq), hs]
            vblk = v_ref[pl.ds(start, tq), hs]
            z = _dot_nt(q, kblk) * scale
            log_sig = jnp.minimum(z, 0.0) - jnp.log(1.0 + jnp.exp(-jnp.abs(z)))
            log_fail = log_sig - z
            if masked:
                log_fail = jnp.where(strict_lower, log_fail, 0.0)
            lf_hi = log_fail.astype(BF16)
            lf_lo = (log_fail - lf_hi.astype(F32)).astype(BF16)
            after = _dot(lf_hi, upper) + _dot(lf_lo, upper) + carry
            w = jnp.exp(log_sig + after)
            if masked:
                w = jnp.where(strict_lower, w, 0.0)
            acc = acc + _dot(w.astype(BF16), vblk)
            carry = carry + jnp.sum(log_fail, axis=1, keepdims=True)
            return carry, acc

        carry, acc = step(qi, jnp.zeros((tq, 1), F32), jnp.zeros((tq, HEAD_DIM), F32), True)

        def cond(s):
            return jnp.logical_and(s[0] >= 0, s[3] > 0)

        def body(s, step=step):
            kb, carry, acc, _ = s
            carry, acc = step(kb, carry, acc, False)
            go = (jnp.max(carry) > SB_SKIP_LOG).astype(I32)
            return kb - 1, carry, acc, go

        go0 = (jnp.max(carry) > SB_SKIP_LOG).astype(I32)
        _, _, acc, _ = lax.while_loop(cond, body, (qi - 1, carry, acc, go0))
        outs.append(acc)
    o_ref[...] = jnp.concatenate(outs, axis=-1).astype(BF16)


def _stick_breaking(proj):
    bsz, t, _ = proj.shape
    tq = min(ATT_BLOCK, t)
    cb = COL_A // GROUP_W
    return pl.pallas_call(
        functools.partial(_sb_kernel, tq=tq, scale=HEAD_DIM ** -0.5),
        grid=(bsz, t // tq),
        in_specs=[pl.BlockSpec((None, tq, GROUP_W), lambda b, i: (b, i, cb)),
                  pl.BlockSpec((None, t, GROUP_W), lambda b, i: (b, 0, cb + 1)),
                  pl.BlockSpec((None, t, GROUP_W), lambda b, i: (b, 0, cb + 2))],
        out_specs=pl.BlockSpec((None, tq, GROUP_W), lambda b, i: (b, i, 0)),
        out_shape=jax.ShapeDtypeStruct((bsz, t, GROUP_W), BF16),
        compiler_params=_cparams("parallel", "arbitrary"),
        name="sb",
    )(proj, proj, proj)


ONES_ROWS = 16


def _softmax_init_t(tq):
    return (jnp.full((1, tq), NEG_BIG, F32), jnp.zeros((1, tq), F32), jnp.zeros((HEAD_DIM, tq), F32))


def _chain_logits(k_ref, start, tk, chains, qt):
    outs = [None] * len(chains)
    for half in range(GROUP_W // LANES):
        idx = [i for i, (lo, hi) in enumerate(chains) if lo // LANES == half]
        assert all((chains[i][1] - 1) // LANES == half for i in idx)
        khalf = k_ref[pl.ds(start, tk), half * LANES:(half + 1) * LANES]
        lane = lax.broadcasted_iota(I32, khalf.shape, 1) + half * LANES
        zero = jnp.zeros_like(khalf)
        lhs = jnp.concatenate([jnp.where(jnp.logical_and(lane >= chains[i][0], lane < chains[i][1]), khalf, zero)
                               for i in idx], axis=0)
        z = _dot(lhs, qt[half * LANES:(half + 1) * LANES, :])
        for j, i in enumerate(idx):
            outs[i] = z[j * tk:(j + 1) * tk]
    return outs


def _ones_rows(tk):
    r = lax.broadcasted_iota(I32, (ONES_ROWS, GROUP_HEADS * tk), 0)
    c = lax.broadcasted_iota(I32, (ONES_ROWS, GROUP_HEADS * tk), 1)
    lo = r * tk
    return jnp.where(jnp.logical_and(c >= lo, c < lo + tk), 1.0, 0.0).astype(BF16)


def _value_blockdiag(vblk, ones):
    tk = vblk.shape[1]
    zero = jnp.zeros((HEAD_DIM, tk), BF16)
    rows = [jnp.concatenate([vblk[h * HEAD_DIM:(h + 1) * HEAD_DIM, :] if j == h else zero
                             for j in range(GROUP_HEADS)], axis=1) for h in range(GROUP_HEADS)]
    return jnp.concatenate(rows + [ones], axis=0)


def _softmax_weights(z, m):
    m_new = jnp.maximum(m, jnp.max(z, axis=0, keepdims=True))
    return m_new, jnp.exp2(m - m_new), jnp.exp2(z - m_new).astype(BF16)


def _diff_kernel(lam_ref, qt_ref, k_ref, vt_ref, bias_ref, g_ref, o_ref, *, tq, c_scale, out_scale):
    qi = pl.program_id(1)
    r = lax.broadcasted_iota(I32, (tq, tq), 0)
    c = lax.broadcasted_iota(I32, (tq, tq), 1)
    causal = r <= c
    lam = lam_ref[0]
    qt = qt_ref[...]
    ones = _ones_rows(tq)
    chains = [(h * HEAD_DIM + j * DIFF_QK_DIM, h * HEAD_DIM + (j + 1) * DIFF_QK_DIM)
              for h in range(GROUP_HEADS) for j in range(2)]

    def step(kb, st, which, masked):
        start = pl.multiple_of(kb * tq, tq)
        zs = _chain_logits(k_ref, start, tq, chains, qt)
        ms, alphas, ps = [], [], []
        for i, z in enumerate(zs):
            z = z * c_scale
            if which is not None:
                z = z + bias_ref[i // 2, which]
            if masked:
                z = jnp.where(causal, z, NEG_BIG)
            m_new, alpha, p = _softmax_weights(z, st[i][0])
            ms.append(m_new)
            alphas.append(alpha)
            ps.append(p)
        pmat = jnp.concatenate([jnp.concatenate([ps[2 * h], ps[2 * h + 1]], axis=1) for h in range(GROUP_HEADS)],
                               axis=0)
        pv = _dot(_value_blockdiag(vt_ref[kb], ones), pmat)
        out = []
        for i in range(len(chains)):
            h, j = divmod(i, 2)
            cols = slice(j * tq, (j + 1) * tq)
            l = alphas[i] * st[i][1] + pv[GROUP_W + h:GROUP_W + h + 1, cols]
            acc = alphas[i] * st[i][2] + pv[h * HEAD_DIM:(h + 1) * HEAD_DIM, cols]
            out.append((ms[i], l, acc))
        return tuple(out)

    st = tuple(_softmax_init_t(tq) for _ in chains)
    st = lax.fori_loop(0, jnp.maximum(qi - 1, 0), lambda kb, s: step(kb, s, None, False), st)
    st = lax.cond(qi >= 1, lambda s: step(qi - 1, s, 1, False), lambda s: s, st)
    st = step(qi, st, 0, True)
    outs = []
    for h in range(GROUP_HEADS):
        (_, l1, a1), (_, l2, a2) = st[2 * h], st[2 * h + 1]
        o = a1 / l1 - lam * (a2 / l2)
        o = o * lax.rsqrt(jnp.mean(o * o, axis=0, keepdims=True) + LN_EPS)
        outs.append(o * g_ref[...] * out_scale)
    o_ref[...] = jnp.concatenate(outs, axis=0).T.astype(BF16)


def _differential(proj, proj_t, lam, bias_tiles, diff_g, lambda_init):
    bsz, t, _ = proj.shape
    nq, tq = proj_t.shape[1], proj_t.shape[3]
    grid_spec = pltpu.PrefetchScalarGridSpec(
        num_scalar_prefetch=1,
        grid=(bsz, nq),
        in_specs=[pl.BlockSpec((None, None, GROUP_W, tq), lambda b, i, lam: (b, i, ROW_QD // GROUP_W, 0)),
                  pl.BlockSpec((None, t, GROUP_W), lambda b, i, lam: (b, 0, COL_KD // GROUP_W)),
                  pl.BlockSpec((None, nq, GROUP_W, tq), lambda b, i, lam: (b, 0, ROW_VD // GROUP_W, 0)),
                  pl.BlockSpec(bias_tiles.shape, lambda b, i, lam: (0, 0, 0, 0)),
                  pl.BlockSpec((HEAD_DIM, 1), lambda b, i, lam: (0, 0))],
        out_specs=pl.BlockSpec((None, tq, GROUP_W), lambda b, i, lam: (b, i, 0)),
    )
    return pl.pallas_call(
        functools.partial(_diff_kernel, tq=tq, c_scale=DIFF_QK_DIM ** -0.5 * LOG2E, out_scale=1.0 - lambda_init),
        grid_spec=grid_spec,
        out_shape=jax.ShapeDtypeStruct((bsz, t, GROUP_W), BF16),
        compiler_params=_cparams("parallel", "arbitrary"),
        name="diff",
    )(lam.reshape(1).astype(F32), proj_t, proj, proj_t, bias_tiles, diff_g.reshape(HEAD_DIM, 1).astype(F32))


def _dsa_kernel(qx_ref, wx_ref, tail_ref, qt_ref, k_ref, vt_ref, bias_ref, o_ref, key_scr, cut_scr,
                *, tq, topk, c_scale, row_bits):
    qi = pl.program_id(1)
    nkb = qi + 1
    r = lax.broadcasted_iota(I32, (tq, tq), 0)
    c = lax.broadcasted_iota(I32, (tq, tq), 1)
    wx = wx_ref[...].astype(F32) * (IDX_HEADS ** -0.5 * IDX_DIM ** -0.5)
    zpad = jnp.zeros((LANES - IDX_DIM, tq), BF16)
    qx = [jnp.concatenate([qx_ref[h * IDX_DIM:(h + 1) * IDX_DIM, :], zpad], axis=0) for h in range(IDX_HEADS)]

    def score_block(kb, _):
        start = pl.multiple_of(kb * tq, tq)
        kt = tail_ref[pl.ds(start, tq), :]
        s = jnp.zeros((tq, tq), F32)
        for h in range(IDX_HEADS):
            s = s + wx[h:h + 1, :] * jnp.maximum(_dot(kt, qx[h]), 0.0)
        s = jnp.where(s == 0.0, 0.0, s)
        s = jnp.where(r + kb * tq <= c + qi * tq, s, -jnp.inf)
        bits = pltpu.bitcast(s, I32)
        key_scr[kb] = bits ^ ((bits >> 31) & 0x7FFFFFFF)
        return 0

    lax.fori_loop(0, nkb, score_block, 0)

    def count(pred):
        def body(kb, acc):
            hit = jnp.where(pred(key_scr[kb], kb), 1.0, 0.0)
            return acc + jnp.sum(hit.reshape(tq // SUBLANES, SUBLANES, tq), axis=0)
        acc = lax.fori_loop(0, nkb, body, jnp.zeros((SUBLANES, tq), F32))
        return jnp.sum(acc, axis=0, keepdims=True)

    def bit_step(i, thr):
        cand = thr + lax.shift_left(jnp.int32(1), 31 - i)
        cnt = count(lambda key, kb: key >= cand)
        return jnp.where(cnt >= topk, cand, thr)

    thr = lax.fori_loop(0, 32, bit_step, jnp.full((1, tq), INT_MIN, I32))
    n_gt = count(lambda key, kb: key > thr)
    n_eq = count(lambda key, kb: key == thr)
    need = topk - n_gt
    cut_scr[...] = jnp.full((SUBLANES, tq), 2 ** 30, I32)

    @pl.when(jnp.max(n_eq - need) > 0.0)
    def _():
        def row_step(i, lo):
            cand = lo + lax.shift_left(jnp.int32(1), row_bits - 1 - i)
            cnt = count(lambda key, kb: jnp.logical_and(key == thr, r + kb * tq < cand))
            return jnp.where(cnt < need, cand, lo)
        lo = lax.fori_loop(0, row_bits, row_step, jnp.zeros((1, tq), I32))
        cut_scr[...] = jnp.broadcast_to(lo, (SUBLANES, tq))

    cut = cut_scr[0:1, :]

    def select_block(kb, _):
        key = key_scr[kb]
        pos = r + kb * tq
        sel = jnp.logical_or(key > thr, jnp.logical_and(key == thr, pos <= cut))
        sel = jnp.logical_and(sel, pos <= c + qi * tq)
        key_scr[kb] = jnp.where(sel, 0, NEG_BIG_BITS)
        return 0

    lax.fori_loop(0, nkb, select_block, 0)

    qt = qt_ref[...]
    ones = _ones_rows(tq)
    chains = [(h * HEAD_DIM, (h + 1) * HEAD_DIM) for h in range(GROUP_HEADS)]

    def step(kb, st, which):
        start = pl.multiple_of(kb * tq, tq)
        zs = _chain_logits(k_ref, start, tq, chains, qt)
        mask = pltpu.bitcast(key_scr[kb], F32)
        ms, alphas, ps = [], [], []
        for h, z in enumerate(zs):
            z = z * c_scale + mask
            if which is not None:
                z = z + bias_ref[h, which]
            m_new, alpha, p = _softmax_weights(z, st[h][0])
            ms.append(m_new)
            alphas.append(alpha)
            ps.append(p)
        pv = _dot(_value_blockdiag(vt_ref[kb], ones), jnp.concatenate(ps, axis=0))
        return tuple((ms[h], alphas[h] * st[h][1] + pv[GROUP_W + h:GROUP_W + h + 1, :],
                      alphas[h] * st[h][2] + pv[h * HEAD_DIM:(h + 1) * HEAD_DIM, :]) for h in range(GROUP_HEADS))

    st = tuple(_softmax_init_t(tq) for _ in range(GROUP_HEADS))
    st = lax.fori_loop(0, jnp.maximum(qi - 1, 0), lambda kb, s: step(kb, s, None), st)
    st = lax.cond(qi >= 1, lambda s: step(qi - 1, s, 1), lambda s: s, st)
    st = step(qi, st, 0)
    o_ref[...] = jnp.concatenate([acc / l for _, l, acc in st], axis=0).T.astype(BF16)


def _dsa(proj, proj_t, bias_tiles):
    bsz, t, _ = proj.shape
    nq, tq = proj_t.shape[1], proj_t.shape[3]
    topk = min(INDEX_TOPK_MAX, t // 4)
    assert tq >= topk, "the threshold search needs at least topk keys in the first block"
    nqx = IDX_HEADS * IDX_DIM
    return pl.pallas_call(
        functools.partial(_dsa_kernel, tq=tq, topk=float(topk), c_scale=HEAD_DIM ** -0.5 * LOG2E,
                          row_bits=max(1, (t - 1).bit_length())),
        grid=(bsz, nq),
        in_specs=[pl.BlockSpec((None, None, nqx, tq), lambda b, i: (b, i, ROW_QX // nqx, 0)),
                  pl.BlockSpec((None, None, IDX_HEADS, tq), lambda b, i: (b, i, ROW_WX // IDX_HEADS, 0)),
                  pl.BlockSpec((None, t, LANES), lambda b, i: (b, 0, COL_TAIL // LANES)),
                  pl.BlockSpec((None, None, GROUP_W, tq), lambda b, i: (b, i, ROW_QC // GROUP_W, 0)),
                  pl.BlockSpec((None, t, GROUP_W), lambda b, i: (b, 0, COL_KC // GROUP_W)),
                  pl.BlockSpec((None, nq, GROUP_W, tq), lambda b, i: (b, 0, ROW_VC // GROUP_W, 0)),
                  pl.BlockSpec(bias_tiles.shape, lambda b, i: (0, 0, 0, 0))],
        out_specs=pl.BlockSpec((None, tq, GROUP_W), lambda b, i: (b, i, 0)),
        out_shape=jax.ShapeDtypeStruct((bsz, t, GROUP_W), BF16),
        scratch_shapes=[pltpu.VMEM((nq, tq, tq), I32), pltpu.VMEM((SUBLANES, tq), I32)],
        compiler_params=_cparams("parallel", "arbitrary"),
        name="dsa",
    )(proj_t, proj_t, proj, proj_t, proj, proj_t, bias_tiles)


def _dil_kernel(q_ref, kp_ref, kd_ref, vp_ref, vd_ref, bias_ref, o_ref, lse_ref, *, tq, scale):
    qi = pl.program_id(1)
    r = lax.broadcasted_iota(I32, (tq, tq), 0)
    c = lax.broadcasted_iota(I32, (tq, tq), 1)
    prev_ok = jnp.logical_and(r <= c, qi > 0)
    diag_ok = c <= r
    outs, lses = [], []
    for h in range(GROUP_HEADS):
        hs = slice(h * HEAD_DIM, (h + 1) * HEAD_DIM)
        q = q_ref[:, hs]
        zp = jnp.where(prev_ok, _dot_nt(q, kp_ref[:, hs]) * scale + bias_ref[h, 1], NEG_BIG)
        zd = jnp.where(diag_ok, _dot_nt(q, kd_ref[:, hs]) * scale + bias_ref[h, 0], NEG_BIG)
        m = jnp.maximum(jnp.max(zp, axis=1, keepdims=True), jnp.max(zd, axis=1, keepdims=True))
        pp = jnp.exp(zp - m)
        pd = jnp.exp(zd - m)
        den = jnp.sum(pp, axis=1, keepdims=True) + jnp.sum(pd, axis=1, keepdims=True)
        o = (_dot(pp.astype(BF16), vp_ref[:, hs]) + _dot(pd.astype(BF16), vd_ref[:, hs])) / den
        outs.append(o)
        lses.append(jnp.broadcast_to(m + jnp.log(den), (tq, HEAD_DIM)))
    o_ref[...] = jnp.concatenate(outs, axis=-1)
    lse_ref[...] = jnp.concatenate(lses, axis=-1)


def _dilated_one(qp, kp, vp, bias_tiles):
    n, length, _ = qp.shape
    tq = bias_tiles.shape[-1]
    blk = lambda f: pl.BlockSpec((None, tq, GROUP_W), f)
    prev = lambda b, i: (b, jnp.maximum(i - 1, 0), 0)
    cur = lambda b, i: (b, i, 0)
    return pl.pallas_call(
        functools.partial(_dil_kernel, tq=tq, scale=HEAD_DIM ** -0.5),
        grid=(n, length // tq),
        in_specs=[blk(cur), blk(prev), blk(cur), blk(prev), blk(cur),
                  pl.BlockSpec(bias_tiles.shape, lambda b, i: (0, 0, 0, 0))],
        out_specs=[blk(cur), blk(cur)],
        out_shape=[jax.ShapeDtypeStruct((n, length, GROUP_W), F32)] * 2,
        compiler_params=_cparams("parallel", "arbitrary"),
        name="dil",
    )(qp, kp, kp, vp, vp, bias_tiles)


def _dilmix_kernel(o0, o1, o2, l0, l1, l2, out_ref):
    a0, a1, a2 = l0[...], l1[...], l2[...]
    m = jnp.maximum(jnp.maximum(a0, a1), a2)
    e0, e1, e2 = jnp.exp(a0 - m), jnp.exp(a1 - m), jnp.exp(a2 - m)
    out_ref[...] = ((e0 * o0[...] + e1 * o1[...] + e2 * o2[...]) / (e0 + e1 + e2)).astype(BF16)


def _dilated_mix(outs, lses):
    bsz, t, w = outs[0].shape
    tm = min(512, t)
    spec = pl.BlockSpec((None, tm, w), lambda b, i: (b, i, 0))
    return pl.pallas_call(
        _dilmix_kernel,
        grid=(bsz, t // tm),
        in_specs=[spec] * 6,
        out_specs=spec,
        out_shape=jax.ShapeDtypeStruct((bsz, t, w), BF16),
        compiler_params=_cparams("parallel", "parallel"),
        name="dilmix",
    )(*outs, *lses)


def _dilated(proj, bias_tiles_per_cfg):
    bsz, t, _ = proj.shape
    q, k, v = (proj[:, :, COL_B + j * GROUP_W:COL_B + (j + 1) * GROUP_W] for j in range(3))
    outs, lses = [], []
    for (_, dil), tiles in zip(DILATED_CONFIGS, bias_tiles_per_cfg):
        def perm(a, dil=dil):
            return a.reshape(bsz, t // dil, dil, GROUP_W).transpose(0, 2, 1, 3).reshape(bsz * dil, t // dil, GROUP_W)

        def unperm(a, dil=dil):
            return a.reshape(bsz, dil, t // dil, GROUP_W).transpose(0, 2, 1, 3).reshape(bsz, t, GROUP_W)

        o, lse = _dilated_one(perm(q), perm(k), perm(v), tiles)
        outs.append(unperm(o))
        lses.append(unperm(lse))
    return _dilated_mix(outs, lses)


def _post_kernel(oa_ref, ob_ref, oc_ref, od_ref, wo_ref, x_ref, mod_ref, ln_ref, wr_ref, br_ref,
                 x1_ref, h2_ref, idx_ref, gate_ref, *, alpha):
    y = jnp.zeros(x_ref.shape, F32)
    for g, o_ref in enumerate((oa_ref, ob_ref, oc_ref, od_ref)):
        y = y + _dot(o_ref[...], wo_ref[g * GROUP_W:(g + 1) * GROUP_W, :])
    u = alpha * x_ref[...] + (1.0 + mod_ref[2:3, :]) * y
    x1 = _ln(u) * ln_ref[0:1, :] + ln_ref[1:2, :]
    x1_ref[...] = x1
    h2 = _ln(x1) * (1.0 + mod_ref[4:5, :]) + mod_ref[3:4, :]
    _store_token_tiles(h2_ref, 0, h2)
    logits = lax.dot_general(wr_ref[...], h2, (((1,), (1,)), ((), ())), precision=lax.Precision.HIGHEST,
                             preferred_element_type=F32) + br_ref[...]
    n_exp, tm = logits.shape
    eid = lax.broadcasted_iota(I32, (n_exp, tm), 0)
    vals, ids = [], []
    for _ in range(TOP_K):
        m = jnp.max(logits, axis=0, keepdims=True)
        first = jnp.min(jnp.where(logits == m, eid, n_exp), axis=0, keepdims=True)
        vals.append(m)
        ids.append(first)
        logits = jnp.where(eid == first, -jnp.inf, logits)
    ex = [jnp.exp(v - vals[0]) for v in vals]
    den = ex[0] + ex[1] + ex[2] + ex[3]
    zero_f = jnp.zeros((8 - TOP_K, tm), F32)
    gate_ref[...] = jnp.concatenate([e / den for e in ex] + [zero_f], axis=0)
    idx_ref[...] = jnp.concatenate(ids + [zero_f.astype(I32)], axis=0)


def _post_mixer(o_groups, w_out, x, mod, ln_rows, w_router_t, b_router, alpha):
    bsz, t, d = x.shape
    tm = min(256, t)
    n_exp = w_router_t.shape[0]
    og = pl.BlockSpec((None, tm, GROUP_W), lambda b, i: (b, i, 0))
    row = pl.BlockSpec((None, tm, d), lambda b, i: (b, i, 0))
    small = pl.BlockSpec((None, 8, tm), lambda b, i: (b, 0, i))
    assert d == SUBLANES * LANES, "token-tile layout needs one (8,128) tile per token"
    nt = t // tm
    tiles = pl.BlockSpec((tm * SUBLANES, LANES), lambda b, i: (b * nt + i, 0))
    return pl.pallas_call(
        functools.partial(_post_kernel, alpha=alpha),
        grid=(bsz, nt),
        in_specs=[og, og, og, og,
                  pl.BlockSpec(w_out.shape, lambda b, i: (0, 0)),
                  row,
                  pl.BlockSpec((None, 8, d), lambda b, i: (b, 0, 0)),
                  pl.BlockSpec((4, d), lambda b, i: (0, 0)),
                  pl.BlockSpec((n_exp, d), lambda b, i: (0, 0)),
                  pl.BlockSpec((n_exp, 1), lambda b, i: (0, 0))],
        out_specs=[row, tiles, small, small],
        out_shape=[jax.ShapeDtypeStruct((bsz, t, d), F32), jax.ShapeDtypeStruct((bsz * t * SUBLANES, LANES), F32),
                   jax.ShapeDtypeStruct((bsz, 8, t), I32), jax.ShapeDtypeStruct((bsz, 8, t), F32)],
        compiler_params=_cparams("parallel", "parallel"),
        name="post",
    )(*o_groups, w_out, x, mod, ln_rows, w_router_t, b_router.reshape(n_exp, 1))


def _deint_kernel(w_ref, p_ref, o_ref):
    o_ref[...] = _dot(w_ref[...].astype(BF16), p_ref[...]).astype(BF16)


def _deinterleave_w1(w1_all, layer):
    _, n_exp, d, two_f = w1_all.shape
    grp = 2 * FF_GROUP
    j = np.arange(grp)
    src = np.where(j < FF_GROUP, 2 * j, 2 * (j - FF_GROUP) + 1)
    perm = jnp.asarray(np.arange(grp)[:, None] == src[None, :], BF16)
    return pl.pallas_call(
        _deint_kernel,
        grid=(n_exp, two_f // grp),
        in_specs=[pl.BlockSpec((None, None, d, grp), lambda e, g: (layer, e, 0, g)),
                  pl.BlockSpec((grp, grp), lambda e, g: (0, 0))],
        out_specs=pl.BlockSpec((None, d, grp), lambda e, g: (e, 0, g)),
        out_shape=jax.ShapeDtypeStruct((n_exp, d, two_f), BF16),
        compiler_params=_cparams("parallel", "parallel"),
        name="deint",
    )(w1_all, perm)


def _moe_kernel(be_ref, nval_ref, nblk_ref, cur_ref, nxt_ref, h_hbm, w1_ref, b1_ref, w2_ref, b2_ref,
                out_hbm, xbuf, ybuf, gsem, ssem, *, bm):
    i = pl.program_id(0)
    nblk = nblk_ref[0]
    slot = i % 2

    def tile_rows(tok):
        return pl.ds(pl.multiple_of(tok * SUBLANES, SUBLANES), SUBLANES)

    def gather_row_copy(tok, r, s):
        return pltpu.make_async_copy(h_hbm.at[tile_rows(tok)], xbuf.at[tile_rows(s * bm + r)], gsem.at[s])

    def scatter_row_copy(row, r, s):
        return pltpu.make_async_copy(ybuf.at[tile_rows(s * bm + r)], out_hbm.at[tile_rows(row)], ssem.at[s])

    def block_rows(s):
        return pl.ds(pl.multiple_of(s * bm * SUBLANES, bm * SUBLANES), bm * SUBLANES)

    def start_gather(idx_ref, s):
        for r in range(bm):
            gather_row_copy(idx_ref[0, r], r, s).start()

    def start_gather_rolled(idx_ref, s):
        def body(r, _):
            gather_row_copy(idx_ref[0, r], r, s).start()
            return 0
        lax.fori_loop(0, bm, body, 0)

    def wait_gather(s):
        pltpu.make_async_copy(h_hbm.at[pl.ds(0, bm * SUBLANES)], xbuf.at[block_rows(s)], gsem.at[s]).wait()

    def start_scatter(s, n):
        @pl.when(n == bm)
        def _():
            for r in range(bm):
                scatter_row_copy(cur_ref[0, bm + r], r, s).start()

        @pl.when(n < bm)
        def _():
            def body(r, _):
                scatter_row_copy(cur_ref[0, bm + r], r, s).start()
                return 0
            lax.fori_loop(0, n, body, 0)

    def wait_scatter(s, n):
        @pl.when(n == bm)
        def _():
            pltpu.make_async_copy(ybuf.at[block_rows(s)], out_hbm.at[pl.ds(0, bm * SUBLANES)], ssem.at[s]).wait()

        @pl.when(n < bm)
        def _():
            def body(r, _):
                scatter_row_copy(0, r, s).wait()
                return 0
            lax.fori_loop(0, n, body, 0)

    @pl.when(jnp.logical_and(i == 0, nblk > 0))
    def _():
        start_gather_rolled(cur_ref, 0)

    @pl.when(i + 1 < nblk)
    def _():
        start_gather(nxt_ref, 1 - slot)

    @pl.when(i < nblk)
    def _():
        wait_gather(slot)

        @pl.when(i >= 2)
        def _():
            wait_scatter(slot, nval_ref[jnp.maximum(i - 2, 0)])

        x = _load_token_tiles(xbuf, slot * bm, bm).astype(BF16)
        y = jnp.zeros((bm, w2_ref.shape[1]), F32) + b2_ref[...]
        for g in range(w2_ref.shape[0] // FF_GROUP):
            cols = slice(2 * g * FF_GROUP, 2 * (g + 1) * FF_GROUP)
            hh = _dot(x, w1_ref[:, cols]) + b1_ref[:, cols]
            glu = jnp.minimum(hh[:, :FF_GROUP], SWIGLU_LIMIT)
            lin = jnp.clip(hh[:, FF_GROUP:], -SWIGLU_LIMIT, SWIGLU_LIMIT)
            act = glu * jax.nn.sigmoid(SWIGLU_ALPHA * glu) * (lin + 1.0)
            y = y + _dot(act.astype(BF16), w2_ref[g * FF_GROUP:(g + 1) * FF_GROUP, :])
        _store_token_tiles(ybuf, slot * bm, y)
        start_scatter(slot, nval_ref[i])

    last = pl.num_programs(0) - 1

    @pl.when(jnp.logical_and(i == last, nblk >= 2))
    def _():
        wait_scatter(nblk % 2, nval_ref[jnp.maximum(nblk - 2, 0)])

    @pl.when(jnp.logical_and(i == last, nblk >= 1))
    def _():
        wait_scatter((nblk + 1) % 2, nval_ref[jnp.maximum(nblk - 1, 0)])


def _moe_experts(h2, blk_expert, blk_valid, n_used, slot_idx, w1p, b1p, w2_all, b2, layer):
    n_tok = h2.shape[0] // SUBLANES
    d = SUBLANES * LANES
    n_blocks, _, two_bm = slot_idx.shape
    bm = two_bm // 2
    n_exp, _, two_f = w1p.shape
    f = two_f // 2
    idx_spec = lambda f_: pl.BlockSpec((None, 1, two_bm), f_, memory_space=pltpu.SMEM)
    wspec = lambda shp: pl.BlockSpec((None,) + shp, lambda i, be, nv, nb: (be[i], 0, 0))
    grid_spec = pltpu.PrefetchScalarGridSpec(
        num_scalar_prefetch=3,
        grid=(n_blocks,),
        in_specs=[idx_spec(lambda i, be, nv, nb: (i, 0, 0)),
                  idx_spec(lambda i, be, nv, nb: (jnp.minimum(i + 1, n_blocks - 1), 0, 0)),
                  pl.BlockSpec(memory_space=pl.ANY),
                  wspec((d, two_f)), wspec((1, two_f)),
                  pl.BlockSpec((None, None, f, d), lambda i, be, nv, nb: (layer, be[i], 0, 0)),
                  wspec((1, d))],
        out_specs=pl.BlockSpec(memory_space=pl.ANY),
        scratch_shapes=[pltpu.VMEM((2 * bm * SUBLANES, LANES), F32), pltpu.VMEM((2 * bm * SUBLANES, LANES), F32),
                        pltpu.SemaphoreType.DMA((2,)), pltpu.SemaphoreType.DMA((2,))],
    )
    return pl.pallas_call(
        functools.partial(_moe_kernel, bm=bm),
        grid_spec=grid_spec,
        out_shape=jax.ShapeDtypeStruct((n_tok * TOP_K * SUBLANES, LANES), F32),
        compiler_params=_cparams("arbitrary"),
        name="moe",
    )(blk_expert, blk_valid, n_used, slot_idx, slot_idx, h2, w1p, b1p, w2_all, b2)


def _moe_dispatch(top_idx, bm):
    n_tok = top_idx.shape[0]
    m = n_tok * TOP_K
    assert m % bm == 0
    e_flat = top_idx.reshape(-1)
    experts = jnp.arange(N_EXPERTS, dtype=I32)
    counts = jnp.sum((e_flat[:, None] == experts[None, :]).astype(I32), axis=0)
    padded = (counts + bm - 1) // bm * bm
    pend = jnp.cumsum(padded)
    n_blocks = m // bm + N_EXPERTS
    pad_ok = jnp.arange(bm, dtype=I32)[None, :] < (padded - counts)[:, None]
    pad_key = jnp.where(pad_ok, 2 * experts[:, None] + 1, 2 * N_EXPERTS).reshape(-1)
    keys = jnp.concatenate([2 * e_flat, pad_key])
    vals = jnp.concatenate([jnp.arange(m, dtype=I32), jnp.full((N_EXPERTS * bm,), -1, I32)])
    _, asg = lax.sort((keys, vals), num_keys=1, is_stable=True)
    valid = asg >= 0
    tok = jnp.maximum(asg, 0) // TOP_K
    choice = jnp.maximum(asg, 0) % TOP_K
    slot_idx = jnp.concatenate([tok.reshape(n_blocks, bm), (choice * n_tok + tok).reshape(n_blocks, bm)], axis=1)
    blk_valid = jnp.sum(valid.reshape(n_blocks, bm).astype(I32), axis=1)
    blk_start = jnp.arange(n_blocks, dtype=I32) * bm
    blk_expert = jnp.minimum(jnp.sum((blk_start[:, None] >= pend[None, :]).astype(I32), axis=1), N_EXPERTS - 1)
    n_used = (pend[-1] // bm).astype(I32).reshape(1)
    return blk_expert, blk_valid, n_used, slot_idx.reshape(n_blocks, 1, 2 * bm)


def _comb_kernel(y4_ref, gate_ref, x_ref, mod_ref, ln_ref, o_ref, *, alpha):
    y = jnp.zeros(x_ref.shape, F32)
    for k in range(TOP_K):
        y = y + gate_ref[:, k:k + 1] * _load_token_tiles(y4_ref.at[k], 0, x_ref.shape[0])
    u = alpha * x_ref[...] + (1.0 + mod_ref[5:6, :]) * y
    o_ref[...] = _ln(u) * ln_ref[2:3, :] + ln_ref[3:4, :]


def _combine(y4, gate, x1, mod, ln_rows, alpha):
    bsz, t, d = x1.shape
    tm = min(256, t)
    nt = t // tm
    return pl.pallas_call(
        functools.partial(_comb_kernel, alpha=alpha),
        grid=(bsz, nt),
        in_specs=[pl.BlockSpec((TOP_K, tm * SUBLANES, LANES), lambda b, i: (0, b * nt + i, 0)),
                  pl.BlockSpec((None, tm, 8), lambda b, i: (b, i, 0)),
                  pl.BlockSpec((None, tm, d), lambda b, i: (b, i, 0)),
                  pl.BlockSpec((None, 8, d), lambda b, i: (b, 0, 0)),
                  pl.BlockSpec((4, d), lambda b, i: (0, 0))],
        out_specs=pl.BlockSpec((None, tm, d), lambda b, i: (b, i, 0)),
        out_shape=jax.ShapeDtypeStruct((bsz, t, d), F32),
        compiler_params=_cparams("parallel", "parallel"),
        name="comb",
    )(y4, gate, x1, mod, ln_rows)


def _rel_bucket(dist):
    n = jnp.maximum(dist, 0)
    max_exact = N_BUCKETS // 2
    nf = jnp.maximum(n, 1).astype(F32)
    large = max_exact + (jnp.log(nf / max_exact) / math.log(MAX_DISTANCE / max_exact)
                         * (N_BUCKETS - max_exact)).astype(I32)
    large = jnp.minimum(large, N_BUCKETS - 1)
    return jnp.where(n < max_exact, n, large)


def _bias_tiles(bias_tab, tq, dil, key_major):
    period = 2 * tq
    k = np.arange(period)
    d = np.where(k < tq, -k, period - k)
    dist = np.stack([np.maximum(d, 0), d + tq]) * dil
    line = bias_tab.astype(F32)[_rel_bucket(jnp.asarray(dist, I32))]
    line = jnp.moveaxis(line, -1, 0)
    flat = jnp.tile(line, (1, 1, tq))[..., :tq * (period - 1)]
    tiles = flat.reshape(line.shape[0], 2, tq, period - 1)[..., :tq]
    if key_major:
        tiles = (tiles - bias_tab.astype(F32)[N_BUCKETS - 1][:, None, None, None]) * LOG2E
        tiles = jnp.swapaxes(tiles, -1, -2)
    return tiles


def _split_w_in(w_in):
    d = w_in.shape[0]
    g = GROUP_W
    segs = {}
    o = 0
    for name, width in (("qa", g), ("ka", g), ("va", g), ("qb", g), ("kb", g), ("vb", g),
                        ("qc", g), ("kc", g), ("vc", g), ("qx", IDX_HEADS * IDX_DIM), ("kx", IDX_DIM),
                        ("wx", IDX_HEADS), ("qd", g), ("kd", g), ("vd", g)):
        segs[name] = w_in[:, o:o + width]
        o += width
    pad = jnp.zeros((d, LANES - IDX_DIM - IDX_HEADS), w_in.dtype)
    w = jnp.concatenate([segs[n] for n in ("qa", "ka", "va", "qb", "kb", "vb", "kc", "kd", "kx", "wx")] + [pad],
                        axis=1)
    wt = jnp.concatenate([segs[n] for n in ("qx", "qc", "vc", "qd", "vd", "wx")], axis=1).T
    return w.astype(BF16), wt.astype(BF16)


def _layer(x, c, layer, depth, p, tiles):
    bsz, t, d = x.shape
    alpha = (2 * depth) ** 0.25
    mod = _ada_mod(c, p["w_ada_all"], p["b_ada_all"], layer).reshape(bsz, 6, d)
    mod = jnp.concatenate([mod, jnp.zeros((bsz, 2, d), F32)], axis=1)
    ln_rows = jnp.concatenate([p["ln_g"][0:1], p["ln_b"][0:1], p["ln_g"][1:2], p["ln_b"][1:2]], axis=0)

    proj, proj_t = _ln_mod_proj(x, mod, *_split_w_in(p["w_in"]))

    o_a = _stick_breaking(proj)
    o_b = _dilated(proj, tiles["b"])
    o_c = _dsa(proj, proj_t, tiles["c"])
    lamp = p["diff_lam"].astype(F32)
    lambda_init = 0.8 - 0.6 * math.exp(-0.3 * layer)
    lam = jnp.exp(jnp.sum(lamp[0] * lamp[1])) - jnp.exp(jnp.sum(lamp[2] * lamp[3])) + lambda_init
    o_d = _differential(proj, proj_t, lam, tiles["d"], p["diff_g"], lambda_init)

    x1, h2, top_idx, gate = _post_mixer((o_a, o_b, o_c, o_d), p["w_out"].astype(BF16), x, mod, ln_rows,
                                        p["w_router"].T, p["b_router"], alpha)

    top_idx = top_idx[:, :TOP_K, :].transpose(0, 2, 1).reshape(bsz * t, TOP_K)
    blk_expert, blk_valid, n_used, slot_idx = _moe_dispatch(top_idx, MOE_BLOCK)
    n_grp = p["b1"].shape[-1] // (2 * FF_GROUP)
    b1p = p["b1"].reshape(N_EXPERTS, n_grp, FF_GROUP, 2).transpose(0, 1, 3, 2).reshape(N_EXPERTS, 1, -1)
    y_rows = _moe_experts(h2, blk_expert, blk_valid, n_used, slot_idx,
                          _deinterleave_w1(p["w1_all"], layer), b1p, p["w2_all"], p["b2"][:, None, :], layer)
    return _combine(y_rows.reshape(TOP_K, bsz * t * SUBLANES, LANES), gate.transpose(0, 2, 1), x1, mod, ln_rows,
                    alpha)


def kernel(x, c, w_ada, b_ada, w_in, w_out, diff_lam, diff_g, ln_g, ln_b, w_router, b_router, w1, b1, w2, b2,
           rel_bias):
    depth = w_in.shape[0]
    t = x.shape[1]
    tq = min(ATT_BLOCK, t)
    tiles = dict(
        b=[_bias_tiles(rel_bias[:, :GROUP_HEADS], min(128, t // dil), dil, False) for _, dil in DILATED_CONFIGS],
        c=_bias_tiles(rel_bias[:, GROUP_HEADS:2 * GROUP_HEADS], tq, 1, True),
        d=_bias_tiles(rel_bias[:, 2 * GROUP_HEADS:], tq, 1, True))
    w2_all = w2.astype(BF16)
    for layer in range(depth):
        p = dict(w_ada_all=w_ada, b_ada_all=b_ada, w_in=w_in[layer], w_out=w_out[layer],
                 diff_lam=diff_lam[layer], diff_g=diff_g[layer], ln_g=ln_g[layer], ln_b=ln_b[layer],
                 w_router=w_router[layer], b_router=b_router[layer], w1_all=w1, b1=b1[layer],
                 w2_all=w2_all, b2=b2[layer])
        x = _layer(x, c, layer, depth, p, tiles)
    return x
```

```python
import functools
import math

import numpy as np
import jax
import jax.numpy as jnp
from jax import lax
from jax.experimental import pallas as pl
from jax.experimental.pallas import tpu as pltpu

F32 = jnp.float32
BF16 = jnp.bfloat16
I32 = jnp.int32

HEAD_DIM = 64
GROUP_HEADS = 4
GROUP_W = GROUP_HEADS * HEAD_DIM
DIFF_QK_DIM = HEAD_DIM // 2
DILATED_CONFIGS = ((128, 1), (512, 4), (2048, 16))
IDX_HEADS = 16
IDX_DIM = 64
INDEX_TOPK_MAX = 256
N_EXPERTS = 32
TOP_K = 4
SWIGLU_ALPHA = 1.702
SWIGLU_LIMIT = 7.0
N_BUCKETS = 32
MAX_DISTANCE = 128
LN_EPS = 1e-5
MOE_BLOCK = 512

LANES = 128
SUBLANES = 8
VMEM_LIMIT_BYTES = 56 * 1024 * 1024
NEG_BIG = -1e30
SB_SKIP_LOG = -100.0
INT_MIN = -2 ** 31
NEG_BIG_BITS = int(np.float32(NEG_BIG).view(np.int32))
LOG2E = math.log2(math.e)
ATT_BLOCK = 256
FF_GROUP = 256

COL_A = 0
COL_B = COL_A + 3 * GROUP_W
COL_KC = COL_B + 3 * GROUP_W
COL_KD = COL_KC + GROUP_W
COL_TAIL = COL_KD + GROUP_W
PROJ_COLS = COL_TAIL + LANES
ROW_QX = 0
ROW_QC = ROW_QX + IDX_HEADS * IDX_DIM
ROW_VC = ROW_QC + GROUP_W
ROW_QD = ROW_VC + GROUP_W
ROW_VD = ROW_QD + GROUP_W
ROW_WX = ROW_VD + GROUP_W
PROJ_ROWS = ROW_WX + IDX_HEADS


def _cparams(*sem):
    return pltpu.CompilerParams(dimension_semantics=sem, vmem_limit_bytes=VMEM_LIMIT_BYTES)


def _ln(x):
    mu = jnp.mean(x, axis=-1, keepdims=True)
    xc = x - mu
    return xc * lax.rsqrt(jnp.mean(xc * xc, axis=-1, keepdims=True) + LN_EPS)


def _dot_nt(a, b):
    return lax.dot_general(a, b, (((1,), (1,)), ((), ())), preferred_element_type=F32)


def _dot(a, b):
    return jnp.dot(a, b, preferred_element_type=F32)


def _load_token_tiles(ref, first_tok, n):
    base = first_tok * SUBLANES
    return jnp.concatenate([ref[pl.ds(base + c, n, stride=SUBLANES), :] for c in range(SUBLANES)], axis=1)


def _store_token_tiles(ref, first_tok, val):
    n = val.shape[0]
    base = first_tok * SUBLANES
    for c in range(SUBLANES):
        ref[pl.ds(base + c, n, stride=SUBLANES), :] = val[:, c * LANES:(c + 1) * LANES]


def _ada_kernel(c_ref, w_ref, b_ref, o_ref):
    o_ref[...] = jnp.dot(c_ref[...], w_ref[...], precision=lax.Precision.HIGHEST,
                         preferred_element_type=F32) + b_ref[...]


def _ada_mod(c, w_all, b_all, layer):
    bsz, d = c.shape
    depth, _, n = w_all.shape
    return pl.pallas_call(
        _ada_kernel,
        grid=(n // d,),
        in_specs=[pl.BlockSpec((bsz, d), lambda j: (0, 0)),
                  pl.BlockSpec((None, d, d), lambda j: (layer, 0, j)),
                  pl.BlockSpec((None, 1, d), lambda j: (layer, 0, j))],
        out_specs=pl.BlockSpec((bsz, d), lambda j: (0, j)),
        out_shape=jax.ShapeDtypeStruct((bsz, n), F32),
        compiler_params=_cparams("arbitrary"),
        name="ada",
    )(c, w_all, b_all.reshape(depth, 1, n))


def _proj_kernel(x_ref, mod_ref, w_ref, wt_ref, o_ref, ot_ref, *, chunk):
    h = (_ln(x_ref[...]) * (1.0 + mod_ref[1:2, :]) + mod_ref[0:1, :]).astype(BF16)
    ncol = o_ref.shape[-1]
    for c0 in range(0, ncol, chunk):
        c1 = min(c0 + chunk, ncol)
        o_ref[:, c0:c1] = _dot(h, w_ref[:, c0:c1]).astype(BF16)
    n_sub, nrow, tq = ot_ref.shape
    for r0 in range(0, nrow, chunk):
        r1 = min(r0 + chunk, nrow)
        res = _dot_nt(wt_ref[r0:r1, :], h).astype(BF16)
        for j in range(n_sub):
            ot_ref[j, r0:r1, :] = res[:, j * tq:(j + 1) * tq]


def _ln_mod_proj(x, mod, w, wt):
    bsz, t, d = x.shape
    ncol, nrow = w.shape[1], wt.shape[0]
    tq = min(ATT_BLOCK, t)
    tm = min(2 * tq, t)
    return pl.pallas_call(
        functools.partial(_proj_kernel, chunk=2 * LANES),
        grid=(bsz, t // tm),
        in_specs=[pl.BlockSpec((None, tm, d), lambda b, i: (b, i, 0)),
                  pl.BlockSpec((None, 8, d), lambda b, i: (b, 0, 0)),
                  pl.BlockSpec((d, ncol), lambda b, i: (0, 0)),
                  pl.BlockSpec((nrow, d), lambda b, i: (0, 0))],
        out_specs=[pl.BlockSpec((None, tm, ncol), lambda b, i: (b, i, 0)),
                   pl.BlockSpec((None, tm // tq, nrow, tq), lambda b, i: (b, i, 0, 0))],
        out_shape=[jax.ShapeDtypeStruct((bsz, t, ncol), BF16),
                   jax.ShapeDtypeStruct((bsz, t // tq, nrow, tq), BF16)],
        compiler_params=_cparams("parallel", "parallel"),
        name="proj",
    )(x, mod, w, wt)


def _sb_kernel(q_ref, k_ref, v_ref, o_ref, *, tq, scale):
    qi = pl.program_id(1)
    r = lax.broadcasted_iota(I32, (tq, tq), 0)
    c = lax.broadcasted_iota(I32, (tq, tq), 1)
    strict_lower = c < r
    upper = jnp.where(r > c, 1.0, 0.0).astype(BF16)

    outs = []
    for h in range(GROUP_HEADS):
        hs = slice(h * HEAD_DIM, (h + 1) * HEAD_DIM)
        q = q_ref[:, hs]

        def step(kb, carry, acc, masked, hs=hs, q=q):
            start = pl.multiple_of(kb * tq, tq)
            kblk = k_ref[pl.ds(start, tq), hs]
            vblk = v_ref[pl.ds(start, tq), hs]
            z = _dot_nt(q, kblk) * scale
            log_sig = jnp.minimum(z, 0.0) - jnp.log(1.0 + jnp.exp(-jnp.abs(z)))
            log_fail = log_sig - z
            if masked:
                log_fail = jnp.where(strict_lower, log_fail, 0.0)
            lf_hi = log_fail.astype(BF16)
            lf_lo = (log_fail - lf_hi.astype(F32)).astype(BF16)
            after = _dot(lf_hi, upper) + _dot(lf_lo, upper) + carry
            w = jnp.exp(log_sig + after)
            if masked:
                w = jnp.where(strict_lower, w, 0.0)
            acc = acc + _dot(w.astype(BF16), vblk)
            carry = carry + jnp.sum(log_fail, axis=1, keepdims=True)
            return carry, acc

        carry, acc = step(qi, jnp.zeros((tq, 1), F32), jnp.zeros((tq, HEAD_DIM), F32), True)

        def cond(s):
            return jnp.logical_and(s[0] >= 0, s[3] > 0)

        def body(s, step=step):
            kb, carry, acc, _ = s
            carry, acc = step(kb, carry, acc, False)
            go = (jnp.max(carry) > SB_SKIP_LOG).astype(I32)
            return kb - 1, carry, acc, go

        go0 = (jnp.max(carry) > SB_SKIP_LOG).astype(I32)
        _, _, acc, _ = lax.while_loop(cond, body, (qi - 1, carry, acc, go0))
        outs.append(acc)
    o_ref[...] = jnp.concatenate(outs, axis=-1).astype(BF16)


def _stick_breaking(proj):
    bsz, t, _ = proj.shape
    tq = min(ATT_BLOCK, t)
    cb = COL_A // GROUP_W
    return pl.pallas_call(
        functools.partial(_sb_kernel, tq=tq, scale=HEAD_DIM ** -0.5),
        grid=(bsz, t // tq),
        in_specs=[pl.BlockSpec((None, tq, GROUP_W), lambda b, i: (b, i, cb)),
                  pl.BlockSpec((None, t, GROUP_W), lambda b, i: (b, 0, cb + 1)),
                  pl.BlockSpec((None, t, GROUP_W), lambda b, i: (b, 0, cb + 2))],
        out_specs=pl.BlockSpec((None, tq, GROUP_W), lambda b, i: (b, i, 0)),
        out_shape=jax.ShapeDtypeStruct((bsz, t, GROUP_W), BF16),
        compiler_params=_cparams("parallel", "arbitrary"),
        name="sb",
    )(proj, proj, proj)


ONES_ROWS = 16


def _softmax_init_t(tq):
    return (jnp.full((1, tq), NEG_BIG, F32), jnp.zeros((1, tq), F32), jnp.zeros((HEAD_DIM, tq), F32))


def _chain_logits(k_ref, start, tk, chains, qt):
    outs = [None] * len(chains)
    for half in range(GROUP_W // LANES):
        idx = [i for i, (lo, hi) in enumerate(chains) if lo // LANES == half]
        assert all((chains[i][1] - 1) // LANES == half for i in idx)
        khalf = k_ref[pl.ds(start, tk), half * LANES:(half + 1) * LANES]
        lane = lax.broadcasted_iota(I32, khalf.shape, 1) + half * LANES
        zero = jnp.zeros_like(khalf)
        lhs = jnp.concatenate([jnp.where(jnp.logical_and(lane >= chains[i][0], lane < chains[i][1]), khalf, zero)
                               for i in idx], axis=0)
        z = _dot(lhs, qt[half * LANES:(half + 1) * LANES, :])
        for j, i in enumerate(idx):
            outs[i] = z[j * tk:(j + 1) * tk]
    return outs


def _ones_rows(tk):
    r = lax.broadcasted_iota(I32, (ONES_ROWS, GROUP_HEADS * tk), 0)
    c = lax.broadcasted_iota(I32, (ONES_ROWS, GROUP_HEADS * tk), 1)
    lo = r * tk
    return jnp.where(jnp.logical_and(c >= lo, c < lo + tk), 1.0, 0.0).astype(BF16)


def _value_blockdiag(vblk, ones):
    tk = vblk.shape[1]
    zero = jnp.zeros((HEAD_DIM, tk), BF16)
    rows = [jnp.concatenate([vblk[h * HEAD_DIM:(h + 1) * HEAD_DIM, :] if j == h else zero
                             for j in range(GROUP_HEADS)], axis=1) for h in range(GROUP_HEADS)]
    return jnp.concatenate(rows + [ones], axis=0)


def _softmax_weights(z, m):
    m_new = jnp.maximum(m, jnp.max(z, axis=0, keepdims=True))
    return m_new, jnp.exp2(m - m_new), jnp.exp2(z - m_new).astype(BF16)


def _causal_blocks(qi, logits, step, st):
    zs = logits(0)

    def far(kb, carry):
        zs, st = carry
        zs_next = logits(kb + 1)
        return zs_next, step(zs, kb, st, None, False)

    zs, st = lax.fori_loop(0, jnp.maximum(qi - 1, 0), far, (zs, st))
    zs, st = lax.cond(qi >= 1, lambda c: (logits(qi), step(c[0], qi - 1, c[1], 1, False)), lambda c: c, (zs, st))
    return step(zs, qi, st, 0, True)


def _diff_kernel(lam_ref, qt_ref, k_ref, vt_ref, bias_ref, g_ref, o_ref, *, tq, c_scale, out_scale):
    qi = pl.program_id(1)
    r = lax.broadcasted_iota(I32, (tq, tq), 0)
    c = lax.broadcasted_iota(I32, (tq, tq), 1)
    causal = r <= c
    lam = lam_ref[0]
    qt = qt_ref[...]
    ones = _ones_rows(tq)
    chains = [(h * HEAD_DIM + j * DIFF_QK_DIM, h * HEAD_DIM + (j + 1) * DIFF_QK_DIM)
              for h in range(GROUP_HEADS) for j in range(2)]

    def logits(kb):
        return tuple(_chain_logits(k_ref, pl.multiple_of(kb * tq, tq), tq, chains, qt))

    def step(zs, kb, st, which, masked):
        ms, alphas, ps = [], [], []
        for i, z in enumerate(zs):
            z = z * c_scale
            if which is not None:
                z = z + bias_ref[i // 2, which]
            if masked:
                z = jnp.where(causal, z, NEG_BIG)
            m_new, alpha, p = _softmax_weights(z, st[i][0])
            ms.append(m_new)
            alphas.append(alpha)
            ps.append(p)
        pmat = jnp.concatenate([jnp.concatenate([ps[2 * h], ps[2 * h + 1]], axis=1) for h in range(GROUP_HEADS)],
                               axis=0)
        pv = _dot(_value_blockdiag(vt_ref[kb], ones), pmat)
        out = []
        for i in range(len(chains)):
            h, j = divmod(i, 2)
            cols = slice(j * tq, (j + 1) * tq)
            l = alphas[i] * st[i][1] + pv[GROUP_W + h:GROUP_W + h + 1, cols]
            acc = alphas[i] * st[i][2] + pv[h * HEAD_DIM:(h + 1) * HEAD_DIM, cols]
            out.append((ms[i], l, acc))
        return tuple(out)

    st = _causal_blocks(qi, logits, step, tuple(_softmax_init_t(tq) for _ in chains))
    outs = []
    for h in range(GROUP_HEADS):
        (_, l1, a1), (_, l2, a2) = st[2 * h], st[2 * h + 1]
        o = a1 / l1 - lam * (a2 / l2)
        o = o * lax.rsqrt(jnp.mean(o * o, axis=0, keepdims=True) + LN_EPS)
        outs.append(o * g_ref[...] * out_scale)
    o_ref[...] = jnp.concatenate(outs, axis=0).T.astype(BF16)


def _differential(proj, proj_t, lam, bias_tiles, diff_g, lambda_init):
    bsz, t, _ = proj.shape
    nq, tq = proj_t.shape[1], proj_t.shape[3]
    grid_spec = pltpu.PrefetchScalarGridSpec(
        num_scalar_prefetch=1,
        grid=(bsz, nq),
        in_specs=[pl.BlockSpec((None, None, GROUP_W, tq), lambda b, i, lam: (b, i, ROW_QD // GROUP_W, 0)),
                  pl.BlockSpec((None, t, GROUP_W), lambda b, i, lam: (b, 0, COL_KD // GROUP_W)),
                  pl.BlockSpec((None, nq, GROUP_W, tq), lambda b, i, lam: (b, 0, ROW_VD // GROUP_W, 0)),
                  pl.BlockSpec(bias_tiles.shape, lambda b, i, lam: (0, 0, 0, 0)),
                  pl.BlockSpec((HEAD_DIM, 1), lambda b, i, lam: (0, 0))],
        out_specs=pl.BlockSpec((None, tq, GROUP_W), lambda b, i, lam: (b, i, 0)),
    )
    return pl.pallas_call(
        functools.partial(_diff_kernel, tq=tq, c_scale=DIFF_QK_DIM ** -0.5 * LOG2E, out_scale=1.0 - lambda_init),
        grid_spec=grid_spec,
        out_shape=jax.ShapeDtypeStruct((bsz, t, GROUP_W), BF16),
        compiler_params=_cparams("parallel", "arbitrary"),
        name="diff",
    )(lam.reshape(1).astype(F32), proj_t, proj, proj_t, bias_tiles, diff_g.reshape(HEAD_DIM, 1).astype(F32))


def _dsa_kernel(qx_ref, wx_ref, tail_ref, qt_ref, k_ref, vt_ref, bias_ref, o_ref, key_scr, cut_scr,
                *, tq, topk, c_scale, row_bits):
    qi = pl.program_id(1)
    nkb = qi + 1
    r = lax.broadcasted_iota(I32, (tq, tq), 0)
    c = lax.broadcasted_iota(I32, (tq, tq), 1)
    wx = wx_ref[...].astype(F32) * (IDX_HEADS ** -0.5 * IDX_DIM ** -0.5)
    zpad = jnp.zeros((LANES - IDX_DIM, tq), BF16)
    qx = [jnp.concatenate([qx_ref[h * IDX_DIM:(h + 1) * IDX_DIM, :], zpad], axis=0) for h in range(IDX_HEADS)]

    def score_block(kb, _):
        start = pl.multiple_of(kb * tq, tq)
        kt = tail_ref[pl.ds(start, tq), :]
        s = jnp.zeros((tq, tq), F32)
        for h in range(IDX_HEADS):
            s = s + wx[h:h + 1, :] * jnp.maximum(_dot(kt, qx[h]), 0.0)
        s = jnp.where(s == 0.0, 0.0, s)
        s = jnp.where(r + kb * tq <= c + qi * tq, s, -jnp.inf)
        bits = pltpu.bitcast(s, I32)
        key_scr[kb] = bits ^ ((bits >> 31) & 0x7FFFFFFF)
        return 0

    lax.fori_loop(0, nkb, score_block, 0)

    def count(pred):
        def body(kb, acc):
            hit = jnp.where(pred(key_scr[kb], kb), 1.0, 0.0)
            return acc + jnp.sum(hit.reshape(tq // SUBLANES, SUBLANES, tq), axis=0)
        acc = lax.fori_loop(0, nkb, body, jnp.zeros((SUBLANES, tq), F32))
        return jnp.sum(acc, axis=0, keepdims=True)

    def bit_step(i, s):
        thr, c_thr = s
        cand = thr + lax.shift_left(jnp.int32(1), 31 - i)
        cnt = count(lambda key, kb: key >= cand)
        ge = cnt >= topk
        return jnp.where(ge, cand, thr), jnp.where(ge, cnt, c_thr)

    n_keys = (nkb * tq).astype(F32)
    thr, c_thr = lax.fori_loop(0, 32, bit_step, (jnp.full((1, tq), INT_MIN, I32), jnp.full((1, tq), 1.0, F32) * n_keys))
    cut_scr[...] = jnp.full((SUBLANES, tq), 2 ** 30, I32)

    @pl.when(jnp.max(c_thr) > topk)
    def _():
        need = topk - count(lambda key, kb: key > thr)

        def row_step(i, lo):
            cand = lo + lax.shift_left(jnp.int32(1), row_bits - 1 - i)
            cnt = count(lambda key, kb: jnp.logical_and(key == thr, r + kb * tq < cand))
            return jnp.where(cnt < need, cand, lo)
        lo = lax.fori_loop(0, row_bits, row_step, jnp.zeros((1, tq), I32))
        cut_scr[...] = jnp.broadcast_to(lo, (SUBLANES, tq))

    cut = cut_scr[0:1, :]

    def select_block(kb, _):
        key = key_scr[kb]
        pos = r + kb * tq
        sel = jnp.logical_or(key > thr, jnp.logical_and(key == thr, pos <= cut))
        sel = jnp.logical_and(sel, pos <= c + qi * tq)
        key_scr[kb] = jnp.where(sel, 0, NEG_BIG_BITS)
        return 0

    lax.fori_loop(0, nkb, select_block, 0)

    qt = qt_ref[...]
    ones = _ones_rows(tq)
    chains = [(h * HEAD_DIM, (h + 1) * HEAD_DIM) for h in range(GROUP_HEADS)]

    def logits(kb):
        return tuple(_chain_logits(k_ref, pl.multiple_of(kb * tq, tq), tq, chains, qt))

    def step(zs, kb, st, which, masked):
        del masked
        mask = pltpu.bitcast(key_scr[kb], F32)
        ms, alphas, ps = [], [], []
        for h, z in enumerate(zs):
            z = z * c_scale + mask
            if which is not None:
                z = z + bias_ref[h, which]
            m_new, alpha, p = _softmax_weights(z, st[h][0])
            ms.append(m_new)
            alphas.append(alpha)
            ps.append(p)
        pv = _dot(_value_blockdiag(vt_ref[kb], ones), jnp.concatenate(ps, axis=0))
        return tuple((ms[h], alphas[h] * st[h][1] + pv[GROUP_W + h:GROUP_W + h + 1, :],
                      alphas[h] * st[h][2] + pv[h * HEAD_DIM:(h + 1) * HEAD_DIM, :]) for h in range(GROUP_HEADS))

    st = _causal_blocks(qi, logits, step, tuple(_softmax_init_t(tq) for _ in range(GROUP_HEADS)))
    o_ref[...] = jnp.concatenate([acc / l for _, l, acc in st], axis=0).T.astype(BF16)


def _dsa(proj, proj_t, bias_tiles):
    bsz, t, _ = proj.shape
    nq, tq = proj_t.shape[1], proj_t.shape[3]
    topk = min(INDEX_TOPK_MAX, t // 4)
    assert tq >= topk, "the threshold search needs at least topk keys in the first block"
    nqx = IDX_HEADS * IDX_DIM
    return pl.pallas_call(
        functools.partial(_dsa_kernel, tq=tq, topk=float(topk), c_scale=HEAD_DIM ** -0.5 * LOG2E,
                          row_bits=max(1, (t - 1).bit_length())),
        grid=(bsz, nq),
        in_specs=[pl.BlockSpec((None, None, nqx, tq), lambda b, i: (b, i, ROW_QX // nqx, 0)),
                  pl.BlockSpec((None, None, IDX_HEADS, tq), lambda b, i: (b, i, ROW_WX // IDX_HEADS, 0)),
                  pl.BlockSpec((None, t, LANES), lambda b, i: (b, 0, COL_TAIL // LANES)),
                  pl.BlockSpec((None, None, GROUP_W, tq), lambda b, i: (b, i, ROW_QC // GROUP_W, 0)),
                  pl.BlockSpec((None, t, GROUP_W), lambda b, i: (b, 0, COL_KC // GROUP_W)),
                  pl.BlockSpec((None, nq, GROUP_W, tq), lambda b, i: (b, 0, ROW_VC // GROUP_W, 0)),
                  pl.BlockSpec(bias_tiles.shape, lambda b, i: (0, 0, 0, 0))],
        out_specs=pl.BlockSpec((None, tq, GROUP_W), lambda b, i: (b, i, 0)),
        out_shape=jax.ShapeDtypeStruct((bsz, t, GROUP_W), BF16),
        scratch_shapes=[pltpu.VMEM((nq, tq, tq), I32), pltpu.VMEM((SUBLANES, tq), I32)],
        compiler_params=_cparams("parallel", "arbitrary"),
        name="dsa",
    )(proj_t, proj_t, proj, proj_t, proj, proj_t, bias_tiles)


def _dil_kernel(q_ref, kp_ref, kd_ref, vp_ref, vd_ref, bias_ref, o_ref, lse_ref, *, tq, scale):
    qi = pl.program_id(1)
    r = lax.broadcasted_iota(I32, (tq, tq), 0)
    c = lax.broadcasted_iota(I32, (tq, tq), 1)
    prev_ok = jnp.logical_and(r <= c, qi > 0)
    diag_ok = c <= r
    lane = lax.broadcasted_iota(I32, (tq, GROUP_W), 1)

    def by_head(x):
        zero = jnp.zeros_like(x)
        return jnp.concatenate([jnp.where(jnp.logical_and(lane >= h * HEAD_DIM, lane < (h + 1) * HEAD_DIM), x, zero)
                                for h in range(GROUP_HEADS)], axis=0)

    q = q_ref[...]
    zp_all = _dot_nt(q, by_head(kp_ref[...])) * scale
    zd_all = _dot_nt(q, by_head(kd_ref[...])) * scale
    pps, pds, dens, lses = [], [], [], []
    for h in range(GROUP_HEADS):
        cols = slice(h * tq, (h + 1) * tq)
        zp = jnp.where(prev_ok, zp_all[:, cols] + bias_ref[h, 1], NEG_BIG)
        zd = jnp.where(diag_ok, zd_all[:, cols] + bias_ref[h, 0], NEG_BIG)
        m = jnp.maximum(jnp.max(zp, axis=1, keepdims=True), jnp.max(zd, axis=1, keepdims=True))
        pp = jnp.exp(zp - m)
        pd = jnp.exp(zd - m)
        den = jnp.sum(pp, axis=1, keepdims=True) + jnp.sum(pd, axis=1, keepdims=True)
        pps.append(pp.astype(BF16))
        pds.append(pd.astype(BF16))
        dens.append(jnp.broadcast_to(den, (tq, HEAD_DIM)))
        lses.append(jnp.broadcast_to(m + jnp.log(den), (tq, HEAD_DIM)))
    values = jnp.concatenate([by_head(vp_ref[...]), by_head(vd_ref[...])], axis=0)
    o_ref[...] = _dot(jnp.concatenate(pps + pds, axis=1), values) / jnp.concatenate(dens, axis=-1)
    lse_ref[...] = jnp.concatenate(lses, axis=-1)


def _dilated_one(qp, kp, vp, bias_tiles):
    n, length, _ = qp.shape
    tq = bias_tiles.shape[-1]
    blk = lambda f: pl.BlockSpec((None, tq, GROUP_W), f)
    prev = lambda b, i: (b, jnp.maximum(i - 1, 0), 0)
    cur = lambda b, i: (b, i, 0)
    return pl.pallas_call(
        functools.partial(_dil_kernel, tq=tq, scale=HEAD_DIM ** -0.5),
        grid=(n, length // tq),
        in_specs=[blk(cur), blk(prev), blk(cur), blk(prev), blk(cur),
                  pl.BlockSpec(bias_tiles.shape, lambda b, i: (0, 0, 0, 0))],
        out_specs=[blk(cur), blk(cur)],
        out_shape=[jax.ShapeDtypeStruct((n, length, GROUP_W), F32)] * 2,
        compiler_params=_cparams("parallel", "arbitrary"),
        name="dil",
    )(qp, kp, kp, vp, vp, bias_tiles)


def _dilmix_kernel(o0, o1, o2, l0, l1, l2, out_ref):
    a0, a1, a2 = l0[...], l1[...], l2[...]
    m = jnp.maximum(jnp.maximum(a0, a1), a2)
    e0, e1, e2 = jnp.exp(a0 - m), jnp.exp(a1 - m), jnp.exp(a2 - m)
    out_ref[...] = ((e0 * o0[...] + e1 * o1[...] + e2 * o2[...]) / (e0 + e1 + e2)).astype(BF16)


def _dilated_mix(outs, lses):
    bsz, t, w = outs[0].shape
    tm = min(512, t)
    spec = pl.BlockSpec((None, tm, w), lambda b, i: (b, i, 0))
    return pl.pallas_call(
        _dilmix_kernel,
        grid=(bsz, t // tm),
        in_specs=[spec] * 6,
        out_specs=spec,
        out_shape=jax.ShapeDtypeStruct((bsz, t, w), BF16),
        compiler_params=_cparams("parallel", "parallel"),
        name="dilmix",
    )(*outs, *lses)


def _dilated(proj, bias_tiles_per_cfg):
    bsz, t, _ = proj.shape
    q, k, v = (proj[:, :, COL_B + j * GROUP_W:COL_B + (j + 1) * GROUP_W] for j in range(3))
    outs, lses = [], []
    for (_, dil), tiles in zip(DILATED_CONFIGS, bias_tiles_per_cfg):
        def perm(a, dil=dil):
            return a.reshape(bsz, t // dil, dil, GROUP_W).transpose(0, 2, 1, 3).reshape(bsz * dil, t // dil, GROUP_W)

        def unperm(a, dil=dil):
            return a.reshape(bsz, dil, t // dil, GROUP_W).transpose(0, 2, 1, 3).reshape(bsz, t, GROUP_W)

        o, lse = _dilated_one(perm(q), perm(k), perm(v), tiles)
        outs.append(unperm(o))
        lses.append(unperm(lse))
    return _dilated_mix(outs, lses)


def _post_kernel(oa_ref, ob_ref, oc_ref, od_ref, wo_ref, x_ref, mod_ref, ln_ref, wr_ref, br_ref,
                 x1_ref, h2_ref, idx_ref, gate_ref, *, alpha):
    y = jnp.zeros(x_ref.shape, F32)
    for g, o_ref in enumerate((oa_ref, ob_ref, oc_ref, od_ref)):
        y = y + _dot(o_ref[...], wo_ref[g * GROUP_W:(g + 1) * GROUP_W, :])
    u = alpha * x_ref[...] + (1.0 + mod_ref[2:3, :]) * y
    x1 = _ln(u) * ln_ref[0:1, :] + ln_ref[1:2, :]
    x1_ref[...] = x1
    h2 = _ln(x1) * (1.0 + mod_ref[4:5, :]) + mod_ref[3:4, :]
    _store_token_tiles(h2_ref, 0, h2)
    logits = lax.dot_general(wr_ref[...], h2, (((1,), (1,)), ((), ())), precision=lax.Precision.HIGHEST,
                             preferred_element_type=F32) + br_ref[...]
    n_exp, tm = logits.shape
    eid = lax.broadcasted_iota(I32, (n_exp, tm), 0)
    vals, ids = [], []
    for _ in range(TOP_K):
        m = jnp.max(logits, axis=0, keepdims=True)
        first = jnp.min(jnp.where(logits == m, eid, n_exp), axis=0, keepdims=True)
        vals.append(m)
        ids.append(first)
        logits = jnp.where(eid == first, -jnp.inf, logits)
    ex = [jnp.exp(v - vals[0]) for v in vals]
    den = ex[0] + ex[1] + ex[2] + ex[3]
    zero_f = jnp.zeros((8 - TOP_K, tm), F32)
    gate_ref[...] = jnp.concatenate([e / den for e in ex] + [zero_f], axis=0)
    idx_ref[...] = jnp.concatenate(ids + [zero_f.astype(I32)], axis=0)


def _post_mixer(o_groups, w_out, x, mod, ln_rows, w_router_t, b_router, alpha):
    bsz, t, d = x.shape
    tm = min(256, t)
    n_exp = w_router_t.shape[0]
    og = pl.BlockSpec((None, tm, GROUP_W), lambda b, i: (b, i, 0))
    row = pl.BlockSpec((None, tm, d), lambda b, i: (b, i, 0))
    small = pl.BlockSpec((None, 8, tm), lambda b, i: (b, 0, i))
    assert d == SUBLANES * LANES, "token-tile layout needs one (8,128) tile per token"
    nt = t // tm
    tiles = pl.BlockSpec((tm * SUBLANES, LANES), lambda b, i: (b * nt + i, 0))
    return pl.pallas_call(
        functools.partial(_post_kernel, alpha=alpha),
        grid=(bsz, nt),
        in_specs=[og, og, og, og,
                  pl.BlockSpec(w_out.shape, lambda b, i: (0, 0)),
                  row,
                  pl.BlockSpec((None, 8, d), lambda b, i: (b, 0, 0)),
                  pl.BlockSpec((4, d), lambda b, i: (0, 0)),
                  pl.BlockSpec((n_exp, d), lambda b, i: (0, 0)),
                  pl.BlockSpec((n_exp, 1), lambda b, i: (0, 0))],
        out_specs=[row, tiles, small, small],
        out_shape=[jax.ShapeDtypeStruct((bsz, t, d), F32), jax.ShapeDtypeStruct((bsz * t * SUBLANES, LANES), F32),
                   jax.ShapeDtypeStruct((bsz, 8, t), I32), jax.ShapeDtypeStruct((bsz, 8, t), F32)],
        compiler_params=_cparams("parallel", "parallel"),
        name="post",
    )(*o_groups, w_out, x, mod, ln_rows, w_router_t, b_router.reshape(n_exp, 1))


def _deint_kernel(w_ref, p_ref, o_ref):
    o_ref[...] = _dot(w_ref[...].astype(BF16), p_ref[...]).astype(BF16)


def _deinterleave_w1(w1_all, layer):
    _, n_exp, d, two_f = w1_all.shape
    grp = 2 * FF_GROUP
    j = np.arange(grp)
    src = np.where(j < FF_GROUP, 2 * j, 2 * (j - FF_GROUP) + 1)
    perm = jnp.asarray(np.arange(grp)[:, None] == src[None, :], BF16)
    return pl.pallas_call(
        _deint_kernel,
        grid=(n_exp, two_f // grp),
        in_specs=[pl.BlockSpec((None, None, d, grp), lambda e, g: (layer, e, 0, g)),
                  pl.BlockSpec((grp, grp), lambda e, g: (0, 0))],
        out_specs=pl.BlockSpec((None, d, grp), lambda e, g: (e, 0, g)),
        out_shape=jax.ShapeDtypeStruct((n_exp, d, two_f), BF16),
        compiler_params=_cparams("parallel", "parallel"),
        name="deint",
    )(w1_all, perm)


def _moe_kernel(be_ref, nval_ref, nblk_ref, cur_ref, nxt_ref, h_hbm, w1_ref, b1_ref, w2_ref, b2_ref,
                out_hbm, xbuf, ybuf, gsem, ssem, *, bm):
    i = pl.program_id(0)
    nblk = nblk_ref[0]
    slot = i % 2

    def tile_rows(tok):
        return pl.ds(pl.multiple_of(tok * SUBLANES, SUBLANES), SUBLANES)

    def gather_row_copy(tok, r, s):
        return pltpu.make_async_copy(h_hbm.at[tile_rows(tok)], xbuf.at[tile_rows(s * bm + r)], gsem.at[s])

    def scatter_row_copy(row, r, s):
        return pltpu.make_async_copy(ybuf.at[tile_rows(s * bm + r)], out_hbm.at[tile_rows(row)], ssem.at[s])

    def block_rows(s):
        return pl.ds(pl.multiple_of(s * bm * SUBLANES, bm * SUBLANES), bm * SUBLANES)

    def start_gather(idx_ref, s):
        for r in range(bm):
            gather_row_copy(idx_ref[0, r], r, s).start()

    def start_gather_rolled(idx_ref, s):
        def body(r, _):
            gather_row_copy(idx_ref[0, r], r, s).start()
            return 0
        lax.fori_loop(0, bm, body, 0)

    def wait_gather(s):
        pltpu.make_async_copy(h_hbm.at[pl.ds(0, bm * SUBLANES)], xbuf.at[block_rows(s)], gsem.at[s]).wait()

    def start_scatter(s, n):
        @pl.when(n == bm)
        def _():
            for r in range(bm):
                scatter_row_copy(cur_ref[0, bm + r], r, s).start()

        @pl.when(n < bm)
        def _():
            def body(r, _):
                scatter_row_copy(cur_ref[0, bm + r], r, s).start()
                return 0
            lax.fori_loop(0, n, body, 0)

    def wait_scatter(s, n):
        @pl.when(n == bm)
        def _():
            pltpu.make_async_copy(ybuf.at[block_rows(s)], out_hbm.at[pl.ds(0, bm * SUBLANES)], ssem.at[s]).wait()

        @pl.when(n < bm)
        def _():
            def body(r, _):
                scatter_row_copy(0, r, s).wait()
                return 0
            lax.fori_loop(0, n, body, 0)

    @pl.when(jnp.logical_and(i == 0, nblk > 0))
    def _():
        start_gather_rolled(cur_ref, 0)

    @pl.when(i + 1 < nblk)
    def _():
        start_gather(nxt_ref, 1 - slot)

    @pl.when(i < nblk)
    def _():
        wait_gather(slot)

        @pl.when(i >= 2)
        def _():
            wait_scatter(slot, nval_ref[jnp.maximum(i - 2, 0)])

        x = _load_token_tiles(xbuf, slot * bm, bm).astype(BF16)
        y = jnp.zeros((bm, w2_ref.shape[1]), F32) + b2_ref[...]
        for g in range(w2_ref.shape[0] // FF_GROUP):
            cols = slice(2 * g * FF_GROUP, 2 * (g + 1) * FF_GROUP)
            hh = _dot(x, w1_ref[:, cols]) + b1_ref[:, cols]
            glu = jnp.minimum(hh[:, :FF_GROUP], SWIGLU_LIMIT)
            lin = jnp.clip(hh[:, FF_GROUP:], -SWIGLU_LIMIT, SWIGLU_LIMIT)
            act = glu * jax.nn.sigmoid(SWIGLU_ALPHA * glu) * (lin + 1.0)
            y = y + _dot(act.astype(BF16), w2_ref[g * FF_GROUP:(g + 1) * FF_GROUP, :])
        _store_token_tiles(ybuf, slot * bm, y)
        start_scatter(slot, nval_ref[i])

    last = pl.num_programs(0) - 1

    @pl.when(jnp.logical_and(i == last, nblk >= 2))
    def _():
        wait_scatter(nblk % 2, nval_ref[jnp.maximum(nblk - 2, 0)])

    @pl.when(jnp.logical_and(i == last, nblk >= 1))
    def _():
        wait_scatter((nblk + 1) % 2, nval_ref[jnp.maximum(nblk - 1, 0)])


def _moe_experts(h2, blk_expert, blk_valid, n_used, slot_idx, w1p, b1p, w2_all, b2, layer):
    n_tok = h2.shape[0] // SUBLANES
    d = SUBLANES * LANES
    n_blocks, _, two_bm = slot_idx.shape
    bm = two_bm // 2
    n_exp, _, two_f = w1p.shape
    f = two_f // 2
    idx_spec = lambda f_: pl.BlockSpec((None, 1, two_bm), f_, memory_space=pltpu.SMEM)
    wspec = lambda shp: pl.BlockSpec((None,) + shp, lambda i, be, nv, nb: (be[i], 0, 0))
    grid_spec = pltpu.PrefetchScalarGridSpec(
        num_scalar_prefetch=3,
        grid=(n_blocks,),
        in_specs=[idx_spec(lambda i, be, nv, nb: (i, 0, 0)),
                  idx_spec(lambda i, be, nv, nb: (jnp.minimum(i + 1, n_blocks - 1), 0, 0)),
                  pl.BlockSpec(memory_space=pl.ANY),
                  wspec((d, two_f)), wspec((1, two_f)),
                  pl.BlockSpec((None, None, f, d), lambda i, be, nv, nb: (layer, be[i], 0, 0)),
                  wspec((1, d))],
        out_specs=pl.BlockSpec(memory_space=pl.ANY),
        scratch_shapes=[pltpu.VMEM((2 * bm * SUBLANES, LANES), F32), pltpu.VMEM((2 * bm * SUBLANES, LANES), F32),
                        pltpu.SemaphoreType.DMA((2,)), pltpu.SemaphoreType.DMA((2,))],
    )
    return pl.pallas_call(
        functools.partial(_moe_kernel, bm=bm),
        grid_spec=grid_spec,
        out_shape=jax.ShapeDtypeStruct((n_tok * TOP_K * SUBLANES, LANES), F32),
        compiler_params=_cparams("arbitrary"),
        name="moe",
    )(blk_expert, blk_valid, n_used, slot_idx, slot_idx, h2, w1p, b1p, w2_all, b2)


def _moe_dispatch(top_idx, bm):
    n_tok = top_idx.shape[0]
    m = n_tok * TOP_K
    assert m % bm == 0
    e_flat = top_idx.reshape(-1)
    experts = jnp.arange(N_EXPERTS, dtype=I32)
    counts = jnp.sum((e_flat[:, None] == experts[None, :]).astype(I32), axis=0)
    padded = (counts + bm - 1) // bm * bm
    pend = jnp.cumsum(padded)
    n_blocks = m // bm + N_EXPERTS
    pad_ok = jnp.arange(bm, dtype=I32)[None, :] < (padded - counts)[:, None]
    pad_key = jnp.where(pad_ok, 2 * experts[:, None] + 1, 2 * N_EXPERTS).reshape(-1)
    keys = jnp.concatenate([2 * e_flat, pad_key])
    vals = jnp.concatenate([jnp.arange(m, dtype=I32), jnp.full((N_EXPERTS * bm,), -1, I32)])
    _, asg = lax.sort((keys, vals), num_keys=1, is_stable=True)
    valid = asg >= 0
    tok = jnp.maximum(asg, 0) // TOP_K
    choice = jnp.maximum(asg, 0) % TOP_K
    slot_idx = jnp.concatenate([tok.reshape(n_blocks, bm), (choice * n_tok + tok).reshape(n_blocks, bm)], axis=1)
    blk_valid = jnp.sum(valid.reshape(n_blocks, bm).astype(I32), axis=1)
    blk_start = jnp.arange(n_blocks, dtype=I32) * bm
    blk_expert = jnp.minimum(jnp.sum((blk_start[:, None] >= pend[None, :]).astype(I32), axis=1), N_EXPERTS - 1)
    n_used = (pend[-1] // bm).astype(I32).reshape(1)
    return blk_expert, blk_valid, n_used, slot_idx.reshape(n_blocks, 1, 2 * bm)


def _comb_kernel(y4_ref, gate_ref, x_ref, mod_ref, ln_ref, o_ref, *, alpha):
    y = jnp.zeros(x_ref.shape, F32)
    for k in range(TOP_K):
        y = y + gate_ref[:, k:k + 1] * _load_token_tiles(y4_ref.at[k], 0, x_ref.shape[0])
    u = alpha * x_ref[...] + (1.0 + mod_ref[5:6, :]) * y
    o_ref[...] = _ln(u) * ln_ref[2:3, :] + ln_ref[3:4, :]


def _combine(y4, gate, x1, mod, ln_rows, alpha):
    bsz, t, d = x1.shape
    tm = min(256, t)
    nt = t // tm
    return pl.pallas_call(
        functools.partial(_comb_kernel, alpha=alpha),
        grid=(bsz, nt),
        in_specs=[pl.BlockSpec((TOP_K, tm * SUBLANES, LANES), lambda b, i: (0, b * nt + i, 0)),
                  pl.BlockSpec((None, tm, 8), lambda b, i: (b, i, 0)),
                  pl.BlockSpec((None, tm, d), lambda b, i: (b, i, 0)),
                  pl.BlockSpec((None, 8, d), lambda b, i: (b, 0, 0)),
                  pl.BlockSpec((4, d), lambda b, i: (0, 0))],
        out_specs=pl.BlockSpec((None, tm, d), lambda b, i: (b, i, 0)),
        out_shape=jax.ShapeDtypeStruct((bsz, t, d), F32),
        compiler_params=_cparams("parallel", "parallel"),
        name="comb",
    )(y4, gate, x1, mod, ln_rows)


def _rel_bucket(dist):
    n = jnp.maximum(dist, 0)
    max_exact = N_BUCKETS // 2
    nf = jnp.maximum(n, 1).astype(F32)
    large = max_exact + (jnp.log(nf / max_exact) / math.log(MAX_DISTANCE / max_exact)
                         * (N_BUCKETS - max_exact)).astype(I32)
    large = jnp.minimum(large, N_BUCKETS - 1)
    return jnp.where(n < max_exact, n, large)


def _bias_tiles(bias_tab, tq, dil, key_major):
    period = 2 * tq
    k = np.arange(period)
    d = np.where(k < tq, -k, period - k)
    dist = np.stack([np.maximum(d, 0), d + tq]) * dil
    line = bias_tab.astype(F32)[_rel_bucket(jnp.asarray(dist, I32))]
    line = jnp.moveaxis(line, -1, 0)
    flat = jnp.tile(line, (1, 1, tq))[..., :tq * (period - 1)]
    tiles = flat.reshape(line.shape[0], 2, tq, period - 1)[..., :tq]
    if key_major:
        tiles = (tiles - bias_tab.astype(F32)[N_BUCKETS - 1][:, None, None, None]) * LOG2E
        tiles = jnp.swapaxes(tiles, -1, -2)
    return tiles


def _split_w_in(w_in):
    d = w_in.shape[0]
    g = GROUP_W
    segs = {}
    o = 0
    for name, width in (("qa", g), ("ka", g), ("va", g), ("qb", g), ("kb", g), ("vb", g),
                        ("qc", g), ("kc", g), ("vc", g), ("qx", IDX_HEADS * IDX_DIM), ("kx", IDX_DIM),
                        ("wx", IDX_HEADS), ("qd", g), ("kd", g), ("vd", g)):
        segs[name] = w_in[:, o:o + width]
        o += width
    pad = jnp.zeros((d, LANES - IDX_DIM - IDX_HEADS), w_in.dtype)
    w = jnp.concatenate([segs[n] for n in ("qa", "ka", "va", "qb", "kb", "vb", "kc", "kd", "kx", "wx")] + [pad],
                        axis=1)
    wt = jnp.concatenate([segs[n] for n in ("qx", "qc", "vc", "qd", "vd", "wx")], axis=1).T
    return w.astype(BF16), wt.astype(BF16)


def _layer(x, c, layer, depth, p, tiles):
    bsz, t, d = x.shape
    alpha = (2 * depth) ** 0.25
    mod = _ada_mod(c, p["w_ada_all"], p["b_ada_all"], layer).reshape(bsz, 6, d)
    mod = jnp.concatenate([mod, jnp.zeros((bsz, 2, d), F32)], axis=1)
    ln_rows = jnp.concatenate([p["ln_g"][0:1], p["ln_b"][0:1], p["ln_g"][1:2], p["ln_b"][1:2]], axis=0)

    proj, proj_t = _ln_mod_proj(x, mod, *_split_w_in(p["w_in"]))

    o_a = _stick_breaking(proj)
    o_b = _dilated(proj, tiles["b"])
    o_c = _dsa(proj, proj_t, tiles["c"])
    lamp = p["diff_lam"].astype(F32)
    lambda_init = 0.8 - 0.6 * math.exp(-0.3 * layer)
    lam = jnp.exp(jnp.sum(lamp[0] * lamp[1])) - jnp.exp(jnp.sum(lamp[2] * lamp[3])) + lambda_init
    o_d = _differential(proj, proj_t, lam, tiles["d"], p["diff_g"], lambda_init)

    x1, h2, top_idx, gate = _post_mixer((o_a, o_b, o_c, o_d), p["w_out"].astype(BF16), x, mod, ln_rows,
                                        p["w_router"].T, p["b_router"], alpha)

    top_idx = top_idx[:, :TOP_K, :].transpose(0, 2, 1).reshape(bsz * t, TOP_K)
    blk_expert, blk_valid, n_used, slot_idx = _moe_dispatch(top_idx, MOE_BLOCK)
    n_grp = p["b1"].shape[-1] // (2 * FF_GROUP)
    b1p = p["b1"].reshape(N_EXPERTS, n_grp, FF_GROUP, 2).transpose(0, 1, 3, 2).reshape(N_EXPERTS, 1, -1)
    y_rows = _moe_experts(h2, blk_expert, blk_valid, n_used, slot_idx,
                          _deinterleave_w1(p["w1_all"], layer), b1p, p["w2_all"], p["b2"][:, None, :], layer)
    return _combine(y_rows.reshape(TOP_K, bsz * t * SUBLANES, LANES), gate.transpose(0, 2, 1), x1, mod, ln_rows,
                    alpha)


def kernel(x, c, w_ada, b_ada, w_in, w_out, diff_lam, diff_g, ln_g, ln_b, w_router, b_router, w1, b1, w2, b2,
           rel_bias):
    depth = w_in.shape[0]
    t = x.shape[1]
    tq = min(ATT_BLOCK, t)
    tiles = dict(
        b=[_bias_tiles(rel_bias[:, :GROUP_HEADS], min(128, t // dil), dil, False) for _, dil in DILATED_CONFIGS],
        c=_bias_tiles(rel_bias[:, GROUP_HEADS:2 * GROUP_HEADS], tq, 1, True),
        d=_bias_tiles(rel_bias[:, 2 * GROUP_HEADS:], tq, 1, True))
    w2_all = w2.astype(BF16)
    for layer in range(depth):
        p = dict(w_ada_all=w_ada, b_ada_all=b_ada, w_in=w_in[layer], w_out=w_out[layer],
                 diff_lam=diff_lam[layer], diff_g=diff_g[layer], ln_g=ln_g[layer], ln_b=ln_b[layer],
                 w_router=w_router[layer], b_router=b_router[layer], w1_all=w1, b1=b1[layer],
                 w2_all=w2_all, b2=b2[layer])
        x = _layer(x, c, layer, depth, p, tiles)
    return x
```

```python
import functools
import math

import numpy as np
import jax
import jax.numpy as jnp
from jax import lax
from jax.experimental import pallas as pl
from jax.experimental.pallas import tpu as pltpu

F32 = jnp.float32
BF16 = jnp.bfloat16
I32 = jnp.int32

HEAD_DIM = 64
GROUP_HEADS = 4
GROUP_W = GROUP_HEADS * HEAD_DIM
DIFF_QK_DIM = HEAD_DIM // 2
DILATED_CONFIGS = ((128, 1), (512, 4), (2048, 16))
IDX_HEADS = 16
IDX_DIM = 64
INDEX_TOPK_MAX = 256
N_EXPERTS = 32
TOP_K = 4
SWIGLU_ALPHA = 1.702
SWIGLU_LIMIT = 7.0
N_BUCKETS = 32
MAX_DISTANCE = 128
LN_EPS = 1e-5
MOE_BLOCK = 512

LANES = 128
SUBLANES = 8
VMEM_LIMIT_BYTES = 56 * 1024 * 1024
NEG_BIG = -1e30
SB_SKIP_LOG = -100.0
INT_MIN = -2 ** 31
NEG_BIG_BITS = int(np.float32(NEG_BIG).view(np.int32))
LOG2E = math.log2(math.e)
ATT_BLOCK = 256
FF_GROUP = 256

COL_A = 0
COL_B = COL_A + 3 * GROUP_W
COL_KC = COL_B + 3 * GROUP_W
COL_KD = COL_KC + GROUP_W
COL_TAIL = COL_KD + GROUP_W
PROJ_COLS = COL_TAIL + LANES
ROW_QX = 0
ROW_QC = ROW_QX + IDX_HEADS * IDX_DIM
ROW_VC = ROW_QC + GROUP_W
ROW_QD = ROW_VC + GROUP_W
ROW_VD = ROW_QD + GROUP_W
ROW_WX = ROW_VD + GROUP_W
PROJ_ROWS = ROW_WX + IDX_HEADS


def _cparams(*sem):
    return pltpu.CompilerParams(dimension_semantics=sem, vmem_limit_bytes=VMEM_LIMIT_BYTES)


def _ln(x):
    mu = jnp.mean(x, axis=-1, keepdims=True)
    xc = x - mu
    return xc * lax.rsqrt(jnp.mean(xc * xc, axis=-1, keepdims=True) + LN_EPS)


def _dot_nt(a, b):
    return lax.dot_general(a, b, (((1,), (1,)), ((), ())), preferred_element_type=F32)


def _dot(a, b):
    return jnp.dot(a, b, preferred_element_type=F32)


def _load_token_tiles(ref, first_tok, n):
    base = first_tok * SUBLANES
    return jnp.concatenate([ref[pl.ds(base + c, n, stride=SUBLANES), :] for c in range(SUBLANES)], axis=1)


def _store_token_tiles(ref, first_tok, val):
    n = val.shape[0]
    base = first_tok * SUBLANES
    for c in range(SUBLANES):
        ref[pl.ds(base + c, n, stride=SUBLANES), :] = val[:, c * LANES:(c + 1) * LANES]


def _ada_kernel(c_ref, w_ref, b_ref, o_ref):
    o_ref[...] = jnp.dot(c_ref[...], w_ref[...], precision=lax.Precision.HIGHEST,
                         preferred_element_type=F32) + b_ref[...]


def _ada_mod(c, w_all, b_all, layer):
    bsz, d = c.shape
    depth, _, n = w_all.shape
    return pl.pallas_call(
        _ada_kernel,
        grid=(n // d,),
        in_specs=[pl.BlockSpec((bsz, d), lambda j: (0, 0)),
                  pl.BlockSpec((None, d, d), lambda j: (layer, 0, j)),
                  pl.BlockSpec((None, 1, d), lambda j: (layer, 0, j))],
        out_specs=pl.BlockSpec((bsz, d), lambda j: (0, j)),
        out_shape=jax.ShapeDtypeStruct((bsz, n), F32),
        compiler_params=_cparams("arbitrary"),
        name="ada",
    )(c, w_all, b_all.reshape(depth, 1, n))


def _proj_kernel(x_ref, mod_ref, w_ref, wt_ref, o_ref, ot_ref, *, chunk):
    h = (_ln(x_ref[...]) * (1.0 + mod_ref[1:2, :]) + mod_ref[0:1, :]).astype(BF16)
    ncol = o_ref.shape[-1]
    for c0 in range(0, ncol, chunk):
        c1 = min(c0 + chunk, ncol)
        o_ref[:, c0:c1] = _dot(h, w_ref[:, c0:c1]).astype(BF16)
    n_sub, nrow, tq = ot_ref.shape
    for r0 in range(0, nrow, chunk):
        r1 = min(r0 + chunk, nrow)
        res = _dot_nt(wt_ref[r0:r1, :], h).astype(BF16)
        for j in range(n_sub):
            ot_ref[j, r0:r1, :] = res[:, j * tq:(j + 1) * tq]


def _ln_mod_proj(x, mod, w, wt):
    bsz, t, d = x.shape
    ncol, nrow = w.shape[1], wt.shape[0]
    tq = min(ATT_BLOCK, t)
    tm = min(2 * tq, t)
    return pl.pallas_call(
        functools.partial(_proj_kernel, chunk=2 * LANES),
        grid=(bsz, t // tm),
        in_specs=[pl.BlockSpec((None, tm, d), lambda b, i: (b, i, 0)),
                  pl.BlockSpec((None, 8, d), lambda b, i: (b, 0, 0)),
                  pl.BlockSpec((d, ncol), lambda b, i: (0, 0)),
                  pl.BlockSpec((nrow, d), lambda b, i: (0, 0))],
        out_specs=[pl.BlockSpec((None, tm, ncol), lambda b, i: (b, i, 0)),
                   pl.BlockSpec((None, tm // tq, nrow, tq), lambda b, i: (b, i, 0, 0))],
        out_shape=[jax.ShapeDtypeStruct((bsz, t, ncol), BF16),
                   jax.ShapeDtypeStruct((bsz, t // tq, nrow, tq), BF16)],
        compiler_params=_cparams("parallel", "parallel"),
        name="proj",
    )(x, mod, w, wt)


def _by_head(x):
    lane = lax.broadcasted_iota(I32, x.shape, 1)
    zero = jnp.zeros_like(x)
    return jnp.concatenate([jnp.where(jnp.logical_and(lane >= h * HEAD_DIM, lane < (h + 1) * HEAD_DIM), x, zero)
                            for h in range(GROUP_HEADS)], axis=0)


def _sb_kernel(q_ref, k_ref, v_ref, o_ref, *, tq, scale):
    qi = pl.program_id(1)
    r = lax.broadcasted_iota(I32, (tq, tq), 0)
    c = lax.broadcasted_iota(I32, (tq, tq), 1)
    strict_lower = c < r
    upper = jnp.where(r > c, 1.0, 0.0).astype(BF16)

    q = q_ref[...]
    valid = jnp.concatenate([strict_lower] * GROUP_HEADS, axis=1)

    def step(kb, carry, acc, masked):
        start = pl.multiple_of(kb * tq, tq)
        z = _dot_nt(q, _by_head(k_ref[pl.ds(start, tq), :])) * scale
        log_sig = jnp.minimum(z, 0.0) - jnp.log(1.0 + jnp.exp(-jnp.abs(z)))
        log_fail = log_sig - z
        if masked:
            log_fail = jnp.where(valid, log_fail, 0.0)
        lf = jnp.concatenate([log_fail[:, h * tq:(h + 1) * tq] for h in range(GROUP_HEADS)], axis=0)
        lf_hi = lf.astype(BF16)
        lf_lo = (lf - lf_hi.astype(F32)).astype(BF16)
        after = _dot(lf_hi, upper) + _dot(lf_lo, upper)
        after = jnp.concatenate([after[h * tq:(h + 1) * tq] + carry[h] for h in range(GROUP_HEADS)], axis=1)
        w = jnp.exp(log_sig + after)
        if masked:
            w = jnp.where(valid, w, 0.0)
        acc = acc + _dot(w.astype(BF16), _by_head(v_ref[pl.ds(start, tq), :]))
        carry = tuple(carry[h] + jnp.sum(log_fail[:, h * tq:(h + 1) * tq], axis=1, keepdims=True)
                      for h in range(GROUP_HEADS))
        return carry, acc

    def live(carry):
        top = carry[0]
        for h in range(1, GROUP_HEADS):
            top = jnp.maximum(top, carry[h])
        return (jnp.max(top) > SB_SKIP_LOG).astype(I32)

    carry, acc = step(qi, tuple(jnp.zeros((tq, 1), F32) for _ in range(GROUP_HEADS)),
                      jnp.zeros((tq, GROUP_W), F32), True)

    def body(s):
        kb, carry, acc, _ = s
        carry, acc = step(kb, carry, acc, False)
        return kb - 1, carry, acc, live(carry)

    _, _, acc, _ = lax.while_loop(lambda s: jnp.logical_and(s[0] >= 0, s[3] > 0), body,
                                  (qi - 1, carry, acc, live(carry)))
    o_ref[...] = acc.astype(BF16)


def _stick_breaking(proj):
    bsz, t, _ = proj.shape
    tq = min(ATT_BLOCK, t)
    cb = COL_A // GROUP_W
    return pl.pallas_call(
        functools.partial(_sb_kernel, tq=tq, scale=HEAD_DIM ** -0.5),
        grid=(bsz, t // tq),
        in_specs=[pl.BlockSpec((None, tq, GROUP_W), lambda b, i: (b, i, cb)),
                  pl.BlockSpec((None, t, GROUP_W), lambda b, i: (b, 0, cb + 1)),
                  pl.BlockSpec((None, t, GROUP_W), lambda b, i: (b, 0, cb + 2))],
        out_specs=pl.BlockSpec((None, tq, GROUP_W), lambda b, i: (b, i, 0)),
        out_shape=jax.ShapeDtypeStruct((bsz, t, GROUP_W), BF16),
        compiler_params=_cparams("parallel", "arbitrary"),
        name="sb",
    )(proj, proj, proj)


ONES_ROWS = 16


def _softmax_init_t(tq):
    return (jnp.full((1, tq), NEG_BIG, F32), jnp.zeros((1, tq), F32), jnp.zeros((HEAD_DIM, tq), F32))


def _chain_logits(k_ref, start, tk, chains, qt):
    outs = [None] * len(chains)
    for half in range(GROUP_W // LANES):
        idx = [i for i, (lo, hi) in enumerate(chains) if lo // LANES == half]
        assert all((chains[i][1] - 1) // LANES == half for i in idx)
        khalf = k_ref[pl.ds(start, tk), half * LANES:(half + 1) * LANES]
        lane = lax.broadcasted_iota(I32, khalf.shape, 1) + half * LANES
        zero = jnp.zeros_like(khalf)
        lhs = jnp.concatenate([jnp.where(jnp.logical_and(lane >= chains[i][0], lane < chains[i][1]), khalf, zero)
                               for i in idx], axis=0)
        z = _dot(lhs, qt[half * LANES:(half + 1) * LANES, :])
        for j, i in enumerate(idx):
            outs[i] = z[j * tk:(j + 1) * tk]
    return outs


def _ones_rows(tk):
    r = lax.broadcasted_iota(I32, (ONES_ROWS, GROUP_HEADS * tk), 0)
    c = lax.broadcasted_iota(I32, (ONES_ROWS, GROUP_HEADS * tk), 1)
    lo = r * tk
    return jnp.where(jnp.logical_and(c >= lo, c < lo + tk), 1.0, 0.0).astype(BF16)


def _value_blockdiag(vblk, ones):
    tk = vblk.shape[1]
    zero = jnp.zeros((HEAD_DIM, tk), BF16)
    rows = [jnp.concatenate([vblk[h * HEAD_DIM:(h + 1) * HEAD_DIM, :] if j == h else zero
                             for j in range(GROUP_HEADS)], axis=1) for h in range(GROUP_HEADS)]
    return jnp.concatenate(rows + [ones], axis=0)


def _softmax_weights(z, m):
    m_new = jnp.maximum(m, jnp.max(z, axis=0, keepdims=True))
    return m_new, jnp.exp2(m - m_new), jnp.exp2(z - m_new).astype(BF16)


def _causal_blocks(qi, logits, step, st):
    st = lax.fori_loop(0, jnp.maximum(qi - 1, 0), lambda kb, s: step(logits(kb), kb, s, None, False), st)
    st = lax.cond(qi >= 1, lambda s: step(logits(qi - 1), qi - 1, s, 1, False), lambda s: s, st)
    return step(logits(qi), qi, st, 0, True)


def _diff_kernel(lam_ref, qt_ref, k_ref, vt_ref, bias_ref, g_ref, o_ref, *, tq, c_scale, out_scale):
    qi = pl.program_id(1)
    r = lax.broadcasted_iota(I32, (tq, tq), 0)
    c = lax.broadcasted_iota(I32, (tq, tq), 1)
    causal = r <= c
    lam = lam_ref[0]
    qt = qt_ref[...]
    ones = _ones_rows(tq)
    chains = [(h * HEAD_DIM + j * DIFF_QK_DIM, h * HEAD_DIM + (j + 1) * DIFF_QK_DIM)
              for h in range(GROUP_HEADS) for j in range(2)]

    def logits(kb):
        return tuple(_chain_logits(k_ref, pl.multiple_of(kb * tq, tq), tq, chains, qt))

    def step(zs, kb, st, which, masked):
        ms, alphas, ps = [], [], []
        for i, z in enumerate(zs):
            z = z * c_scale
            if which is not None:
                z = z + bias_ref[i // 2, which]
            if masked:
                z = jnp.where(causal, z, NEG_BIG)
            m_new, alpha, p = _softmax_weights(z, st[i][0])
            ms.append(m_new)
            alphas.append(alpha)
            ps.append(p)
        pmat = jnp.concatenate([jnp.concatenate([ps[2 * h], ps[2 * h + 1]], axis=1) for h in range(GROUP_HEADS)],
                               axis=0)
        pv = _dot(_value_blockdiag(vt_ref[kb], ones), pmat)
        out = []
        for i in range(len(chains)):
            h, j = divmod(i, 2)
            cols = slice(j * tq, (j + 1) * tq)
            l = alphas[i] * st[i][1] + pv[GROUP_W + h:GROUP_W + h + 1, cols]
            acc = alphas[i] * st[i][2] + pv[h * HEAD_DIM:(h + 1) * HEAD_DIM, cols]
            out.append((ms[i], l, acc))
        return tuple(out)

    st = _causal_blocks(qi, logits, step, tuple(_softmax_init_t(tq) for _ in chains))
    outs = []
    for h in range(GROUP_HEADS):
        (_, l1, a1), (_, l2, a2) = st[2 * h], st[2 * h + 1]
        o = a1 / l1 - lam * (a2 / l2)
        o = o * lax.rsqrt(jnp.mean(o * o, axis=0, keepdims=True) + LN_EPS)
        outs.append(o * g_ref[...] * out_scale)
    o_ref[...] = jnp.concatenate(outs, axis=0).T.astype(BF16)


def _differential(proj, proj_t, lam, bias_tiles, diff_g, lambda_init):
    bsz, t, _ = proj.shape
    nq, tq = proj_t.shape[1], proj_t.shape[3]
    grid_spec = pltpu.PrefetchScalarGridSpec(
        num_scalar_prefetch=1,
        grid=(bsz, nq),
        in_specs=[pl.BlockSpec((None, None, GROUP_W, tq), lambda b, i, lam: (b, i, ROW_QD // GROUP_W, 0)),
                  pl.BlockSpec((None, t, GROUP_W), lambda b, i, lam: (b, 0, COL_KD // GROUP_W)),
                  pl.BlockSpec((None, nq, GROUP_W, tq), lambda b, i, lam: (b, 0, ROW_VD // GROUP_W, 0)),
                  pl.BlockSpec(bias_tiles.shape, lambda b, i, lam: (0, 0, 0, 0)),
                  pl.BlockSpec((HEAD_DIM, 1), lambda b, i, lam: (0, 0))],
        out_specs=pl.BlockSpec((None, tq, GROUP_W), lambda b, i, lam: (b, i, 0)),
    )
    return pl.pallas_call(
        functools.partial(_diff_kernel, tq=tq, c_scale=DIFF_QK_DIM ** -0.5 * LOG2E, out_scale=1.0 - lambda_init),
        grid_spec=grid_spec,
        out_shape=jax.ShapeDtypeStruct((bsz, t, GROUP_W), BF16),
        compiler_params=_cparams("parallel", "arbitrary"),
        name="diff",
    )(lam.reshape(1).astype(F32), proj_t, proj, proj_t, bias_tiles, diff_g.reshape(HEAD_DIM, 1).astype(F32))


def _dsa_kernel(qx_ref, wx_ref, tail_ref, qt_ref, k_ref, vt_ref, bias_ref, o_ref, key_scr, cut_scr,
                *, tq, topk, c_scale, row_bits):
    qi = pl.program_id(1)
    nkb = qi + 1
    r = lax.broadcasted_iota(I32, (tq, tq), 0)
    c = lax.broadcasted_iota(I32, (tq, tq), 1)
    wx = wx_ref[...].astype(F32) * (IDX_HEADS ** -0.5 * IDX_DIM ** -0.5)
    zpad = jnp.zeros((LANES - IDX_DIM, tq), BF16)
    qx = [jnp.concatenate([qx_ref[h * IDX_DIM:(h + 1) * IDX_DIM, :], zpad], axis=0) for h in range(IDX_HEADS)]

    def score_block(kb, _):
        start = pl.multiple_of(kb * tq, tq)
        kt = tail_ref[pl.ds(start, tq), :]
        s = jnp.zeros((tq, tq), F32)
        for h in range(IDX_HEADS):
            s = s + wx[h:h + 1, :] * jnp.maximum(_dot(kt, qx[h]), 0.0)
        s = jnp.where(s == 0.0, 0.0, s)
        s = jnp.where(r + kb * tq <= c + qi * tq, s, -jnp.inf)
        bits = pltpu.bitcast(s, I32)
        key_scr[kb] = bits ^ ((bits >> 31) & 0x7FFFFFFF)
        return 0

    lax.fori_loop(0, nkb, score_block, 0)

    def count(pred):
        def body(kb, acc):
            hit = jnp.where(pred(key_scr[kb], kb), 1.0, 0.0)
            return acc + jnp.sum(hit.reshape(tq // SUBLANES, SUBLANES, tq), axis=0)
        acc = lax.fori_loop(0, nkb, body, jnp.zeros((SUBLANES, tq), F32))
        return jnp.sum(acc, axis=0, keepdims=True)

    def bit_step(i, s):
        thr, c_thr = s
        cand = thr + lax.shift_left(jnp.int32(1), 31 - i)
        cnt = count(lambda key, kb: key >= cand)
        ge = cnt >= topk
        return jnp.where(ge, cand, thr), jnp.where(ge, cnt, c_thr)

    n_keys = (nkb * tq).astype(F32)
    thr, c_thr = lax.fori_loop(0, 32, bit_step, (jnp.full((1, tq), INT_MIN, I32), jnp.full((1, tq), 1.0, F32) * n_keys))
    cut_scr[...] = jnp.full((SUBLANES, tq), 2 ** 30, I32)

    @pl.when(jnp.max(c_thr) > topk)
    def _():
        need = topk - count(lambda key, kb: key > thr)

        def row_step(i, lo):
            cand = lo + lax.shift_left(jnp.int32(1), row_bits - 1 - i)
            cnt = count(lambda key, kb: jnp.logical_and(key == thr, r + kb * tq < cand))
            return jnp.where(cnt < need, cand, lo)
        lo = lax.fori_loop(0, row_bits, row_step, jnp.zeros((1, tq), I32))
        cut_scr[...] = jnp.broadcast_to(lo, (SUBLANES, tq))

    cut = cut_scr[0:1, :]

    def select_block(kb, _):
        key = key_scr[kb]
        pos = r + kb * tq
        sel = jnp.logical_or(key > thr, jnp.logical_and(key == thr, pos <= cut))
        sel = jnp.logical_and(sel, pos <= c + qi * tq)
        key_scr[kb] = jnp.where(sel, 0, NEG_BIG_BITS)
        return 0

    lax.fori_loop(0, nkb, select_block, 0)

    qt = qt_ref[...]
    ones = _ones_rows(tq)
    chains = [(h * HEAD_DIM, (h + 1) * HEAD_DIM) for h in range(GROUP_HEADS)]

    def logits(kb):
        return tuple(_chain_logits(k_ref, pl.multiple_of(kb * tq, tq), tq, chains, qt))

    def step(zs, kb, st, which, masked):
        del masked
        mask = pltpu.bitcast(key_scr[kb], F32)
        ms, alphas, ps = [], [], []
        for h, z in enumerate(zs):
            z = z * c_scale + mask
            if which is not None:
                z = z + bias_ref[h, which]
            m_new, alpha, p = _softmax_weights(z, st[h][0])
            ms.append(m_new)
            alphas.append(alpha)
            ps.append(p)
        pv = _dot(_value_blockdiag(vt_ref[kb], ones), jnp.concatenate(ps, axis=0))
        return tuple((ms[h], alphas[h] * st[h][1] + pv[GROUP_W + h:GROUP_W + h + 1, :],
                      alphas[h] * st[h][2] + pv[h * HEAD_DIM:(h + 1) * HEAD_DIM, :]) for h in range(GROUP_HEADS))

    st = _causal_blocks(qi, logits, step, tuple(_softmax_init_t(tq) for _ in range(GROUP_HEADS)))
    o_ref[...] = jnp.concatenate([acc / l for _, l, acc in st], axis=0).T.astype(BF16)


def _dsa(proj, proj_t, bias_tiles):
    bsz, t, _ = proj.shape
    nq, tq = proj_t.shape[1], proj_t.shape[3]
    topk = min(INDEX_TOPK_MAX, t // 4)
    assert tq >= topk, "the threshold search needs at least topk keys in the first block"
    nqx = IDX_HEADS * IDX_DIM
    return pl.pallas_call(
        functools.partial(_dsa_kernel, tq=tq, topk=float(topk), c_scale=HEAD_DIM ** -0.5 * LOG2E,
                          row_bits=max(1, (t - 1).bit_length())),
        grid=(bsz, nq),
        in_specs=[pl.BlockSpec((None, None, nqx, tq), lambda b, i: (b, i, ROW_QX // nqx, 0)),
                  pl.BlockSpec((None, None, IDX_HEADS, tq), lambda b, i: (b, i, ROW_WX // IDX_HEADS, 0)),
                  pl.BlockSpec((None, t, LANES), lambda b, i: (b, 0, COL_TAIL // LANES)),
                  pl.BlockSpec((None, None, GROUP_W, tq), lambda b, i: (b, i, ROW_QC // GROUP_W, 0)),
                  pl.BlockSpec((None, t, GROUP_W), lambda b, i: (b, 0, COL_KC // GROUP_W)),
                  pl.BlockSpec((None, nq, GROUP_W, tq), lambda b, i: (b, 0, ROW_VC // GROUP_W, 0)),
                  pl.BlockSpec(bias_tiles.shape, lambda b, i: (0, 0, 0, 0))],
        out_specs=pl.BlockSpec((None, tq, GROUP_W), lambda b, i: (b, i, 0)),
        out_shape=jax.ShapeDtypeStruct((bsz, t, GROUP_W), BF16),
        scratch_shapes=[pltpu.VMEM((nq, tq, tq), I32), pltpu.VMEM((SUBLANES, tq), I32)],
        compiler_params=_cparams("parallel", "arbitrary"),
        name="dsa",
    )(proj_t, proj_t, proj, proj_t, proj, proj_t, bias_tiles)


def _dil_kernel(q_ref, kp_ref, kd_ref, vp_ref, vd_ref, bias_ref, o_ref, lse_ref, *, tq, scale):
    qi = pl.program_id(1)
    r = lax.broadcasted_iota(I32, (tq, tq), 0)
    c = lax.broadcasted_iota(I32, (tq, tq), 1)
    prev_ok = jnp.logical_and(r <= c, qi > 0)
    diag_ok = c <= r
    q = q_ref[...]
    zp_all = _dot_nt(q, _by_head(kp_ref[...])) * scale
    zd_all = _dot_nt(q, _by_head(kd_ref[...])) * scale
    pps, pds, dens, lses = [], [], [], []
    for h in range(GROUP_HEADS):
        cols = slice(h * tq, (h + 1) * tq)
        zp = jnp.where(prev_ok, zp_all[:, cols] + bias_ref[h, 1], NEG_BIG)
        zd = jnp.where(diag_ok, zd_all[:, cols] + bias_ref[h, 0], NEG_BIG)
        m = jnp.maximum(jnp.max(zp, axis=1, keepdims=True), jnp.max(zd, axis=1, keepdims=True))
        pp = jnp.exp(zp - m)
        pd = jnp.exp(zd - m)
        den = jnp.sum(pp, axis=1, keepdims=True) + jnp.sum(pd, axis=1, keepdims=True)
        pps.append(pp.astype(BF16))
        pds.append(pd.astype(BF16))
        dens.append(jnp.broadcast_to(den, (tq, HEAD_DIM)))
        lses.append(jnp.broadcast_to(m + jnp.log(den), (tq, HEAD_DIM)))
    values = jnp.concatenate([_by_head(vp_ref[...]), _by_head(vd_ref[...])], axis=0)
    o_ref[...] = _dot(jnp.concatenate(pps + pds, axis=1), values) / jnp.concatenate(dens, axis=-1)
    lse_ref[...] = jnp.concatenate(lses, axis=-1)


def _dilated_one(qp, kp, vp, bias_tiles):
    n, length, _ = qp.shape
    tq = bias_tiles.shape[-1]
    blk = lambda f: pl.BlockSpec((None, tq, GROUP_W), f)
    prev = lambda b, i: (b, jnp.maximum(i - 1, 0), 0)
    cur = lambda b, i: (b, i, 0)
    return pl.pallas_call(
        functools.partial(_dil_kernel, tq=tq, scale=HEAD_DIM ** -0.5),
        grid=(n, length // tq),
        in_specs=[blk(cur), blk(prev), blk(cur), blk(prev), blk(cur),
                  pl.BlockSpec(bias_tiles.shape, lambda b, i: (0, 0, 0, 0))],
        out_specs=[blk(cur), blk(cur)],
        out_shape=[jax.ShapeDtypeStruct((n, length, GROUP_W), F32)] * 2,
        compiler_params=_cparams("parallel", "arbitrary"),
        name="dil",
    )(qp, kp, kp, vp, vp, bias_tiles)


def _dilmix_kernel(o0, o1, o2, l0, l1, l2, out_ref):
    a0, a1, a2 = l0[...], l1[...], l2[...]
    m = jnp.maximum(jnp.maximum(a0, a1), a2)
    e0, e1, e2 = jnp.exp(a0 - m), jnp.exp(a1 - m), jnp.exp(a2 - m)
    out_ref[...] = ((e0 * o0[...] + e1 * o1[...] + e2 * o2[...]) / (e0 + e1 + e2)).astype(BF16)


def _dilated_mix(outs, lses):
    bsz, t, w = outs[0].shape
    tm = min(512, t)
    spec = pl.BlockSpec((None, tm, w), lambda b, i: (b, i, 0))
    return pl.pallas_call(
        _dilmix_kernel,
        grid=(bsz, t // tm),
        in_specs=[spec] * 6,
        out_specs=spec,
        out_shape=jax.ShapeDtypeStruct((bsz, t, w), BF16),
        compiler_params=_cparams("parallel", "parallel"),
        name="dilmix",
    )(*outs, *lses)


def _dilated(proj, bias_tiles_per_cfg):
    bsz, t, _ = proj.shape
    q, k, v = (proj[:, :, COL_B + j * GROUP_W:COL_B + (j + 1) * GROUP_W] for j in range(3))
    outs, lses = [], []
    for (_, dil), tiles in zip(DILATED_CONFIGS, bias_tiles_per_cfg):
        def perm(a, dil=dil):
            return a.reshape(bsz, t // dil, dil, GROUP_W).transpose(0, 2, 1, 3).reshape(bsz * dil, t // dil, GROUP_W)

        def unperm(a, dil=dil):
            return a.reshape(bsz, dil, t // dil, GROUP_W).transpose(0, 2, 1, 3).reshape(bsz, t, GROUP_W)

        o, lse = _dilated_one(perm(q), perm(k), perm(v), tiles)
        outs.append(unperm(o))
        lses.append(unperm(lse))
    return _dilated_mix(outs, lses)


def _post_kernel(oa_ref, ob_ref, oc_ref, od_ref, wo_ref, x_ref, mod_ref, ln_ref, wr_ref, br_ref,
                 x1_ref, h2_ref, idx_ref, gate_ref, *, alpha):
    y = jnp.zeros(x_ref.shape, F32)
    for g, o_ref in enumerate((oa_ref, ob_ref, oc_ref, od_ref)):
        y = y + _dot(o_ref[...], wo_ref[g * GROUP_W:(g + 1) * GROUP_W, :])
    u = alpha * x_ref[...] + (1.0 + mod_ref[2:3, :]) * y
    x1 = _ln(u) * ln_ref[0:1, :] + ln_ref[1:2, :]
    x1_ref[...] = x1
    h2 = _ln(x1) * (1.0 + mod_ref[4:5, :]) + mod_ref[3:4, :]
    _store_token_tiles(h2_ref, 0, h2)
    logits = lax.dot_general(wr_ref[...], h2, (((1,), (1,)), ((), ())), precision=lax.Precision.HIGHEST,
                             preferred_element_type=F32) + br_ref[...]
    n_exp, tm = logits.shape
    eid = lax.broadcasted_iota(I32, (n_exp, tm), 0)
    vals, ids = [], []
    for _ in range(TOP_K):
        m = jnp.max(logits, axis=0, keepdims=True)
        first = jnp.min(jnp.where(logits == m, eid, n_exp), axis=0, keepdims=True)
        vals.append(m)
        ids.append(first)
        logits = jnp.where(eid == first, -jnp.inf, logits)
    ex = [jnp.exp(v - vals[0]) for v in vals]
    den = ex[0] + ex[1] + ex[2] + ex[3]
    zero_f = jnp.zeros((8 - TOP_K, tm), F32)
    gate_ref[...] = jnp.concatenate([e / den for e in ex] + [zero_f], axis=0)
    idx_ref[...] = jnp.concatenate(ids + [zero_f.astype(I32)], axis=0)


def _post_mixer(o_groups, w_out, x, mod, ln_rows, w_router_t, b_router, alpha):
    bsz, t, d = x.shape
    tm = min(256, t)
    n_exp = w_router_t.shape[0]
    og = pl.BlockSpec((None, tm, GROUP_W), lambda b, i: (b, i, 0))
    row = pl.BlockSpec((None, tm, d), lambda b, i: (b, i, 0))
    small = pl.BlockSpec((None, 8, tm), lambda b, i: (b, 0, i))
    assert d == SUBLANES * LANES, "token-tile layout needs one (8,128) tile per token"
    nt = t // tm
    tiles = pl.BlockSpec((tm * SUBLANES, LANES), lambda b, i: (b * nt + i, 0))
    return pl.pallas_call(
        functools.partial(_post_kernel, alpha=alpha),
        grid=(bsz, nt),
        in_specs=[og, og, og, og,
                  pl.BlockSpec(w_out.shape, lambda b, i: (0, 0)),
                  row,
                  pl.BlockSpec((None, 8, d), lambda b, i: (b, 0, 0)),
                  pl.BlockSpec((4, d), lambda b, i: (0, 0)),
                  pl.BlockSpec((n_exp, d), lambda b, i: (0, 0)),
                  pl.BlockSpec((n_exp, 1), lambda b, i: (0, 0))],
        out_specs=[row, tiles, small, small],
        out_shape=[jax.ShapeDtypeStruct((bsz, t, d), F32), jax.ShapeDtypeStruct((bsz * t * SUBLANES, LANES), F32),
                   jax.ShapeDtypeStruct((bsz, 8, t), I32), jax.ShapeDtypeStruct((bsz, 8, t), F32)],
        compiler_params=_cparams("parallel", "parallel"),
        name="post",
    )(*o_groups, w_out, x, mod, ln_rows, w_router_t, b_router.reshape(n_exp, 1))


def _deint_kernel(w_ref, p_ref, o_ref):
    o_ref[...] = _dot(w_ref[...].astype(BF16), p_ref[...]).astype(BF16)


def _deinterleave_w1(w1_all, layer):
    _, n_exp, d, two_f = w1_all.shape
    grp = 2 * FF_GROUP
    j = np.arange(grp)
    src = np.where(j < FF_GROUP, 2 * j, 2 * (j - FF_GROUP) + 1)
    perm = jnp.asarray(np.arange(grp)[:, None] == src[None, :], BF16)
    return pl.pallas_call(
        _deint_kernel,
        grid=(n_exp, two_f // grp),
        in_specs=[pl.BlockSpec((None, None, d, grp), lambda e, g: (layer, e, 0, g)),
                  pl.BlockSpec((grp, grp), lambda e, g: (0, 0))],
        out_specs=pl.BlockSpec((None, d, grp), lambda e, g: (e, 0, g)),
        out_shape=jax.ShapeDtypeStruct((n_exp, d, two_f), BF16),
        compiler_params=_cparams("parallel", "parallel"),
        name="deint",
    )(w1_all, perm)


def _moe_kernel(be_ref, nval_ref, nblk_ref, cur_ref, nxt_ref, h_hbm, w1_ref, b1_ref, w2_ref, b2_ref,
                out_hbm, xbuf, ybuf, gsem, ssem, *, bm):
    i = pl.program_id(0)
    nblk = nblk_ref[0]
    slot = i % 2

    def tile_rows(tok):
        return pl.ds(pl.multiple_of(tok * SUBLANES, SUBLANES), SUBLANES)

    def gather_row_copy(tok, r, s):
        return pltpu.make_async_copy(h_hbm.at[tile_rows(tok)], xbuf.at[tile_rows(s * bm + r)], gsem.at[s])

    def scatter_row_copy(row, r, s):
        return pltpu.make_async_copy(ybuf.at[tile_rows(s * bm + r)], out_hbm.at[tile_rows(row)], ssem.at[s])

    def block_rows(s):
        return pl.ds(pl.multiple_of(s * bm * SUBLANES, bm * SUBLANES), bm * SUBLANES)

    def start_gather(idx_ref, s):
        for r in range(bm):
            gather_row_copy(idx_ref[0, r], r, s).start()

    def start_gather_rolled(idx_ref, s):
        def body(r, _):
            gather_row_copy(idx_ref[0, r], r, s).start()
            return 0
        lax.fori_loop(0, bm, body, 0)

    def wait_gather(s):
        pltpu.make_async_copy(h_hbm.at[pl.ds(0, bm * SUBLANES)], xbuf.at[block_rows(s)], gsem.at[s]).wait()

    def start_scatter(s, n):
        @pl.when(n == bm)
        def _():
            for r in range(bm):
                scatter_row_copy(cur_ref[0, bm + r], r, s).start()

        @pl.when(n < bm)
        def _():
            def body(r, _):
                scatter_row_copy(cur_ref[0, bm + r], r, s).start()
                return 0
            lax.fori_loop(0, n, body, 0)

    def wait_scatter(s, n):
        @pl.when(n == bm)
        def _():
            pltpu.make_async_copy(ybuf.at[block_rows(s)], out_hbm.at[pl.ds(0, bm * SUBLANES)], ssem.at[s]).wait()

        @pl.when(n < bm)
        def _():
            def body(r, _):
                scatter_row_copy(0, r, s).wait()
                return 0
            lax.fori_loop(0, n, body, 0)

    @pl.when(jnp.logical_and(i == 0, nblk > 0))
    def _():
        start_gather_rolled(cur_ref, 0)

    @pl.when(i + 1 < nblk)
    def _():
        start_gather(nxt_ref, 1 - slot)

    @pl.when(i < nblk)
    def _():
        wait_gather(slot)

        @pl.when(i >= 2)
        def _():
            wait_scatter(slot, nval_ref[jnp.maximum(i - 2, 0)])

        x = _load_token_tiles(xbuf, slot * bm, bm).astype(BF16)
        y = jnp.zeros((bm, w2_ref.shape[1]), F32) + b2_ref[...]
        for g in range(w2_ref.shape[0] // FF_GROUP):
            cols = slice(2 * g * FF_GROUP, 2 * (g + 1) * FF_GROUP)
            hh = _dot(x, w1_ref[:, cols]) + b1_ref[:, cols]
            glu = jnp.minimum(hh[:, :FF_GROUP], SWIGLU_LIMIT)
            lin = jnp.clip(hh[:, FF_GROUP:], -SWIGLU_LIMIT, SWIGLU_LIMIT)
            act = glu * jax.nn.sigmoid(SWIGLU_ALPHA * glu) * (lin + 1.0)
            y = y + _dot(act.astype(BF16), w2_ref[g * FF_GROUP:(g + 1) * FF_GROUP, :])
        _store_token_tiles(ybuf, slot * bm, y)
        start_scatter(slot, nval_ref[i])

    last = pl.num_programs(0) - 1

    @pl.when(jnp.logical_and(i == last, nblk >= 2))
    def _():
        wait_scatter(nblk % 2, nval_ref[jnp.maximum(nblk - 2, 0)])

    @pl.when(jnp.logical_and(i == last, nblk >= 1))
    def _():
        wait_scatter((nblk + 1) % 2, nval_ref[jnp.maximum(nblk - 1, 0)])


def _moe_experts(h2, blk_expert, blk_valid, n_used, slot_idx, w1p, b1p, w2_all, b2, layer):
    n_tok = h2.shape[0] // SUBLANES
    d = SUBLANES * LANES
    n_blocks, _, two_bm = slot_idx.shape
    bm = two_bm // 2
    n_exp, _, two_f = w1p.shape
    f = two_f // 2
    idx_spec = lambda f_: pl.BlockSpec((None, 1, two_bm), f_, memory_space=pltpu.SMEM)
    wspec = lambda shp: pl.BlockSpec((None,) + shp, lambda i, be, nv, nb: (be[i], 0, 0))
    grid_spec = pltpu.PrefetchScalarGridSpec(
        num_scalar_prefetch=3,
        grid=(n_blocks,),
        in_specs=[idx_spec(lambda i, be, nv, nb: (i, 0, 0)),
                  idx_spec(lambda i, be, nv, nb: (jnp.minimum(i + 1, n_blocks - 1), 0, 0)),
                  pl.BlockSpec(memory_space=pl.ANY),
                  wspec((d, two_f)), wspec((1, two_f)),
                  pl.BlockSpec((None, None, f, d), lambda i, be, nv, nb: (layer, be[i], 0, 0)),
                  wspec((1, d))],
        out_specs=pl.BlockSpec(memory_space=pl.ANY),
        scratch_shapes=[pltpu.VMEM((2 * bm * SUBLANES, LANES), F32), pltpu.VMEM((2 * bm * SUBLANES, LANES), F32),
                        pltpu.SemaphoreType.DMA((2,)), pltpu.SemaphoreType.DMA((2,))],
    )
    return pl.pallas_call(
        functools.partial(_moe_kernel, bm=bm),
        grid_spec=grid_spec,
        out_shape=jax.ShapeDtypeStruct((n_tok * TOP_K * SUBLANES, LANES), F32),
        compiler_params=_cparams("arbitrary"),
        name="moe",
    )(blk_expert, blk_valid, n_used, slot_idx, slot_idx, h2, w1p, b1p, w2_all, b2)


def _moe_dispatch(top_idx, bm):
    n_tok = top_idx.shape[0]
    m = n_tok * TOP_K
    assert m % bm == 0
    e_flat = top_idx.reshape(-1)
    experts = jnp.arange(N_EXPERTS, dtype=I32)
    counts = jnp.sum((e_flat[:, None] == experts[None, :]).astype(I32), axis=0)
    padded = (counts + bm - 1) // bm * bm
    pend = jnp.cumsum(padded)
    n_blocks = m // bm + N_EXPERTS
    pad_ok = jnp.arange(bm, dtype=I32)[None, :] < (padded - counts)[:, None]
    pad_key = jnp.where(pad_ok, 2 * experts[:, None] + 1, 2 * N_EXPERTS).reshape(-1)
    keys = jnp.concatenate([2 * e_flat, pad_key])
    vals = jnp.concatenate([jnp.arange(m, dtype=I32), jnp.full((N_EXPERTS * bm,), -1, I32)])
    _, asg = lax.sort((keys, vals), num_keys=1, is_stable=True)
    valid = asg >= 0
    tok = jnp.maximum(asg, 0) // TOP_K
    choice = jnp.maximum(asg, 0) % TOP_K
    slot_idx = jnp.concatenate([tok.reshape(n_blocks, bm), (choice * n_tok + tok).reshape(n_blocks, bm)], axis=1)
    blk_valid = jnp.sum(valid.reshape(n_blocks, bm).astype(I32), axis=1)
    blk_start = jnp.arange(n_blocks, dtype=I32) * bm
    blk_expert = jnp.minimum(jnp.sum((blk_start[:, None] >= pend[None, :]).astype(I32), axis=1), N_EXPERTS - 1)
    n_used = (pend[-1] // bm).astype(I32).reshape(1)
    return blk_expert, blk_valid, n_used, slot_idx.reshape(n_blocks, 1, 2 * bm)


def _comb_kernel(y4_ref, gate_ref, x_ref, mod_ref, ln_ref, o_ref, *, alpha):
    y = jnp.zeros(x_ref.shape, F32)
    for k in range(TOP_K):
        y = y + gate_ref[:, k:k + 1] * _load_token_tiles(y4_ref.at[k], 0, x_ref.shape[0])
    u = alpha * x_ref[...] + (1.0 + mod_ref[5:6, :]) * y
    o_ref[...] = _ln(u) * ln_ref[2:3, :] + ln_ref[3:4, :]


def _combine(y4, gate, x1, mod, ln_rows, alpha):
    bsz, t, d = x1.shape
    tm = min(256, t)
    nt = t // tm
    return pl.pallas_call(
        functools.partial(_comb_kernel, alpha=alpha),
        grid=(bsz, nt),
        in_specs=[pl.BlockSpec((TOP_K, tm * SUBLANES, LANES), lambda b, i: (0, b * nt + i, 0)),
                  pl.BlockSpec((None, tm, 8), lambda b, i: (b, i, 0)),
                  pl.BlockSpec((None, tm, d), lambda b, i: (b, i, 0)),
                  pl.BlockSpec((None, 8, d), lambda b, i: (b, 0, 0)),
                  pl.BlockSpec((4, d), lambda b, i: (0, 0))],
        out_specs=pl.BlockSpec((None, tm, d), lambda b, i: (b, i, 0)),
        out_shape=jax.ShapeDtypeStruct((bsz, t, d), F32),
        compiler_params=_cparams("parallel", "parallel"),
        name="comb",
    )(y4, gate, x1, mod, ln_rows)


def _rel_bucket(dist):
    n = jnp.maximum(dist, 0)
    max_exact = N_BUCKETS // 2
    nf = jnp.maximum(n, 1).astype(F32)
    large = max_exact + (jnp.log(nf / max_exact) / math.log(MAX_DISTANCE / max_exact)
                         * (N_BUCKETS - max_exact)).astype(I32)
    large = jnp.minimum(large, N_BUCKETS - 1)
    return jnp.where(n < max_exact, n, large)


def _bias_tiles(bias_tab, tq, dil, key_major):
    period = 2 * tq
    k = np.arange(period)
    d = np.where(k < tq, -k, period - k)
    dist = np.stack([np.maximum(d, 0), d + tq]) * dil
    line = bias_tab.astype(F32)[_rel_bucket(jnp.asarray(dist, I32))]
    line = jnp.moveaxis(line, -1, 0)
    flat = jnp.tile(line, (1, 1, tq))[..., :tq * (period - 1)]
    tiles = flat.reshape(line.shape[0], 2, tq, period - 1)[..., :tq]
    if key_major:
        tiles = (tiles - bias_tab.astype(F32)[N_BUCKETS - 1][:, None, None, None]) * LOG2E
        tiles = jnp.swapaxes(tiles, -1, -2)
    return tiles


def _split_w_in(w_in):
    d = w_in.shape[0]
    g = GROUP_W
    segs = {}
    o = 0
    for name, width in (("qa", g), ("ka", g), ("va", g), ("qb", g), ("kb", g), ("vb", g),
                        ("qc", g), ("kc", g), ("vc", g), ("qx", IDX_HEADS * IDX_DIM), ("kx", IDX_DIM),
                        ("wx", IDX_HEADS), ("qd", g), ("kd", g), ("vd", g)):
        segs[name] = w_in[:, o:o + width]
        o += width
    pad = jnp.zeros((d, LANES - IDX_DIM - IDX_HEADS), w_in.dtype)
    w = jnp.concatenate([segs[n] for n in ("qa", "ka", "va", "qb", "kb", "vb", "kc", "kd", "kx", "wx")] + [pad],
                        axis=1)
    wt = jnp.concatenate([segs[n] for n in ("qx", "qc", "vc", "qd", "vd", "wx")], axis=1).T
    return w.astype(BF16), wt.astype(BF16)


def _layer(x, c, layer, depth, p, tiles):
    bsz, t, d = x.shape
    alpha = (2 * depth) ** 0.25
    mod = _ada_mod(c, p["w_ada_all"], p["b_ada_all"], layer).reshape(bsz, 6, d)
    mod = jnp.concatenate([mod, jnp.zeros((bsz, 2, d), F32)], axis=1)
    ln_rows = jnp.concatenate([p["ln_g"][0:1], p["ln_b"][0:1], p["ln_g"][1:2], p["ln_b"][1:2]], axis=0)

    proj, proj_t = _ln_mod_proj(x, mod, *_split_w_in(p["w_in"]))

    o_a = _stick_breaking(proj)
    o_b = _dilated(proj, tiles["b"])
    o_c = _dsa(proj, proj_t, tiles["c"])
    lamp = p["diff_lam"].astype(F32)
    lambda_init = 0.8 - 0.6 * math.exp(-0.3 * layer)
    lam = jnp.exp(jnp.sum(lamp[0] * lamp[1])) - jnp.exp(jnp.sum(lamp[2] * lamp[3])) + lambda_init
    o_d = _differential(proj, proj_t, lam, tiles["d"], p["diff_g"], lambda_init)

    x1, h2, top_idx, gate = _post_mixer((o_a, o_b, o_c, o_d), p["w_out"].astype(BF16), x, mod, ln_rows,
                                        p["w_router"].T, p["b_router"], alpha)

    top_idx = top_idx[:, :TOP_K, :].transpose(0, 2, 1).reshape(bsz * t, TOP_K)
    blk_expert, blk_valid, n_used, slot_idx = _moe_dispatch(top_idx, MOE_BLOCK)
    n_grp = p["b1"].shape[-1] // (2 * FF_GROUP)
    b1p = p["b1"].reshape(N_EXPERTS, n_grp, FF_GROUP, 2).transpose(0, 1, 3, 2).reshape(N_EXPERTS, 1, -1)
    y_rows = _moe_experts(h2, blk_expert, blk_valid, n_used, slot_idx,
                          _deinterleave_w1(p["w1_all"], layer), b1p, p["w2_all"], p["b2"][:, None, :], layer)
    return _combine(y_rows.reshape(TOP_K, bsz * t * SUBLANES, LANES), gate.transpose(0, 2, 1), x1, mod, ln_rows,
                    alpha)


def kernel(x, c, w_ada, b_ada, w_in, w_out, diff_lam, diff_g, ln_g, ln_b, w_router, b_router, w1, b1, w2, b2,
           rel_bias):
    depth = w_in.shape[0]
    t = x.shape[1]
    tq = min(ATT_BLOCK, t)
    tiles = dict(
        b=[_bias_tiles(rel_bias[:, :GROUP_HEADS], min(128, t // dil), dil, False) for _, dil in DILATED_CONFIGS],
        c=_bias_tiles(rel_bias[:, GROUP_HEADS:2 * GROUP_HEADS], tq, 1, True),
        d=_bias_tiles(rel_bias[:, 2 * GROUP_HEADS:], tq, 1, True))
    w2_all = w2.astype(BF16)
    for layer in range(depth):
        p = dict(w_ada_all=w_ada, b_ada_all=b_ada, w_in=w_in[layer], w_out=w_out[layer],
                 diff_lam=diff_lam[layer], diff_g=diff_g[layer], ln_g=ln_g[layer], ln_b=ln_b[layer],
                 w_router=w_router[layer], b_router=b_router[layer], w1_all=w1, b1=b1[layer],
                 w2_all=w2_all, b2=b2[layer])
        x = _layer(x, c, layer, depth, p, tiles)
    return x
```

```python
import functools
import math

import numpy as np
import jax
import jax.numpy as jnp
from jax import lax
from jax.experimental import pallas as pl
from jax.experimental.pallas import tpu as pltpu

F32 = jnp.float32
BF16 = jnp.bfloat16
I32 = jnp.int32

HEAD_DIM = 64
GROUP_HEADS = 4
GROUP_W = GROUP_HEADS * HEAD_DIM
DIFF_QK_DIM = HEAD_DIM // 2
DILATED_CONFIGS = ((128, 1), (512, 4), (2048, 16))
IDX_HEADS = 16
IDX_DIM = 64
INDEX_TOPK_MAX = 256
N_EXPERTS = 32
TOP_K = 4
SWIGLU_ALPHA = 1.702
SWIGLU_LIMIT = 7.0
N_BUCKETS = 32
MAX_DISTANCE = 128
LN_EPS = 1e-5
MOE_BLOCK = 512

LANES = 128
SUBLANES = 8
VMEM_LIMIT_BYTES = 56 * 1024 * 1024
NEG_BIG = -1e30
SB_SKIP_LOG = -100.0
INT_MIN = -2 ** 31
NEG_BIG_BITS = int(np.float32(NEG_BIG).view(np.int32))
LOG2E = math.log2(math.e)
ATT_BLOCK = 256
COUNT_ROWS = 4 * SUBLANES
FF_GROUP = 256

COL_A = 0
COL_B = COL_A + 3 * GROUP_W
COL_KC = COL_B + 3 * GROUP_W
COL_KD = COL_KC + GROUP_W
COL_TAIL = COL_KD + GROUP_W
PROJ_COLS = COL_TAIL + LANES
ROW_QX = 0
ROW_QC = ROW_QX + IDX_HEADS * IDX_DIM
ROW_VC = ROW_QC + GROUP_W
ROW_QD = ROW_VC + GROUP_W
ROW_VD = ROW_QD + GROUP_W
ROW_WX = ROW_VD + GROUP_W
PROJ_ROWS = ROW_WX + IDX_HEADS


def _cparams(*sem):
    return pltpu.CompilerParams(dimension_semantics=sem, vmem_limit_bytes=VMEM_LIMIT_BYTES)


def _ln(x):
    mu = jnp.mean(x, axis=-1, keepdims=True)
    xc = x - mu
    return xc * lax.rsqrt(jnp.mean(xc * xc, axis=-1, keepdims=True) + LN_EPS)


def _dot_nt(a, b):
    return lax.dot_general(a, b, (((1,), (1,)), ((), ())), preferred_element_type=F32)


def _dot(a, b):
    return jnp.dot(a, b, preferred_element_type=F32)


def _load_token_tiles(ref, first_tok, n):
    base = first_tok * SUBLANES
    return jnp.concatenate([ref[pl.ds(base + c, n, stride=SUBLANES), :] for c in range(SUBLANES)], axis=1)


def _store_token_tiles(ref, first_tok, val):
    n = val.shape[0]
    base = first_tok * SUBLANES
    for c in range(SUBLANES):
        ref[pl.ds(base + c, n, stride=SUBLANES), :] = val[:, c * LANES:(c + 1) * LANES]


def _ada_kernel(c_ref, w_ref, b_ref, o_ref):
    o_ref[...] = jnp.dot(c_ref[...], w_ref[...], precision=lax.Precision.HIGHEST,
                         preferred_element_type=F32) + b_ref[...]


def _ada_mod(c, w_all, b_all, layer):
    bsz, d = c.shape
    depth, _, n = w_all.shape
    return pl.pallas_call(
        _ada_kernel,
        grid=(n // d,),
        in_specs=[pl.BlockSpec((bsz, d), lambda j: (0, 0)),
                  pl.BlockSpec((None, d, d), lambda j: (layer, 0, j)),
                  pl.BlockSpec((None, 1, d), lambda j: (layer, 0, j))],
        out_specs=pl.BlockSpec((bsz, d), lambda j: (0, j)),
        out_shape=jax.ShapeDtypeStruct((bsz, n), F32),
        compiler_params=_cparams("arbitrary"),
        name="ada",
    )(c, w_all, b_all.reshape(depth, 1, n))


def _proj_kernel(x_ref, mod_ref, w_ref, wt_ref, o_ref, ot_ref, *, chunk):
    h = (_ln(x_ref[...]) * (1.0 + mod_ref[1:2, :]) + mod_ref[0:1, :]).astype(BF16)
    ncol = o_ref.shape[-1]
    for c0 in range(0, ncol, chunk):
        c1 = min(c0 + chunk, ncol)
        o_ref[:, c0:c1] = _dot(h, w_ref[:, c0:c1]).astype(BF16)
    n_sub, nrow, tq = ot_ref.shape
    for r0 in range(0, nrow, chunk):
        r1 = min(r0 + chunk, nrow)
        res = _dot_nt(wt_ref[r0:r1, :], h).astype(BF16)
        for j in range(n_sub):
            ot_ref[j, r0:r1, :] = res[:, j * tq:(j + 1) * tq]


def _ln_mod_proj(x, mod, w, wt):
    bsz, t, d = x.shape
    ncol, nrow = w.shape[1], wt.shape[0]
    tq = min(ATT_BLOCK, t)
    tm = min(2 * tq, t)
    return pl.pallas_call(
        functools.partial(_proj_kernel, chunk=2 * LANES),
        grid=(bsz, t // tm),
        in_specs=[pl.BlockSpec((None, tm, d), lambda b, i: (b, i, 0)),
                  pl.BlockSpec((None, 8, d), lambda b, i: (b, 0, 0)),
                  pl.BlockSpec((d, ncol), lambda b, i: (0, 0)),
                  pl.BlockSpec((nrow, d), lambda b, i: (0, 0))],
        out_specs=[pl.BlockSpec((None, tm, ncol), lambda b, i: (b, i, 0)),
                   pl.BlockSpec((None, tm // tq, nrow, tq), lambda b, i: (b, i, 0, 0))],
        out_shape=[jax.ShapeDtypeStruct((bsz, t, ncol), BF16),
                   jax.ShapeDtypeStruct((bsz, t // tq, nrow, tq), BF16)],
        compiler_params=_cparams("parallel", "parallel"),
        name="proj",
    )(x, mod, w, wt)


def _by_head(x):
    lane = lax.broadcasted_iota(I32, x.shape, 1)
    zero = jnp.zeros_like(x)
    return jnp.concatenate([jnp.where(jnp.logical_and(lane >= h * HEAD_DIM, lane < (h + 1) * HEAD_DIM), x, zero)
                            for h in range(GROUP_HEADS)], axis=0)


def _sb_kernel(q_ref, k_ref, v_ref, o_ref, *, tq, scale):
    qi = pl.program_id(1)
    r = lax.broadcasted_iota(I32, (tq, tq), 0)
    c = lax.broadcasted_iota(I32, (tq, tq), 1)
    strict_lower = c < r
    upper = jnp.where(r > c, 1.0, 0.0).astype(BF16)

    q = q_ref[...]
    valid = jnp.concatenate([strict_lower] * GROUP_HEADS, axis=1)

    def step(kb, carry, acc, masked):
        start = pl.multiple_of(kb * tq, tq)
        z = _dot_nt(q, _by_head(k_ref[pl.ds(start, tq), :])) * scale
        log_sig = jnp.minimum(z, 0.0) - jnp.log(1.0 + jnp.exp(-jnp.abs(z)))
        log_fail = log_sig - z
        if masked:
            log_fail = jnp.where(valid, log_fail, 0.0)
        lf = jnp.concatenate([log_fail[:, h * tq:(h + 1) * tq] for h in range(GROUP_HEADS)], axis=0)
        lf_hi = lf.astype(BF16)
        lf_lo = (lf - lf_hi.astype(F32)).astype(BF16)
        after = _dot(lf_hi, upper) + _dot(lf_lo, upper)
        after = jnp.concatenate([after[h * tq:(h + 1) * tq] + carry[h] for h in range(GROUP_HEADS)], axis=1)
        w = jnp.exp(log_sig + after)
        if masked:
            w = jnp.where(valid, w, 0.0)
        acc = acc + _dot(w.astype(BF16), _by_head(v_ref[pl.ds(start, tq), :]))
        carry = tuple(carry[h] + jnp.sum(log_fail[:, h * tq:(h + 1) * tq], axis=1, keepdims=True)
                      for h in range(GROUP_HEADS))
        return carry, acc

    def live(carry):
        top = carry[0]
        for h in range(1, GROUP_HEADS):
            top = jnp.maximum(top, carry[h])
        return (jnp.max(top) > SB_SKIP_LOG).astype(I32)

    carry, acc = step(qi, tuple(jnp.zeros((tq, 1), F32) for _ in range(GROUP_HEADS)),
                      jnp.zeros((tq, GROUP_W), F32), True)

    def body(s):
        kb, carry, acc, _ = s
        carry, acc = step(kb, carry, acc, False)
        return kb - 1, carry, acc, live(carry)

    _, _, acc, _ = lax.while_loop(lambda s: jnp.logical_and(s[0] >= 0, s[3] > 0), body,
                                  (qi - 1, carry, acc, live(carry)))
    o_ref[...] = acc.astype(BF16)


def _stick_breaking(proj):
    bsz, t, _ = proj.shape
    tq = min(ATT_BLOCK, t)
    cb = COL_A // GROUP_W
    return pl.pallas_call(
        functools.partial(_sb_kernel, tq=tq, scale=HEAD_DIM ** -0.5),
        grid=(bsz, t // tq),
        in_specs=[pl.BlockSpec((None, tq, GROUP_W), lambda b, i: (b, i, cb)),
                  pl.BlockSpec((None, t, GROUP_W), lambda b, i: (b, 0, cb + 1)),
                  pl.BlockSpec((None, t, GROUP_W), lambda b, i: (b, 0, cb + 2))],
        out_specs=pl.BlockSpec((None, tq, GROUP_W), lambda b, i: (b, i, 0)),
        out_shape=jax.ShapeDtypeStruct((bsz, t, GROUP_W), BF16),
        compiler_params=_cparams("parallel", "arbitrary"),
        name="sb",
    )(proj, proj, proj)


ONES_ROWS = 16


def _softmax_init_t(tq):
    return (jnp.full((1, tq), NEG_BIG, F32), jnp.zeros((1, tq), F32), jnp.zeros((HEAD_DIM, tq), F32))


def _chain_logits(k_ref, start, tk, chains, qt):
    outs = [None] * len(chains)
    for half in range(GROUP_W // LANES):
        idx = [i for i, (lo, hi) in enumerate(chains) if lo // LANES == half]
        assert all((chains[i][1] - 1) // LANES == half for i in idx)
        khalf = k_ref[pl.ds(start, tk), half * LANES:(half + 1) * LANES]
        lane = lax.broadcasted_iota(I32, khalf.shape, 1) + half * LANES
        zero = jnp.zeros_like(khalf)
        lhs = jnp.concatenate([jnp.where(jnp.logical_and(lane >= chains[i][0], lane < chains[i][1]), khalf, zero)
                               for i in idx], axis=0)
        z = _dot(lhs, qt[half * LANES:(half + 1) * LANES, :])
        for j, i in enumerate(idx):
            outs[i] = z[j * tk:(j + 1) * tk]
    return outs


def _ones_rows(tk):
    r = lax.broadcasted_iota(I32, (ONES_ROWS, GROUP_HEADS * tk), 0)
    c = lax.broadcasted_iota(I32, (ONES_ROWS, GROUP_HEADS * tk), 1)
    lo = r * tk
    return jnp.where(jnp.logical_and(c >= lo, c < lo + tk), 1.0, 0.0).astype(BF16)


def _value_blockdiag(vblk, ones):
    tk = vblk.shape[1]
    zero = jnp.zeros((HEAD_DIM, tk), BF16)
    rows = [jnp.concatenate([vblk[h * HEAD_DIM:(h + 1) * HEAD_DIM, :] if j == h else zero
                             for j in range(GROUP_HEADS)], axis=1) for h in range(GROUP_HEADS)]
    return jnp.concatenate(rows + [ones], axis=0)


def _softmax_weights(z, m):
    m_new = jnp.maximum(m, jnp.max(z, axis=0, keepdims=True))
    return m_new, jnp.exp2(m - m_new), jnp.exp2(z - m_new).astype(BF16)


def _causal_blocks(qi, logits, step, st):
    st = lax.fori_loop(0, jnp.maximum(qi - 1, 0), lambda kb, s: step(logits(kb), kb, s, None, False), st)
    st = lax.cond(qi >= 1, lambda s: step(logits(qi - 1), qi - 1, s, 1, False), lambda s: s, st)
    return step(logits(qi), qi, st, 0, True)


def _diff_kernel(lam_ref, qt_ref, k_ref, vt_ref, bias_ref, g_ref, o_ref, *, tq, c_scale, out_scale):
    qi = pl.program_id(1)
    r = lax.broadcasted_iota(I32, (tq, tq), 0)
    c = lax.broadcasted_iota(I32, (tq, tq), 1)
    causal = r <= c
    lam = lam_ref[0]
    qt = qt_ref[...]
    ones = _ones_rows(tq)
    chains = [(h * HEAD_DIM + j * DIFF_QK_DIM, h * HEAD_DIM + (j + 1) * DIFF_QK_DIM)
              for h in range(GROUP_HEADS) for j in range(2)]

    def logits(kb):
        return tuple(_chain_logits(k_ref, pl.multiple_of(kb * tq, tq), tq, chains, qt))

    def step(zs, kb, st, which, masked):
        ms, alphas, ps = [], [], []
        for i, z in enumerate(zs):
            z = z * c_scale
            if which is not None:
                z = z + bias_ref[i // 2, which]
            if masked:
                z = jnp.where(causal, z, NEG_BIG)
            m_new, alpha, p = _softmax_weights(z, st[i][0])
            ms.append(m_new)
            alphas.append(alpha)
            ps.append(p)
        pmat = jnp.concatenate([jnp.concatenate([ps[2 * h], ps[2 * h + 1]], axis=1) for h in range(GROUP_HEADS)],
                               axis=0)
        pv = _dot(_value_blockdiag(vt_ref[kb], ones), pmat)
        out = []
        for i in range(len(chains)):
            h, j = divmod(i, 2)
            cols = slice(j * tq, (j + 1) * tq)
            l = alphas[i] * st[i][1] + pv[GROUP_W + h:GROUP_W + h + 1, cols]
            acc = alphas[i] * st[i][2] + pv[h * HEAD_DIM:(h + 1) * HEAD_DIM, cols]
            out.append((ms[i], l, acc))
        return tuple(out)

    st = _causal_blocks(qi, logits, step, tuple(_softmax_init_t(tq) for _ in chains))
    outs = []
    for h in range(GROUP_HEADS):
        (_, l1, a1), (_, l2, a2) = st[2 * h], st[2 * h + 1]
        o = a1 / l1 - lam * (a2 / l2)
        o = o * lax.rsqrt(jnp.mean(o * o, axis=0, keepdims=True) + LN_EPS)
        outs.append(o * g_ref[...] * out_scale)
    o_ref[...] = jnp.concatenate(outs, axis=0).T.astype(BF16)


def _differential(proj, proj_t, lam, bias_tiles, diff_g, lambda_init):
    bsz, t, _ = proj.shape
    nq, tq = proj_t.shape[1], proj_t.shape[3]
    grid_spec = pltpu.PrefetchScalarGridSpec(
        num_scalar_prefetch=1,
        grid=(bsz, nq),
        in_specs=[pl.BlockSpec((None, None, GROUP_W, tq), lambda b, i, lam: (b, i, ROW_QD // GROUP_W, 0)),
                  pl.BlockSpec((None, t, GROUP_W), lambda b, i, lam: (b, 0, COL_KD // GROUP_W)),
                  pl.BlockSpec((None, nq, GROUP_W, tq), lambda b, i, lam: (b, 0, ROW_VD // GROUP_W, 0)),
                  pl.BlockSpec(bias_tiles.shape, lambda b, i, lam: (0, 0, 0, 0)),
                  pl.BlockSpec((HEAD_DIM, 1), lambda b, i, lam: (0, 0))],
        out_specs=pl.BlockSpec((None, tq, GROUP_W), lambda b, i, lam: (b, i, 0)),
    )
    return pl.pallas_call(
        functools.partial(_diff_kernel, tq=tq, c_scale=DIFF_QK_DIM ** -0.5 * LOG2E, out_scale=1.0 - lambda_init),
        grid_spec=grid_spec,
        out_shape=jax.ShapeDtypeStruct((bsz, t, GROUP_W), BF16),
        compiler_params=_cparams("parallel", "arbitrary"),
        name="diff",
    )(lam.reshape(1).astype(F32), proj_t, proj, proj_t, bias_tiles, diff_g.reshape(HEAD_DIM, 1).astype(F32))


def _dsa_kernel(qx_ref, wx_ref, tail_ref, qt_ref, k_ref, vt_ref, bias_ref, o_ref, key_scr, cut_scr,
                *, tq, topk, c_scale, row_bits):
    qi = pl.program_id(1)
    nkb = qi + 1
    r = lax.broadcasted_iota(I32, (tq, tq), 0)
    c = lax.broadcasted_iota(I32, (tq, tq), 1)
    wx = wx_ref[...].astype(F32) * (IDX_HEADS ** -0.5 * IDX_DIM ** -0.5)
    zpad = jnp.zeros((LANES - IDX_DIM, tq), BF16)
    qx = [jnp.concatenate([qx_ref[h * IDX_DIM:(h + 1) * IDX_DIM, :], zpad], axis=0) for h in range(IDX_HEADS)]

    def score_block(kb, _):
        start = pl.multiple_of(kb * tq, tq)
        kt = tail_ref[pl.ds(start, tq), :]
        s = jnp.zeros((tq, tq), F32)
        for h in range(IDX_HEADS):
            s = s + wx[h:h + 1, :] * jnp.maximum(_dot(kt, qx[h]), 0.0)
        s = jnp.where(s == 0.0, 0.0, s)
        s = jnp.where(r + kb * tq <= c + qi * tq, s, -jnp.inf)
        bits = pltpu.bitcast(s, I32)
        key_scr[kb] = bits ^ ((bits >> 31) & 0x7FFFFFFF)
        return 0

    lax.fori_loop(0, nkb, score_block, 0)

    def count(pred):
        def body(kb, acc):
            hit = jnp.where(pred(key_scr[kb], kb), 1.0, 0.0)
            return acc + jnp.sum(hit.reshape(tq // COUNT_ROWS, COUNT_ROWS, tq), axis=0)
        acc = lax.fori_loop(0, nkb, body, jnp.zeros((COUNT_ROWS, tq), F32))
        return jnp.sum(acc, axis=0, keepdims=True)

    def bit_step(i, s):
        thr, c_thr = s
        cand = thr + lax.shift_left(jnp.int32(1), 31 - i)
        cnt = count(lambda key, kb: key >= cand)
        ge = cnt >= topk
        return jnp.where(ge, cand, thr), jnp.where(ge, cnt, c_thr)

    n_keys = (nkb * tq).astype(F32)
    thr, c_thr = lax.fori_loop(0, 32, bit_step, (jnp.full((1, tq), INT_MIN, I32), jnp.full((1, tq), 1.0, F32) * n_keys))
    cut_scr[...] = jnp.full((SUBLANES, tq), 2 ** 30, I32)

    @pl.when(jnp.max(c_thr) > topk)
    def _():
        need = topk - count(lambda key, kb: key > thr)

        def row_step(i, lo):
            cand = lo + lax.shift_left(jnp.int32(1), row_bits - 1 - i)
            cnt = count(lambda key, kb: jnp.logical_and(key == thr, r + kb * tq < cand))
            return jnp.where(cnt < need, cand, lo)
        lo = lax.fori_loop(0, row_bits, row_step, jnp.zeros((1, tq), I32))
        cut_scr[...] = jnp.broadcast_to(lo, (SUBLANES, tq))

    cut = cut_scr[0:1, :]

    def select_block(kb, _):
        key = key_scr[kb]
        pos = r + kb * tq
        sel = jnp.logical_or(key > thr, jnp.logical_and(key == thr, pos <= cut))
        sel = jnp.logical_and(sel, pos <= c + qi * tq)
        key_scr[kb] = jnp.where(sel, 0, NEG_BIG_BITS)
        return 0

    lax.fori_loop(0, nkb, select_block, 0)

    qt = qt_ref[...]
    ones = _ones_rows(tq)
    chains = [(h * HEAD_DIM, (h + 1) * HEAD_DIM) for h in range(GROUP_HEADS)]

    def logits(kb):
        return tuple(_chain_logits(k_ref, pl.multiple_of(kb * tq, tq), tq, chains, qt))

    def step(zs, kb, st, which, masked):
        del masked
        mask = pltpu.bitcast(key_scr[kb], F32)
        ms, alphas, ps = [], [], []
        for h, z in enumerate(zs):
            z = z * c_scale + mask
            if which is not None:
                z = z + bias_ref[h, which]
            m_new, alpha, p = _softmax_weights(z, st[h][0])
            ms.append(m_new)
            alphas.append(alpha)
            ps.append(p)
        pv = _dot(_value_blockdiag(vt_ref[kb], ones), jnp.concatenate(ps, axis=0))
        return tuple((ms[h], alphas[h] * st[h][1] + pv[GROUP_W + h:GROUP_W + h + 1, :],
                      alphas[h] * st[h][2] + pv[h * HEAD_DIM:(h + 1) * HEAD_DIM, :]) for h in range(GROUP_HEADS))

    st = _causal_blocks(qi, logits, step, tuple(_softmax_init_t(tq) for _ in range(GROUP_HEADS)))
    o_ref[...] = jnp.concatenate([acc / l for _, l, acc in st], axis=0).T.astype(BF16)


def _dsa(proj, proj_t, bias_tiles):
    bsz, t, _ = proj.shape
    nq, tq = proj_t.shape[1], proj_t.shape[3]
    topk = min(INDEX_TOPK_MAX, t // 4)
    assert tq >= topk, "the threshold search needs at least topk keys in the first block"
    nqx = IDX_HEADS * IDX_DIM
    return pl.pallas_call(
        functools.partial(_dsa_kernel, tq=tq, topk=float(topk), c_scale=HEAD_DIM ** -0.5 * LOG2E,
                          row_bits=max(1, (t - 1).bit_length())),
        grid=(bsz, nq),
        in_specs=[pl.BlockSpec((None, None, nqx, tq), lambda b, i: (b, i, ROW_QX // nqx, 0)),
                  pl.BlockSpec((None, None, IDX_HEADS, tq), lambda b, i: (b, i, ROW_WX // IDX_HEADS, 0)),
                  pl.BlockSpec((None, t, LANES), lambda b, i: (b, 0, COL_TAIL // LANES)),
                  pl.BlockSpec((None, None, GROUP_W, tq), lambda b, i: (b, i, ROW_QC // GROUP_W, 0)),
                  pl.BlockSpec((None, t, GROUP_W), lambda b, i: (b, 0, COL_KC // GROUP_W)),
                  pl.BlockSpec((None, nq, GROUP_W, tq), lambda b, i: (b, 0, ROW_VC // GROUP_W, 0)),
                  pl.BlockSpec(bias_tiles.shape, lambda b, i: (0, 0, 0, 0))],
        out_specs=pl.BlockSpec((None, tq, GROUP_W), lambda b, i: (b, i, 0)),
        out_shape=jax.ShapeDtypeStruct((bsz, t, GROUP_W), BF16),
        scratch_shapes=[pltpu.VMEM((nq, tq, tq), I32), pltpu.VMEM((SUBLANES, tq), I32)],
        compiler_params=_cparams("parallel", "arbitrary"),
        name="dsa",
    )(proj_t, proj_t, proj, proj_t, proj, proj_t, bias_tiles)


def _dil_kernel(q_ref, kp_ref, kd_ref, vp_ref, vd_ref, bias_ref, ol_ref, *, tq, scale):
    qi = pl.program_id(1)
    r = lax.broadcasted_iota(I32, (tq, tq), 0)
    c = lax.broadcasted_iota(I32, (tq, tq), 1)
    prev_ok = jnp.logical_and(r <= c, qi > 0)
    diag_ok = c <= r
    q = q_ref[...]
    zp_all = _dot_nt(q, _by_head(kp_ref[...])) * scale
    zd_all = _dot_nt(q, _by_head(kd_ref[...])) * scale
    pps, pds, dens, lses = [], [], [], []
    for h in range(GROUP_HEADS):
        cols = slice(h * tq, (h + 1) * tq)
        zp = jnp.where(prev_ok, zp_all[:, cols] + bias_ref[h, 1], NEG_BIG)
        zd = jnp.where(diag_ok, zd_all[:, cols] + bias_ref[h, 0], NEG_BIG)
        m = jnp.maximum(jnp.max(zp, axis=1, keepdims=True), jnp.max(zd, axis=1, keepdims=True))
        pp = jnp.exp(zp - m)
        pd = jnp.exp(zd - m)
        den = jnp.sum(pp, axis=1, keepdims=True) + jnp.sum(pd, axis=1, keepdims=True)
        pps.append(pp.astype(BF16))
        pds.append(pd.astype(BF16))
        dens.append(jnp.broadcast_to(den, (tq, HEAD_DIM)))
        lses.append(jnp.broadcast_to(m + jnp.log(den), (tq, HEAD_DIM)))
    values = jnp.concatenate([_by_head(vp_ref[...]), _by_head(vd_ref[...])], axis=0)
    ol_ref[:, :GROUP_W] = _dot(jnp.concatenate(pps + pds, axis=1), values) / jnp.concatenate(dens, axis=-1)
    ol_ref[:, GROUP_W:] = jnp.concatenate(lses, axis=-1)


def _dilated_one(qkv, bias_tiles):
    n, length, _ = qkv.shape
    tq = bias_tiles.shape[-1]
    blk = lambda f: pl.BlockSpec((None, tq, GROUP_W), f)
    prev = lambda col: (lambda b, i: (b, jnp.maximum(i - 1, 0), col))
    cur = lambda col: (lambda b, i: (b, i, col))
    return pl.pallas_call(
        functools.partial(_dil_kernel, tq=tq, scale=HEAD_DIM ** -0.5),
        grid=(n, length // tq),
        in_specs=[blk(cur(0)), blk(prev(1)), blk(cur(1)), blk(prev(2)), blk(cur(2)),
                  pl.BlockSpec(bias_tiles.shape, lambda b, i: (0, 0, 0, 0))],
        out_specs=pl.BlockSpec((None, tq, 2 * GROUP_W), lambda b, i: (b, i, 0)),
        out_shape=jax.ShapeDtypeStruct((n, length, 2 * GROUP_W), F32),
        compiler_params=_cparams("parallel", "arbitrary"),
        name="dil",
    )(qkv, qkv, qkv, qkv, qkv, bias_tiles)


def _dilmix_kernel(p0, p1, p2, out_ref):
    a0, a1, a2 = p0[:, GROUP_W:], p1[:, GROUP_W:], p2[:, GROUP_W:]
    m = jnp.maximum(jnp.maximum(a0, a1), a2)
    e0, e1, e2 = jnp.exp(a0 - m), jnp.exp(a1 - m), jnp.exp(a2 - m)
    mixed = e0 * p0[:, :GROUP_W] + e1 * p1[:, :GROUP_W] + e2 * p2[:, :GROUP_W]
    out_ref[...] = (mixed / (e0 + e1 + e2)).astype(BF16)


def _dilated_mix(patterns):
    bsz, t, w = patterns[0].shape
    tm = min(512, t)
    return pl.pallas_call(
        _dilmix_kernel,
        grid=(bsz, t // tm),
        in_specs=[pl.BlockSpec((None, tm, w), lambda b, i: (b, i, 0))] * 3,
        out_specs=pl.BlockSpec((None, tm, w // 2), lambda b, i: (b, i, 0)),
        out_shape=jax.ShapeDtypeStruct((bsz, t, w // 2), BF16),
        compiler_params=_cparams("parallel", "parallel"),
        name="dilmix",
    )(*patterns)


def _dilated(proj, bias_tiles_per_cfg):
    bsz, t, _ = proj.shape
    qkv = proj[:, :, COL_B:COL_B + 3 * GROUP_W]
    patterns = []
    for (_, dil), tiles in zip(DILATED_CONFIGS, bias_tiles_per_cfg):
        def perm(a, dil=dil):
            w = a.shape[-1]
            return a.reshape(bsz, t // dil, dil, w).transpose(0, 2, 1, 3).reshape(bsz * dil, t // dil, w)

        def unperm(a, dil=dil):
            w = a.shape[-1]
            return a.reshape(bsz, dil, t // dil, w).transpose(0, 2, 1, 3).reshape(bsz, t, w)

        patterns.append(unperm(_dilated_one(perm(qkv), tiles)))
    return _dilated_mix(patterns)


def _post_kernel(oa_ref, ob_ref, oc_ref, od_ref, wo_ref, x_ref, mod_ref, ln_ref, wr_ref, br_ref,
                 x1_ref, h2_ref, idx_ref, gate_ref, *, alpha):
    y = jnp.zeros(x_ref.shape, F32)
    for g, o_ref in enumerate((oa_ref, ob_ref, oc_ref, od_ref)):
        y = y + _dot(o_ref[...], wo_ref[g * GROUP_W:(g + 1) * GROUP_W, :])
    u = alpha * x_ref[...] + (1.0 + mod_ref[2:3, :]) * y
    x1 = _ln(u) * ln_ref[0:1, :] + ln_ref[1:2, :]
    x1_ref[...] = x1
    h2 = _ln(x1) * (1.0 + mod_ref[4:5, :]) + mod_ref[3:4, :]
    _store_token_tiles(h2_ref, 0, h2)
    logits = lax.dot_general(wr_ref[...], h2, (((1,), (1,)), ((), ())), precision=lax.Precision.HIGHEST,
                             preferred_element_type=F32) + br_ref[...]
    n_exp, tm = logits.shape
    eid = lax.broadcasted_iota(I32, (n_exp, tm), 0)
    vals, ids = [], []
    for _ in range(TOP_K):
        m = jnp.max(logits, axis=0, keepdims=True)
        first = jnp.min(jnp.where(logits == m, eid, n_exp), axis=0, keepdims=True)
        vals.append(m)
        ids.append(first)
        logits = jnp.where(eid == first, -jnp.inf, logits)
    ex = [jnp.exp(v - vals[0]) for v in vals]
    den = ex[0] + ex[1] + ex[2] + ex[3]
    zero_f = jnp.zeros((8 - TOP_K, tm), F32)
    gate_ref[...] = jnp.concatenate([e / den for e in ex] + [zero_f], axis=0)
    idx_ref[...] = jnp.concatenate(ids + [zero_f.astype(I32)], axis=0)


def _post_mixer(o_groups, w_out, x, mod, ln_rows, w_router_t, b_router, alpha):
    bsz, t, d = x.shape
    tm = min(256, t)
    n_exp = w_router_t.shape[0]
    og = pl.BlockSpec((None, tm, GROUP_W), lambda b, i: (b, i, 0))
    row = pl.BlockSpec((None, tm, d), lambda b, i: (b, i, 0))
    small = pl.BlockSpec((None, 8, tm), lambda b, i: (b, 0, i))
    assert d == SUBLANES * LANES, "token-tile layout needs one (8,128) tile per token"
    nt = t // tm
    tiles = pl.BlockSpec((tm * SUBLANES, LANES), lambda b, i: (b * nt + i, 0))
    return pl.pallas_call(
        functools.partial(_post_kernel, alpha=alpha),
        grid=(bsz, nt),
        in_specs=[og, og, og, og,
                  pl.BlockSpec(w_out.shape, lambda b, i: (0, 0)),
                  row,
                  pl.BlockSpec((None, 8, d), lambda b, i: (b, 0, 0)),
                  pl.BlockSpec((4, d), lambda b, i: (0, 0)),
                  pl.BlockSpec((n_exp, d), lambda b, i: (0, 0)),
                  pl.BlockSpec((n_exp, 1), lambda b, i: (0, 0))],
        out_specs=[row, tiles, small, small],
        out_shape=[jax.ShapeDtypeStruct((bsz, t, d), F32), jax.ShapeDtypeStruct((bsz * t * SUBLANES, LANES), F32),
                   jax.ShapeDtypeStruct((bsz, 8, t), I32), jax.ShapeDtypeStruct((bsz, 8, t), F32)],
        compiler_params=_cparams("parallel", "parallel"),
        name="post",
    )(*o_groups, w_out, x, mod, ln_rows, w_router_t, b_router.reshape(n_exp, 1))


def _deint_kernel(w_ref, p_ref, o_ref):
    o_ref[...] = _dot(w_ref[...].astype(BF16), p_ref[...]).astype(BF16)


def _deinterleave_w1(w1_all, layer):
    _, n_exp, d, two_f = w1_all.shape
    grp = 2 * FF_GROUP
    j = np.arange(grp)
    src = np.where(j < FF_GROUP, 2 * j, 2 * (j - FF_GROUP) + 1)
    perm = jnp.asarray(np.arange(grp)[:, None] == src[None, :], BF16)
    return pl.pallas_call(
        _deint_kernel,
        grid=(n_exp, two_f // grp),
        in_specs=[pl.BlockSpec((None, None, d, grp), lambda e, g: (layer, e, 0, g)),
                  pl.BlockSpec((grp, grp), lambda e, g: (0, 0))],
        out_specs=pl.BlockSpec((None, d, grp), lambda e, g: (e, 0, g)),
        out_shape=jax.ShapeDtypeStruct((n_exp, d, two_f), BF16),
        compiler_params=_cparams("parallel", "parallel"),
        name="deint",
    )(w1_all, perm)


def _moe_kernel(be_ref, nval_ref, nblk_ref, cur_ref, nxt_ref, h_hbm, w1_ref, b1_ref, w2_ref, b2_ref,
                out_hbm, xbuf, ybuf, gsem, ssem, *, bm):
    i = pl.program_id(0)
    nblk = nblk_ref[0]
    slot = i % 2

    def tile_rows(tok):
        return pl.ds(pl.multiple_of(tok * SUBLANES, SUBLANES), SUBLANES)

    def gather_row_copy(tok, r, s):
        return pltpu.make_async_copy(h_hbm.at[tile_rows(tok)], xbuf.at[tile_rows(s * bm + r)], gsem.at[s])

    def scatter_row_copy(row, r, s):
        return pltpu.make_async_copy(ybuf.at[tile_rows(s * bm + r)], out_hbm.at[tile_rows(row)], ssem.at[s])

    def block_rows(s):
        return pl.ds(pl.multiple_of(s * bm * SUBLANES, bm * SUBLANES), bm * SUBLANES)

    def start_gather(idx_ref, s):
        for r in range(bm):
            gather_row_copy(idx_ref[0, r], r, s).start(priority=r % 2)

    def start_gather_rolled(idx_ref, s):
        def body(r, _):
            gather_row_copy(idx_ref[0, r], r, s).start()
            return 0
        lax.fori_loop(0, bm, body, 0)

    def wait_gather(s):
        pltpu.make_async_copy(h_hbm.at[pl.ds(0, bm * SUBLANES)], xbuf.at[block_rows(s)], gsem.at[s]).wait()

    def start_scatter(s, n):
        @pl.when(n == bm)
        def _():
            for r in range(bm):
                scatter_row_copy(cur_ref[0, bm + r], r, s).start(priority=r % 2)

        @pl.when(n < bm)
        def _():
            def body(r, _):
                scatter_row_copy(cur_ref[0, bm + r], r, s).start()
                return 0
            lax.fori_loop(0, n, body, 0)

    def wait_scatter(s, n):
        @pl.when(n == bm)
        def _():
            pltpu.make_async_copy(ybuf.at[block_rows(s)], out_hbm.at[pl.ds(0, bm * SUBLANES)], ssem.at[s]).wait()

        @pl.when(n < bm)
        def _():
            def body(r, _):
                scatter_row_copy(0, r, s).wait()
                return 0
            lax.fori_loop(0, n, body, 0)

    @pl.when(jnp.logical_and(i == 0, nblk > 0))
    def _():
        start_gather_rolled(cur_ref, 0)

    @pl.when(i + 1 < nblk)
    def _():
        start_gather(nxt_ref, 1 - slot)

    @pl.when(i < nblk)
    def _():
        wait_gather(slot)

        @pl.when(i >= 2)
        def _():
            wait_scatter(slot, nval_ref[jnp.maximum(i - 2, 0)])

        x = _load_token_tiles(xbuf, slot * bm, bm).astype(BF16)
        y = jnp.zeros((bm, w2_ref.shape[1]), F32) + b2_ref[...]
        for g in range(w2_ref.shape[0] // FF_GROUP):
            cols = slice(2 * g * FF_GROUP, 2 * (g + 1) * FF_GROUP)
            hh = _dot(x, w1_ref[:, cols]) + b1_ref[:, cols]
            glu = jnp.minimum(hh[:, :FF_GROUP], SWIGLU_LIMIT)
            lin = jnp.clip(hh[:, FF_GROUP:], -SWIGLU_LIMIT, SWIGLU_LIMIT)
            act = glu * jax.nn.sigmoid(SWIGLU_ALPHA * glu) * (lin + 1.0)
            y = y + _dot(act.astype(BF16), w2_ref[g * FF_GROUP:(g + 1) * FF_GROUP, :])
        _store_token_tiles(ybuf, slot * bm, y)
        start_scatter(slot, nval_ref[i])

    last = pl.num_programs(0) - 1

    @pl.when(jnp.logical_and(i == last, nblk >= 2))
    def _():
        wait_scatter(nblk % 2, nval_ref[jnp.maximum(nblk - 2, 0)])

    @pl.when(jnp.logical_and(i == last, nblk >= 1))
    def _():
        wait_scatter((nblk + 1) % 2, nval_ref[jnp.maximum(nblk - 1, 0)])


def _moe_experts(h2, blk_expert, blk_valid, n_used, slot_idx, w1p, b1p, w2_all, b2, layer):
    n_tok = h2.shape[0] // SUBLANES
    d = SUBLANES * LANES
    n_blocks, _, two_bm = slot_idx.shape
    bm = two_bm // 2
    n_exp, _, two_f = w1p.shape
    f = two_f // 2
    idx_spec = lambda f_: pl.BlockSpec((None, 1, two_bm), f_, memory_space=pltpu.SMEM)
    wspec = lambda shp: pl.BlockSpec((None,) + shp, lambda i, be, nv, nb: (be[i], 0, 0))
    grid_spec = pltpu.PrefetchScalarGridSpec(
        num_scalar_prefetch=3,
        grid=(n_blocks,),
        in_specs=[idx_spec(lambda i, be, nv, nb: (i, 0, 0)),
                  idx_spec(lambda i, be, nv, nb: (jnp.minimum(i + 1, n_blocks - 1), 0, 0)),
                  pl.BlockSpec(memory_space=pl.ANY),
                  wspec((d, two_f)), wspec((1, two_f)),
                  pl.BlockSpec((None, None, f, d), lambda i, be, nv, nb: (layer, be[i], 0, 0)),
                  wspec((1, d))],
        out_specs=pl.BlockSpec(memory_space=pl.ANY),
        scratch_shapes=[pltpu.VMEM((2 * bm * SUBLANES, LANES), F32), pltpu.VMEM((2 * bm * SUBLANES, LANES), F32),
                        pltpu.SemaphoreType.DMA((2,)), pltpu.SemaphoreType.DMA((2,))],
    )
    return pl.pallas_call(
        functools.partial(_moe_kernel, bm=bm),
        grid_spec=grid_spec,
        out_shape=jax.ShapeDtypeStruct((n_tok * TOP_K * SUBLANES, LANES), F32),
        compiler_params=_cparams("arbitrary"),
        name="moe",
    )(blk_expert, blk_valid, n_used, slot_idx, slot_idx, h2, w1p, b1p, w2_all, b2)


def _moe_dispatch(top_idx, bm):
    n_tok = top_idx.shape[0]
    m = n_tok * TOP_K
    assert m % bm == 0
    e_flat = top_idx.reshape(-1)
    experts = jnp.arange(N_EXPERTS, dtype=I32)
    counts = jnp.sum((e_flat[:, None] == experts[None, :]).astype(I32), axis=0)
    padded = (counts + bm - 1) // bm * bm
    pend = jnp.cumsum(padded)
    n_blocks = m // bm + N_EXPERTS
    pad_ok = jnp.arange(bm, dtype=I32)[None, :] < (padded - counts)[:, None]
    pad_key = jnp.where(pad_ok, 2 * experts[:, None] + 1, 2 * N_EXPERTS).reshape(-1)
    keys = jnp.concatenate([2 * e_flat, pad_key])
    vals = jnp.concatenate([jnp.arange(m, dtype=I32), jnp.full((N_EXPERTS * bm,), -1, I32)])
    _, asg = lax.sort((keys, vals), num_keys=1, is_stable=True)
    valid = asg >= 0
    tok = jnp.maximum(asg, 0) // TOP_K
    choice = jnp.maximum(asg, 0) % TOP_K
    slot_idx = jnp.concatenate([tok.reshape(n_blocks, bm), (choice * n_tok + tok).reshape(n_blocks, bm)], axis=1)
    blk_valid = jnp.sum(valid.reshape(n_blocks, bm).astype(I32), axis=1)
    blk_start = jnp.arange(n_blocks, dtype=I32) * bm
    blk_expert = jnp.minimum(jnp.sum((blk_start[:, None] >= pend[None, :]).astype(I32), axis=1), N_EXPERTS - 1)
    n_used = (pend[-1] // bm).astype(I32).reshape(1)
    return blk_expert, blk_valid, n_used, slot_idx.reshape(n_blocks, 1, 2 * bm)


def _comb_kernel(y4_ref, gate_ref, x_ref, mod_ref, ln_ref, o_ref, *, alpha):
    y = jnp.zeros(x_ref.shape, F32)
    for k in range(TOP_K):
        y = y + gate_ref[:, k:k + 1] * _load_token_tiles(y4_ref.at[k], 0, x_ref.shape[0])
    u = alpha * x_ref[...] + (1.0 + mod_ref[5:6, :]) * y
    o_ref[...] = _ln(u) * ln_ref[2:3, :] + ln_ref[3:4, :]


def _combine(y4, gate, x1, mod, ln_rows, alpha):
    bsz, t, d = x1.shape
    tm = min(256, t)
    nt = t // tm
    return pl.pallas_call(
        functools.partial(_comb_kernel, alpha=alpha),
        grid=(bsz, nt),
        in_specs=[pl.BlockSpec((TOP_K, tm * SUBLANES, LANES), lambda b, i: (0, b * nt + i, 0)),
                  pl.BlockSpec((None, tm, 8), lambda b, i: (b, i, 0)),
                  pl.BlockSpec((None, tm, d), lambda b, i: (b, i, 0)),
                  pl.BlockSpec((None, 8, d), lambda b, i: (b, 0, 0)),
                  pl.BlockSpec((4, d), lambda b, i: (0, 0))],
        out_specs=pl.BlockSpec((None, tm, d), lambda b, i: (b, i, 0)),
        out_shape=jax.ShapeDtypeStruct((bsz, t, d), F32),
        compiler_params=_cparams("parallel", "parallel"),
        name="comb",
    )(y4, gate, x1, mod, ln_rows)


def _rel_bucket(dist):
    n = jnp.maximum(dist, 0)
    max_exact = N_BUCKETS // 2
    nf = jnp.maximum(n, 1).astype(F32)
    large = max_exact + (jnp.log(nf / max_exact) / math.log(MAX_DISTANCE / max_exact)
                         * (N_BUCKETS - max_exact)).astype(I32)
    large = jnp.minimum(large, N_BUCKETS - 1)
    return jnp.where(n < max_exact, n, large)


def _bias_tiles(bias_tab, tq, dil, key_major):
    period = 2 * tq
    k = np.arange(period)
    d = np.where(k < tq, -k, period - k)
    dist = np.stack([np.maximum(d, 0), d + tq]) * dil
    line = bias_tab.astype(F32)[_rel_bucket(jnp.asarray(dist, I32))]
    line = jnp.moveaxis(line, -1, 0)
    flat = jnp.tile(line, (1, 1, tq))[..., :tq * (period - 1)]
    tiles = flat.reshape(line.shape[0], 2, tq, period - 1)[..., :tq]
    if key_major:
        tiles = (tiles - bias_tab.astype(F32)[N_BUCKETS - 1][:, None, None, None]) * LOG2E
        tiles = jnp.swapaxes(tiles, -1, -2)
    return tiles


def _split_w_in(w_in):
    d = w_in.shape[0]
    g = GROUP_W
    segs = {}
    o = 0
    for name, width in (("qa", g), ("ka", g), ("va", g), ("qb", g), ("kb", g), ("vb", g),
                        ("qc", g), ("kc", g), ("vc", g), ("qx", IDX_HEADS * IDX_DIM), ("kx", IDX_DIM),
                        ("wx", IDX_HEADS), ("qd", g), ("kd", g), ("vd", g)):
        segs[name] = w_in[:, o:o + width]
        o += width
    pad = jnp.zeros((d, LANES - IDX_DIM - IDX_HEADS), w_in.dtype)
    w = jnp.concatenate([segs[n] for n in ("qa", "ka", "va", "qb", "kb", "vb", "kc", "kd", "kx", "wx")] + [pad],
                        axis=1)
    wt = jnp.concatenate([segs[n] for n in ("qx", "qc", "vc", "qd", "vd", "wx")], axis=1).T
    return w.astype(BF16), wt.astype(BF16)


def _layer(x, c, layer, depth, p, tiles):
    bsz, t, d = x.shape
    alpha = (2 * depth) ** 0.25
    mod = _ada_mod(c, p["w_ada_all"], p["b_ada_all"], layer).reshape(bsz, 6, d)
    mod = jnp.concatenate([mod, jnp.zeros((bsz, 2, d), F32)], axis=1)
    ln_rows = jnp.concatenate([p["ln_g"][0:1], p["ln_b"][0:1], p["ln_g"][1:2], p["ln_b"][1:2]], axis=0)

    proj, proj_t = _ln_mod_proj(x, mod, *_split_w_in(p["w_in"]))

    o_a = _stick_breaking(proj)
    o_b = _dilated(proj, tiles["b"])
    o_c = _dsa(proj, proj_t, tiles["c"])
    lamp = p["diff_lam"].astype(F32)
    lambda_init = 0.8 - 0.6 * math.exp(-0.3 * layer)
    lam = jnp.exp(jnp.sum(lamp[0] * lamp[1])) - jnp.exp(jnp.sum(lamp[2] * lamp[3])) + lambda_init
    o_d = _differential(proj, proj_t, lam, tiles["d"], p["diff_g"], lambda_init)

    x1, h2, top_idx, gate = _post_mixer((o_a, o_b, o_c, o_d), p["w_out"].astype(BF16), x, mod, ln_rows,
                                        p["w_router"].T, p["b_router"], alpha)

    top_idx = top_idx[:, :TOP_K, :].transpose(0, 2, 1).reshape(bsz * t, TOP_K)
    blk_expert, blk_valid, n_used, slot_idx = _moe_dispatch(top_idx, MOE_BLOCK)
    n_grp = p["b1"].shape[-1] // (2 * FF_GROUP)
    b1p = p["b1"].reshape(N_EXPERTS, n_grp, FF_GROUP, 2).transpose(0, 1, 3, 2).reshape(N_EXPERTS, 1, -1)
    y_rows = _moe_experts(h2, blk_expert, blk_valid, n_used, slot_idx,
                          _deinterleave_w1(p["w1_all"], layer), b1p, p["w2_all"], p["b2"][:, None, :], layer)
    return _combine(y_rows.reshape(TOP_K, bsz * t * SUBLANES, LANES), gate.transpose(0, 2, 1), x1, mod, ln_rows,
                    alpha)


def kernel(x, c, w_ada, b_ada, w_in, w_out, diff_lam, diff_g, ln_g, ln_b, w_router, b_router, w1, b1, w2, b2,
           rel_bias):
    depth = w_in.shape[0]
    t = x.shape[1]
    tq = min(ATT_BLOCK, t)
    tiles = dict(
        b=[_bias_tiles(rel_bias[:, :GROUP_HEADS], min(128, t // dil), dil, False) for _, dil in DILATED_CONFIGS],
        c=_bias_tiles(rel_bias[:, GROUP_HEADS:2 * GROUP_HEADS], tq, 1, True),
        d=_bias_tiles(rel_bias[:, 2 * GROUP_HEADS:], tq, 1, True))
    w2_all = w2.astype(BF16)
    for layer in range(depth):
        p = dict(w_ada_all=w_ada, b_ada_all=b_ada, w_in=w_in[layer], w_out=w_out[layer],
                 diff_lam=diff_lam[layer], diff_g=diff_g[layer], ln_g=ln_g[layer], ln_b=ln_b[layer],
                 w_router=w_router[layer], b_router=b_router[layer], w1_all=w1, b1=b1[layer],
                 w2_all=w2_all, b2=b2[layer])
        x = _layer(x, c, layer, depth, p, tiles)
    return x
```

```python
import functools
import math

import numpy as np
import jax
import jax.numpy as jnp
from jax import lax
from jax.experimental import pallas as pl
from jax.experimental.pallas import tpu as pltpu

F32 = jnp.float32
BF16 = jnp.bfloat16
I32 = jnp.int32

HEAD_DIM = 64
GROUP_HEADS = 4
GROUP_W = GROUP_HEADS * HEAD_DIM
DIFF_QK_DIM = HEAD_DIM // 2
DILATED_CONFIGS = ((128, 1), (512, 4), (2048, 16))
IDX_HEADS = 16
IDX_DIM = 64
INDEX_TOPK_MAX = 256
N_EXPERTS = 32
TOP_K = 4
SWIGLU_ALPHA = 1.702
SWIGLU_LIMIT = 7.0
N_BUCKETS = 32
MAX_DISTANCE = 128
LN_EPS = 1e-5
MOE_BLOCK = 512
DMA_ISSUE_CHUNK = 64

LANES = 128
SUBLANES = 8
VMEM_LIMIT_BYTES = 56 * 1024 * 1024
NEG_BIG = -1e30
SB_SKIP_LOG = -100.0
INT_MIN = -2 ** 31
NEG_BIG_BITS = int(np.float32(NEG_BIG).view(np.int32))
LOG2E = math.log2(math.e)
ATT_BLOCK = 256
COUNT_ROWS = 8 * SUBLANES
FF_GROUP = 256

COL_A = 0
COL_B = COL_A + 3 * GROUP_W
COL_KC = COL_B + 3 * GROUP_W
COL_KD = COL_KC + GROUP_W
COL_TAIL = COL_KD + GROUP_W
PROJ_COLS = COL_TAIL + LANES
ROW_QX = 0
ROW_QC = ROW_QX + IDX_HEADS * IDX_DIM
ROW_VC = ROW_QC + GROUP_W
ROW_QD = ROW_VC + GROUP_W
ROW_VD = ROW_QD + GROUP_W
ROW_WX = ROW_VD + GROUP_W
PROJ_ROWS = ROW_WX + IDX_HEADS


def _cparams(*sem):
    return pltpu.CompilerParams(dimension_semantics=sem, vmem_limit_bytes=VMEM_LIMIT_BYTES)


def _ln(x):
    mu = jnp.mean(x, axis=-1, keepdims=True)
    xc = x - mu
    return xc * lax.rsqrt(jnp.mean(xc * xc, axis=-1, keepdims=True) + LN_EPS)


def _dot_nt(a, b):
    return lax.dot_general(a, b, (((1,), (1,)), ((), ())), preferred_element_type=F32)


def _dot(a, b):
    return jnp.dot(a, b, preferred_element_type=F32)


def _load_token_tiles(ref, first_tok, n):
    base = first_tok * SUBLANES
    return jnp.concatenate([ref[pl.ds(base + c, n, stride=SUBLANES), :] for c in range(SUBLANES)], axis=1)


def _store_token_tiles(ref, first_tok, val):
    n = val.shape[0]
    base = first_tok * SUBLANES
    for c in range(SUBLANES):
        ref[pl.ds(base + c, n, stride=SUBLANES), :] = val[:, c * LANES:(c + 1) * LANES]


def _ada_kernel(c_ref, w_ref, b_ref, o_ref):
    o_ref[...] = jnp.dot(c_ref[...], w_ref[...], precision=lax.Precision.HIGHEST,
                         preferred_element_type=F32) + b_ref[...]


def _ada_mod(c, w_all, b_all, layer):
    bsz, d = c.shape
    depth, _, n = w_all.shape
    return pl.pallas_call(
        _ada_kernel,
        grid=(n // d,),
        in_specs=[pl.BlockSpec((bsz, d), lambda j: (0, 0)),
                  pl.BlockSpec((None, d, d), lambda j: (layer, 0, j)),
                  pl.BlockSpec((None, 1, d), lambda j: (layer, 0, j))],
        out_specs=pl.BlockSpec((bsz, d), lambda j: (0, j)),
        out_shape=jax.ShapeDtypeStruct((bsz, n), F32),
        compiler_params=_cparams("arbitrary"),
        name="ada",
    )(c, w_all, b_all.reshape(depth, 1, n))


def _proj_kernel(x_ref, mod_ref, w_ref, wt_ref, o_ref, ot_ref, *, chunk):
    h = (_ln(x_ref[...]) * (1.0 + mod_ref[1:2, :]) + mod_ref[0:1, :]).astype(BF16)
    ncol = o_ref.shape[-1]
    for c0 in range(0, ncol, chunk):
        c1 = min(c0 + chunk, ncol)
        o_ref[:, c0:c1] = _dot(h, w_ref[:, c0:c1]).astype(BF16)
    n_sub, nrow, tq = ot_ref.shape
    for r0 in range(0, nrow, chunk):
        r1 = min(r0 + chunk, nrow)
        res = _dot_nt(wt_ref[r0:r1, :], h).astype(BF16)
        for j in range(n_sub):
            ot_ref[j, r0:r1, :] = res[:, j * tq:(j + 1) * tq]


def _ln_mod_proj(x, mod, w, wt):
    bsz, t, d = x.shape
    ncol, nrow = w.shape[1], wt.shape[0]
    tq = min(ATT_BLOCK, t)
    tm = min(2 * tq, t)
    return pl.pallas_call(
        functools.partial(_proj_kernel, chunk=2 * LANES),
        grid=(bsz, t // tm),
        in_specs=[pl.BlockSpec((None, tm, d), lambda b, i: (b, i, 0)),
                  pl.BlockSpec((None, 8, d), lambda b, i: (b, 0, 0)),
                  pl.BlockSpec((d, ncol), lambda b, i: (0, 0)),
                  pl.BlockSpec((nrow, d), lambda b, i: (0, 0))],
        out_specs=[pl.BlockSpec((None, tm, ncol), lambda b, i: (b, i, 0)),
                   pl.BlockSpec((None, tm // tq, nrow, tq), lambda b, i: (b, i, 0, 0))],
        out_shape=[jax.ShapeDtypeStruct((bsz, t, ncol), BF16),
                   jax.ShapeDtypeStruct((bsz, t // tq, nrow, tq), BF16)],
        compiler_params=_cparams("parallel", "parallel"),
        name="proj",
    )(x, mod, w, wt)


def _by_head(x):
    lane = lax.broadcasted_iota(I32, x.shape, 1)
    zero = jnp.zeros_like(x)
    return jnp.concatenate([jnp.where(jnp.logical_and(lane >= h * HEAD_DIM, lane < (h + 1) * HEAD_DIM), x, zero)
                            for h in range(GROUP_HEADS)], axis=0)


def _sb_kernel(q_ref, k_ref, v_ref, o_ref, *, tq, scale):
    qi = pl.program_id(1)
    r = lax.broadcasted_iota(I32, (tq, tq), 0)
    c = lax.broadcasted_iota(I32, (tq, tq), 1)
    strict_lower = c < r
    upper = jnp.where(r > c, 1.0, 0.0).astype(BF16)

    q = q_ref[...]
    valid = jnp.concatenate([strict_lower] * GROUP_HEADS, axis=1)

    def step(kb, carry, acc, masked):
        start = pl.multiple_of(kb * tq, tq)
        z = _dot_nt(q, _by_head(k_ref[pl.ds(start, tq), :])) * scale
        log_sig = jnp.minimum(z, 0.0) - jnp.log(1.0 + jnp.exp(-jnp.abs(z)))
        log_fail = log_sig - z
        if masked:
            log_fail = jnp.where(valid, log_fail, 0.0)
        lf = jnp.concatenate([log_fail[:, h * tq:(h + 1) * tq] for h in range(GROUP_HEADS)], axis=0)
        lf_hi = lf.astype(BF16)
        lf_lo = (lf - lf_hi.astype(F32)).astype(BF16)
        after = _dot(lf_hi, upper) + _dot(lf_lo, upper)
        after = jnp.concatenate([after[h * tq:(h + 1) * tq] + carry[h] for h in range(GROUP_HEADS)], axis=1)
        w = jnp.exp(log_sig + after)
        if masked:
            w = jnp.where(valid, w, 0.0)
        acc = acc + _dot(w.astype(BF16), _by_head(v_ref[pl.ds(start, tq), :]))
        carry = tuple(carry[h] + jnp.sum(log_fail[:, h * tq:(h + 1) * tq], axis=1, keepdims=True)
                      for h in range(GROUP_HEADS))
        return carry, acc

    def live(carry):
        top = carry[0]
        for h in range(1, GROUP_HEADS):
            top = jnp.maximum(top, carry[h])
        return (jnp.max(top) > SB_SKIP_LOG).astype(I32)

    carry, acc = step(qi, tuple(jnp.zeros((tq, 1), F32) for _ in range(GROUP_HEADS)),
                      jnp.zeros((tq, GROUP_W), F32), True)

    def body(s):
        kb, carry, acc, _ = s
        carry, acc = step(kb, carry, acc, False)
        return kb - 1, carry, acc, live(carry)

    _, _, acc, _ = lax.while_loop(lambda s: jnp.logical_and(s[0] >= 0, s[3] > 0), body,
                                  (qi - 1, carry, acc, live(carry)))
    o_ref[...] = acc.astype(BF16)


def _stick_breaking(proj):
    bsz, t, _ = proj.shape
    tq = min(ATT_BLOCK, t)
    cb = COL_A // GROUP_W
    return pl.pallas_call(
        functools.partial(_sb_kernel, tq=tq, scale=HEAD_DIM ** -0.5),
        grid=(bsz, t // tq),
        in_specs=[pl.BlockSpec((None, tq, GROUP_W), lambda b, i: (b, i, cb)),
                  pl.BlockSpec((None, t, GROUP_W), lambda b, i: (b, 0, cb + 1)),
                  pl.BlockSpec((None, t, GROUP_W), lambda b, i: (b, 0, cb + 2))],
        out_specs=pl.BlockSpec((None, tq, GROUP_W), lambda b, i: (b, i, 0)),
        out_shape=jax.ShapeDtypeStruct((bsz, t, GROUP_W), BF16),
        compiler_params=_cparams("parallel", "arbitrary"),
        name="sb",
    )(proj, proj, proj)


ONES_ROWS = 16


def _softmax_init_t(tq):
    return (jnp.full((1, tq), NEG_BIG, F32), jnp.zeros((1, tq), F32), jnp.zeros((HEAD_DIM, tq), F32))


def _chain_logits(k_ref, start, tk, chains, qt):
    outs = [None] * len(chains)
    for half in range(GROUP_W // LANES):
        idx = [i for i, (lo, hi) in enumerate(chains) if lo // LANES == half]
        assert all((chains[i][1] - 1) // LANES == half for i in idx)
        khalf = k_ref[pl.ds(start, tk), half * LANES:(half + 1) * LANES]
        lane = lax.broadcasted_iota(I32, khalf.shape, 1) + half * LANES
        zero = jnp.zeros_like(khalf)
        lhs = jnp.concatenate([jnp.where(jnp.logical_and(lane >= chains[i][0], lane < chains[i][1]), khalf, zero)
                               for i in idx], axis=0)
        z = _dot(lhs, qt[half * LANES:(half + 1) * LANES, :])
        for j, i in enumerate(idx):
            outs[i] = z[j * tk:(j + 1) * tk]
    return outs


def _ones_rows(tk):
    r = lax.broadcasted_iota(I32, (ONES_ROWS, GROUP_HEADS * tk), 0)
    c = lax.broadcasted_iota(I32, (ONES_ROWS, GROUP_HEADS * tk), 1)
    lo = r * tk
    return jnp.where(jnp.logical_and(c >= lo, c < lo + tk), 1.0, 0.0).astype(BF16)


def _value_blockdiag(vblk, ones):
    tk = vblk.shape[1]
    zero = jnp.zeros((HEAD_DIM, tk), BF16)
    rows = [jnp.concatenate([vblk[h * HEAD_DIM:(h + 1) * HEAD_DIM, :] if j == h else zero
                             for j in range(GROUP_HEADS)], axis=1) for h in range(GROUP_HEADS)]
    return jnp.concatenate(rows + [ones], axis=0)


def _softmax_weights(z, m):
    m_new = jnp.maximum(m, jnp.max(z, axis=0, keepdims=True))
    return m_new, jnp.exp2(m - m_new), jnp.exp2(z - m_new).astype(BF16)


def _causal_blocks(qi, logits, step, st):
    st = lax.fori_loop(0, jnp.maximum(qi - 1, 0), lambda kb, s: step(logits(kb), kb, s, None, False), st)
    st = lax.cond(qi >= 1, lambda s: step(logits(qi - 1), qi - 1, s, 1, False), lambda s: s, st)
    return step(logits(qi), qi, st, 0, True)


def _diff_kernel(lam_ref, qt_ref, k_ref, vt_ref, bias_ref, g_ref, o_ref, *, tq, c_scale, out_scale):
    qi = pl.program_id(1)
    r = lax.broadcasted_iota(I32, (tq, tq), 0)
    c = lax.broadcasted_iota(I32, (tq, tq), 1)
    causal = r <= c
    lam = lam_ref[0]
    qt = qt_ref[...]
    ones = _ones_rows(tq)
    chains = [(h * HEAD_DIM + j * DIFF_QK_DIM, h * HEAD_DIM + (j + 1) * DIFF_QK_DIM)
              for h in range(GROUP_HEADS) for j in range(2)]

    def logits(kb):
        return tuple(_chain_logits(k_ref, pl.multiple_of(kb * tq, tq), tq, chains, qt))

    def step(zs, kb, st, which, masked):
        ms, alphas, ps = [], [], []
        for i, z in enumerate(zs):
            z = z * c_scale
            if which is not None:
                z = z + bias_ref[i // 2, which]
            if masked:
                z = jnp.where(causal, z, NEG_BIG)
            m_new, alpha, p = _softmax_weights(z, st[i][0])
            ms.append(m_new)
            alphas.append(alpha)
            ps.append(p)
        pmat = jnp.concatenate([jnp.concatenate([ps[2 * h], ps[2 * h + 1]], axis=1) for h in range(GROUP_HEADS)],
                               axis=0)
        pv = _dot(_value_blockdiag(vt_ref[kb], ones), pmat)
        out = []
        for i in range(len(chains)):
            h, j = divmod(i, 2)
            cols = slice(j * tq, (j + 1) * tq)
            l = alphas[i] * st[i][1] + pv[GROUP_W + h:GROUP_W + h + 1, cols]
            acc = alphas[i] * st[i][2] + pv[h * HEAD_DIM:(h + 1) * HEAD_DIM, cols]
            out.append((ms[i], l, acc))
        return tuple(out)

    st = _causal_blocks(qi, logits, step, tuple(_softmax_init_t(tq) for _ in chains))
    outs = []
    for h in range(GROUP_HEADS):
        (_, l1, a1), (_, l2, a2) = st[2 * h], st[2 * h + 1]
        o = a1 / l1 - lam * (a2 / l2)
        o = o * lax.rsqrt(jnp.mean(o * o, axis=0, keepdims=True) + LN_EPS)
        outs.append(o * g_ref[...] * out_scale)
    o_ref[...] = jnp.concatenate(outs, axis=0).T.astype(BF16)


def _differential(proj, proj_t, lam, bias_tiles, diff_g, lambda_init):
    bsz, t, _ = proj.shape
    nq, tq = proj_t.shape[1], proj_t.shape[3]
    grid_spec = pltpu.PrefetchScalarGridSpec(
        num_scalar_prefetch=1,
        grid=(bsz, nq),
        in_specs=[pl.BlockSpec((None, None, GROUP_W, tq), lambda b, i, lam: (b, i, ROW_QD // GROUP_W, 0)),
                  pl.BlockSpec((None, t, GROUP_W), lambda b, i, lam: (b, 0, COL_KD // GROUP_W)),
                  pl.BlockSpec((None, nq, GROUP_W, tq), lambda b, i, lam: (b, 0, ROW_VD // GROUP_W, 0)),
                  pl.BlockSpec(bias_tiles.shape, lambda b, i, lam: (0, 0, 0, 0)),
                  pl.BlockSpec((HEAD_DIM, 1), lambda b, i, lam: (0, 0))],
        out_specs=pl.BlockSpec((None, tq, GROUP_W), lambda b, i, lam: (b, i, 0)),
    )
    return pl.pallas_call(
        functools.partial(_diff_kernel, tq=tq, c_scale=DIFF_QK_DIM ** -0.5 * LOG2E, out_scale=1.0 - lambda_init),
        grid_spec=grid_spec,
        out_shape=jax.ShapeDtypeStruct((bsz, t, GROUP_W), BF16),
        compiler_params=_cparams("parallel", "arbitrary"),
        name="diff",
    )(lam.reshape(1).astype(F32), proj_t, proj, proj_t, bias_tiles, diff_g.reshape(HEAD_DIM, 1).astype(F32))


def _dsa_kernel(qx_ref, wx_ref, tail_ref, qt_ref, k_ref, vt_ref, bias_ref, o_ref, key_scr, cut_scr,
                *, tq, topk, c_scale, row_bits):
    qi = pl.program_id(1)
    nkb = qi + 1
    r = lax.broadcasted_iota(I32, (tq, tq), 0)
    c = lax.broadcasted_iota(I32, (tq, tq), 1)
    wx = wx_ref[...].astype(F32) * (IDX_HEADS ** -0.5 * IDX_DIM ** -0.5)
    zpad = jnp.zeros((LANES - IDX_DIM, tq), BF16)
    qx = [jnp.concatenate([qx_ref[h * IDX_DIM:(h + 1) * IDX_DIM, :], zpad], axis=0) for h in range(IDX_HEADS)]

    def score_block(kb, _):
        start = pl.multiple_of(kb * tq, tq)
        kt = tail_ref[pl.ds(start, tq), :]
        s = jnp.zeros((tq, tq), F32)
        for h in range(IDX_HEADS):
            s = s + wx[h:h + 1, :] * jnp.maximum(_dot(kt, qx[h]), 0.0)
        s = jnp.where(s == 0.0, 0.0, s)
        s = jnp.where(r + kb * tq <= c + qi * tq, s, -jnp.inf)
        bits = pltpu.bitcast(s, I32)
        key_scr[kb] = bits ^ ((bits >> 31) & 0x7FFFFFFF)
        return 0

    lax.fori_loop(0, nkb, score_block, 0)

    def count(pred):
        def body(kb, acc):
            hit = jnp.where(pred(key_scr[kb], kb), 1.0, 0.0)
            return acc + jnp.sum(hit.reshape(tq // COUNT_ROWS, COUNT_ROWS, tq), axis=0)
        acc = lax.fori_loop(0, nkb, body, jnp.zeros((COUNT_ROWS, tq), F32))
        return jnp.sum(acc, axis=0, keepdims=True)

    def bit_step(i, s):
        thr, c_thr = s
        cand = thr + lax.shift_left(jnp.int32(1), 31 - i)
        cnt = count(lambda key, kb: key >= cand)
        ge = cnt >= topk
        return jnp.where(ge, cand, thr), jnp.where(ge, cnt, c_thr)

    n_keys = (nkb * tq).astype(F32)
    thr, c_thr = lax.fori_loop(0, 32, bit_step, (jnp.full((1, tq), INT_MIN, I32), jnp.full((1, tq), 1.0, F32) * n_keys))
    cut_scr[...] = jnp.full((SUBLANES, tq), 2 ** 30, I32)

    @pl.when(jnp.max(c_thr) > topk)
    def _():
        need = topk - count(lambda key, kb: key > thr)

        def row_step(i, lo):
            cand = lo + lax.shift_left(jnp.int32(1), row_bits - 1 - i)
            cnt = count(lambda key, kb: jnp.logical_and(key == thr, r + kb * tq < cand))
            return jnp.where(cnt < need, cand, lo)
        lo = lax.fori_loop(0, row_bits, row_step, jnp.zeros((1, tq), I32))
        cut_scr[...] = jnp.broadcast_to(lo, (SUBLANES, tq))

    cut = cut_scr[0:1, :]

    def select_block(kb, _):
        key = key_scr[kb]
        pos = r + kb * tq
        sel = jnp.logical_or(key > thr, jnp.logical_and(key == thr, pos <= cut))
        sel = jnp.logical_and(sel, pos <= c + qi * tq)
        key_scr[kb] = jnp.where(sel, 0, NEG_BIG_BITS)
        return 0

    lax.fori_loop(0, nkb, select_block, 0)

    qt = qt_ref[...]
    ones = _ones_rows(tq)
    chains = [(h * HEAD_DIM, (h + 1) * HEAD_DIM) for h in range(GROUP_HEADS)]

    def logits(kb):
        return tuple(_chain_logits(k_ref, pl.multiple_of(kb * tq, tq), tq, chains, qt))

    def step(zs, kb, st, which, masked):
        del masked
        mask = pltpu.bitcast(key_scr[kb], F32)
        ms, alphas, ps = [], [], []
        for h, z in enumerate(zs):
            z = z * c_scale + mask
            if which is not None:
                z = z + bias_ref[h, which]
            m_new, alpha, p = _softmax_weights(z, st[h][0])
            ms.append(m_new)
            alphas.append(alpha)
            ps.append(p)
        pv = _dot(_value_blockdiag(vt_ref[kb], ones), jnp.concatenate(ps, axis=0))
        return tuple((ms[h], alphas[h] * st[h][1] + pv[GROUP_W + h:GROUP_W + h + 1, :],
                      alphas[h] * st[h][2] + pv[h * HEAD_DIM:(h + 1) * HEAD_DIM, :]) for h in range(GROUP_HEADS))

    st = _causal_blocks(qi, logits, step, tuple(_softmax_init_t(tq) for _ in range(GROUP_HEADS)))
    o_ref[...] = jnp.concatenate([acc / l for _, l, acc in st], axis=0).T.astype(BF16)


def _dsa(proj, proj_t, bias_tiles):
    bsz, t, _ = proj.shape
    nq, tq = proj_t.shape[1], proj_t.shape[3]
    topk = min(INDEX_TOPK_MAX, t // 4)
    assert tq >= topk, "the threshold search needs at least topk keys in the first block"
    nqx = IDX_HEADS * IDX_DIM
    return pl.pallas_call(
        functools.partial(_dsa_kernel, tq=tq, topk=float(topk), c_scale=HEAD_DIM ** -0.5 * LOG2E,
                          row_bits=max(1, (t - 1).bit_length())),
        grid=(bsz, nq),
        in_specs=[pl.BlockSpec((None, None, nqx, tq), lambda b, i: (b, i, ROW_QX // nqx, 0)),
                  pl.BlockSpec((None, None, IDX_HEADS, tq), lambda b, i: (b, i, ROW_WX // IDX_HEADS, 0)),
                  pl.BlockSpec((None, t, LANES), lambda b, i: (b, 0, COL_TAIL // LANES)),
                  pl.BlockSpec((None, None, GROUP_W, tq), lambda b, i: (b, i, ROW_QC // GROUP_W, 0)),
                  pl.BlockSpec((None, t, GROUP_W), lambda b, i: (b, 0, COL_KC // GROUP_W)),
                  pl.BlockSpec((None, nq, GROUP_W, tq), lambda b, i: (b, 0, ROW_VC // GROUP_W, 0)),
                  pl.BlockSpec(bias_tiles.shape, lambda b, i: (0, 0, 0, 0))],
        out_specs=pl.BlockSpec((None, tq, GROUP_W), lambda b, i: (b, i, 0)),
        out_shape=jax.ShapeDtypeStruct((bsz, t, GROUP_W), BF16),
        scratch_shapes=[pltpu.VMEM((nq, tq, tq), I32), pltpu.VMEM((SUBLANES, tq), I32)],
        compiler_params=_cparams("parallel", "arbitrary"),
        name="dsa",
    )(proj_t, proj_t, proj, proj_t, proj, proj_t, bias_tiles)


def _dil_kernel(q_ref, kp_ref, kd_ref, vp_ref, vd_ref, bias_ref, ol_ref, *, tq, scale):
    qi = pl.program_id(1)
    r = lax.broadcasted_iota(I32, (tq, tq), 0)
    c = lax.broadcasted_iota(I32, (tq, tq), 1)
    prev_ok = jnp.logical_and(r <= c, qi > 0)
    diag_ok = c <= r
    q = q_ref[...]
    zp_all = _dot_nt(q, _by_head(kp_ref[...])) * scale
    zd_all = _dot_nt(q, _by_head(kd_ref[...])) * scale
    pps, pds, dens, lses = [], [], [], []
    for h in range(GROUP_HEADS):
        cols = slice(h * tq, (h + 1) * tq)
        zp = jnp.where(prev_ok, zp_all[:, cols] + bias_ref[h, 1], NEG_BIG)
        zd = jnp.where(diag_ok, zd_all[:, cols] + bias_ref[h, 0], NEG_BIG)
        m = jnp.maximum(jnp.max(zp, axis=1, keepdims=True), jnp.max(zd, axis=1, keepdims=True))
        pp = jnp.exp(zp - m)
        pd = jnp.exp(zd - m)
        den = jnp.sum(pp, axis=1, keepdims=True) + jnp.sum(pd, axis=1, keepdims=True)
        pps.append(pp.astype(BF16))
        pds.append(pd.astype(BF16))
        dens.append(jnp.broadcast_to(den, (tq, HEAD_DIM)))
        lses.append(jnp.broadcast_to(m + jnp.log(den), (tq, HEAD_DIM)))
    values = jnp.concatenate([_by_head(vp_ref[...]), _by_head(vd_ref[...])], axis=0)
    ol_ref[:, :GROUP_W] = _dot(jnp.concatenate(pps + pds, axis=1), values) / jnp.concatenate(dens, axis=-1)
    ol_ref[:, GROUP_W:] = jnp.concatenate(lses, axis=-1)


def _dilated_one(qkv, bias_tiles):
    n, length, _ = qkv.shape
    tq = bias_tiles.shape[-1]
    blk = lambda f: pl.BlockSpec((None, tq, GROUP_W), f)
    prev = lambda col: (lambda b, i: (b, jnp.maximum(i - 1, 0), col))
    cur = lambda col: (lambda b, i: (b, i, col))
    return pl.pallas_call(
        functools.partial(_dil_kernel, tq=tq, scale=HEAD_DIM ** -0.5),
        grid=(n, length // tq),
        in_specs=[blk(cur(0)), blk(prev(1)), blk(cur(1)), blk(prev(2)), blk(cur(2)),
                  pl.BlockSpec(bias_tiles.shape, lambda b, i: (0, 0, 0, 0))],
        out_specs=pl.BlockSpec((None, tq, 2 * GROUP_W), lambda b, i: (b, i, 0)),
        out_shape=jax.ShapeDtypeStruct((n, length, 2 * GROUP_W), F32),
        compiler_params=_cparams("parallel", "arbitrary"),
        name="dil",
    )(qkv, qkv, qkv, qkv, qkv, bias_tiles)


def _dilmix_kernel(p0, p1, p2, out_ref):
    a0, a1, a2 = p0[:, GROUP_W:], p1[:, GROUP_W:], p2[:, GROUP_W:]
    m = jnp.maximum(jnp.maximum(a0, a1), a2)
    e0, e1, e2 = jnp.exp(a0 - m), jnp.exp(a1 - m), jnp.exp(a2 - m)
    mixed = e0 * p0[:, :GROUP_W] + e1 * p1[:, :GROUP_W] + e2 * p2[:, :GROUP_W]
    out_ref[...] = (mixed / (e0 + e1 + e2)).astype(BF16)


def _dilated_mix(patterns):
    bsz, t, w = patterns[0].shape
    tm = min(512, t)
    return pl.pallas_call(
        _dilmix_kernel,
        grid=(bsz, t // tm),
        in_specs=[pl.BlockSpec((None, tm, w), lambda b, i: (b, i, 0))] * 3,
        out_specs=pl.BlockSpec((None, tm, w // 2), lambda b, i: (b, i, 0)),
        out_shape=jax.ShapeDtypeStruct((bsz, t, w // 2), BF16),
        compiler_params=_cparams("parallel", "parallel"),
        name="dilmix",
    )(*patterns)


def _dilated(proj, bias_tiles_per_cfg):
    bsz, t, _ = proj.shape
    qkv = proj[:, :, COL_B:COL_B + 3 * GROUP_W]
    patterns = []
    for (_, dil), tiles in zip(DILATED_CONFIGS, bias_tiles_per_cfg):
        def perm(a, dil=dil):
            w = a.shape[-1]
            return a.reshape(bsz, t // dil, dil, w).transpose(0, 2, 1, 3).reshape(bsz * dil, t // dil, w)

        def unperm(a, dil=dil):
            w = a.shape[-1]
            return a.reshape(bsz, dil, t // dil, w).transpose(0, 2, 1, 3).reshape(bsz, t, w)

        patterns.append(unperm(_dilated_one(perm(qkv), tiles)))
    return _dilated_mix(patterns)


def _post_kernel(oa_ref, ob_ref, oc_ref, od_ref, wo_ref, x_ref, mod_ref, ln_ref, wr_ref, br_ref,
                 x1_ref, h2_ref, idx_ref, gate_ref, *, alpha):
    y = jnp.zeros(x_ref.shape, F32)
    for g, o_ref in enumerate((oa_ref, ob_ref, oc_ref, od_ref)):
        y = y + _dot(o_ref[...], wo_ref[g * GROUP_W:(g + 1) * GROUP_W, :])
    u = alpha * x_ref[...] + (1.0 + mod_ref[2:3, :]) * y
    x1 = _ln(u) * ln_ref[0:1, :] + ln_ref[1:2, :]
    x1_ref[...] = x1
    h2 = _ln(x1) * (1.0 + mod_ref[4:5, :]) + mod_ref[3:4, :]
    _store_token_tiles(h2_ref, 0, h2)
    logits = lax.dot_general(wr_ref[...], h2, (((1,), (1,)), ((), ())), precision=lax.Precision.HIGHEST,
                             preferred_element_type=F32) + br_ref[...]
    n_exp, tm = logits.shape
    eid = lax.broadcasted_iota(I32, (n_exp, tm), 0)
    vals, ids = [], []
    for _ in range(TOP_K):
        m = jnp.max(logits, axis=0, keepdims=True)
        first = jnp.min(jnp.where(logits == m, eid, n_exp), axis=0, keepdims=True)
        vals.append(m)
        ids.append(first)
        logits = jnp.where(eid == first, -jnp.inf, logits)
    ex = [jnp.exp(v - vals[0]) for v in vals]
    den = ex[0] + ex[1] + ex[2] + ex[3]
    zero_f = jnp.zeros((8 - TOP_K, tm), F32)
    gate_ref[...] = jnp.concatenate([e / den for e in ex] + [zero_f], axis=0)
    idx_ref[...] = jnp.concatenate(ids + [zero_f.astype(I32)], axis=0)


def _post_mixer(o_groups, w_out, x, mod, ln_rows, w_router_t, b_router, alpha):
    bsz, t, d = x.shape
    tm = min(512, t)
    n_exp = w_router_t.shape[0]
    og = pl.BlockSpec((None, tm, GROUP_W), lambda b, i: (b, i, 0))
    row = pl.BlockSpec((None, tm, d), lambda b, i: (b, i, 0))
    small = pl.BlockSpec((None, 8, tm), lambda b, i: (b, 0, i))
    assert d == SUBLANES * LANES, "token-tile layout needs one (8,128) tile per token"
    nt = t // tm
    tiles = pl.BlockSpec((tm * SUBLANES, LANES), lambda b, i: (b * nt + i, 0))
    return pl.pallas_call(
        functools.partial(_post_kernel, alpha=alpha),
        grid=(bsz, nt),
        in_specs=[og, og, og, og,
                  pl.BlockSpec(w_out.shape, lambda b, i: (0, 0)),
                  row,
                  pl.BlockSpec((None, 8, d), lambda b, i: (b, 0, 0)),
                  pl.BlockSpec((4, d), lambda b, i: (0, 0)),
                  pl.BlockSpec((n_exp, d), lambda b, i: (0, 0)),
                  pl.BlockSpec((n_exp, 1), lambda b, i: (0, 0))],
        out_specs=[row, tiles, small, small],
        out_shape=[jax.ShapeDtypeStruct((bsz, t, d), F32), jax.ShapeDtypeStruct((bsz * t * SUBLANES, LANES), F32),
                   jax.ShapeDtypeStruct((bsz, 8, t), I32), jax.ShapeDtypeStruct((bsz, 8, t), F32)],
        compiler_params=_cparams("parallel", "parallel"),
        name="post",
    )(*o_groups, w_out, x, mod, ln_rows, w_router_t, b_router.reshape(n_exp, 1))


def _deint_kernel(w_ref, p_ref, o_ref):
    o_ref[...] = _dot(w_ref[...].astype(BF16), p_ref[...]).astype(BF16)


def _deinterleave_w1(w1_all, layer):
    _, n_exp, d, two_f = w1_all.shape
    grp = 2 * FF_GROUP
    j = np.arange(grp)
    src = np.where(j < FF_GROUP, 2 * j, 2 * (j - FF_GROUP) + 1)
    perm = jnp.asarray(np.arange(grp)[:, None] == src[None, :], BF16)
    return pl.pallas_call(
        _deint_kernel,
        grid=(n_exp, two_f // grp),
        in_specs=[pl.BlockSpec((None, None, d, grp), lambda e, g: (layer, e, 0, g)),
                  pl.BlockSpec((grp, grp), lambda e, g: (0, 0))],
        out_specs=pl.BlockSpec((None, d, grp), lambda e, g: (e, 0, g)),
        out_shape=jax.ShapeDtypeStruct((n_exp, d, two_f), BF16),
        compiler_params=_cparams("parallel", "parallel"),
        name="deint",
    )(w1_all, perm)


def _moe_kernel(be_ref, nval_ref, nblk_ref, cur_ref, nxt_ref, h_hbm, w1_ref, b1_ref, w2_ref, b2_ref,
                out_hbm, xbuf, ybuf, gsem, ssem, *, bm):
    i = pl.program_id(0)
    nblk = nblk_ref[0]
    slot = i % 2

    def tile_rows(tok):
        return pl.ds(pl.multiple_of(tok * SUBLANES, SUBLANES), SUBLANES)

    def gather_row_copy(tok, r, s):
        return pltpu.make_async_copy(h_hbm.at[tile_rows(tok)], xbuf.at[tile_rows(s * bm + r)], gsem.at[s])

    def scatter_row_copy(row, r, s):
        return pltpu.make_async_copy(ybuf.at[tile_rows(s * bm + r)], out_hbm.at[tile_rows(row)], ssem.at[s])

    def block_rows(s):
        return pl.ds(pl.multiple_of(s * bm * SUBLANES, bm * SUBLANES), bm * SUBLANES)

    def issue_rows(start_row):
        def chunk(ci, _):
            base = ci * DMA_ISSUE_CHUNK
            for j in range(DMA_ISSUE_CHUNK):
                start_row(base + j, j % 2)
            return 0
        lax.fori_loop(0, bm // DMA_ISSUE_CHUNK, chunk, 0)

    def start_gather(idx_ref, s):
        issue_rows(lambda r, prio: gather_row_copy(idx_ref[0, r], r, s).start(priority=prio))

    def wait_gather(s):
        pltpu.make_async_copy(h_hbm.at[pl.ds(0, bm * SUBLANES)], xbuf.at[block_rows(s)], gsem.at[s]).wait()

    def start_scatter(s, n):
        @pl.when(n == bm)
        def _():
            issue_rows(lambda r, prio: scatter_row_copy(cur_ref[0, bm + r], r, s).start(priority=prio))

        @pl.when(n < bm)
        def _():
            def body(r, _):
                scatter_row_copy(cur_ref[0, bm + r], r, s).start()
                return 0
            lax.fori_loop(0, n, body, 0)

    def wait_scatter(s, n):
        @pl.when(n == bm)
        def _():
            pltpu.make_async_copy(ybuf.at[block_rows(s)], out_hbm.at[pl.ds(0, bm * SUBLANES)], ssem.at[s]).wait()

        @pl.when(n < bm)
        def _():
            def body(r, _):
                scatter_row_copy(0, r, s).wait()
                return 0
            lax.fori_loop(0, n, body, 0)

    @pl.when(jnp.logical_and(i == 0, nblk > 0))
    def _():
        start_gather(cur_ref, 0)

    @pl.when(i + 1 < nblk)
    def _():
        start_gather(nxt_ref, 1 - slot)

    @pl.when(i < nblk)
    def _():
        wait_gather(slot)

        @pl.when(i >= 2)
        def _():
            wait_scatter(slot, nval_ref[jnp.maximum(i - 2, 0)])

        x = _load_token_tiles(xbuf, slot * bm, bm).astype(BF16)
        y = jnp.zeros((bm, w2_ref.shape[1]), F32) + b2_ref[...]
        for g in range(w2_ref.shape[0] // FF_GROUP):
            cols = slice(2 * g * FF_GROUP, 2 * (g + 1) * FF_GROUP)
            hh = _dot(x, w1_ref[:, cols]) + b1_ref[:, cols]
            glu = jnp.minimum(hh[:, :FF_GROUP], SWIGLU_LIMIT)
            lin = jnp.clip(hh[:, FF_GROUP:], -SWIGLU_LIMIT, SWIGLU_LIMIT)
            act = glu * jax.nn.sigmoid(SWIGLU_ALPHA * glu) * (lin + 1.0)
            y = y + _dot(act.astype(BF16), w2_ref[g * FF_GROUP:(g + 1) * FF_GROUP, :])
        _store_token_tiles(ybuf, slot * bm, y)
        start_scatter(slot, nval_ref[i])

    last = pl.num_programs(0) - 1

    @pl.when(jnp.logical_and(i == last, nblk >= 2))
    def _():
        wait_scatter(nblk % 2, nval_ref[jnp.maximum(nblk - 2, 0)])

    @pl.when(jnp.logical_and(i == last, nblk >= 1))
    def _():
        wait_scatter((nblk + 1) % 2, nval_ref[jnp.maximum(nblk - 1, 0)])


def _moe_experts(h2, blk_expert, blk_valid, n_used, slot_idx, w1p, b1p, w2_all, b2, layer):
    n_tok = h2.shape[0] // SUBLANES
    d = SUBLANES * LANES
    n_blocks, _, two_bm = slot_idx.shape
    bm = two_bm // 2
    n_exp, _, two_f = w1p.shape
    f = two_f // 2
    idx_spec = lambda f_: pl.BlockSpec((None, 1, two_bm), f_, memory_space=pltpu.SMEM)
    wspec = lambda shp: pl.BlockSpec((None,) + shp, lambda i, be, nv, nb: (be[i], 0, 0))
    grid_spec = pltpu.PrefetchScalarGridSpec(
        num_scalar_prefetch=3,
        grid=(n_blocks,),
        in_specs=[idx_spec(lambda i, be, nv, nb: (i, 0, 0)),
                  idx_spec(lambda i, be, nv, nb: (jnp.minimum(i + 1, n_blocks - 1), 0, 0)),
                  pl.BlockSpec(memory_space=pl.ANY),
                  wspec((d, two_f)), wspec((1, two_f)),
                  pl.BlockSpec((None, None, f, d), lambda i, be, nv, nb: (layer, be[i], 0, 0)),
                  wspec((1, d))],
        out_specs=pl.BlockSpec(memory_space=pl.ANY),
        scratch_shapes=[pltpu.VMEM((2 * bm * SUBLANES, LANES), F32), pltpu.VMEM((2 * bm * SUBLANES, LANES), F32),
                        pltpu.SemaphoreType.DMA((2,)), pltpu.SemaphoreType.DMA((2,))],
    )
    return pl.pallas_call(
        functools.partial(_moe_kernel, bm=bm),
        grid_spec=grid_spec,
        out_shape=jax.ShapeDtypeStruct((n_tok * TOP_K * SUBLANES, LANES), F32),
        compiler_params=_cparams("arbitrary"),
        name="moe",
    )(blk_expert, blk_valid, n_used, slot_idx, slot_idx, h2, w1p, b1p, w2_all, b2)


def _moe_dispatch(top_idx, bm):
    n_tok = top_idx.shape[0]
    m = n_tok * TOP_K
    assert m % bm == 0
    e_flat = top_idx.reshape(-1)
    experts = jnp.arange(N_EXPERTS, dtype=I32)
    counts = jnp.sum((e_flat[:, None] == experts[None, :]).astype(I32), axis=0)
    padded = (counts + bm - 1) // bm * bm
    pend = jnp.cumsum(padded)
    n_blocks = m // bm + N_EXPERTS
    pad_ok = jnp.arange(bm, dtype=I32)[None, :] < (padded - counts)[:, None]
    pad_key = jnp.where(pad_ok, 2 * experts[:, None] + 1, 2 * N_EXPERTS).reshape(-1)
    keys = jnp.concatenate([2 * e_flat, pad_key])
    vals = jnp.concatenate([jnp.arange(m, dtype=I32), jnp.full((N_EXPERTS * bm,), -1, I32)])
    _, asg = lax.sort((keys, vals), num_keys=1, is_stable=True)
    valid = asg >= 0
    tok = jnp.maximum(asg, 0) // TOP_K
    choice = jnp.maximum(asg, 0) % TOP_K
    slot_idx = jnp.concatenate([tok.reshape(n_blocks, bm), (choice * n_tok + tok).reshape(n_blocks, bm)], axis=1)
    blk_valid = jnp.sum(valid.reshape(n_blocks, bm).astype(I32), axis=1)
    blk_start = jnp.arange(n_blocks, dtype=I32) * bm
    blk_expert = jnp.minimum(jnp.sum((blk_start[:, None] >= pend[None, :]).astype(I32), axis=1), N_EXPERTS - 1)
    n_used = (pend[-1] // bm).astype(I32).reshape(1)
    return blk_expert, blk_valid, n_used, slot_idx.reshape(n_blocks, 1, 2 * bm)


def _comb_kernel(y4_ref, gate_ref, x_ref, mod_ref, ln_ref, o_ref, *, alpha):
    y = jnp.zeros(x_ref.shape, F32)
    for k in range(TOP_K):
        y = y + gate_ref[:, k:k + 1] * _load_token_tiles(y4_ref.at[k], 0, x_ref.shape[0])
    u = alpha * x_ref[...] + (1.0 + mod_ref[5:6, :]) * y
    o_ref[...] = _ln(u) * ln_ref[2:3, :] + ln_ref[3:4, :]


def _combine(y4, gate, x1, mod, ln_rows, alpha):
    bsz, t, d = x1.shape
    tm = min(512, t)
    nt = t // tm
    return pl.pallas_call(
        functools.partial(_comb_kernel, alpha=alpha),
        grid=(bsz, nt),
        in_specs=[pl.BlockSpec((TOP_K, tm * SUBLANES, LANES), lambda b, i: (0, b * nt + i, 0)),
                  pl.BlockSpec((None, tm, 8), lambda b, i: (b, i, 0)),
                  pl.BlockSpec((None, tm, d), lambda b, i: (b, i, 0)),
                  pl.BlockSpec((None, 8, d), lambda b, i: (b, 0, 0)),
                  pl.BlockSpec((4, d), lambda b, i: (0, 0))],
        out_specs=pl.BlockSpec((None, tm, d), lambda b, i: (b, i, 0)),
        out_shape=jax.ShapeDtypeStruct((bsz, t, d), F32),
        compiler_params=_cparams("parallel", "parallel"),
        name="comb",
    )(y4, gate, x1, mod, ln_rows)


def _rel_bucket(dist):
    n = jnp.maximum(dist, 0)
    max_exact = N_BUCKETS // 2
    nf = jnp.maximum(n, 1).astype(F32)
    large = max_exact + (jnp.log(nf / max_exact) / math.log(MAX_DISTANCE / max_exact)
                         * (N_BUCKETS - max_exact)).astype(I32)
    large = jnp.minimum(large, N_BUCKETS - 1)
    return jnp.where(n < max_exact, n, large)


def _bias_tiles(bias_tab, tq, dil, key_major):
    period = 2 * tq
    k = np.arange(period)
    d = np.where(k < tq, -k, period - k)
    dist = np.stack([np.maximum(d, 0), d + tq]) * dil
    line = bias_tab.astype(F32)[_rel_bucket(jnp.asarray(dist, I32))]
    line = jnp.moveaxis(line, -1, 0)
    flat = jnp.tile(line, (1, 1, tq))[..., :tq * (period - 1)]
    tiles = flat.reshape(line.shape[0], 2, tq, period - 1)[..., :tq]
    if key_major:
        tiles = (tiles - bias_tab.astype(F32)[N_BUCKETS - 1][:, None, None, None]) * LOG2E
        tiles = jnp.swapaxes(tiles, -1, -2)
    return tiles


def _split_w_in(w_in):
    d = w_in.shape[0]
    g = GROUP_W
    segs = {}
    o = 0
    for name, width in (("qa", g), ("ka", g), ("va", g), ("qb", g), ("kb", g), ("vb", g),
                        ("qc", g), ("kc", g), ("vc", g), ("qx", IDX_HEADS * IDX_DIM), ("kx", IDX_DIM),
                        ("wx", IDX_HEADS), ("qd", g), ("kd", g), ("vd", g)):
        segs[name] = w_in[:, o:o + width]
        o += width
    pad = jnp.zeros((d, LANES - IDX_DIM - IDX_HEADS), w_in.dtype)
    w = jnp.concatenate([segs[n] for n in ("qa", "ka", "va", "qb", "kb", "vb", "kc", "kd", "kx", "wx")] + [pad],
                        axis=1)
    wt = jnp.concatenate([segs[n] for n in ("qx", "qc", "vc", "qd", "vd", "wx")], axis=1).T
    return w.astype(BF16), wt.astype(BF16)


def _layer(x, c, layer, depth, p, tiles):
    bsz, t, d = x.shape
    alpha = (2 * depth) ** 0.25
    mod = _ada_mod(c, p["w_ada_all"], p["b_ada_all"], layer).reshape(bsz, 6, d)
    mod = jnp.concatenate([mod, jnp.zeros((bsz, 2, d), F32)], axis=1)
    ln_rows = jnp.concatenate([p["ln_g"][0:1], p["ln_b"][0:1], p["ln_g"][1:2], p["ln_b"][1:2]], axis=0)

    proj, proj_t = _ln_mod_proj(x, mod, *_split_w_in(p["w_in"]))

    o_a = _stick_breaking(proj)
    o_b = _dilated(proj, tiles["b"])
    o_c = _dsa(proj, proj_t, tiles["c"])
    lamp = p["diff_lam"].astype(F32)
    lambda_init = 0.8 - 0.6 * math.exp(-0.3 * layer)
    lam = jnp.exp(jnp.sum(lamp[0] * lamp[1])) - jnp.exp(jnp.sum(lamp[2] * lamp[3])) + lambda_init
    o_d = _differential(proj, proj_t, lam, tiles["d"], p["diff_g"], lambda_init)

    x1, h2, top_idx, gate = _post_mixer((o_a, o_b, o_c, o_d), p["w_out"].astype(BF16), x, mod, ln_rows,
                                        p["w_router"].T, p["b_router"], alpha)

    top_idx = top_idx[:, :TOP_K, :].transpose(0, 2, 1).reshape(bsz * t, TOP_K)
    blk_expert, blk_valid, n_used, slot_idx = _moe_dispatch(top_idx, MOE_BLOCK)
    n_grp = p["b1"].shape[-1] // (2 * FF_GROUP)
    b1p = p["b1"].reshape(N_EXPERTS, n_grp, FF_GROUP, 2).transpose(0, 1, 3, 2).reshape(N_EXPERTS, 1, -1)
    y_rows = _moe_experts(h2, blk_expert, blk_valid, n_used, slot_idx,
                          _deinterleave_w1(p["w1_all"], layer), b1p, p["w2_all"], p["b2"][:, None, :], layer)
    return _combine(y_rows.reshape(TOP_K, bsz * t * SUBLANES, LANES), gate.transpose(0, 2, 1), x1, mod, ln_rows,
                    alpha)


def kernel(x, c, w_ada, b_ada, w_in, w_out, diff_lam, diff_g, ln_g, ln_b, w_router, b_router, w1, b1, w2, b2,
           rel_bias):
    depth = w_in.shape[0]
    t = x.shape[1]
    tq = min(ATT_BLOCK, t)
    tiles = dict(
        b=[_bias_tiles(rel_bias[:, :GROUP_HEADS], min(128, t // dil), dil, False) for _, dil in DILATED_CONFIGS],
        c=_bias_tiles(rel_bias[:, GROUP_HEADS:2 * GROUP_HEADS], tq, 1, True),
        d=_bias_tiles(rel_bias[:, 2 * GROUP_HEADS:], tq, 1, True))
    w2_all = w2.astype(BF16)
    for layer in range(depth):
        p = dict(w_ada_all=w_ada, b_ada_all=b_ada, w_in=w_in[layer], w_out=w_out[layer],
                 diff_lam=diff_lam[layer], diff_g=diff_g[layer], ln_g=ln_g[layer], ln_b=ln_b[layer],
                 w_router=w_router[layer], b_router=b_router[layer], w1_all=w1, b1=b1[layer],
                 w2_all=w2_all, b2=b2[layer])
        x = _layer(x, c, layer, depth, p, tiles)
    return x
```

```python
import functools
import math

import numpy as np
import jax
import jax.numpy as jnp
from jax import lax
from jax.experimental import pallas as pl
from jax.experimental.pallas import tpu as pltpu

F32 = jnp.float32
BF16 = jnp.bfloat16
I32 = jnp.int32

HEAD_DIM = 64
GROUP_HEADS = 4
GROUP_W = GROUP_HEADS * HEAD_DIM
DIFF_QK_DIM = HEAD_DIM // 2
DILATED_CONFIGS = ((128, 1), (512, 4), (2048, 16))
IDX_HEADS = 16
IDX_DIM = 64
INDEX_TOPK_MAX = 256
N_EXPERTS = 32
TOP_K = 4
SWIGLU_ALPHA = 1.702
SWIGLU_LIMIT = 7.0
N_BUCKETS = 32
MAX_DISTANCE = 128
LN_EPS = 1e-5
MOE_BLOCK = 512
DMA_ISSUE_CHUNK = 64

LANES = 128
SUBLANES = 8
VMEM_LIMIT_BYTES = 56 * 1024 * 1024
NEG_BIG = -1e30
SB_SKIP_LOG = -100.0
INT_MIN = -2 ** 31
NEG_BIG_BITS = int(np.float32(NEG_BIG).view(np.int32))
LOG2E = math.log2(math.e)
ATT_BLOCK = 256
COUNT_ROWS = 4 * SUBLANES
DIL_SUB_BLOCKS = 2
FF_GROUP = 256

COL_A = 0
COL_B = COL_A + 3 * GROUP_W
COL_KC = COL_B + 3 * GROUP_W
COL_KD = COL_KC + GROUP_W
COL_TAIL = COL_KD + GROUP_W
PROJ_COLS = COL_TAIL + LANES
ROW_QX = 0
ROW_QC = ROW_QX + IDX_HEADS * IDX_DIM
ROW_VC = ROW_QC + GROUP_W
ROW_QD = ROW_VC + GROUP_W
ROW_VD = ROW_QD + GROUP_W
ROW_WX = ROW_VD + GROUP_W
PROJ_ROWS = ROW_WX + IDX_HEADS


def _cparams(*sem):
    return pltpu.CompilerParams(dimension_semantics=sem, vmem_limit_bytes=VMEM_LIMIT_BYTES)


def _ln(x):
    mu = jnp.mean(x, axis=-1, keepdims=True)
    xc = x - mu
    return xc * lax.rsqrt(jnp.mean(xc * xc, axis=-1, keepdims=True) + LN_EPS)


def _dot_nt(a, b):
    return lax.dot_general(a, b, (((1,), (1,)), ((), ())), preferred_element_type=F32)


def _dot(a, b):
    return jnp.dot(a, b, preferred_element_type=F32)


def _load_token_tiles(ref, first_tok, n):
    base = first_tok * SUBLANES
    return jnp.concatenate([ref[pl.ds(base + c, n, stride=SUBLANES), :] for c in range(SUBLANES)], axis=1)


def _store_token_tiles(ref, first_tok, val):
    n = val.shape[0]
    base = first_tok * SUBLANES
    for c in range(SUBLANES):
        ref[pl.ds(base + c, n, stride=SUBLANES), :] = val[:, c * LANES:(c + 1) * LANES]


def _ada_kernel(c_ref, w_ref, b_ref, o_ref):
    o_ref[...] = jnp.dot(c_ref[...], w_ref[...], precision=lax.Precision.HIGHEST,
                         preferred_element_type=F32) + b_ref[...]


def _ada_mod(c, w_all, b_all, layer):
    bsz, d = c.shape
    depth, _, n = w_all.shape
    return pl.pallas_call(
        _ada_kernel,
        grid=(n // d,),
        in_specs=[pl.BlockSpec((bsz, d), lambda j: (0, 0)),
                  pl.BlockSpec((None, d, d), lambda j: (layer, 0, j)),
                  pl.BlockSpec((None, 1, d), lambda j: (layer, 0, j))],
        out_specs=pl.BlockSpec((bsz, d), lambda j: (0, j)),
        out_shape=jax.ShapeDtypeStruct((bsz, n), F32),
        compiler_params=_cparams("arbitrary"),
        name="ada",
    )(c, w_all, b_all.reshape(depth, 1, n))


def _proj_kernel(x_ref, mod_ref, w_ref, wt_ref, o_ref, ot_ref, *, chunk):
    h = (_ln(x_ref[...]) * (1.0 + mod_ref[1:2, :]) + mod_ref[0:1, :]).astype(BF16)
    ncol = o_ref.shape[-1]
    for c0 in range(0, ncol, chunk):
        c1 = min(c0 + chunk, ncol)
        o_ref[:, c0:c1] = _dot(h, w_ref[:, c0:c1]).astype(BF16)
    n_sub, nrow, tq = ot_ref.shape
    for r0 in range(0, nrow, chunk):
        r1 = min(r0 + chunk, nrow)
        res = _dot_nt(wt_ref[r0:r1, :], h).astype(BF16)
        for j in range(n_sub):
            ot_ref[j, r0:r1, :] = res[:, j * tq:(j + 1) * tq]


def _ln_mod_proj(x, mod, w, wt):
    bsz, t, d = x.shape
    ncol, nrow = w.shape[1], wt.shape[0]
    tq = min(ATT_BLOCK, t)
    tm = min(2 * tq, t)
    return pl.pallas_call(
        functools.partial(_proj_kernel, chunk=2 * LANES),
        grid=(bsz, t // tm),
        in_specs=[pl.BlockSpec((None, tm, d), lambda b, i: (b, i, 0)),
                  pl.BlockSpec((None, 8, d), lambda b, i: (b, 0, 0)),
                  pl.BlockSpec((d, ncol), lambda b, i: (0, 0)),
                  pl.BlockSpec((nrow, d), lambda b, i: (0, 0))],
        out_specs=[pl.BlockSpec((None, tm, ncol), lambda b, i: (b, i, 0)),
                   pl.BlockSpec((None, tm // tq, nrow, tq), lambda b, i: (b, i, 0, 0))],
        out_shape=[jax.ShapeDtypeStruct((bsz, t, ncol), BF16),
                   jax.ShapeDtypeStruct((bsz, t // tq, nrow, tq), BF16)],
        compiler_params=_cparams("parallel", "parallel"),
        name="proj",
    )(x, mod, w, wt)


def _by_head(x):
    lane = lax.broadcasted_iota(I32, x.shape, 1)
    zero = jnp.zeros_like(x)
    return jnp.concatenate([jnp.where(jnp.logical_and(lane >= h * HEAD_DIM, lane < (h + 1) * HEAD_DIM), x, zero)
                            for h in range(GROUP_HEADS)], axis=0)


def _sb_kernel(q_ref, k_ref, v_ref, o_ref, *, tq, scale):
    qi = pl.program_id(1)
    r = lax.broadcasted_iota(I32, (tq, tq), 0)
    c = lax.broadcasted_iota(I32, (tq, tq), 1)
    strict_lower = c < r
    upper = jnp.where(r > c, 1.0, 0.0).astype(BF16)

    q = q_ref[...]
    valid = jnp.concatenate([strict_lower] * GROUP_HEADS, axis=1)

    def step(kb, carry, acc, masked):
        start = pl.multiple_of(kb * tq, tq)
        z = _dot_nt(q, _by_head(k_ref[pl.ds(start, tq), :])) * scale
        log_sig = jnp.minimum(z, 0.0) - jnp.log(1.0 + jnp.exp(-jnp.abs(z)))
        log_fail = log_sig - z
        if masked:
            log_fail = jnp.where(valid, log_fail, 0.0)
        lf = jnp.concatenate([log_fail[:, h * tq:(h + 1) * tq] for h in range(GROUP_HEADS)], axis=0)
        lf_hi = lf.astype(BF16)
        lf_lo = (lf - lf_hi.astype(F32)).astype(BF16)
        after = _dot(lf_hi, upper) + _dot(lf_lo, upper)
        after = jnp.concatenate([after[h * tq:(h + 1) * tq] + carry[h] for h in range(GROUP_HEADS)], axis=1)
        w = jnp.exp(log_sig + after)
        if masked:
            w = jnp.where(valid, w, 0.0)
        acc = acc + _dot(w.astype(BF16), _by_head(v_ref[pl.ds(start, tq), :]))
        carry = tuple(carry[h] + jnp.sum(log_fail[:, h * tq:(h + 1) * tq], axis=1, keepdims=True)
                      for h in range(GROUP_HEADS))
        return carry, acc

    def live(carry):
        top = carry[0]
        for h in range(1, GROUP_HEADS):
            top = jnp.maximum(top, carry[h])
        return (jnp.max(top) > SB_SKIP_LOG).astype(I32)

    carry, acc = step(qi, tuple(jnp.zeros((tq, 1), F32) for _ in range(GROUP_HEADS)),
                      jnp.zeros((tq, GROUP_W), F32), True)

    def body(s):
        kb, carry, acc, _ = s
        carry, acc = step(kb, carry, acc, False)
        return kb - 1, carry, acc, live(carry)

    _, _, acc, _ = lax.while_loop(lambda s: jnp.logical_and(s[0] >= 0, s[3] > 0), body,
                                  (qi - 1, carry, acc, live(carry)))
    o_ref[...] = acc.astype(BF16)


def _stick_breaking(proj):
    bsz, t, _ = proj.shape
    tq = min(ATT_BLOCK, t)
    cb = COL_A // GROUP_W
    return pl.pallas_call(
        functools.partial(_sb_kernel, tq=tq, scale=HEAD_DIM ** -0.5),
        grid=(bsz, t // tq),
        in_specs=[pl.BlockSpec((None, tq, GROUP_W), lambda b, i: (b, i, cb)),
                  pl.BlockSpec((None, t, GROUP_W), lambda b, i: (b, 0, cb + 1)),
                  pl.BlockSpec((None, t, GROUP_W), lambda b, i: (b, 0, cb + 2))],
        out_specs=pl.BlockSpec((None, tq, GROUP_W), lambda b, i: (b, i, 0)),
        out_shape=jax.ShapeDtypeStruct((bsz, t, GROUP_W), BF16),
        compiler_params=_cparams("parallel", "arbitrary"),
        name="sb",
    )(proj, proj, proj)


ONES_ROWS = 16


def _softmax_init_t(tq):
    return (jnp.full((1, tq), NEG_BIG, F32), jnp.zeros((1, tq), F32), jnp.zeros((HEAD_DIM, tq), F32))


def _chain_logits(k_ref, start, tk, chains, qt):
    outs = [None] * len(chains)
    for half in range(GROUP_W // LANES):
        idx = [i for i, (lo, hi) in enumerate(chains) if lo // LANES == half]
        assert all((chains[i][1] - 1) // LANES == half for i in idx)
        khalf = k_ref[pl.ds(start, tk), half * LANES:(half + 1) * LANES]
        lane = lax.broadcasted_iota(I32, khalf.shape, 1) + half * LANES
        zero = jnp.zeros_like(khalf)
        lhs = jnp.concatenate([jnp.where(jnp.logical_and(lane >= chains[i][0], lane < chains[i][1]), khalf, zero)
                               for i in idx], axis=0)
        z = _dot(lhs, qt[half * LANES:(half + 1) * LANES, :])
        for j, i in enumerate(idx):
            outs[i] = z[j * tk:(j + 1) * tk]
    return outs


def _ones_rows(tk):
    r = lax.broadcasted_iota(I32, (ONES_ROWS, GROUP_HEADS * tk), 0)
    c = lax.broadcasted_iota(I32, (ONES_ROWS, GROUP_HEADS * tk), 1)
    lo = r * tk
    return jnp.where(jnp.logical_and(c >= lo, c < lo + tk), 1.0, 0.0).astype(BF16)


def _value_blockdiag(vblk, ones):
    tk = vblk.shape[1]
    zero = jnp.zeros((HEAD_DIM, tk), BF16)
    rows = [jnp.concatenate([vblk[h * HEAD_DIM:(h + 1) * HEAD_DIM, :] if j == h else zero
                             for j in range(GROUP_HEADS)], axis=1) for h in range(GROUP_HEADS)]
    return jnp.concatenate(rows + [ones], axis=0)


def _softmax_weights(z, m):
    m_new = jnp.maximum(m, jnp.max(z, axis=0, keepdims=True))
    return m_new, jnp.exp2(m - m_new), jnp.exp2(z - m_new).astype(BF16)


def _causal_blocks(qi, logits, step, st):
    st = lax.fori_loop(0, jnp.maximum(qi - 1, 0), lambda kb, s: step(logits(kb), kb, s, None, False), st)
    st = lax.cond(qi >= 1, lambda s: step(logits(qi - 1), qi - 1, s, 1, False), lambda s: s, st)
    return step(logits(qi), qi, st, 0, True)


def _diff_kernel(lam_ref, qt_ref, k_ref, vt_ref, bias_ref, g_ref, o_ref, *, tq, c_scale, out_scale):
    qi = pl.program_id(1)
    r = lax.broadcasted_iota(I32, (tq, tq), 0)
    c = lax.broadcasted_iota(I32, (tq, tq), 1)
    causal = r <= c
    lam = lam_ref[0]
    qt = qt_ref[...]
    ones = _ones_rows(tq)
    chains = [(h * HEAD_DIM + j * DIFF_QK_DIM, h * HEAD_DIM + (j + 1) * DIFF_QK_DIM)
              for h in range(GROUP_HEADS) for j in range(2)]

    def logits(kb):
        return tuple(_chain_logits(k_ref, pl.multiple_of(kb * tq, tq), tq, chains, qt))

    def step(zs, kb, st, which, masked):
        ms, alphas, ps = [], [], []
        for i, z in enumerate(zs):
            z = z * c_scale
            if which is not None:
                z = z + bias_ref[i // 2, which]
            if masked:
                z = jnp.where(causal, z, NEG_BIG)
            m_new, alpha, p = _softmax_weights(z, st[i][0])
            ms.append(m_new)
            alphas.append(alpha)
            ps.append(p)
        pmat = jnp.concatenate([jnp.concatenate([ps[2 * h], ps[2 * h + 1]], axis=1) for h in range(GROUP_HEADS)],
                               axis=0)
        pv = _dot(_value_blockdiag(vt_ref[kb], ones), pmat)
        out = []
        for i in range(len(chains)):
            h, j = divmod(i, 2)
            cols = slice(j * tq, (j + 1) * tq)
            l = alphas[i] * st[i][1] + pv[GROUP_W + h:GROUP_W + h + 1, cols]
            acc = alphas[i] * st[i][2] + pv[h * HEAD_DIM:(h + 1) * HEAD_DIM, cols]
            out.append((ms[i], l, acc))
        return tuple(out)

    st = _causal_blocks(qi, logits, step, tuple(_softmax_init_t(tq) for _ in chains))
    outs = []
    for h in range(GROUP_HEADS):
        (_, l1, a1), (_, l2, a2) = st[2 * h], st[2 * h + 1]
        o = a1 / l1 - lam * (a2 / l2)
        o = o * lax.rsqrt(jnp.mean(o * o, axis=0, keepdims=True) + LN_EPS)
        outs.append(o * g_ref[...] * out_scale)
    o_ref[...] = jnp.concatenate(outs, axis=0).T.astype(BF16)


def _differential(proj, proj_t, lam, bias_tiles, diff_g, lambda_init):
    bsz, t, _ = proj.shape
    nq, tq = proj_t.shape[1], proj_t.shape[3]
    grid_spec = pltpu.PrefetchScalarGridSpec(
        num_scalar_prefetch=1,
        grid=(bsz, nq),
        in_specs=[pl.BlockSpec((None, None, GROUP_W, tq), lambda b, i, lam: (b, i, ROW_QD // GROUP_W, 0)),
                  pl.BlockSpec((None, t, GROUP_W), lambda b, i, lam: (b, 0, COL_KD // GROUP_W)),
                  pl.BlockSpec((None, nq, GROUP_W, tq), lambda b, i, lam: (b, 0, ROW_VD // GROUP_W, 0)),
                  pl.BlockSpec(bias_tiles.shape, lambda b, i, lam: (0, 0, 0, 0)),
                  pl.BlockSpec((HEAD_DIM, 1), lambda b, i, lam: (0, 0))],
        out_specs=pl.BlockSpec((None, tq, GROUP_W), lambda b, i, lam: (b, i, 0)),
    )
    return pl.pallas_call(
        functools.partial(_diff_kernel, tq=tq, c_scale=DIFF_QK_DIM ** -0.5 * LOG2E, out_scale=1.0 - lambda_init),
        grid_spec=grid_spec,
        out_shape=jax.ShapeDtypeStruct((bsz, t, GROUP_W), BF16),
        compiler_params=_cparams("parallel", "arbitrary"),
        name="diff",
    )(lam.reshape(1).astype(F32), proj_t, proj, proj_t, bias_tiles, diff_g.reshape(HEAD_DIM, 1).astype(F32))


def _dsa_kernel(qx_ref, wx_ref, tail_ref, qt_ref, k_ref, vt_ref, bias_ref, o_ref, key_scr, cut_scr,
                *, tq, topk, c_scale, row_bits):
    qi = pl.program_id(1)
    nkb = qi + 1
    r = lax.broadcasted_iota(I32, (tq, tq), 0)
    c = lax.broadcasted_iota(I32, (tq, tq), 1)
    wx = wx_ref[...].astype(F32) * (IDX_HEADS ** -0.5 * IDX_DIM ** -0.5)
    zpad = jnp.zeros((LANES - IDX_DIM, tq), BF16)
    qx = [jnp.concatenate([qx_ref[h * IDX_DIM:(h + 1) * IDX_DIM, :], zpad], axis=0) for h in range(IDX_HEADS)]

    def score_block(kb, _):
        start = pl.multiple_of(kb * tq, tq)
        kt = tail_ref[pl.ds(start, tq), :]
        s = jnp.zeros((tq, tq), F32)
        for h in range(IDX_HEADS):
            s = s + wx[h:h + 1, :] * jnp.maximum(_dot(kt, qx[h]), 0.0)
        s = jnp.where(s == 0.0, 0.0, s)
        s = jnp.where(r + kb * tq <= c + qi * tq, s, -jnp.inf)
        bits = pltpu.bitcast(s, I32)
        key_scr[kb] = bits ^ ((bits >> 31) & 0x7FFFFFFF)
        return 0

    lax.fori_loop(0, nkb, score_block, 0)

    def count(pred):
        def body(kb, acc):
            hit = jnp.where(pred(key_scr[kb], kb), 1.0, 0.0)
            return acc + jnp.sum(hit.reshape(tq // COUNT_ROWS, COUNT_ROWS, tq), axis=0)
        acc = lax.fori_loop(0, nkb, body, jnp.zeros((COUNT_ROWS, tq), F32))
        return jnp.sum(acc, axis=0, keepdims=True)

    def bit_step(i, s):
        thr, c_thr = s
        cand = thr + lax.shift_left(jnp.int32(1), 31 - i)
        cnt = count(lambda key, kb: key >= cand)
        ge = cnt >= topk
        return jnp.where(ge, cand, thr), jnp.where(ge, cnt, c_thr)

    n_keys = (nkb * tq).astype(F32)
    thr, c_thr = lax.fori_loop(0, 32, bit_step, (jnp.full((1, tq), INT_MIN, I32), jnp.full((1, tq), 1.0, F32) * n_keys))
    cut_scr[...] = jnp.full((SUBLANES, tq), 2 ** 30, I32)

    @pl.when(jnp.max(c_thr) > topk)
    def _():
        need = topk - count(lambda key, kb: key > thr)

        def row_step(i, lo):
            cand = lo + lax.shift_left(jnp.int32(1), row_bits - 1 - i)
            cnt = count(lambda key, kb: jnp.logical_and(key == thr, r + kb * tq < cand))
            return jnp.where(cnt < need, cand, lo)
        lo = lax.fori_loop(0, row_bits, row_step, jnp.zeros((1, tq), I32))
        cut_scr[...] = jnp.broadcast_to(lo, (SUBLANES, tq))

    cut = cut_scr[0:1, :]

    def select_block(kb, _):
        key = key_scr[kb]
        pos = r + kb * tq
        sel = jnp.logical_or(key > thr, jnp.logical_and(key == thr, pos <= cut))
        sel = jnp.logical_and(sel, pos <= c + qi * tq)
        key_scr[kb] = jnp.where(sel, 0, NEG_BIG_BITS)
        return 0

    lax.fori_loop(0, nkb, select_block, 0)

    qt = qt_ref[...]
    ones = _ones_rows(tq)
    chains = [(h * HEAD_DIM, (h + 1) * HEAD_DIM) for h in range(GROUP_HEADS)]

    def logits(kb):
        return tuple(_chain_logits(k_ref, pl.multiple_of(kb * tq, tq), tq, chains, qt))

    def step(zs, kb, st, which, masked):
        del masked
        mask = pltpu.bitcast(key_scr[kb], F32)
        ms, alphas, ps = [], [], []
        for h, z in enumerate(zs):
            z = z * c_scale + mask
            if which is not None:
                z = z + bias_ref[h, which]
            m_new, alpha, p = _softmax_weights(z, st[h][0])
            ms.append(m_new)
            alphas.append(alpha)
            ps.append(p)
        pv = _dot(_value_blockdiag(vt_ref[kb], ones), jnp.concatenate(ps, axis=0))
        return tuple((ms[h], alphas[h] * st[h][1] + pv[GROUP_W + h:GROUP_W + h + 1, :],
                      alphas[h] * st[h][2] + pv[h * HEAD_DIM:(h + 1) * HEAD_DIM, :]) for h in range(GROUP_HEADS))

    st = _causal_blocks(qi, logits, step, tuple(_softmax_init_t(tq) for _ in range(GROUP_HEADS)))
    o_ref[...] = jnp.concatenate([acc / l for _, l, acc in st], axis=0).T.astype(BF16)


def _dsa(proj, proj_t, bias_tiles):
    bsz, t, _ = proj.shape
    nq, tq = proj_t.shape[1], proj_t.shape[3]
    topk = min(INDEX_TOPK_MAX, t // 4)
    assert tq >= topk, "the threshold search needs at least topk keys in the first block"
    nqx = IDX_HEADS * IDX_DIM
    return pl.pallas_call(
        functools.partial(_dsa_kernel, tq=tq, topk=float(topk), c_scale=HEAD_DIM ** -0.5 * LOG2E,
                          row_bits=max(1, (t - 1).bit_length())),
        grid=(bsz, nq),
        in_specs=[pl.BlockSpec((None, None, nqx, tq), lambda b, i: (b, i, ROW_QX // nqx, 0)),
                  pl.BlockSpec((None, None, IDX_HEADS, tq), lambda b, i: (b, i, ROW_WX // IDX_HEADS, 0)),
                  pl.BlockSpec((None, t, LANES), lambda b, i: (b, 0, COL_TAIL // LANES)),
                  pl.BlockSpec((None, None, GROUP_W, tq), lambda b, i: (b, i, ROW_QC // GROUP_W, 0)),
                  pl.BlockSpec((None, t, GROUP_W), lambda b, i: (b, 0, COL_KC // GROUP_W)),
                  pl.BlockSpec((None, nq, GROUP_W, tq), lambda b, i: (b, 0, ROW_VC // GROUP_W, 0)),
                  pl.BlockSpec(bias_tiles.shape, lambda b, i: (0, 0, 0, 0))],
        out_specs=pl.BlockSpec((None, tq, GROUP_W), lambda b, i: (b, i, 0)),
        out_shape=jax.ShapeDtypeStruct((bsz, t, GROUP_W), BF16),
        scratch_shapes=[pltpu.VMEM((nq, tq, tq), I32), pltpu.VMEM((SUBLANES, tq), I32)],
        compiler_params=_cparams("parallel", "arbitrary"),
        name="dsa",
    )(proj_t, proj_t, proj, proj_t, proj, proj_t, bias_tiles)


def _dil_kernel(q_ref, kp_ref, kc_ref, vp_ref, vc_ref, bias_ref, ol_ref, *, tq, scale):
    qi = pl.program_id(1)
    r = lax.broadcasted_iota(I32, (tq, tq), 0)
    c = lax.broadcasted_iota(I32, (tq, tq), 1)
    diag_ok = c <= r
    for sb in range(q_ref.shape[0] // tq):
        rows = slice(sb * tq, (sb + 1) * tq)
        if sb == 0:
            kp, vp = kp_ref[...], vp_ref[...]
            prev_ok = jnp.logical_and(r <= c, qi > 0)
        else:
            before = slice((sb - 1) * tq, sb * tq)
            kp, vp = kc_ref[before, :], vc_ref[before, :]
            prev_ok = r <= c
        q = q_ref[rows, :]
        zp_all = _dot_nt(q, _by_head(kp)) * scale
        zd_all = _dot_nt(q, _by_head(kc_ref[rows, :])) * scale
        pps, pds, dens, lses = [], [], [], []
        for h in range(GROUP_HEADS):
            cols = slice(h * tq, (h + 1) * tq)
            zp = jnp.where(prev_ok, zp_all[:, cols] + bias_ref[h, 1], NEG_BIG)
            zd = jnp.where(diag_ok, zd_all[:, cols] + bias_ref[h, 0], NEG_BIG)
            m = jnp.maximum(jnp.max(zp, axis=1, keepdims=True), jnp.max(zd, axis=1, keepdims=True))
            pp = jnp.exp(zp - m)
            pd = jnp.exp(zd - m)
            den = jnp.sum(pp, axis=1, keepdims=True) + jnp.sum(pd, axis=1, keepdims=True)
            pps.append(pp.astype(BF16))
            pds.append(pd.astype(BF16))
            dens.append(jnp.broadcast_to(den, (tq, HEAD_DIM)))
            lses.append(jnp.broadcast_to(m + jnp.log(den), (tq, HEAD_DIM)))
        values = jnp.concatenate([_by_head(vp), _by_head(vc_ref[rows, :])], axis=0)
        ol_ref[rows, :GROUP_W] = _dot(jnp.concatenate(pps + pds, axis=1), values) / jnp.concatenate(dens, axis=-1)
        ol_ref[rows, GROUP_W:] = jnp.concatenate(lses, axis=-1)


def _dilated_one(qkv, bias_tiles):
    n, length, _ = qkv.shape
    tq = bias_tiles.shape[-1]
    nsub = DIL_SUB_BLOCKS if length % (DIL_SUB_BLOCKS * tq) == 0 else 1
    tile = nsub * tq
    wide = lambda col: pl.BlockSpec((None, tile, GROUP_W), lambda b, i: (b, i, col))
    back = lambda col: pl.BlockSpec((None, tq, GROUP_W), lambda b, i: (b, jnp.maximum(i * nsub - 1, 0), col))
    return pl.pallas_call(
        functools.partial(_dil_kernel, tq=tq, scale=HEAD_DIM ** -0.5),
        grid=(n, length // tile),
        in_specs=[wide(0), back(1), wide(1), back(2), wide(2),
                  pl.BlockSpec(bias_tiles.shape, lambda b, i: (0, 0, 0, 0))],
        out_specs=pl.BlockSpec((None, tile, 2 * GROUP_W), lambda b, i: (b, i, 0)),
        out_shape=jax.ShapeDtypeStruct((n, length, 2 * GROUP_W), F32),
        compiler_params=_cparams("parallel", "arbitrary"),
        name="dil",
    )(qkv, qkv, qkv, qkv, qkv, bias_tiles)


def _dilmix_kernel(p0, p1, p2, out_ref):
    a0, a1, a2 = p0[:, GROUP_W:], p1[:, GROUP_W:], p2[:, GROUP_W:]
    m = jnp.maximum(jnp.maximum(a0, a1), a2)
    e0, e1, e2 = jnp.exp(a0 - m), jnp.exp(a1 - m), jnp.exp(a2 - m)
    mixed = e0 * p0[:, :GROUP_W] + e1 * p1[:, :GROUP_W] + e2 * p2[:, :GROUP_W]
    out_ref[...] = (mixed / (e0 + e1 + e2)).astype(BF16)


def _dilated_mix(patterns):
    bsz, t, w = patterns[0].shape
    tm = min(512, t)
    return pl.pallas_call(
        _dilmix_kernel,
        grid=(bsz, t // tm),
        in_specs=[pl.BlockSpec((None, tm, w), lambda b, i: (b, i, 0))] * 3,
        out_specs=pl.BlockSpec((None, tm, w // 2), lambda b, i: (b, i, 0)),
        out_shape=jax.ShapeDtypeStruct((bsz, t, w // 2), BF16),
        compiler_params=_cparams("parallel", "parallel"),
        name="dilmix",
    )(*patterns)


def _dilated(proj, bias_tiles_per_cfg):
    bsz, t, _ = proj.shape
    qkv = proj[:, :, COL_B:COL_B + 3 * GROUP_W]
    patterns = []
    for (_, dil), tiles in zip(DILATED_CONFIGS, bias_tiles_per_cfg):
        def perm(a, dil=dil):
            w = a.shape[-1]
            return a.reshape(bsz, t // dil, dil, w).transpose(0, 2, 1, 3).reshape(bsz * dil, t // dil, w)

        def unperm(a, dil=dil):
            w = a.shape[-1]
            return a.reshape(bsz, dil, t // dil, w).transpose(0, 2, 1, 3).reshape(bsz, t, w)

        patterns.append(unperm(_dilated_one(perm(qkv), tiles)))
    return _dilated_mix(patterns)


def _post_kernel(oa_ref, ob_ref, oc_ref, od_ref, wo_ref, x_ref, mod_ref, ln_ref, wr_ref, br_ref,
                 x1_ref, h2_ref, idx_ref, gate_ref, *, alpha):
    y = jnp.zeros(x_ref.shape, F32)
    for g, o_ref in enumerate((oa_ref, ob_ref, oc_ref, od_ref)):
        y = y + _dot(o_ref[...], wo_ref[g * GROUP_W:(g + 1) * GROUP_W, :])
    u = alpha * x_ref[...] + (1.0 + mod_ref[2:3, :]) * y
    x1 = _ln(u) * ln_ref[0:1, :] + ln_ref[1:2, :]
    x1_ref[...] = x1
    h2 = _ln(x1) * (1.0 + mod_ref[4:5, :]) + mod_ref[3:4, :]
    _store_token_tiles(h2_ref, 0, h2)
    logits = lax.dot_general(wr_ref[...], h2, (((1,), (1,)), ((), ())), precision=lax.Precision.HIGHEST,
                             preferred_element_type=F32) + br_ref[...]
    n_exp, tm = logits.shape
    eid = lax.broadcasted_iota(I32, (n_exp, tm), 0)
    vals, ids = [], []
    for _ in range(TOP_K):
        m = jnp.max(logits, axis=0, keepdims=True)
        first = jnp.min(jnp.where(logits == m, eid, n_exp), axis=0, keepdims=True)
        vals.append(m)
        ids.append(first)
        logits = jnp.where(eid == first, -jnp.inf, logits)
    ex = [jnp.exp(v - vals[0]) for v in vals]
    den = ex[0] + ex[1] + ex[2] + ex[3]
    zero_f = jnp.zeros((8 - TOP_K, tm), F32)
    gate_ref[...] = jnp.concatenate([e / den for e in ex] + [zero_f], axis=0)
    idx_ref[...] = jnp.concatenate(ids + [zero_f.astype(I32)], axis=0)


def _post_mixer(o_groups, w_out, x, mod, ln_rows, w_router_t, b_router, alpha):
    bsz, t, d = x.shape
    tm = min(1024, t)
    n_exp = w_router_t.shape[0]
    og = pl.BlockSpec((None, tm, GROUP_W), lambda b, i: (b, i, 0))
    row = pl.BlockSpec((None, tm, d), lambda b, i: (b, i, 0))
    small = pl.BlockSpec((None, 8, tm), lambda b, i: (b, 0, i))
    assert d == SUBLANES * LANES, "token-tile layout needs one (8,128) tile per token"
    nt = t // tm
    tiles = pl.BlockSpec((tm * SUBLANES, LANES), lambda b, i: (b * nt + i, 0))
    return pl.pallas_call(
        functools.partial(_post_kernel, alpha=alpha),
        grid=(bsz, nt),
        in_specs=[og, og, og, og,
                  pl.BlockSpec(w_out.shape, lambda b, i: (0, 0)),
                  row,
                  pl.BlockSpec((None, 8, d), lambda b, i: (b, 0, 0)),
                  pl.BlockSpec((4, d), lambda b, i: (0, 0)),
                  pl.BlockSpec((n_exp, d), lambda b, i: (0, 0)),
                  pl.BlockSpec((n_exp, 1), lambda b, i: (0, 0))],
        out_specs=[row, tiles, small, small],
        out_shape=[jax.ShapeDtypeStruct((bsz, t, d), F32), jax.ShapeDtypeStruct((bsz * t * SUBLANES, LANES), F32),
                   jax.ShapeDtypeStruct((bsz, 8, t), I32), jax.ShapeDtypeStruct((bsz, 8, t), F32)],
        compiler_params=_cparams("parallel", "parallel"),
        name="post",
    )(*o_groups, w_out, x, mod, ln_rows, w_router_t, b_router.reshape(n_exp, 1))


def _deint_kernel(w_ref, p_ref, o_ref):
    o_ref[...] = _dot(w_ref[...].astype(BF16), p_ref[...]).astype(BF16)


def _deinterleave_w1(w1_all, layer):
    _, n_exp, d, two_f = w1_all.shape
    grp = 2 * FF_GROUP
    j = np.arange(grp)
    src = np.where(j < FF_GROUP, 2 * j, 2 * (j - FF_GROUP) + 1)
    perm = jnp.asarray(np.arange(grp)[:, None] == src[None, :], BF16)
    return pl.pallas_call(
        _deint_kernel,
        grid=(n_exp, two_f // grp),
        in_specs=[pl.BlockSpec((None, None, d, grp), lambda e, g: (layer, e, 0, g)),
                  pl.BlockSpec((grp, grp), lambda e, g: (0, 0))],
        out_specs=pl.BlockSpec((None, d, grp), lambda e, g: (e, 0, g)),
        out_shape=jax.ShapeDtypeStruct((n_exp, d, two_f), BF16),
        compiler_params=_cparams("parallel", "parallel"),
        name="deint",
    )(w1_all, perm)


def _moe_kernel(be_ref, nval_ref, nblk_ref, cur_ref, nxt_ref, h_hbm, w1_ref, b1_ref, w2_ref, b2_ref,
                out_hbm, xbuf, ybuf, gsem, ssem, *, bm):
    i = pl.program_id(0)
    nblk = nblk_ref[0]
    slot = i % 2

    def tile_rows(tok):
        return pl.ds(pl.multiple_of(tok * SUBLANES, SUBLANES), SUBLANES)

    def gather_row_copy(tok, r, s):
        return pltpu.make_async_copy(h_hbm.at[tile_rows(tok)], xbuf.at[tile_rows(s * bm + r)], gsem.at[s])

    def scatter_row_copy(row, r, s):
        return pltpu.make_async_copy(ybuf.at[tile_rows(s * bm + r)], out_hbm.at[tile_rows(row)], ssem.at[s])

    def block_rows(s):
        return pl.ds(pl.multiple_of(s * bm * SUBLANES, bm * SUBLANES), bm * SUBLANES)

    def issue_rows(start_row):
        def chunk(ci, _):
            base = ci * DMA_ISSUE_CHUNK
            for j in range(DMA_ISSUE_CHUNK):
                start_row(base + j, j % 2)
            return 0
        lax.fori_loop(0, bm // DMA_ISSUE_CHUNK, chunk, 0)

    def start_gather(idx_ref, s):
        issue_rows(lambda r, prio: gather_row_copy(idx_ref[0, r], r, s).start(priority=prio))

    def wait_gather(s):
        pltpu.make_async_copy(h_hbm.at[pl.ds(0, bm * SUBLANES)], xbuf.at[block_rows(s)], gsem.at[s]).wait()

    def start_scatter(s, n):
        @pl.when(n == bm)
        def _():
            issue_rows(lambda r, prio: scatter_row_copy(cur_ref[0, bm + r], r, s).start(priority=prio))

        @pl.when(n < bm)
        def _():
            def body(r, _):
                scatter_row_copy(cur_ref[0, bm + r], r, s).start()
                return 0
            lax.fori_loop(0, n, body, 0)

    def wait_scatter(s, n):
        @pl.when(n == bm)
        def _():
            pltpu.make_async_copy(ybuf.at[block_rows(s)], out_hbm.at[pl.ds(0, bm * SUBLANES)], ssem.at[s]).wait()

        @pl.when(n < bm)
        def _():
            def body(r, _):
                scatter_row_copy(0, r, s).wait()
                return 0
            lax.fori_loop(0, n, body, 0)

    @pl.when(jnp.logical_and(i == 0, nblk > 0))
    def _():
        start_gather(cur_ref, 0)

    @pl.when(i + 1 < nblk)
    def _():
        start_gather(nxt_ref, 1 - slot)

    @pl.when(i < nblk)
    def _():
        wait_gather(slot)

        @pl.when(i >= 2)
        def _():
            wait_scatter(slot, nval_ref[jnp.maximum(i - 2, 0)])

        x = _load_token_tiles(xbuf, slot * bm, bm).astype(BF16)
        y = jnp.zeros((bm, w2_ref.shape[1]), F32) + b2_ref[...]
        for g in range(w2_ref.shape[0] // FF_GROUP):
            cols = slice(2 * g * FF_GROUP, 2 * (g + 1) * FF_GROUP)
            hh = _dot(x, w1_ref[:, cols]) + b1_ref[:, cols]
            glu = jnp.minimum(hh[:, :FF_GROUP], SWIGLU_LIMIT)
            lin = jnp.clip(hh[:, FF_GROUP:], -SWIGLU_LIMIT, SWIGLU_LIMIT)
            act = glu * jax.nn.sigmoid(SWIGLU_ALPHA * glu) * (lin + 1.0)
            y = y + _dot(act.astype(BF16), w2_ref[g * FF_GROUP:(g + 1) * FF_GROUP, :])
        _store_token_tiles(ybuf, slot * bm, y)
        start_scatter(slot, nval_ref[i])

    last = pl.num_programs(0) - 1

    @pl.when(jnp.logical_and(i == last, nblk >= 2))
    def _():
        wait_scatter(nblk % 2, nval_ref[jnp.maximum(nblk - 2, 0)])

    @pl.when(jnp.logical_and(i == last, nblk >= 1))
    def _():
        wait_scatter((nblk + 1) % 2, nval_ref[jnp.maximum(nblk - 1, 0)])


def _moe_experts(h2, blk_expert, blk_valid, n_used, slot_idx, w1p, b1p, w2_all, b2, layer):
    n_tok = h2.shape[0] // SUBLANES
    d = SUBLANES * LANES
    n_blocks, _, two_bm = slot_idx.shape
    bm = two_bm // 2
    n_exp, _, two_f = w1p.shape
    f = two_f // 2
    idx_spec = lambda f_: pl.BlockSpec((None, 1, two_bm), f_, memory_space=pltpu.SMEM)
    wspec = lambda shp: pl.BlockSpec((None,) + shp, lambda i, be, nv, nb: (be[i], 0, 0))
    grid_spec = pltpu.PrefetchScalarGridSpec(
        num_scalar_prefetch=3,
        grid=(n_blocks,),
        in_specs=[idx_spec(lambda i, be, nv, nb: (i, 0, 0)),
                  idx_spec(lambda i, be, nv, nb: (jnp.minimum(i + 1, n_blocks - 1), 0, 0)),
                  pl.BlockSpec(memory_space=pl.ANY),
                  wspec((d, two_f)), wspec((1, two_f)),
                  pl.BlockSpec((None, None, f, d), lambda i, be, nv, nb: (layer, be[i], 0, 0)),
                  wspec((1, d))],
        out_specs=pl.BlockSpec(memory_space=pl.ANY),
        scratch_shapes=[pltpu.VMEM((2 * bm * SUBLANES, LANES), F32), pltpu.VMEM((2 * bm * SUBLANES, LANES), F32),
                        pltpu.SemaphoreType.DMA((2,)), pltpu.SemaphoreType.DMA((2,))],
    )
    return pl.pallas_call(
        functools.partial(_moe_kernel, bm=bm),
        grid_spec=grid_spec,
        out_shape=jax.ShapeDtypeStruct((n_tok * TOP_K * SUBLANES, LANES), F32),
        compiler_params=_cparams("arbitrary"),
        name="moe",
    )(blk_expert, blk_valid, n_used, slot_idx, slot_idx, h2, w1p, b1p, w2_all, b2)


def _moe_dispatch(top_idx, bm):
    n_tok = top_idx.shape[0]
    m = n_tok * TOP_K
    assert m % bm == 0
    e_flat = top_idx.reshape(-1)
    experts = jnp.arange(N_EXPERTS, dtype=I32)
    counts = jnp.sum((e_flat[:, None] == experts[None, :]).astype(I32), axis=0)
    padded = (counts + bm - 1) // bm * bm
    pend = jnp.cumsum(padded)
    n_blocks = m // bm + N_EXPERTS
    pad_ok = jnp.arange(bm, dtype=I32)[None, :] < (padded - counts)[:, None]
    pad_key = jnp.where(pad_ok, 2 * experts[:, None] + 1, 2 * N_EXPERTS).reshape(-1)
    keys = jnp.concatenate([2 * e_flat, pad_key])
    vals = jnp.concatenate([jnp.arange(m, dtype=I32), jnp.full((N_EXPERTS * bm,), -1, I32)])
    _, asg = lax.sort((keys, vals), num_keys=1, is_stable=True)
    valid = asg >= 0
    tok = jnp.maximum(asg, 0) // TOP_K
    choice = jnp.maximum(asg, 0) % TOP_K
    slot_idx = jnp.concatenate([tok.reshape(n_blocks, bm), (choice * n_tok + tok).reshape(n_blocks, bm)], axis=1)
    blk_valid = jnp.sum(valid.reshape(n_blocks, bm).astype(I32), axis=1)
    blk_start = jnp.arange(n_blocks, dtype=I32) * bm
    blk_expert = jnp.minimum(jnp.sum((blk_start[:, None] >= pend[None, :]).astype(I32), axis=1), N_EXPERTS - 1)
    n_used = (pend[-1] // bm).astype(I32).reshape(1)
    return blk_expert, blk_valid, n_used, slot_idx.reshape(n_blocks, 1, 2 * bm)


def _comb_kernel(y4_ref, gate_ref, x_ref, mod_ref, ln_ref, o_ref, *, alpha):
    y = jnp.zeros(x_ref.shape, F32)
    for k in range(TOP_K):
        y = y + gate_ref[:, k:k + 1] * _load_token_tiles(y4_ref.at[k], 0, x_ref.shape[0])
    u = alpha * x_ref[...] + (1.0 + mod_ref[5:6, :]) * y
    o_ref[...] = _ln(u) * ln_ref[2:3, :] + ln_ref[3:4, :]


def _combine(y4, gate, x1, mod, ln_rows, alpha):
    bsz, t, d = x1.shape
    tm = min(512, t)
    nt = t // tm
    return pl.pallas_call(
        functools.partial(_comb_kernel, alpha=alpha),
        grid=(bsz, nt),
        in_specs=[pl.BlockSpec((TOP_K, tm * SUBLANES, LANES), lambda b, i: (0, b * nt + i, 0)),
                  pl.BlockSpec((None, tm, 8), lambda b, i: (b, i, 0)),
                  pl.BlockSpec((None, tm, d), lambda b, i: (b, i, 0)),
                  pl.BlockSpec((None, 8, d), lambda b, i: (b, 0, 0)),
                  pl.BlockSpec((4, d), lambda b, i: (0, 0))],
        out_specs=pl.BlockSpec((None, tm, d), lambda b, i: (b, i, 0)),
        out_shape=jax.ShapeDtypeStruct((bsz, t, d), F32),
        compiler_params=_cparams("parallel", "parallel"),
        name="comb",
    )(y4, gate, x1, mod, ln_rows)


def _rel_bucket(dist):
    n = jnp.maximum(dist, 0)
    max_exact = N_BUCKETS // 2
    nf = jnp.maximum(n, 1).astype(F32)
    large = max_exact + (jnp.log(nf / max_exact) / math.log(MAX_DISTANCE / max_exact)
                         * (N_BUCKETS - max_exact)).astype(I32)
    large = jnp.minimum(large, N_BUCKETS - 1)
    return jnp.where(n < max_exact, n, large)


def _bias_tiles(bias_tab, tq, dil, key_major):
    period = 2 * tq
    k = np.arange(period)
    d = np.where(k < tq, -k, period - k)
    dist = np.stack([np.maximum(d, 0), d + tq]) * dil
    line = bias_tab.astype(F32)[_rel_bucket(jnp.asarray(dist, I32))]
    line = jnp.moveaxis(line, -1, 0)
    flat = jnp.tile(line, (1, 1, tq))[..., :tq * (period - 1)]
    tiles = flat.reshape(line.shape[0], 2, tq, period - 1)[..., :tq]
    if key_major:
        tiles = (tiles - bias_tab.astype(F32)[N_BUCKETS - 1][:, None, None, None]) * LOG2E
        tiles = jnp.swapaxes(tiles, -1, -2)
    return tiles


def _split_w_in(w_in):
    d = w_in.shape[0]
    g = GROUP_W
    segs = {}
    o = 0
    for name, width in (("qa", g), ("ka", g), ("va", g), ("qb", g), ("kb", g), ("vb", g),
                        ("qc", g), ("kc", g), ("vc", g), ("qx", IDX_HEADS * IDX_DIM), ("kx", IDX_DIM),
                        ("wx", IDX_HEADS), ("qd", g), ("kd", g), ("vd", g)):
        segs[name] = w_in[:, o:o + width]
        o += width
    pad = jnp.zeros((d, LANES - IDX_DIM - IDX_HEADS), w_in.dtype)
    w = jnp.concatenate([segs[n] for n in ("qa", "ka", "va", "qb", "kb", "vb", "kc", "kd", "kx", "wx")] + [pad],
                        axis=1)
    wt = jnp.concatenate([segs[n] for n in ("qx", "qc", "vc", "qd", "vd", "wx")], axis=1).T
    return w.astype(BF16), wt.astype(BF16)


def _layer(x, c, layer, depth, p, tiles):
    bsz, t, d = x.shape
    alpha = (2 * depth) ** 0.25
    mod = _ada_mod(c, p["w_ada_all"], p["b_ada_all"], layer).reshape(bsz, 6, d)
    mod = jnp.concatenate([mod, jnp.zeros((bsz, 2, d), F32)], axis=1)
    ln_rows = jnp.concatenate([p["ln_g"][0:1], p["ln_b"][0:1], p["ln_g"][1:2], p["ln_b"][1:2]], axis=0)

    proj, proj_t = _ln_mod_proj(x, mod, *_split_w_in(p["w_in"]))

    o_a = _stick_breaking(proj)
    o_b = _dilated(proj, tiles["b"])
    o_c = _dsa(proj, proj_t, tiles["c"])
    lamp = p["diff_lam"].astype(F32)
    lambda_init = 0.8 - 0.6 * math.exp(-0.3 * layer)
    lam = jnp.exp(jnp.sum(lamp[0] * lamp[1])) - jnp.exp(jnp.sum(lamp[2] * lamp[3])) + lambda_init
    o_d = _differential(proj, proj_t, lam, tiles["d"], p["diff_g"], lambda_init)

    x1, h2, top_idx, gate = _post_mixer((o_a, o_b, o_c, o_d), p["w_out"].astype(BF16), x, mod, ln_rows,
                                        p["w_router"].T, p["b_router"], alpha)

    top_idx = top_idx[:, :TOP_K, :].transpose(0, 2, 1).reshape(bsz * t, TOP_K)
    blk_expert, blk_valid, n_used, slot_idx = _moe_dispatch(top_idx, MOE_BLOCK)
    n_grp = p["b1"].shape[-1] // (2 * FF_GROUP)
    b1p = p["b1"].reshape(N_EXPERTS, n_grp, FF_GROUP, 2).transpose(0, 1, 3, 2).reshape(N_EXPERTS, 1, -1)
    y_rows = _moe_experts(h2, blk_expert, blk_valid, n_used, slot_idx,
                          _deinterleave_w1(p["w1_all"], layer), b1p, p["w2_all"], p["b2"][:, None, :], layer)
    return _combine(y_rows.reshape(TOP_K, bsz * t * SUBLANES, LANES), gate.transpose(0, 2, 1), x1, mod, ln_rows,
                    alpha)


def kernel(x, c, w_ada, b_ada, w_in, w_out, diff_lam, diff_g, ln_g, ln_b, w_router, b_router, w1, b1, w2, b2,
           rel_bias):
    depth = w_in.shape[0]
    t = x.shape[1]
    tq = min(ATT_BLOCK, t)
    tiles = dict(
        b=[_bias_tiles(rel_bias[:, :GROUP_HEADS], min(128, t // dil), dil, False) for _, dil in DILATED_CONFIGS],
        c=_bias_tiles(rel_bias[:, GROUP_HEADS:2 * GROUP_HEADS], tq, 1, True),
        d=_bias_tiles(rel_bias[:, 2 * GROUP_HEADS:], tq, 1, True))
    w2_all = w2.astype(BF16)
    for layer in range(depth):
        p = dict(w_ada_all=w_ada, b_ada_all=b_ada, w_in=w_in[layer], w_out=w_out[layer],
                 diff_lam=diff_lam[layer], diff_g=diff_g[layer], ln_g=ln_g[layer], ln_b=ln_b[layer],
                 w_router=w_router[layer], b_router=b_router[layer], w1_all=w1, b1=b1[layer],
                 w2_all=w2_all, b2=b2[layer])
        x = _layer(x, c, layer, depth, p, tiles)
    return x
```

```python
import functools
import math

import numpy as np
import jax
import jax.numpy as jnp
from jax import lax
from jax.experimental import pallas as pl
from jax.experimental.pallas import tpu as pltpu

F32 = jnp.float32
BF16 = jnp.bfloat16
I32 = jnp.int32

HEAD_DIM = 64
GROUP_HEADS = 4
GROUP_W = GROUP_HEADS * HEAD_DIM
DIFF_QK_DIM = HEAD_DIM // 2
DILATED_CONFIGS = ((128, 1), (512, 4), (2048, 16))
IDX_HEADS = 16
IDX_DIM = 64
INDEX_TOPK_MAX = 256
N_EXPERTS = 32
TOP_K = 4
SWIGLU_ALPHA = 1.702
SWIGLU_LIMIT = 7.0
N_BUCKETS = 32
MAX_DISTANCE = 128
LN_EPS = 1e-5
MOE_BLOCK = 512
DMA_ISSUE_CHUNK = 64

LANES = 128
SUBLANES = 8
VMEM_LIMIT_BYTES = 56 * 1024 * 1024
NEG_BIG = -1e30
SB_SKIP_LOG = -100.0
INT_MIN = -2 ** 31
NEG_BIG_BITS = int(np.float32(NEG_BIG).view(np.int32))
LOG2E = math.log2(math.e)
ATT_BLOCK = 256
COUNT_ROWS = 4 * SUBLANES
DIL_SUB_BLOCKS = 4
FF_GROUP = 256

COL_A = 0
COL_B = COL_A + 3 * GROUP_W
COL_KC = COL_B + 3 * GROUP_W
COL_KD = COL_KC + GROUP_W
COL_TAIL = COL_KD + GROUP_W
PROJ_COLS = COL_TAIL + LANES
ROW_QX = 0
ROW_QC = ROW_QX + IDX_HEADS * IDX_DIM
ROW_VC = ROW_QC + GROUP_W
ROW_QD = ROW_VC + GROUP_W
ROW_VD = ROW_QD + GROUP_W
ROW_WX = ROW_VD + GROUP_W
PROJ_ROWS = ROW_WX + IDX_HEADS


def _cparams(*sem):
    return pltpu.CompilerParams(dimension_semantics=sem, vmem_limit_bytes=VMEM_LIMIT_BYTES)


def _ln(x):
    mu = jnp.mean(x, axis=-1, keepdims=True)
    xc = x - mu
    return xc * lax.rsqrt(jnp.mean(xc * xc, axis=-1, keepdims=True) + LN_EPS)


def _dot_nt(a, b):
    return lax.dot_general(a, b, (((1,), (1,)), ((), ())), preferred_element_type=F32)


def _dot(a, b):
    return jnp.dot(a, b, preferred_element_type=F32)


def _load_token_tiles(ref, first_tok, n):
    base = first_tok * SUBLANES
    return jnp.concatenate([ref[pl.ds(base + c, n, stride=SUBLANES), :] for c in range(SUBLANES)], axis=1)


def _store_token_tiles(ref, first_tok, val):
    n = val.shape[0]
    base = first_tok * SUBLANES
    for c in range(SUBLANES):
        ref[pl.ds(base + c, n, stride=SUBLANES), :] = val[:, c * LANES:(c + 1) * LANES]


def _ada_kernel(c_ref, w_ref, b_ref, o_ref):
    o_ref[...] = jnp.dot(c_ref[...], w_ref[...], precision=lax.Precision.HIGHEST,
                         preferred_element_type=F32) + b_ref[...]


def _ada_mod(c, w_all, b_all, layer):
    bsz, d = c.shape
    depth, _, n = w_all.shape
    return pl.pallas_call(
        _ada_kernel,
        grid=(n // d,),
        in_specs=[pl.BlockSpec((bsz, d), lambda j: (0, 0)),
                  pl.BlockSpec((None, d, d), lambda j: (layer, 0, j)),
                  pl.BlockSpec((None, 1, d), lambda j: (layer, 0, j))],
        out_specs=pl.BlockSpec((bsz, d), lambda j: (0, j)),
        out_shape=jax.ShapeDtypeStruct((bsz, n), F32),
        compiler_params=_cparams("arbitrary"),
        name="ada",
    )(c, w_all, b_all.reshape(depth, 1, n))


def _proj_kernel(x_ref, mod_ref, w_ref, wt_ref, o_ref, ot_ref, *, chunk):
    h = (_ln(x_ref[...]) * (1.0 + mod_ref[1:2, :]) + mod_ref[0:1, :]).astype(BF16)
    ncol = o_ref.shape[-1]
    for c0 in range(0, ncol, chunk):
        c1 = min(c0 + chunk, ncol)
        o_ref[:, c0:c1] = _dot(h, w_ref[:, c0:c1]).astype(BF16)
    n_sub, nrow, tq = ot_ref.shape
    for r0 in range(0, nrow, chunk):
        r1 = min(r0 + chunk, nrow)
        res = _dot_nt(wt_ref[r0:r1, :], h).astype(BF16)
        for j in range(n_sub):
            ot_ref[j, r0:r1, :] = res[:, j * tq:(j + 1) * tq]


def _ln_mod_proj(x, mod, w, wt):
    bsz, t, d = x.shape
    ncol, nrow = w.shape[1], wt.shape[0]
    tq = min(ATT_BLOCK, t)
    tm = min(2 * tq, t)
    return pl.pallas_call(
        functools.partial(_proj_kernel, chunk=2 * LANES),
        grid=(bsz, t // tm),
        in_specs=[pl.BlockSpec((None, tm, d), lambda b, i: (b, i, 0)),
                  pl.BlockSpec((None, 8, d), lambda b, i: (b, 0, 0)),
                  pl.BlockSpec((d, ncol), lambda b, i: (0, 0)),
                  pl.BlockSpec((nrow, d), lambda b, i: (0, 0))],
        out_specs=[pl.BlockSpec((None, tm, ncol), lambda b, i: (b, i, 0)),
                   pl.BlockSpec((None, tm // tq, nrow, tq), lambda b, i: (b, i, 0, 0))],
        out_shape=[jax.ShapeDtypeStruct((bsz, t, ncol), BF16),
                   jax.ShapeDtypeStruct((bsz, t // tq, nrow, tq), BF16)],
        compiler_params=_cparams("parallel", "parallel"),
        name="proj",
    )(x, mod, w, wt)


def _by_head(x):
    lane = lax.broadcasted_iota(I32, x.shape, 1)
    zero = jnp.zeros_like(x)
    return jnp.concatenate([jnp.where(jnp.logical_and(lane >= h * HEAD_DIM, lane < (h + 1) * HEAD_DIM), x, zero)
                            for h in range(GROUP_HEADS)], axis=0)


def _sb_kernel(q_ref, k_ref, v_ref, o_ref, *, tq, scale):
    qi = pl.program_id(1)
    r = lax.broadcasted_iota(I32, (tq, tq), 0)
    c = lax.broadcasted_iota(I32, (tq, tq), 1)
    strict_lower = c < r
    upper = jnp.where(r > c, 1.0, 0.0).astype(BF16)

    q = q_ref[...]
    valid = jnp.concatenate([strict_lower] * GROUP_HEADS, axis=1)

    def step(kb, carry, acc, masked):
        start = pl.multiple_of(kb * tq, tq)
        z = _dot_nt(q, _by_head(k_ref[pl.ds(start, tq), :])) * scale
        log_sig = jnp.minimum(z, 0.0) - jnp.log(1.0 + jnp.exp(-jnp.abs(z)))
        log_fail = log_sig - z
        if masked:
            log_fail = jnp.where(valid, log_fail, 0.0)
        lf = jnp.concatenate([log_fail[:, h * tq:(h + 1) * tq] for h in range(GROUP_HEADS)], axis=0)
        lf_hi = lf.astype(BF16)
        lf_lo = (lf - lf_hi.astype(F32)).astype(BF16)
        after = _dot(lf_hi, upper) + _dot(lf_lo, upper)
        after = jnp.concatenate([after[h * tq:(h + 1) * tq] + carry[h] for h in range(GROUP_HEADS)], axis=1)
        w = jnp.exp(log_sig + after)
        if masked:
            w = jnp.where(valid, w, 0.0)
        acc = acc + _dot(w.astype(BF16), _by_head(v_ref[pl.ds(start, tq), :]))
        carry = tuple(carry[h] + jnp.sum(log_fail[:, h * tq:(h + 1) * tq], axis=1, keepdims=True)
                      for h in range(GROUP_HEADS))
        return carry, acc

    def live(carry):
        top = carry[0]
        for h in range(1, GROUP_HEADS):
            top = jnp.maximum(top, carry[h])
        return (jnp.max(top) > SB_SKIP_LOG).astype(I32)

    carry, acc = step(qi, tuple(jnp.zeros((tq, 1), F32) for _ in range(GROUP_HEADS)),
                      jnp.zeros((tq, GROUP_W), F32), True)

    def body(s):
        kb, carry, acc, _ = s
        carry, acc = step(kb, carry, acc, False)
        return kb - 1, carry, acc, live(carry)

    _, _, acc, _ = lax.while_loop(lambda s: jnp.logical_and(s[0] >= 0, s[3] > 0), body,
                                  (qi - 1, carry, acc, live(carry)))
    o_ref[...] = acc.astype(BF16)


def _stick_breaking(proj):
    bsz, t, _ = proj.shape
    tq = min(ATT_BLOCK, t)
    cb = COL_A // GROUP_W
    return pl.pallas_call(
        functools.partial(_sb_kernel, tq=tq, scale=HEAD_DIM ** -0.5),
        grid=(bsz, t // tq),
        in_specs=[pl.BlockSpec((None, tq, GROUP_W), lambda b, i: (b, i, cb)),
                  pl.BlockSpec((None, t, GROUP_W), lambda b, i: (b, 0, cb + 1)),
                  pl.BlockSpec((None, t, GROUP_W), lambda b, i: (b, 0, cb + 2))],
        out_specs=pl.BlockSpec((None, tq, GROUP_W), lambda b, i: (b, i, 0)),
        out_shape=jax.ShapeDtypeStruct((bsz, t, GROUP_W), BF16),
        compiler_params=_cparams("parallel", "arbitrary"),
        name="sb",
    )(proj, proj, proj)


ONES_ROWS = 16


def _softmax_init_t(tq):
    return (jnp.full((1, tq), NEG_BIG, F32), jnp.zeros((1, tq), F32), jnp.zeros((HEAD_DIM, tq), F32))


def _chain_logits(k_ref, start, tk, chains, qt):
    outs = [None] * len(chains)
    for half in range(GROUP_W // LANES):
        idx = [i for i, (lo, hi) in enumerate(chains) if lo // LANES == half]
        assert all((chains[i][1] - 1) // LANES == half for i in idx)
        khalf = k_ref[pl.ds(start, tk), half * LANES:(half + 1) * LANES]
        lane = lax.broadcasted_iota(I32, khalf.shape, 1) + half * LANES
        zero = jnp.zeros_like(khalf)
        lhs = jnp.concatenate([jnp.where(jnp.logical_and(lane >= chains[i][0], lane < chains[i][1]), khalf, zero)
                               for i in idx], axis=0)
        z = _dot(lhs, qt[half * LANES:(half + 1) * LANES, :])
        for j, i in enumerate(idx):
            outs[i] = z[j * tk:(j + 1) * tk]
    return outs


def _ones_rows(tk):
    r = lax.broadcasted_iota(I32, (ONES_ROWS, GROUP_HEADS * tk), 0)
    c = lax.broadcasted_iota(I32, (ONES_ROWS, GROUP_HEADS * tk), 1)
    lo = r * tk
    return jnp.where(jnp.logical_and(c >= lo, c < lo + tk), 1.0, 0.0).astype(BF16)


def _value_blockdiag(vblk, ones):
    tk = vblk.shape[1]
    zero = jnp.zeros((HEAD_DIM, tk), BF16)
    rows = [jnp.concatenate([vblk[h * HEAD_DIM:(h + 1) * HEAD_DIM, :] if j == h else zero
                             for j in range(GROUP_HEADS)], axis=1) for h in range(GROUP_HEADS)]
    return jnp.concatenate(rows + [ones], axis=0)


def _softmax_weights(z, m):
    m_new = jnp.maximum(m, jnp.max(z, axis=0, keepdims=True))
    return m_new, jnp.exp2(m - m_new), jnp.exp2(z - m_new).astype(BF16)


def _causal_blocks(qi, logits, step, st):
    st = lax.fori_loop(0, jnp.maximum(qi - 1, 0), lambda kb, s: step(logits(kb), kb, s, None, False), st)
    st = lax.cond(qi >= 1, lambda s: step(logits(qi - 1), qi - 1, s, 1, False), lambda s: s, st)
    return step(logits(qi), qi, st, 0, True)


def _diff_kernel(lam_ref, qt_ref, k_ref, vt_ref, bias_ref, g_ref, o_ref, *, tq, c_scale, out_scale):
    qi = pl.program_id(1)
    r = lax.broadcasted_iota(I32, (tq, tq), 0)
    c = lax.broadcasted_iota(I32, (tq, tq), 1)
    causal = r <= c
    lam = lam_ref[0]
    qt = qt_ref[...]
    ones = _ones_rows(tq)
    chains = [(h * HEAD_DIM + j * DIFF_QK_DIM, h * HEAD_DIM + (j + 1) * DIFF_QK_DIM)
              for h in range(GROUP_HEADS) for j in range(2)]

    def logits(kb):
        return tuple(_chain_logits(k_ref, pl.multiple_of(kb * tq, tq), tq, chains, qt))

    def step(zs, kb, st, which, masked):
        ms, alphas, ps = [], [], []
        for i, z in enumerate(zs):
            z = z * c_scale
            if which is not None:
                z = z + bias_ref[i // 2, which]
            if masked:
                z = jnp.where(causal, z, NEG_BIG)
            m_new, alpha, p = _softmax_weights(z, st[i][0])
            ms.append(m_new)
            alphas.append(alpha)
            ps.append(p)
        pmat = jnp.concatenate([jnp.concatenate([ps[2 * h], ps[2 * h + 1]], axis=1) for h in range(GROUP_HEADS)],
                               axis=0)
        pv = _dot(_value_blockdiag(vt_ref[kb], ones), pmat)
        out = []
        for i in range(len(chains)):
            h, j = divmod(i, 2)
            cols = slice(j * tq, (j + 1) * tq)
            l = alphas[i] * st[i][1] + pv[GROUP_W + h:GROUP_W + h + 1, cols]
            acc = alphas[i] * st[i][2] + pv[h * HEAD_DIM:(h + 1) * HEAD_DIM, cols]
            out.append((ms[i], l, acc))
        return tuple(out)

    st = _causal_blocks(qi, logits, step, tuple(_softmax_init_t(tq) for _ in chains))
    outs = []
    for h in range(GROUP_HEADS):
        (_, l1, a1), (_, l2, a2) = st[2 * h], st[2 * h + 1]
        o = a1 / l1 - lam * (a2 / l2)
        o = o * lax.rsqrt(jnp.mean(o * o, axis=0, keepdims=True) + LN_EPS)
        outs.append(o * g_ref[...] * out_scale)
    o_ref[...] = jnp.concatenate(outs, axis=0).T.astype(BF16)


def _differential(proj, proj_t, lam, bias_tiles, diff_g, lambda_init):
    bsz, t, _ = proj.shape
    nq, tq = proj_t.shape[1], proj_t.shape[3]
    grid_spec = pltpu.PrefetchScalarGridSpec(
        num_scalar_prefetch=1,
        grid=(bsz, nq),
        in_specs=[pl.BlockSpec((None, None, GROUP_W, tq), lambda b, i, lam: (b, i, ROW_QD // GROUP_W, 0)),
                  pl.BlockSpec((None, t, GROUP_W), lambda b, i, lam: (b, 0, COL_KD // GROUP_W)),
                  pl.BlockSpec((None, nq, GROUP_W, tq), lambda b, i, lam: (b, 0, ROW_VD // GROUP_W, 0)),
                  pl.BlockSpec(bias_tiles.shape, lambda b, i, lam: (0, 0, 0, 0)),
                  pl.BlockSpec((HEAD_DIM, 1), lambda b, i, lam: (0, 0))],
        out_specs=pl.BlockSpec((None, tq, GROUP_W), lambda b, i, lam: (b, i, 0)),
    )
    return pl.pallas_call(
        functools.partial(_diff_kernel, tq=tq, c_scale=DIFF_QK_DIM ** -0.5 * LOG2E, out_scale=1.0 - lambda_init),
        grid_spec=grid_spec,
        out_shape=jax.ShapeDtypeStruct((bsz, t, GROUP_W), BF16),
        compiler_params=_cparams("parallel", "arbitrary"),
        name="diff",
    )(lam.reshape(1).astype(F32), proj_t, proj, proj_t, bias_tiles, diff_g.reshape(HEAD_DIM, 1).astype(F32))


def _dsa_kernel(qx_ref, wx_ref, tail_ref, qt_ref, k_ref, vt_ref, bias_ref, o_ref, key_scr, cut_scr,
                *, tq, topk, c_scale, row_bits):
    qi = pl.program_id(1)
    nkb = qi + 1
    r = lax.broadcasted_iota(I32, (tq, tq), 0)
    c = lax.broadcasted_iota(I32, (tq, tq), 1)
    wx = wx_ref[...].astype(F32) * (IDX_HEADS ** -0.5 * IDX_DIM ** -0.5)
    wxb = [jnp.broadcast_to(wx[h:h + 1, :], (SUBLANES, tq)) for h in range(IDX_HEADS)]
    zpad = jnp.zeros((LANES - IDX_DIM, tq), BF16)
    qx = [jnp.concatenate([qx_ref[h * IDX_DIM:(h + 1) * IDX_DIM, :], zpad], axis=0) for h in range(IDX_HEADS)]

    def score_block(kb, _):
        start = pl.multiple_of(kb * tq, tq)
        kt = tail_ref[pl.ds(start, tq), :]
        s = jnp.zeros((tq // SUBLANES, SUBLANES, tq), F32)
        for h in range(IDX_HEADS):
            s = s + wxb[h] * jnp.maximum(_dot(kt, qx[h]), 0.0).reshape(tq // SUBLANES, SUBLANES, tq)
        s = s.reshape(tq, tq)
        s = jnp.where(s == 0.0, 0.0, s)
        s = jnp.where(r + kb * tq <= c + qi * tq, s, -jnp.inf)
        bits = pltpu.bitcast(s, I32)
        key_scr[kb] = bits ^ ((bits >> 31) & 0x7FFFFFFF)
        return 0

    lax.fori_loop(0, nkb, score_block, 0)

    def count(pred):
        def body(kb, acc):
            hit = jnp.where(pred(key_scr[kb], kb), 1.0, 0.0)
            return acc + jnp.sum(hit.reshape(tq // COUNT_ROWS, COUNT_ROWS, tq), axis=0)
        acc = lax.fori_loop(0, nkb, body, jnp.zeros((COUNT_ROWS, tq), F32))
        return jnp.sum(acc, axis=0, keepdims=True)

    def bit_step(i, s):
        thr, c_thr = s
        cand = thr + lax.shift_left(jnp.int32(1), 31 - i)
        cnt = count(lambda key, kb: key >= cand)
        ge = cnt >= topk
        return jnp.where(ge, cand, thr), jnp.where(ge, cnt, c_thr)

    n_keys = (nkb * tq).astype(F32)
    thr, c_thr = lax.fori_loop(0, 32, bit_step, (jnp.full((1, tq), INT_MIN, I32), jnp.full((1, tq), 1.0, F32) * n_keys))
    cut_scr[...] = jnp.full((SUBLANES, tq), 2 ** 30, I32)

    @pl.when(jnp.max(c_thr) > topk)
    def _():
        need = topk - count(lambda key, kb: key > thr)

        def row_step(i, lo):
            cand = lo + lax.shift_left(jnp.int32(1), row_bits - 1 - i)
            cnt = count(lambda key, kb: jnp.logical_and(key == thr, r + kb * tq < cand))
            return jnp.where(cnt < need, cand, lo)
        lo = lax.fori_loop(0, row_bits, row_step, jnp.zeros((1, tq), I32))
        cut_scr[...] = jnp.broadcast_to(lo, (SUBLANES, tq))

    cut = cut_scr[0:1, :]

    def select_block(kb, _):
        key = key_scr[kb]
        pos = r + kb * tq
        sel = jnp.logical_or(key > thr, jnp.logical_and(key == thr, pos <= cut))
        sel = jnp.logical_and(sel, pos <= c + qi * tq)
        key_scr[kb] = jnp.where(sel, 0, NEG_BIG_BITS)
        return 0

    lax.fori_loop(0, nkb, select_block, 0)

    qt = qt_ref[...]
    ones = _ones_rows(tq)
    chains = [(h * HEAD_DIM, (h + 1) * HEAD_DIM) for h in range(GROUP_HEADS)]

    def logits(kb):
        return tuple(_chain_logits(k_ref, pl.multiple_of(kb * tq, tq), tq, chains, qt))

    def step(zs, kb, st, which, masked):
        del masked
        mask = pltpu.bitcast(key_scr[kb], F32)
        ms, alphas, ps = [], [], []
        for h, z in enumerate(zs):
            z = z * c_scale + mask
            if which is not None:
                z = z + bias_ref[h, which]
            m_new, alpha, p = _softmax_weights(z, st[h][0])
            ms.append(m_new)
            alphas.append(alpha)
            ps.append(p)
        pv = _dot(_value_blockdiag(vt_ref[kb], ones), jnp.concatenate(ps, axis=0))
        return tuple((ms[h], alphas[h] * st[h][1] + pv[GROUP_W + h:GROUP_W + h + 1, :],
                      alphas[h] * st[h][2] + pv[h * HEAD_DIM:(h + 1) * HEAD_DIM, :]) for h in range(GROUP_HEADS))

    st = _causal_blocks(qi, logits, step, tuple(_softmax_init_t(tq) for _ in range(GROUP_HEADS)))
    o_ref[...] = jnp.concatenate([acc / l for _, l, acc in st], axis=0).T.astype(BF16)


def _dsa(proj, proj_t, bias_tiles):
    bsz, t, _ = proj.shape
    nq, tq = proj_t.shape[1], proj_t.shape[3]
    topk = min(INDEX_TOPK_MAX, t // 4)
    assert tq >= topk, "the threshold search needs at least topk keys in the first block"
    nqx = IDX_HEADS * IDX_DIM
    return pl.pallas_call(
        functools.partial(_dsa_kernel, tq=tq, topk=float(topk), c_scale=HEAD_DIM ** -0.5 * LOG2E,
                          row_bits=max(1, (t - 1).bit_length())),
        grid=(bsz, nq),
        in_specs=[pl.BlockSpec((None, None, nqx, tq), lambda b, i: (b, i, ROW_QX // nqx, 0)),
                  pl.BlockSpec((None, None, IDX_HEADS, tq), lambda b, i: (b, i, ROW_WX // IDX_HEADS, 0)),
                  pl.BlockSpec((None, t, LANES), lambda b, i: (b, 0, COL_TAIL // LANES)),
                  pl.BlockSpec((None, None, GROUP_W, tq), lambda b, i: (b, i, ROW_QC // GROUP_W, 0)),
                  pl.BlockSpec((None, t, GROUP_W), lambda b, i: (b, 0, COL_KC // GROUP_W)),
                  pl.BlockSpec((None, nq, GROUP_W, tq), lambda b, i: (b, 0, ROW_VC // GROUP_W, 0)),
                  pl.BlockSpec(bias_tiles.shape, lambda b, i: (0, 0, 0, 0))],
        out_specs=pl.BlockSpec((None, tq, GROUP_W), lambda b, i: (b, i, 0)),
        out_shape=jax.ShapeDtypeStruct((bsz, t, GROUP_W), BF16),
        scratch_shapes=[pltpu.VMEM((nq, tq, tq), I32), pltpu.VMEM((SUBLANES, tq), I32)],
        compiler_params=_cparams("parallel", "arbitrary"),
        name="dsa",
    )(proj_t, proj_t, proj, proj_t, proj, proj_t, bias_tiles)


def _dil_kernel(q_ref, kp_ref, kc_ref, vp_ref, vc_ref, bias_ref, ol_ref, *, tq, scale):
    qi = pl.program_id(1)
    r = lax.broadcasted_iota(I32, (tq, tq), 0)
    c = lax.broadcasted_iota(I32, (tq, tq), 1)
    diag_ok = c <= r
    for sb in range(q_ref.shape[0] // tq):
        rows = slice(sb * tq, (sb + 1) * tq)
        if sb == 0:
            kp, vp = kp_ref[...], vp_ref[...]
            prev_ok = jnp.logical_and(r <= c, qi > 0)
        else:
            before = slice((sb - 1) * tq, sb * tq)
            kp, vp = kc_ref[before, :], vc_ref[before, :]
            prev_ok = r <= c
        q = q_ref[rows, :]
        zp_all = _dot_nt(q, _by_head(kp)) * scale
        zd_all = _dot_nt(q, _by_head(kc_ref[rows, :])) * scale
        pps, pds, dens, lses = [], [], [], []
        for h in range(GROUP_HEADS):
            cols = slice(h * tq, (h + 1) * tq)
            zp = jnp.where(prev_ok, zp_all[:, cols] + bias_ref[h, 1], NEG_BIG)
            zd = jnp.where(diag_ok, zd_all[:, cols] + bias_ref[h, 0], NEG_BIG)
            m = jnp.maximum(jnp.max(zp, axis=1, keepdims=True), jnp.max(zd, axis=1, keepdims=True))
            pp = jnp.exp(zp - m)
            pd = jnp.exp(zd - m)
            den = jnp.sum(pp, axis=1, keepdims=True) + jnp.sum(pd, axis=1, keepdims=True)
            pps.append(pp.astype(BF16))
            pds.append(pd.astype(BF16))
            dens.append(jnp.broadcast_to(den, (tq, HEAD_DIM)))
            lses.append(jnp.broadcast_to(m + jnp.log(den), (tq, HEAD_DIM)))
        values = jnp.concatenate([_by_head(vp), _by_head(vc_ref[rows, :])], axis=0)
        ol_ref[rows, :GROUP_W] = _dot(jnp.concatenate(pps + pds, axis=1), values) / jnp.concatenate(dens, axis=-1)
        ol_ref[rows, GROUP_W:] = jnp.concatenate(lses, axis=-1)


def _dilated_one(qkv, bias_tiles):
    n, length, _ = qkv.shape
    tq = bias_tiles.shape[-1]
    nsub = max(n for n in (1, 2, DIL_SUB_BLOCKS) if length % (n * tq) == 0)
    tile = nsub * tq
    wide = lambda col: pl.BlockSpec((None, tile, GROUP_W), lambda b, i: (b, i, col))
    back = lambda col: pl.BlockSpec((None, tq, GROUP_W), lambda b, i: (b, jnp.maximum(i * nsub - 1, 0), col))
    return pl.pallas_call(
        functools.partial(_dil_kernel, tq=tq, scale=HEAD_DIM ** -0.5),
        grid=(n, length // tile),
        in_specs=[wide(0), back(1), wide(1), back(2), wide(2),
                  pl.BlockSpec(bias_tiles.shape, lambda b, i: (0, 0, 0, 0))],
        out_specs=pl.BlockSpec((None, tile, 2 * GROUP_W), lambda b, i: (b, i, 0)),
        out_shape=jax.ShapeDtypeStruct((n, length, 2 * GROUP_W), F32),
        compiler_params=_cparams("parallel", "arbitrary"),
        name="dil",
    )(qkv, qkv, qkv, qkv, qkv, bias_tiles)


def _dilmix_kernel(p0, p1, p2, out_ref):
    a0, a1, a2 = p0[:, GROUP_W:], p1[:, GROUP_W:], p2[:, GROUP_W:]
    m = jnp.maximum(jnp.maximum(a0, a1), a2)
    e0, e1, e2 = jnp.exp(a0 - m), jnp.exp(a1 - m), jnp.exp(a2 - m)
    mixed = e0 * p0[:, :GROUP_W] + e1 * p1[:, :GROUP_W] + e2 * p2[:, :GROUP_W]
    out_ref[...] = (mixed / (e0 + e1 + e2)).astype(BF16)


def _dilated_mix(patterns):
    bsz, t, w = patterns[0].shape
    tm = min(512, t)
    return pl.pallas_call(
        _dilmix_kernel,
        grid=(bsz, t // tm),
        in_specs=[pl.BlockSpec((None, tm, w), lambda b, i: (b, i, 0))] * 3,
        out_specs=pl.BlockSpec((None, tm, w // 2), lambda b, i: (b, i, 0)),
        out_shape=jax.ShapeDtypeStruct((bsz, t, w // 2), BF16),
        compiler_params=_cparams("parallel", "parallel"),
        name="dilmix",
    )(*patterns)


def _dilated(proj, bias_tiles_per_cfg):
    bsz, t, _ = proj.shape
    qkv = proj[:, :, COL_B:COL_B + 3 * GROUP_W]
    patterns = []
    for (_, dil), tiles in zip(DILATED_CONFIGS, bias_tiles_per_cfg):
        def perm(a, dil=dil):
            w = a.shape[-1]
            return a.reshape(bsz, t // dil, dil, w).transpose(0, 2, 1, 3).reshape(bsz * dil, t // dil, w)

        def unperm(a, dil=dil):
            w = a.shape[-1]
            return a.reshape(bsz, dil, t // dil, w).transpose(0, 2, 1, 3).reshape(bsz, t, w)

        patterns.append(unperm(_dilated_one(perm(qkv), tiles)))
    return _dilated_mix(patterns)


def _post_kernel(oa_ref, ob_ref, oc_ref, od_ref, wo_ref, x_ref, mod_ref, ln_ref, wr_ref, br_ref,
                 x1_ref, h2_ref, idx_ref, gate_ref, *, alpha):
    y = jnp.zeros(x_ref.shape, F32)
    for g, o_ref in enumerate((oa_ref, ob_ref, oc_ref, od_ref)):
        y = y + _dot(o_ref[...], wo_ref[g * GROUP_W:(g + 1) * GROUP_W, :])
    u = alpha * x_ref[...] + (1.0 + mod_ref[2:3, :]) * y
    x1 = _ln(u) * ln_ref[0:1, :] + ln_ref[1:2, :]
    x1_ref[...] = x1
    h2 = _ln(x1) * (1.0 + mod_ref[4:5, :]) + mod_ref[3:4, :]
    _store_token_tiles(h2_ref, 0, h2)
    logits = lax.dot_general(wr_ref[...], h2, (((1,), (1,)), ((), ())), precision=lax.Precision.HIGHEST,
                             preferred_element_type=F32) + br_ref[...]
    n_exp, tm = logits.shape
    eid = lax.broadcasted_iota(I32, (n_exp, tm), 0)
    vals, ids = [], []
    for _ in range(TOP_K):
        m = jnp.max(logits, axis=0, keepdims=True)
        first = jnp.min(jnp.where(logits == m, eid, n_exp), axis=0, keepdims=True)
        vals.append(m)
        ids.append(first)
        logits = jnp.where(eid == first, -jnp.inf, logits)
    ex = [jnp.exp(v - vals[0]) for v in vals]
    den = ex[0] + ex[1] + ex[2] + ex[3]
    zero_f = jnp.zeros((8 - TOP_K, tm), F32)
    gate_ref[...] = jnp.concatenate([e / den for e in ex] + [zero_f], axis=0)
    idx_ref[...] = jnp.concatenate(ids + [zero_f.astype(I32)], axis=0)


def _post_mixer(o_groups, w_out, x, mod, ln_rows, w_router_t, b_router, alpha):
    bsz, t, d = x.shape
    tm = min(1024, t)
    n_exp = w_router_t.shape[0]
    og = pl.BlockSpec((None, tm, GROUP_W), lambda b, i: (b, i, 0))
    row = pl.BlockSpec((None, tm, d), lambda b, i: (b, i, 0))
    small = pl.BlockSpec((None, 8, tm), lambda b, i: (b, 0, i))
    assert d == SUBLANES * LANES, "token-tile layout needs one (8,128) tile per token"
    nt = t // tm
    tiles = pl.BlockSpec((tm * SUBLANES, LANES), lambda b, i: (b * nt + i, 0))
    return pl.pallas_call(
        functools.partial(_post_kernel, alpha=alpha),
        grid=(bsz, nt),
        in_specs=[og, og, og, og,
                  pl.BlockSpec(w_out.shape, lambda b, i: (0, 0)),
                  row,
                  pl.BlockSpec((None, 8, d), lambda b, i: (b, 0, 0)),
                  pl.BlockSpec((4, d), lambda b, i: (0, 0)),
                  pl.BlockSpec((n_exp, d), lambda b, i: (0, 0)),
                  pl.BlockSpec((n_exp, 1), lambda b, i: (0, 0))],
        out_specs=[row, tiles, small, small],
        out_shape=[jax.ShapeDtypeStruct((bsz, t, d), F32), jax.ShapeDtypeStruct((bsz * t * SUBLANES, LANES), F32),
                   jax.ShapeDtypeStruct((bsz, 8, t), I32), jax.ShapeDtypeStruct((bsz, 8, t), F32)],
        compiler_params=_cparams("parallel", "parallel"),
        name="post",
    )(*o_groups, w_out, x, mod, ln_rows, w_router_t, b_router.reshape(n_exp, 1))


def _deint_kernel(w_ref, p_ref, o_ref):
    grp = p_ref.shape[0]
    for g in range(w_ref.shape[1] // grp):
        cols = slice(g * grp, (g + 1) * grp)
        o_ref[:, cols] = _dot(w_ref[:, cols].astype(BF16), p_ref[...]).astype(BF16)


def _deinterleave_w1(w1_all, layer):
    _, n_exp, d, two_f = w1_all.shape
    grp = 2 * FF_GROUP
    j = np.arange(grp)
    src = np.where(j < FF_GROUP, 2 * j, 2 * (j - FF_GROUP) + 1)
    perm = jnp.asarray(np.arange(grp)[:, None] == src[None, :], BF16)
    return pl.pallas_call(
        _deint_kernel,
        grid=(n_exp,),
        in_specs=[pl.BlockSpec((None, None, d, two_f), lambda e: (layer, e, 0, 0)),
                  pl.BlockSpec((grp, grp), lambda e: (0, 0))],
        out_specs=pl.BlockSpec((None, d, two_f), lambda e: (e, 0, 0)),
        out_shape=jax.ShapeDtypeStruct((n_exp, d, two_f), BF16),
        compiler_params=_cparams("parallel"),
        name="deint",
    )(w1_all, perm)


def _moe_kernel(be_ref, nval_ref, nblk_ref, cur_ref, nxt_ref, h_hbm, w1_ref, b1_ref, w2_ref, b2_ref,
                out_hbm, xbuf, ybuf, gsem, ssem, *, bm):
    i = pl.program_id(0)
    nblk = nblk_ref[0]
    slot = i % 2

    def tile_rows(tok):
        return pl.ds(pl.multiple_of(tok * SUBLANES, SUBLANES), SUBLANES)

    def gather_row_copy(tok, r, s):
        return pltpu.make_async_copy(h_hbm.at[tile_rows(tok)], xbuf.at[tile_rows(s * bm + r)], gsem.at[s])

    def scatter_row_copy(row, r, s):
        return pltpu.make_async_copy(ybuf.at[tile_rows(s * bm + r)], out_hbm.at[tile_rows(row)], ssem.at[s])

    def block_rows(s):
        return pl.ds(pl.multiple_of(s * bm * SUBLANES, bm * SUBLANES), bm * SUBLANES)

    def issue_rows(start_row):
        def chunk(ci, _):
            base = ci * DMA_ISSUE_CHUNK
            for j in range(DMA_ISSUE_CHUNK):
                start_row(base + j, j % 2)
            return 0
        lax.fori_loop(0, bm // DMA_ISSUE_CHUNK, chunk, 0)

    def start_gather(idx_ref, s):
        issue_rows(lambda r, prio: gather_row_copy(idx_ref[0, r], r, s).start(priority=prio))

    def wait_gather(s):
        pltpu.make_async_copy(h_hbm.at[pl.ds(0, bm * SUBLANES)], xbuf.at[block_rows(s)], gsem.at[s]).wait()

    def start_scatter(s, n):
        @pl.when(n == bm)
        def _():
            issue_rows(lambda r, prio: scatter_row_copy(cur_ref[0, bm + r], r, s).start(priority=prio))

        @pl.when(n < bm)
        def _():
            def body(r, _):
                scatter_row_copy(cur_ref[0, bm + r], r, s).start()
                return 0
            lax.fori_loop(0, n, body, 0)

    def wait_scatter(s, n):
        @pl.when(n == bm)
        def _():
            pltpu.make_async_copy(ybuf.at[block_rows(s)], out_hbm.at[pl.ds(0, bm * SUBLANES)], ssem.at[s]).wait()

        @pl.when(n < bm)
        def _():
            def body(r, _):
                scatter_row_copy(0, r, s).wait()
                return 0
            lax.fori_loop(0, n, body, 0)

    @pl.when(jnp.logical_and(i == 0, nblk > 0))
    def _():
        start_gather(cur_ref, 0)

    @pl.when(i + 1 < nblk)
    def _():
        start_gather(nxt_ref, 1 - slot)

    @pl.when(i < nblk)
    def _():
        wait_gather(slot)

        @pl.when(i >= 2)
        def _():
            wait_scatter(slot, nval_ref[jnp.maximum(i - 2, 0)])

        x = _load_token_tiles(xbuf, slot * bm, bm).astype(BF16)
        y = jnp.zeros((bm, w2_ref.shape[1]), F32) + b2_ref[...]
        for g in range(w2_ref.shape[0] // FF_GROUP):
            cols = slice(2 * g * FF_GROUP, 2 * (g + 1) * FF_GROUP)
            hh = _dot(x, w1_ref[:, cols]) + b1_ref[:, cols]
            glu = jnp.minimum(hh[:, :FF_GROUP], SWIGLU_LIMIT)
            lin = jnp.clip(hh[:, FF_GROUP:], -SWIGLU_LIMIT, SWIGLU_LIMIT)
            act = glu * jax.nn.sigmoid(SWIGLU_ALPHA * glu) * (lin + 1.0)
            y = y + _dot(act.astype(BF16), w2_ref[g * FF_GROUP:(g + 1) * FF_GROUP, :])
        _store_token_tiles(ybuf, slot * bm, y)
        start_scatter(slot, nval_ref[i])

    last = pl.num_programs(0) - 1

    @pl.when(jnp.logical_and(i == last, nblk >= 2))
    def _():
        wait_scatter(nblk % 2, nval_ref[jnp.maximum(nblk - 2, 0)])

    @pl.when(jnp.logical_and(i == last, nblk >= 1))
    def _():
        wait_scatter((nblk + 1) % 2, nval_ref[jnp.maximum(nblk - 1, 0)])


def _moe_experts(h2, blk_expert, blk_valid, n_used, slot_idx, w1p, b1p, w2_all, b2, layer):
    n_tok = h2.shape[0] // SUBLANES
    d = SUBLANES * LANES
    n_blocks, _, two_bm = slot_idx.shape
    bm = two_bm // 2
    n_exp, _, two_f = w1p.shape
    f = two_f // 2
    idx_spec = lambda f_: pl.BlockSpec((None, 1, two_bm), f_, memory_space=pltpu.SMEM)
    wspec = lambda shp: pl.BlockSpec((None,) + shp, lambda i, be, nv, nb: (be[i], 0, 0))
    grid_spec = pltpu.PrefetchScalarGridSpec(
        num_scalar_prefetch=3,
        grid=(n_blocks,),
        in_specs=[idx_spec(lambda i, be, nv, nb: (i, 0, 0)),
                  idx_spec(lambda i, be, nv, nb: (jnp.minimum(i + 1, n_blocks - 1), 0, 0)),
                  pl.BlockSpec(memory_space=pl.ANY),
                  wspec((d, two_f)), wspec((1, two_f)),
                  pl.BlockSpec((None, None, f, d), lambda i, be, nv, nb: (layer, be[i], 0, 0)),
                  wspec((1, d))],
        out_specs=pl.BlockSpec(memory_space=pl.ANY),
        scratch_shapes=[pltpu.VMEM((2 * bm * SUBLANES, LANES), F32), pltpu.VMEM((2 * bm * SUBLANES, LANES), F32),
                        pltpu.SemaphoreType.DMA((2,)), pltpu.SemaphoreType.DMA((2,))],
    )
    return pl.pallas_call(
        functools.partial(_moe_kernel, bm=bm),
        grid_spec=grid_spec,
        out_shape=jax.ShapeDtypeStruct((n_tok * TOP_K * SUBLANES, LANES), F32),
        compiler_params=_cparams("arbitrary"),
        name="moe",
    )(blk_expert, blk_valid, n_used, slot_idx, slot_idx, h2, w1p, b1p, w2_all, b2)


def _moe_dispatch(top_idx, bm):
    n_tok = top_idx.shape[0]
    m = n_tok * TOP_K
    assert m % bm == 0
    e_flat = top_idx.reshape(-1)
    experts = jnp.arange(N_EXPERTS, dtype=I32)
    counts = jnp.sum((e_flat[:, None] == experts[None, :]).astype(I32), axis=0)
    padded = (counts + bm - 1) // bm * bm
    pend = jnp.cumsum(padded)
    n_blocks = m // bm + N_EXPERTS
    pad_ok = jnp.arange(bm, dtype=I32)[None, :] < (padded - counts)[:, None]
    pad_key = jnp.where(pad_ok, 2 * experts[:, None] + 1, 2 * N_EXPERTS).reshape(-1)
    keys = jnp.concatenate([2 * e_flat, pad_key])
    vals = jnp.concatenate([jnp.arange(m, dtype=I32), jnp.full((N_EXPERTS * bm,), -1, I32)])
    _, asg = lax.sort((keys, vals), num_keys=1, is_stable=True)
    valid = asg >= 0
    tok = jnp.maximum(asg, 0) // TOP_K
    choice = jnp.maximum(asg, 0) % TOP_K
    slot_idx = jnp.concatenate([tok.reshape(n_blocks, bm), (choice * n_tok + tok).reshape(n_blocks, bm)], axis=1)
    blk_valid = jnp.sum(valid.reshape(n_blocks, bm).astype(I32), axis=1)
    blk_start = jnp.arange(n_blocks, dtype=I32) * bm
    blk_expert = jnp.minimum(jnp.sum((blk_start[:, None] >= pend[None, :]).astype(I32), axis=1), N_EXPERTS - 1)
    n_used = (pend[-1] // bm).astype(I32).reshape(1)
    return blk_expert, blk_valid, n_used, slot_idx.reshape(n_blocks, 1, 2 * bm)


def _comb_kernel(y4_ref, gate_ref, x_ref, mod_ref, ln_ref, o_ref, *, alpha):
    y = jnp.zeros(x_ref.shape, F32)
    for k in range(TOP_K):
        y = y + gate_ref[:, k:k + 1] * _load_token_tiles(y4_ref.at[k], 0, x_ref.shape[0])
    u = alpha * x_ref[...] + (1.0 + mod_ref[5:6, :]) * y
    o_ref[...] = _ln(u) * ln_ref[2:3, :] + ln_ref[3:4, :]


def _combine(y4, gate, x1, mod, ln_rows, alpha):
    bsz, t, d = x1.shape
    tm = min(512, t)
    nt = t // tm
    return pl.pallas_call(
        functools.partial(_comb_kernel, alpha=alpha),
        grid=(bsz, nt),
        in_specs=[pl.BlockSpec((TOP_K, tm * SUBLANES, LANES), lambda b, i: (0, b * nt + i, 0)),
                  pl.BlockSpec((None, tm, 8), lambda b, i: (b, i, 0)),
                  pl.BlockSpec((None, tm, d), lambda b, i: (b, i, 0)),
                  pl.BlockSpec((None, 8, d), lambda b, i: (b, 0, 0)),
                  pl.BlockSpec((4, d), lambda b, i: (0, 0))],
        out_specs=pl.BlockSpec((None, tm, d), lambda b, i: (b, i, 0)),
        out_shape=jax.ShapeDtypeStruct((bsz, t, d), F32),
        compiler_params=_cparams("parallel", "parallel"),
        name="comb",
    )(y4, gate, x1, mod, ln_rows)


def _rel_bucket(dist):
    n = jnp.maximum(dist, 0)
    max_exact = N_BUCKETS // 2
    nf = jnp.maximum(n, 1).astype(F32)
    large = max_exact + (jnp.log(nf / max_exact) / math.log(MAX_DISTANCE / max_exact)
                         * (N_BUCKETS - max_exact)).astype(I32)
    large = jnp.minimum(large, N_BUCKETS - 1)
    return jnp.where(n < max_exact, n, large)


def _bias_tiles(bias_tab, tq, dil, key_major):
    period = 2 * tq
    k = np.arange(period)
    d = np.where(k < tq, -k, period - k)
    dist = np.stack([np.maximum(d, 0), d + tq]) * dil
    line = bias_tab.astype(F32)[_rel_bucket(jnp.asarray(dist, I32))]
    line = jnp.moveaxis(line, -1, 0)
    flat = jnp.tile(line, (1, 1, tq))[..., :tq * (period - 1)]
    tiles = flat.reshape(line.shape[0], 2, tq, period - 1)[..., :tq]
    if key_major:
        tiles = (tiles - bias_tab.astype(F32)[N_BUCKETS - 1][:, None, None, None]) * LOG2E
        tiles = jnp.swapaxes(tiles, -1, -2)
    return tiles


def _split_w_in(w_in):
    d = w_in.shape[0]
    g = GROUP_W
    segs = {}
    o = 0
    for name, width in (("qa", g), ("ka", g), ("va", g), ("qb", g), ("kb", g), ("vb", g),
                        ("qc", g), ("kc", g), ("vc", g), ("qx", IDX_HEADS * IDX_DIM), ("kx", IDX_DIM),
                        ("wx", IDX_HEADS), ("qd", g), ("kd", g), ("vd", g)):
        segs[name] = w_in[:, o:o + width]
        o += width
    pad = jnp.zeros((d, LANES - IDX_DIM - IDX_HEADS), w_in.dtype)
    w = jnp.concatenate([segs[n] for n in ("qa", "ka", "va", "qb", "kb", "vb", "kc", "kd", "kx", "wx")] + [pad],
                        axis=1)
    wt = jnp.concatenate([segs[n] for n in ("qx", "qc", "vc", "qd", "vd", "wx")], axis=1).T
    return w.astype(BF16), wt.astype(BF16)


def _layer(x, c, layer, depth, p, tiles):
    bsz, t, d = x.shape
    alpha = (2 * depth) ** 0.25
    mod = _ada_mod(c, p["w_ada_all"], p["b_ada_all"], layer).reshape(bsz, 6, d)
    mod = jnp.concatenate([mod, jnp.zeros((bsz, 2, d), F32)], axis=1)
    ln_rows = jnp.concatenate([p["ln_g"][0:1], p["ln_b"][0:1], p["ln_g"][1:2], p["ln_b"][1:2]], axis=0)

    proj, proj_t = _ln_mod_proj(x, mod, *_split_w_in(p["w_in"]))

    o_a = _stick_breaking(proj)
    o_b = _dilated(proj, tiles["b"])
    o_c = _dsa(proj, proj_t, tiles["c"])
    lamp = p["diff_lam"].astype(F32)
    lambda_init = 0.8 - 0.6 * math.exp(-0.3 * layer)
    lam = jnp.exp(jnp.sum(lamp[0] * lamp[1])) - jnp.exp(jnp.sum(lamp[2] * lamp[3])) + lambda_init
    o_d = _differential(proj, proj_t, lam, tiles["d"], p["diff_g"], lambda_init)

    x1, h2, top_idx, gate = _post_mixer((o_a, o_b, o_c, o_d), p["w_out"].astype(BF16), x, mod, ln_rows,
                                        p["w_router"].T, p["b_router"], alpha)

    top_idx = top_idx[:, :TOP_K, :].transpose(0, 2, 1).reshape(bsz * t, TOP_K)
    blk_expert, blk_valid, n_used, slot_idx = _moe_dispatch(top_idx, MOE_BLOCK)
    n_grp = p["b1"].shape[-1] // (2 * FF_GROUP)
    b1p = p["b1"].reshape(N_EXPERTS, n_grp, FF_GROUP, 2).transpose(0, 1, 3, 2).reshape(N_EXPERTS, 1, -1)
    y_rows = _moe_experts(h2, blk_expert, blk_valid, n_used, slot_idx,
                          _deinterleave_w1(p["w1_all"], layer), b1p, p["w2_all"], p["b2"][:, None, :], layer)
    return _combine(y_rows.reshape(TOP_K, bsz * t * SUBLANES, LANES), gate.transpose(0, 2, 1), x1, mod, ln_rows,
                    alpha)


def kernel(x, c, w_ada, b_ada, w_in, w_out, diff_lam, diff_g, ln_g, ln_b, w_router, b_router, w1, b1, w2, b2,
           rel_bias):
    depth = w_in.shape[0]
    t = x.shape[1]
    tq = min(ATT_BLOCK, t)
    tiles = dict(
        b=[_bias_tiles(rel_bias[:, :GROUP_HEADS], min(128, t // dil), dil, False) for _, dil in DILATED_CONFIGS],
        c=_bias_tiles(rel_bias[:, GROUP_HEADS:2 * GROUP_HEADS], tq, 1, True),
        d=_bias_tiles(rel_bias[:, 2 * GROUP_HEADS:], tq, 1, True))
    w2_all = w2.astype(BF16)
    for layer in range(depth):
        p = dict(w_ada_all=w_ada, b_ada_all=b_ada, w_in=w_in[layer], w_out=w_out[layer],
                 diff_lam=diff_lam[layer], diff_g=diff_g[layer], ln_g=ln_g[layer], ln_b=ln_b[layer],
                 w_router=w_router[layer], b_router=b_router[layer], w1_all=w1, b1=b1[layer],
                 w2_all=w2_all, b2=b2[layer])
        x = _layer(x, c, layer, depth, p, tiles)
    return x
```

```python
import functools
import math

import numpy as np
import jax
import jax.numpy as jnp
from jax import lax
from jax.experimental import pallas as pl
from jax.experimental.pallas import tpu as pltpu

F32 = jnp.float32
BF16 = jnp.bfloat16
I32 = jnp.int32

HEAD_DIM = 64
GROUP_HEADS = 4
GROUP_W = GROUP_HEADS * HEAD_DIM
DIFF_QK_DIM = HEAD_DIM // 2
DILATED_CONFIGS = ((128, 1), (512, 4), (2048, 16))
IDX_HEADS = 16
IDX_DIM = 64
INDEX_TOPK_MAX = 256
N_EXPERTS = 32
TOP_K = 4
SWIGLU_ALPHA = 1.702
SWIGLU_LIMIT = 7.0
N_BUCKETS = 32
MAX_DISTANCE = 128
LN_EPS = 1e-5
MOE_BLOCK = 512
DMA_ISSUE_CHUNK = 64
PROJ_TILE = 1024
POST_TILE = 1024
COMB_TILE = 512
MIX_TILE = 1024

LANES = 128
SUBLANES = 8
VMEM_LIMIT_BYTES = 56 * 1024 * 1024
NEG_BIG = -1e30
SB_SKIP_LOG = -100.0
INT_MIN = -2 ** 31
NEG_BIG_BITS = int(np.float32(NEG_BIG).view(np.int32))
LOG2E = math.log2(math.e)
ATT_BLOCK = 256
COUNT_ROWS = 4 * SUBLANES
DIL_SUB_BLOCKS = 4
FF_GROUP = 256

COL_A = 0
COL_B = COL_A + 3 * GROUP_W
COL_KC = COL_B + 3 * GROUP_W
COL_KD = COL_KC + GROUP_W
COL_TAIL = COL_KD + GROUP_W
PROJ_COLS = COL_TAIL + LANES
ROW_QX = 0
ROW_QC = ROW_QX + IDX_HEADS * IDX_DIM
ROW_VC = ROW_QC + GROUP_W
ROW_QD = ROW_VC + GROUP_W
ROW_VD = ROW_QD + GROUP_W
ROW_WX = ROW_VD + GROUP_W
PROJ_ROWS = ROW_WX + IDX_HEADS


def _cparams(*sem):
    return pltpu.CompilerParams(dimension_semantics=sem, vmem_limit_bytes=VMEM_LIMIT_BYTES)


def _ln(x):
    mu = jnp.mean(x, axis=-1, keepdims=True)
    xc = x - mu
    return xc * lax.rsqrt(jnp.mean(xc * xc, axis=-1, keepdims=True) + LN_EPS)


def _dot_nt(a, b):
    return lax.dot_general(a, b, (((1,), (1,)), ((), ())), preferred_element_type=F32)


def _dot(a, b):
    return jnp.dot(a, b, preferred_element_type=F32)


def _load_token_tiles(ref, first_tok, n):
    base = first_tok * SUBLANES
    return jnp.concatenate([ref[pl.ds(base + c, n, stride=SUBLANES), :] for c in range(SUBLANES)], axis=1)


def _store_token_tiles(ref, first_tok, val):
    n = val.shape[0]
    base = first_tok * SUBLANES
    for c in range(SUBLANES):
        ref[pl.ds(base + c, n, stride=SUBLANES), :] = val[:, c * LANES:(c + 1) * LANES]


def _ada_kernel(c_ref, w_ref, b_ref, o_ref):
    o_ref[...] = jnp.dot(c_ref[...], w_ref[...], precision=lax.Precision.HIGHEST,
                         preferred_element_type=F32) + b_ref[...]


def _ada_mod(c, w_all, b_all, layer):
    bsz, d = c.shape
    depth, _, n = w_all.shape
    return pl.pallas_call(
        _ada_kernel,
        grid=(n // d,),
        in_specs=[pl.BlockSpec((bsz, d), lambda j: (0, 0)),
                  pl.BlockSpec((None, d, d), lambda j: (layer, 0, j)),
                  pl.BlockSpec((None, 1, d), lambda j: (layer, 0, j))],
        out_specs=pl.BlockSpec((bsz, d), lambda j: (0, j)),
        out_shape=jax.ShapeDtypeStruct((bsz, n), F32),
        compiler_params=_cparams("arbitrary"),
        name="ada",
    )(c, w_all, b_all.reshape(depth, 1, n))


def _proj_kernel(x_ref, mod_ref, w_ref, wt_ref, o_ref, ot_ref, *, chunk):
    h = (_ln(x_ref[...]) * (1.0 + mod_ref[1:2, :]) + mod_ref[0:1, :]).astype(BF16)
    ncol = o_ref.shape[-1]
    for c0 in range(0, ncol, chunk):
        c1 = min(c0 + chunk, ncol)
        o_ref[:, c0:c1] = _dot(h, w_ref[:, c0:c1]).astype(BF16)
    n_sub, nrow, tq = ot_ref.shape
    for r0 in range(0, nrow, chunk):
        r1 = min(r0 + chunk, nrow)
        res = _dot_nt(wt_ref[r0:r1, :], h).astype(BF16)
        for j in range(n_sub):
            ot_ref[j, r0:r1, :] = res[:, j * tq:(j + 1) * tq]


def _ln_mod_proj(x, mod, w, wt):
    bsz, t, d = x.shape
    ncol, nrow = w.shape[1], wt.shape[0]
    tq = min(ATT_BLOCK, t)
    tm = min(PROJ_TILE, t)
    return pl.pallas_call(
        functools.partial(_proj_kernel, chunk=2 * LANES),
        grid=(bsz, t // tm),
        in_specs=[pl.BlockSpec((None, tm, d), lambda b, i: (b, i, 0)),
                  pl.BlockSpec((None, 8, d), lambda b, i: (b, 0, 0)),
                  pl.BlockSpec((d, ncol), lambda b, i: (0, 0)),
                  pl.BlockSpec((nrow, d), lambda b, i: (0, 0))],
        out_specs=[pl.BlockSpec((None, tm, ncol), lambda b, i: (b, i, 0)),
                   pl.BlockSpec((None, tm // tq, nrow, tq), lambda b, i: (b, i, 0, 0))],
        out_shape=[jax.ShapeDtypeStruct((bsz, t, ncol), BF16),
                   jax.ShapeDtypeStruct((bsz, t // tq, nrow, tq), BF16)],
        compiler_params=_cparams("parallel", "parallel"),
        name="proj",
    )(x, mod, w, wt)


def _by_head(x):
    lane = lax.broadcasted_iota(I32, x.shape, 1)
    zero = jnp.zeros_like(x)
    return jnp.concatenate([jnp.where(jnp.logical_and(lane >= h * HEAD_DIM, lane < (h + 1) * HEAD_DIM), x, zero)
                            for h in range(GROUP_HEADS)], axis=0)


def _sb_kernel(q_ref, k_ref, v_ref, o_ref, *, tq, scale):
    qi = pl.program_id(1)
    r = lax.broadcasted_iota(I32, (tq, tq), 0)
    c = lax.broadcasted_iota(I32, (tq, tq), 1)
    strict_lower = c < r
    upper = jnp.where(r > c, 1.0, 0.0).astype(BF16)

    q = q_ref[...]
    valid = jnp.concatenate([strict_lower] * GROUP_HEADS, axis=1)

    def step(kb, carry, acc, masked):
        start = pl.multiple_of(kb * tq, tq)
        z = _dot_nt(q, _by_head(k_ref[pl.ds(start, tq), :])) * scale
        log_sig = jnp.minimum(z, 0.0) - jnp.log(1.0 + jnp.exp(-jnp.abs(z)))
        log_fail = log_sig - z
        if masked:
            log_fail = jnp.where(valid, log_fail, 0.0)
        lf = jnp.concatenate([log_fail[:, h * tq:(h + 1) * tq] for h in range(GROUP_HEADS)], axis=0)
        lf_hi = lf.astype(BF16)
        lf_lo = (lf - lf_hi.astype(F32)).astype(BF16)
        after = _dot(lf_hi, upper) + _dot(lf_lo, upper)
        after = jnp.concatenate([after[h * tq:(h + 1) * tq] + carry[h] for h in range(GROUP_HEADS)], axis=1)
        w = jnp.exp(log_sig + after)
        if masked:
            w = jnp.where(valid, w, 0.0)
        acc = acc + _dot(w.astype(BF16), _by_head(v_ref[pl.ds(start, tq), :]))
        carry = tuple(carry[h] + jnp.sum(log_fail[:, h * tq:(h + 1) * tq], axis=1, keepdims=True)
                      for h in range(GROUP_HEADS))
        return carry, acc

    def live(carry):
        top = carry[0]
        for h in range(1, GROUP_HEADS):
            top = jnp.maximum(top, carry[h])
        return (jnp.max(top) > SB_SKIP_LOG).astype(I32)

    carry, acc = step(qi, tuple(jnp.zeros((tq, 1), F32) for _ in range(GROUP_HEADS)),
                      jnp.zeros((tq, GROUP_W), F32), True)

    def body(s):
        kb, carry, acc, _ = s
        carry, acc = step(kb, carry, acc, False)
        return kb - 1, carry, acc, live(carry)

    _, _, acc, _ = lax.while_loop(lambda s: jnp.logical_and(s[0] >= 0, s[3] > 0), body,
                                  (qi - 1, carry, acc, live(carry)))
    o_ref[...] = acc.astype(BF16)


def _stick_breaking(proj):
    bsz, t, _ = proj.shape
    tq = min(ATT_BLOCK, t)
    cb = COL_A // GROUP_W
    return pl.pallas_call(
        functools.partial(_sb_kernel, tq=tq, scale=HEAD_DIM ** -0.5),
        grid=(bsz, t // tq),
        in_specs=[pl.BlockSpec((None, tq, GROUP_W), lambda b, i: (b, i, cb)),
                  pl.BlockSpec((None, t, GROUP_W), lambda b, i: (b, 0, cb + 1)),
                  pl.BlockSpec((None, t, GROUP_W), lambda b, i: (b, 0, cb + 2))],
        out_specs=pl.BlockSpec((None, tq, GROUP_W), lambda b, i: (b, i, 0)),
        out_shape=jax.ShapeDtypeStruct((bsz, t, GROUP_W), BF16),
        compiler_params=_cparams("parallel", "arbitrary"),
        name="sb",
    )(proj, proj, proj)


ONES_ROWS = 16


def _softmax_init_t(tq):
    return (jnp.full((1, tq), NEG_BIG, F32), jnp.zeros((1, tq), F32), jnp.zeros((HEAD_DIM, tq), F32))


def _chain_logits(k_ref, start, tk, chains, qt):
    outs = [None] * len(chains)
    for half in range(GROUP_W // LANES):
        idx = [i for i, (lo, hi) in enumerate(chains) if lo // LANES == half]
        assert all((chains[i][1] - 1) // LANES == half for i in idx)
        khalf = k_ref[pl.ds(start, tk), half * LANES:(half + 1) * LANES]
        lane = lax.broadcasted_iota(I32, khalf.shape, 1) + half * LANES
        zero = jnp.zeros_like(khalf)
        lhs = jnp.concatenate([jnp.where(jnp.logical_and(lane >= chains[i][0], lane < chains[i][1]), khalf, zero)
                               for i in idx], axis=0)
        z = _dot(lhs, qt[half * LANES:(half + 1) * LANES, :])
        for j, i in enumerate(idx):
            outs[i] = z[j * tk:(j + 1) * tk]
    return outs


def _ones_rows(tk):
    r = lax.broadcasted_iota(I32, (ONES_ROWS, GROUP_HEADS * tk), 0)
    c = lax.broadcasted_iota(I32, (ONES_ROWS, GROUP_HEADS * tk), 1)
    lo = r * tk
    return jnp.where(jnp.logical_and(c >= lo, c < lo + tk), 1.0, 0.0).astype(BF16)


def _value_blockdiag(vblk, ones):
    tk = vblk.shape[1]
    zero = jnp.zeros((HEAD_DIM, tk), BF16)
    rows = [jnp.concatenate([vblk[h * HEAD_DIM:(h + 1) * HEAD_DIM, :] if j == h else zero
                             for j in range(GROUP_HEADS)], axis=1) for h in range(GROUP_HEADS)]
    return jnp.concatenate(rows + [ones], axis=0)


def _softmax_weights(z, m):
    m_new = jnp.maximum(m, jnp.max(z, axis=0, keepdims=True))
    return m_new, jnp.exp2(m - m_new), jnp.exp2(z - m_new).astype(BF16)


def _causal_blocks(qi, logits, step, st):
    st = lax.fori_loop(0, jnp.maximum(qi - 1, 0), lambda kb, s: step(logits(kb), kb, s, None, False), st)
    st = lax.cond(qi >= 1, lambda s: step(logits(qi - 1), qi - 1, s, 1, False), lambda s: s, st)
    return step(logits(qi), qi, st, 0, True)


def _diff_kernel(lam_ref, qt_ref, k_ref, vt_ref, bias_ref, g_ref, o_ref, *, tq, c_scale, out_scale):
    qi = pl.program_id(1)
    r = lax.broadcasted_iota(I32, (tq, tq), 0)
    c = lax.broadcasted_iota(I32, (tq, tq), 1)
    causal = r <= c
    lam = lam_ref[0]
    qt = qt_ref[...]
    ones = _ones_rows(tq)
    chains = [(h * HEAD_DIM + j * DIFF_QK_DIM, h * HEAD_DIM + (j + 1) * DIFF_QK_DIM)
              for h in range(GROUP_HEADS) for j in range(2)]

    def logits(kb):
        return tuple(_chain_logits(k_ref, pl.multiple_of(kb * tq, tq), tq, chains, qt))

    def step(zs, kb, st, which, masked):
        ms, alphas, ps = [], [], []
        for i, z in enumerate(zs):
            z = z * c_scale
            if which is not None:
                z = z + bias_ref[i // 2, which]
            if masked:
                z = jnp.where(causal, z, NEG_BIG)
            m_new, alpha, p = _softmax_weights(z, st[i][0])
            ms.append(m_new)
            alphas.append(alpha)
            ps.append(p)
        pmat = jnp.concatenate([jnp.concatenate([ps[2 * h], ps[2 * h + 1]], axis=1) for h in range(GROUP_HEADS)],
                               axis=0)
        pv = _dot(_value_blockdiag(vt_ref[kb], ones), pmat)
        out = []
        for i in range(len(chains)):
            h, j = divmod(i, 2)
            cols = slice(j * tq, (j + 1) * tq)
            l = alphas[i] * st[i][1] + pv[GROUP_W + h:GROUP_W + h + 1, cols]
            acc = alphas[i] * st[i][2] + pv[h * HEAD_DIM:(h + 1) * HEAD_DIM, cols]
            out.append((ms[i], l, acc))
        return tuple(out)

    st = _causal_blocks(qi, logits, step, tuple(_softmax_init_t(tq) for _ in chains))
    outs = []
    for h in range(GROUP_HEADS):
        (_, l1, a1), (_, l2, a2) = st[2 * h], st[2 * h + 1]
        o = a1 / l1 - lam * (a2 / l2)
        o = o * lax.rsqrt(jnp.mean(o * o, axis=0, keepdims=True) + LN_EPS)
        outs.append(o * g_ref[...] * out_scale)
    o_ref[...] = jnp.concatenate(outs, axis=0).T.astype(BF16)


def _differential(proj, proj_t, lam, bias_tiles, diff_g, lambda_init):
    bsz, t, _ = proj.shape
    nq, tq = proj_t.shape[1], proj_t.shape[3]
    grid_spec = pltpu.PrefetchScalarGridSpec(
        num_scalar_prefetch=1,
        grid=(bsz, nq),
        in_specs=[pl.BlockSpec((None, None, GROUP_W, tq), lambda b, i, lam: (b, i, ROW_QD // GROUP_W, 0)),
                  pl.BlockSpec((None, t, GROUP_W), lambda b, i, lam: (b, 0, COL_KD // GROUP_W)),
                  pl.BlockSpec((None, nq, GROUP_W, tq), lambda b, i, lam: (b, 0, ROW_VD // GROUP_W, 0)),
                  pl.BlockSpec(bias_tiles.shape, lambda b, i, lam: (0, 0, 0, 0)),
                  pl.BlockSpec((HEAD_DIM, 1), lambda b, i, lam: (0, 0))],
        out_specs=pl.BlockSpec((None, tq, GROUP_W), lambda b, i, lam: (b, i, 0)),
    )
    return pl.pallas_call(
        functools.partial(_diff_kernel, tq=tq, c_scale=DIFF_QK_DIM ** -0.5 * LOG2E, out_scale=1.0 - lambda_init),
        grid_spec=grid_spec,
        out_shape=jax.ShapeDtypeStruct((bsz, t, GROUP_W), BF16),
        compiler_params=_cparams("parallel", "arbitrary"),
        name="diff",
    )(lam.reshape(1).astype(F32), proj_t, proj, proj_t, bias_tiles, diff_g.reshape(HEAD_DIM, 1).astype(F32))


def _dsa_kernel(qx_ref, wx_ref, tail_ref, qt_ref, k_ref, vt_ref, bias_ref, o_ref, key_scr, cut_scr,
                *, tq, topk, c_scale, row_bits):
    qi = pl.program_id(1)
    nkb = qi + 1
    r = lax.broadcasted_iota(I32, (tq, tq), 0)
    c = lax.broadcasted_iota(I32, (tq, tq), 1)
    wx = wx_ref[...].astype(F32) * (IDX_HEADS ** -0.5 * IDX_DIM ** -0.5)
    wxb = [jnp.broadcast_to(wx[h:h + 1, :], (SUBLANES, tq)) for h in range(IDX_HEADS)]
    zpad = jnp.zeros((LANES - IDX_DIM, tq), BF16)
    qx = [jnp.concatenate([qx_ref[h * IDX_DIM:(h + 1) * IDX_DIM, :], zpad], axis=0) for h in range(IDX_HEADS)]

    def score_block(kb, _):
        start = pl.multiple_of(kb * tq, tq)
        kt = tail_ref[pl.ds(start, tq), :]
        s = jnp.zeros((tq // SUBLANES, SUBLANES, tq), F32)
        for h in range(IDX_HEADS):
            s = s + wxb[h] * jnp.maximum(_dot(kt, qx[h]), 0.0).reshape(tq // SUBLANES, SUBLANES, tq)
        s = s.reshape(tq, tq)
        s = jnp.where(s == 0.0, 0.0, s)
        s = jnp.where(r + kb * tq <= c + qi * tq, s, -jnp.inf)
        bits = pltpu.bitcast(s, I32)
        key_scr[kb] = bits ^ ((bits >> 31) & 0x7FFFFFFF)
        return 0

    lax.fori_loop(0, nkb, score_block, 0)

    def count(pred):
        def body(kb, acc):
            hit = jnp.where(pred(key_scr[kb], kb), 1.0, 0.0)
            return acc + jnp.sum(hit.reshape(tq // COUNT_ROWS, COUNT_ROWS, tq), axis=0)
        acc = lax.fori_loop(0, nkb, body, jnp.zeros((COUNT_ROWS, tq), F32))
        return jnp.sum(acc, axis=0, keepdims=True)

    def bit_step(i, s):
        thr, c_thr = s
        cand = thr + lax.shift_left(jnp.int32(1), 31 - i)
        cnt = count(lambda key, kb: key >= cand)
        ge = cnt >= topk
        return jnp.where(ge, cand, thr), jnp.where(ge, cnt, c_thr)

    n_keys = (nkb * tq).astype(F32)
    thr, c_thr = lax.fori_loop(0, 32, bit_step, (jnp.full((1, tq), INT_MIN, I32), jnp.full((1, tq), 1.0, F32) * n_keys))
    cut_scr[...] = jnp.full((SUBLANES, tq), 2 ** 30, I32)

    @pl.when(jnp.max(c_thr) > topk)
    def _():
        need = topk - count(lambda key, kb: key > thr)

        def row_step(i, lo):
            cand = lo + lax.shift_left(jnp.int32(1), row_bits - 1 - i)
            cnt = count(lambda key, kb: jnp.logical_and(key == thr, r + kb * tq < cand))
            return jnp.where(cnt < need, cand, lo)
        lo = lax.fori_loop(0, row_bits, row_step, jnp.zeros((1, tq), I32))
        cut_scr[...] = jnp.broadcast_to(lo, (SUBLANES, tq))

    cut = cut_scr[0:1, :]

    def select_block(kb, _):
        key = key_scr[kb]
        pos = r + kb * tq
        sel = jnp.logical_or(key > thr, jnp.logical_and(key == thr, pos <= cut))
        sel = jnp.logical_and(sel, pos <= c + qi * tq)
        key_scr[kb] = jnp.where(sel, 0, NEG_BIG_BITS)
        return 0

    lax.fori_loop(0, nkb, select_block, 0)

    qt = qt_ref[...]
    ones = _ones_rows(tq)
    chains = [(h * HEAD_DIM, (h + 1) * HEAD_DIM) for h in range(GROUP_HEADS)]

    def logits(kb):
        return tuple(_chain_logits(k_ref, pl.multiple_of(kb * tq, tq), tq, chains, qt))

    def step(zs, kb, st, which, masked):
        del masked
        mask = pltpu.bitcast(key_scr[kb], F32)
        ms, alphas, ps = [], [], []
        for h, z in enumerate(zs):
            z = z * c_scale + mask
            if which is not None:
                z = z + bias_ref[h, which]
            m_new, alpha, p = _softmax_weights(z, st[h][0])
            ms.append(m_new)
            alphas.append(alpha)
            ps.append(p)
        pv = _dot(_value_blockdiag(vt_ref[kb], ones), jnp.concatenate(ps, axis=0))
        return tuple((ms[h], alphas[h] * st[h][1] + pv[GROUP_W + h:GROUP_W + h + 1, :],
                      alphas[h] * st[h][2] + pv[h * HEAD_DIM:(h + 1) * HEAD_DIM, :]) for h in range(GROUP_HEADS))

    st = _causal_blocks(qi, logits, step, tuple(_softmax_init_t(tq) for _ in range(GROUP_HEADS)))
    o_ref[...] = jnp.concatenate([acc / l for _, l, acc in st], axis=0).T.astype(BF16)


def _dsa(proj, proj_t, bias_tiles):
    bsz, t, _ = proj.shape
    nq, tq = proj_t.shape[1], proj_t.shape[3]
    topk = min(INDEX_TOPK_MAX, t // 4)
    assert tq >= topk, "the threshold search needs at least topk keys in the first block"
    nqx = IDX_HEADS * IDX_DIM
    return pl.pallas_call(
        functools.partial(_dsa_kernel, tq=tq, topk=float(topk), c_scale=HEAD_DIM ** -0.5 * LOG2E,
                          row_bits=max(1, (t - 1).bit_length())),
        grid=(bsz, nq),
        in_specs=[pl.BlockSpec((None, None, nqx, tq), lambda b, i: (b, i, ROW_QX // nqx, 0)),
                  pl.BlockSpec((None, None, IDX_HEADS, tq), lambda b, i: (b, i, ROW_WX // IDX_HEADS, 0)),
                  pl.BlockSpec((None, t, LANES), lambda b, i: (b, 0, COL_TAIL // LANES)),
                  pl.BlockSpec((None, None, GROUP_W, tq), lambda b, i: (b, i, ROW_QC // GROUP_W, 0)),
                  pl.BlockSpec((None, t, GROUP_W), lambda b, i: (b, 0, COL_KC // GROUP_W)),
                  pl.BlockSpec((None, nq, GROUP_W, tq), lambda b, i: (b, 0, ROW_VC // GROUP_W, 0)),
                  pl.BlockSpec(bias_tiles.shape, lambda b, i: (0, 0, 0, 0))],
        out_specs=pl.BlockSpec((None, tq, GROUP_W), lambda b, i: (b, i, 0)),
        out_shape=jax.ShapeDtypeStruct((bsz, t, GROUP_W), BF16),
        scratch_shapes=[pltpu.VMEM((nq, tq, tq), I32), pltpu.VMEM((SUBLANES, tq), I32)],
        compiler_params=_cparams("parallel", "arbitrary"),
        name="dsa",
    )(proj_t, proj_t, proj, proj_t, proj, proj_t, bias_tiles)


def _dil_kernel(q_ref, kp_ref, kc_ref, vp_ref, vc_ref, bias_ref, ol_ref, *, tq, scale):
    qi = pl.program_id(1)
    r = lax.broadcasted_iota(I32, (tq, tq), 0)
    c = lax.broadcasted_iota(I32, (tq, tq), 1)
    diag_ok = c <= r
    for sb in range(q_ref.shape[0] // tq):
        rows = slice(sb * tq, (sb + 1) * tq)
        if sb == 0:
            kp, vp = kp_ref[...], vp_ref[...]
            prev_ok = jnp.logical_and(r <= c, qi > 0)
        else:
            before = slice((sb - 1) * tq, sb * tq)
            kp, vp = kc_ref[before, :], vc_ref[before, :]
            prev_ok = r <= c
        q = q_ref[rows, :]
        zp_all = _dot_nt(q, _by_head(kp)) * scale
        zd_all = _dot_nt(q, _by_head(kc_ref[rows, :])) * scale
        pps, pds, dens, lses = [], [], [], []
        for h in range(GROUP_HEADS):
            cols = slice(h * tq, (h + 1) * tq)
            zp = jnp.where(prev_ok, zp_all[:, cols] + bias_ref[h, 1], NEG_BIG)
            zd = jnp.where(diag_ok, zd_all[:, cols] + bias_ref[h, 0], NEG_BIG)
            m = jnp.maximum(jnp.max(zp, axis=1, keepdims=True), jnp.max(zd, axis=1, keepdims=True))
            pp = jnp.exp(zp - m)
            pd = jnp.exp(zd - m)
            den = jnp.sum(pp, axis=1, keepdims=True) + jnp.sum(pd, axis=1, keepdims=True)
            pps.append(pp.astype(BF16))
            pds.append(pd.astype(BF16))
            dens.append(jnp.broadcast_to(den, (tq, HEAD_DIM)))
            lses.append(jnp.broadcast_to(m + jnp.log(den), (tq, HEAD_DIM)))
        values = jnp.concatenate([_by_head(vp), _by_head(vc_ref[rows, :])], axis=0)
        ol_ref[rows, :GROUP_W] = _dot(jnp.concatenate(pps + pds, axis=1), values) / jnp.concatenate(dens, axis=-1)
        ol_ref[rows, GROUP_W:] = jnp.concatenate(lses, axis=-1)


def _dilated_one(qkv, bias_tiles):
    n, length, _ = qkv.shape
    tq = bias_tiles.shape[-1]
    nsub = max(n for n in (1, 2, DIL_SUB_BLOCKS) if length % (n * tq) == 0)
    tile = nsub * tq
    wide = lambda col: pl.BlockSpec((None, tile, GROUP_W), lambda b, i: (b, i, col))
    back = lambda col: pl.BlockSpec((None, tq, GROUP_W), lambda b, i: (b, jnp.maximum(i * nsub - 1, 0), col))
    return pl.pallas_call(
        functools.partial(_dil_kernel, tq=tq, scale=HEAD_DIM ** -0.5),
        grid=(n, length // tile),
        in_specs=[wide(0), back(1), wide(1), back(2), wide(2),
                  pl.BlockSpec(bias_tiles.shape, lambda b, i: (0, 0, 0, 0))],
        out_specs=pl.BlockSpec((None, tile, 2 * GROUP_W), lambda b, i: (b, i, 0)),
        out_shape=jax.ShapeDtypeStruct((n, length, 2 * GROUP_W), F32),
        compiler_params=_cparams("parallel", "arbitrary"),
        name="dil",
    )(qkv, qkv, qkv, qkv, qkv, bias_tiles)


def _dilmix_kernel(p0, p1, p2, out_ref):
    a0, a1, a2 = p0[:, GROUP_W:], p1[:, GROUP_W:], p2[:, GROUP_W:]
    m = jnp.maximum(jnp.maximum(a0, a1), a2)
    e0, e1, e2 = jnp.exp(a0 - m), jnp.exp(a1 - m), jnp.exp(a2 - m)
    mixed = e0 * p0[:, :GROUP_W] + e1 * p1[:, :GROUP_W] + e2 * p2[:, :GROUP_W]
    out_ref[...] = (mixed / (e0 + e1 + e2)).astype(BF16)


def _dilated_mix(patterns):
    bsz, t, w = patterns[0].shape
    tm = min(MIX_TILE, t)
    return pl.pallas_call(
        _dilmix_kernel,
        grid=(bsz, t // tm),
        in_specs=[pl.BlockSpec((None, tm, w), lambda b, i: (b, i, 0))] * 3,
        out_specs=pl.BlockSpec((None, tm, w // 2), lambda b, i: (b, i, 0)),
        out_shape=jax.ShapeDtypeStruct((bsz, t, w // 2), BF16),
        compiler_params=_cparams("parallel", "parallel"),
        name="dilmix",
    )(*patterns)


def _dilated(proj, bias_tiles_per_cfg):
    bsz, t, _ = proj.shape
    qkv = proj[:, :, COL_B:COL_B + 3 * GROUP_W]
    patterns = []
    for (_, dil), tiles in zip(DILATED_CONFIGS, bias_tiles_per_cfg):
        def perm(a, dil=dil):
            w = a.shape[-1]
            return a.reshape(bsz, t // dil, dil, w).transpose(0, 2, 1, 3).reshape(bsz * dil, t // dil, w)

        def unperm(a, dil=dil):
            w = a.shape[-1]
            return a.reshape(bsz, dil, t // dil, w).transpose(0, 2, 1, 3).reshape(bsz, t, w)

        patterns.append(unperm(_dilated_one(perm(qkv), tiles)))
    return _dilated_mix(patterns)


def _post_kernel(oa_ref, ob_ref, oc_ref, od_ref, wo_ref, x_ref, mod_ref, ln_ref, wr_ref, br_ref,
                 x1_ref, h2_ref, idx_ref, gate_ref, *, alpha):
    y = jnp.zeros(x_ref.shape, F32)
    for g, o_ref in enumerate((oa_ref, ob_ref, oc_ref, od_ref)):
        y = y + _dot(o_ref[...], wo_ref[g * GROUP_W:(g + 1) * GROUP_W, :])
    u = alpha * x_ref[...] + (1.0 + mod_ref[2:3, :]) * y
    x1 = _ln(u) * ln_ref[0:1, :] + ln_ref[1:2, :]
    x1_ref[...] = x1
    h2 = _ln(x1) * (1.0 + mod_ref[4:5, :]) + mod_ref[3:4, :]
    _store_token_tiles(h2_ref, 0, h2)
    logits = lax.dot_general(wr_ref[...], h2, (((1,), (1,)), ((), ())), precision=lax.Precision.HIGHEST,
                             preferred_element_type=F32) + br_ref[...]
    n_exp, tm = logits.shape
    eid = lax.broadcasted_iota(I32, (n_exp, tm), 0)
    vals, ids = [], []
    for _ in range(TOP_K):
        m = jnp.max(logits, axis=0, keepdims=True)
        first = jnp.min(jnp.where(logits == m, eid, n_exp), axis=0, keepdims=True)
        vals.append(m)
        ids.append(first)
        logits = jnp.where(eid == first, -jnp.inf, logits)
    ex = [jnp.exp(v - vals[0]) for v in vals]
    den = ex[0] + ex[1] + ex[2] + ex[3]
    zero_f = jnp.zeros((8 - TOP_K, tm), F32)
    gate_ref[...] = jnp.concatenate([e / den for e in ex] + [zero_f], axis=0)
    idx_ref[...] = jnp.concatenate(ids + [zero_f.astype(I32)], axis=0)


def _post_mixer(o_groups, w_out, x, mod, ln_rows, w_router_t, b_router, alpha):
    bsz, t, d = x.shape
    tm = min(POST_TILE, t)
    n_exp = w_router_t.shape[0]
    og = pl.BlockSpec((None, tm, GROUP_W), lambda b, i: (b, i, 0))
    row = pl.BlockSpec((None, tm, d), lambda b, i: (b, i, 0))
    small = pl.BlockSpec((None, 8, tm), lambda b, i: (b, 0, i))
    assert d == SUBLANES * LANES, "token-tile layout needs one (8,128) tile per token"
    nt = t // tm
    tiles = pl.BlockSpec((tm * SUBLANES, LANES), lambda b, i: (b * nt + i, 0))
    return pl.pallas_call(
        functools.partial(_post_kernel, alpha=alpha),
        grid=(bsz, nt),
        in_specs=[og, og, og, og,
                  pl.BlockSpec(w_out.shape, lambda b, i: (0, 0)),
                  row,
                  pl.BlockSpec((None, 8, d), lambda b, i: (b, 0, 0)),
                  pl.BlockSpec((4, d), lambda b, i: (0, 0)),
                  pl.BlockSpec((n_exp, d), lambda b, i: (0, 0)),
                  pl.BlockSpec((n_exp, 1), lambda b, i: (0, 0))],
        out_specs=[row, tiles, small, small],
        out_shape=[jax.ShapeDtypeStruct((bsz, t, d), F32), jax.ShapeDtypeStruct((bsz * t * SUBLANES, LANES), F32),
                   jax.ShapeDtypeStruct((bsz, 8, t), I32), jax.ShapeDtypeStruct((bsz, 8, t), F32)],
        compiler_params=_cparams("parallel", "parallel"),
        name="post",
    )(*o_groups, w_out, x, mod, ln_rows, w_router_t, b_router.reshape(n_exp, 1))


def _deint_kernel(w_ref, p_ref, o_ref):
    grp = p_ref.shape[0]
    for g in range(w_ref.shape[1] // grp):
        cols = slice(g * grp, (g + 1) * grp)
        o_ref[:, cols] = _dot(w_ref[:, cols].astype(BF16), p_ref[...]).astype(BF16)


def _deinterleave_w1(w1_all, layer):
    _, n_exp, d, two_f = w1_all.shape
    grp = 2 * FF_GROUP
    j = np.arange(grp)
    src = np.where(j < FF_GROUP, 2 * j, 2 * (j - FF_GROUP) + 1)
    perm = jnp.asarray(np.arange(grp)[:, None] == src[None, :], BF16)
    return pl.pallas_call(
        _deint_kernel,
        grid=(n_exp,),
        in_specs=[pl.BlockSpec((None, None, d, two_f), lambda e: (layer, e, 0, 0)),
                  pl.BlockSpec((grp, grp), lambda e: (0, 0))],
        out_specs=pl.BlockSpec((None, d, two_f), lambda e: (e, 0, 0)),
        out_shape=jax.ShapeDtypeStruct((n_exp, d, two_f), BF16),
        compiler_params=_cparams("parallel"),
        name="deint",
    )(w1_all, perm)


def _moe_kernel(be_ref, nval_ref, nblk_ref, cur_ref, nxt_ref, h_hbm, w1_ref, b1_ref, w2_ref, b2_ref,
                out_hbm, xbuf, ybuf, gsem, ssem, *, bm):
    i = pl.program_id(0)
    nblk = nblk_ref[0]
    slot = i % 2

    def tile_rows(tok):
        return pl.ds(pl.multiple_of(tok * SUBLANES, SUBLANES), SUBLANES)

    def gather_row_copy(tok, r, s):
        return pltpu.make_async_copy(h_hbm.at[tile_rows(tok)], xbuf.at[tile_rows(s * bm + r)], gsem.at[s])

    def scatter_row_copy(row, r, s):
        return pltpu.make_async_copy(ybuf.at[tile_rows(s * bm + r)], out_hbm.at[tile_rows(row)], ssem.at[s])

    def block_rows(s):
        return pl.ds(pl.multiple_of(s * bm * SUBLANES, bm * SUBLANES), bm * SUBLANES)

    def issue_rows(start_row):
        def chunk(ci, _):
            base = ci * DMA_ISSUE_CHUNK
            for j in range(DMA_ISSUE_CHUNK):
                start_row(base + j, j % 2)
            return 0
        lax.fori_loop(0, bm // DMA_ISSUE_CHUNK, chunk, 0)

    def start_gather(idx_ref, s):
        issue_rows(lambda r, prio: gather_row_copy(idx_ref[0, r], r, s).start(priority=prio))

    def wait_gather(s):
        pltpu.make_async_copy(h_hbm.at[pl.ds(0, bm * SUBLANES)], xbuf.at[block_rows(s)], gsem.at[s]).wait()

    def start_scatter(s, n):
        @pl.when(n == bm)
        def _():
            issue_rows(lambda r, prio: scatter_row_copy(cur_ref[0, bm + r], r, s).start(priority=prio))

        @pl.when(n < bm)
        def _():
            def body(r, _):
                scatter_row_copy(cur_ref[0, bm + r], r, s).start()
                return 0
            lax.fori_loop(0, n, body, 0)

    def wait_scatter(s, n):
        @pl.when(n == bm)
        def _():
            pltpu.make_async_copy(ybuf.at[block_rows(s)], out_hbm.at[pl.ds(0, bm * SUBLANES)], ssem.at[s]).wait()

        @pl.when(n < bm)
        def _():
            def body(r, _):
                scatter_row_copy(0, r, s).wait()
                return 0
            lax.fori_loop(0, n, body, 0)

    @pl.when(jnp.logical_and(i == 0, nblk > 0))
    def _():
        start_gather(cur_ref, 0)

    @pl.when(i + 1 < nblk)
    def _():
        start_gather(nxt_ref, 1 - slot)

    @pl.when(i < nblk)
    def _():
        wait_gather(slot)

        @pl.when(i >= 2)
        def _():
            wait_scatter(slot, nval_ref[jnp.maximum(i - 2, 0)])

        x = _load_token_tiles(xbuf, slot * bm, bm).astype(BF16)
        y = jnp.zeros((bm, w2_ref.shape[1]), F32) + b2_ref[...]
        for g in range(w2_ref.shape[0] // FF_GROUP):
            cols = slice(2 * g * FF_GROUP, 2 * (g + 1) * FF_GROUP)
            hh = _dot(x, w1_ref[:, cols]) + b1_ref[:, cols]
            glu = jnp.minimum(hh[:, :FF_GROUP], SWIGLU_LIMIT)
            lin = jnp.clip(hh[:, FF_GROUP:], -SWIGLU_LIMIT, SWIGLU_LIMIT)
            act = glu * jax.nn.sigmoid(SWIGLU_ALPHA * glu) * (lin + 1.0)
            y = y + _dot(act.astype(BF16), w2_ref[g * FF_GROUP:(g + 1) * FF_GROUP, :])
        _store_token_tiles(ybuf, slot * bm, y)
        start_scatter(slot, nval_ref[i])

    last = pl.num_programs(0) - 1

    @pl.when(jnp.logical_and(i == last, nblk >= 2))
    def _():
        wait_scatter(nblk % 2, nval_ref[jnp.maximum(nblk - 2, 0)])

    @pl.when(jnp.logical_and(i == last, nblk >= 1))
    def _():
        wait_scatter((nblk + 1) % 2, nval_ref[jnp.maximum(nblk - 1, 0)])


def _moe_experts(h2, blk_expert, blk_valid, n_used, slot_idx, w1p, b1p, w2_all, b2, layer):
    n_tok = h2.shape[0] // SUBLANES
    d = SUBLANES * LANES
    n_blocks, _, two_bm = slot_idx.shape
    bm = two_bm // 2
    n_exp, _, two_f = w1p.shape
    f = two_f // 2
    idx_spec = lambda f_: pl.BlockSpec((None, 1, two_bm), f_, memory_space=pltpu.SMEM)
    wspec = lambda shp: pl.BlockSpec((None,) + shp, lambda i, be, nv, nb: (be[i], 0, 0))
    grid_spec = pltpu.PrefetchScalarGridSpec(
        num_scalar_prefetch=3,
        grid=(n_blocks,),
        in_specs=[idx_spec(lambda i, be, nv, nb: (i, 0, 0)),
                  idx_spec(lambda i, be, nv, nb: (jnp.minimum(i + 1, n_blocks - 1), 0, 0)),
                  pl.BlockSpec(memory_space=pl.ANY),
                  wspec((d, two_f)), wspec((1, two_f)),
                  pl.BlockSpec((None, None, f, d), lambda i, be, nv, nb: (layer, be[i], 0, 0)),
                  wspec((1, d))],
        out_specs=pl.BlockSpec(memory_space=pl.ANY),
        scratch_shapes=[pltpu.VMEM((2 * bm * SUBLANES, LANES), F32), pltpu.VMEM((2 * bm * SUBLANES, LANES), F32),
                        pltpu.SemaphoreType.DMA((2,)), pltpu.SemaphoreType.DMA((2,))],
    )
    return pl.pallas_call(
        functools.partial(_moe_kernel, bm=bm),
        grid_spec=grid_spec,
        out_shape=jax.ShapeDtypeStruct((n_tok * TOP_K * SUBLANES, LANES), F32),
        compiler_params=_cparams("arbitrary"),
        name="moe",
    )(blk_expert, blk_valid, n_used, slot_idx, slot_idx, h2, w1p, b1p, w2_all, b2)


def _moe_dispatch(top_idx, bm):
    n_tok = top_idx.shape[0]
    m = n_tok * TOP_K
    assert m % bm == 0
    e_flat = top_idx.reshape(-1)
    experts = jnp.arange(N_EXPERTS, dtype=I32)
    counts = jnp.sum((e_flat[:, None] == experts[None, :]).astype(I32), axis=0)
    padded = (counts + bm - 1) // bm * bm
    pend = jnp.cumsum(padded)
    n_blocks = m // bm + N_EXPERTS
    pad_ok = jnp.arange(bm, dtype=I32)[None, :] < (padded - counts)[:, None]
    pad_key = jnp.where(pad_ok, 2 * experts[:, None] + 1, 2 * N_EXPERTS).reshape(-1)
    keys = jnp.concatenate([2 * e_flat, pad_key])
    vals = jnp.concatenate([jnp.arange(m, dtype=I32), jnp.full((N_EXPERTS * bm,), -1, I32)])
    _, asg = lax.sort((keys, vals), num_keys=1, is_stable=True)
    valid = asg >= 0
    tok = jnp.maximum(asg, 0) // TOP_K
    choice = jnp.maximum(asg, 0) % TOP_K
    slot_idx = jnp.concatenate([tok.reshape(n_blocks, bm), (choice * n_tok + tok).reshape(n_blocks, bm)], axis=1)
    blk_valid = jnp.sum(valid.reshape(n_blocks, bm).astype(I32), axis=1)
    blk_start = jnp.arange(n_blocks, dtype=I32) * bm
    blk_expert = jnp.minimum(jnp.sum((blk_start[:, None] >= pend[None, :]).astype(I32), axis=1), N_EXPERTS - 1)
    n_used = (pend[-1] // bm).astype(I32).reshape(1)
    return blk_expert, blk_valid, n_used, slot_idx.reshape(n_blocks, 1, 2 * bm)


def _comb_kernel(y4_ref, gate_ref, x_ref, mod_ref, ln_ref, o_ref, *, alpha):
    y = jnp.zeros(x_ref.shape, F32)
    for k in range(TOP_K):
        y = y + gate_ref[:, k:k + 1] * _load_token_tiles(y4_ref.at[k], 0, x_ref.shape[0])
    u = alpha * x_ref[...] + (1.0 + mod_ref[5:6, :]) * y
    o_ref[...] = _ln(u) * ln_ref[2:3, :] + ln_ref[3:4, :]


def _combine(y4, gate, x1, mod, ln_rows, alpha):
    bsz, t, d = x1.shape
    tm = min(COMB_TILE, t)
    nt = t // tm
    return pl.pallas_call(
        functools.partial(_comb_kernel, alpha=alpha),
        grid=(bsz, nt),
        in_specs=[pl.BlockSpec((TOP_K, tm * SUBLANES, LANES), lambda b, i: (0, b * nt + i, 0)),
                  pl.BlockSpec((None, tm, 8), lambda b, i: (b, i, 0)),
                  pl.BlockSpec((None, tm, d), lambda b, i: (b, i, 0)),
                  pl.BlockSpec((None, 8, d), lambda b, i: (b, 0, 0)),
                  pl.BlockSpec((4, d), lambda b, i: (0, 0))],
        out_specs=pl.BlockSpec((None, tm, d), lambda b, i: (b, i, 0)),
        out_shape=jax.ShapeDtypeStruct((bsz, t, d), F32),
        compiler_params=_cparams("parallel", "parallel"),
        name="comb",
    )(y4, gate, x1, mod, ln_rows)


def _rel_bucket(dist):
    n = jnp.maximum(dist, 0)
    max_exact = N_BUCKETS // 2
    nf = jnp.maximum(n, 1).astype(F32)
    large = max_exact + (jnp.log(nf / max_exact) / math.log(MAX_DISTANCE / max_exact)
                         * (N_BUCKETS - max_exact)).astype(I32)
    large = jnp.minimum(large, N_BUCKETS - 1)
    return jnp.where(n < max_exact, n, large)


def _bias_tiles(bias_tab, tq, dil, key_major):
    period = 2 * tq
    k = np.arange(period)
    d = np.where(k < tq, -k, period - k)
    dist = np.stack([np.maximum(d, 0), d + tq]) * dil
    line = bias_tab.astype(F32)[_rel_bucket(jnp.asarray(dist, I32))]
    line = jnp.moveaxis(line, -1, 0)
    flat = jnp.tile(line, (1, 1, tq))[..., :tq * (period - 1)]
    tiles = flat.reshape(line.shape[0], 2, tq, period - 1)[..., :tq]
    if key_major:
        tiles = (tiles - bias_tab.astype(F32)[N_BUCKETS - 1][:, None, None, None]) * LOG2E
        tiles = jnp.swapaxes(tiles, -1, -2)
    return tiles


def _split_w_in(w_in):
    d = w_in.shape[0]
    g = GROUP_W
    segs = {}
    o = 0
    for name, width in (("qa", g), ("ka", g), ("va", g), ("qb", g), ("kb", g), ("vb", g),
                        ("qc", g), ("kc", g), ("vc", g), ("qx", IDX_HEADS * IDX_DIM), ("kx", IDX_DIM),
                        ("wx", IDX_HEADS), ("qd", g), ("kd", g), ("vd", g)):
        segs[name] = w_in[:, o:o + width]
        o += width
    pad = jnp.zeros((d, LANES - IDX_DIM - IDX_HEADS), w_in.dtype)
    w = jnp.concatenate([segs[n] for n in ("qa", "ka", "va", "qb", "kb", "vb", "kc", "kd", "kx", "wx")] + [pad],
                        axis=1)
    wt = jnp.concatenate([segs[n] for n in ("qx", "qc", "vc", "qd", "vd", "wx")], axis=1).T
    return w.astype(BF16), wt.astype(BF16)


def _layer(x, c, layer, depth, p, tiles):
    bsz, t, d = x.shape
    alpha = (2 * depth) ** 0.25
    mod = _ada_mod(c, p["w_ada_all"], p["b_ada_all"], layer).reshape(bsz, 6, d)
    mod = jnp.concatenate([mod, jnp.zeros((bsz, 2, d), F32)], axis=1)
    ln_rows = jnp.concatenate([p["ln_g"][0:1], p["ln_b"][0:1], p["ln_g"][1:2], p["ln_b"][1:2]], axis=0)

    proj, proj_t = _ln_mod_proj(x, mod, *_split_w_in(p["w_in"]))

    o_a = _stick_breaking(proj)
    o_b = _dilated(proj, tiles["b"])
    o_c = _dsa(proj, proj_t, tiles["c"])
    lamp = p["diff_lam"].astype(F32)
    lambda_init = 0.8 - 0.6 * math.exp(-0.3 * layer)
    lam = jnp.exp(jnp.sum(lamp[0] * lamp[1])) - jnp.exp(jnp.sum(lamp[2] * lamp[3])) + lambda_init
    o_d = _differential(proj, proj_t, lam, tiles["d"], p["diff_g"], lambda_init)

    x1, h2, top_idx, gate = _post_mixer((o_a, o_b, o_c, o_d), p["w_out"].astype(BF16), x, mod, ln_rows,
                                        p["w_router"].T, p["b_router"], alpha)

    top_idx = top_idx[:, :TOP_K, :].transpose(0, 2, 1).reshape(bsz * t, TOP_K)
    blk_expert, blk_valid, n_used, slot_idx = _moe_dispatch(top_idx, MOE_BLOCK)
    n_grp = p["b1"].shape[-1] // (2 * FF_GROUP)
    b1p = p["b1"].reshape(N_EXPERTS, n_grp, FF_GROUP, 2).transpose(0, 1, 3, 2).reshape(N_EXPERTS, 1, -1)
    y_rows = _moe_experts(h2, blk_expert, blk_valid, n_used, slot_idx,
                          _deinterleave_w1(p["w1_all"], layer), b1p, p["w2_all"], p["b2"][:, None, :], layer)
    return _combine(y_rows.reshape(TOP_K, bsz * t * SUBLANES, LANES), gate.transpose(0, 2, 1), x1, mod, ln_rows,
                    alpha)


def kernel(x, c, w_ada, b_ada, w_in, w_out, diff_lam, diff_g, ln_g, ln_b, w_router, b_router, w1, b1, w2, b2,
           rel_bias):
    depth = w_in.shape[0]
    t = x.shape[1]
    tq = min(ATT_BLOCK, t)
    tiles = dict(
        b=[_bias_tiles(rel_bias[:, :GROUP_HEADS], min(128, t // dil), dil, False) for _, dil in DILATED_CONFIGS],
        c=_bias_tiles(rel_bias[:, GROUP_HEADS:2 * GROUP_HEADS], tq, 1, True),
        d=_bias_tiles(rel_bias[:, 2 * GROUP_HEADS:], tq, 1, True))
    w2_all = w2.astype(BF16)
    for layer in range(depth):
        p = dict(w_ada_all=w_ada, b_ada_all=b_ada, w_in=w_in[layer], w_out=w_out[layer],
                 diff_lam=diff_lam[layer], diff_g=diff_g[layer], ln_g=ln_g[layer], ln_b=ln_b[layer],
                 w_router=w_router[layer], b_router=b_router[layer], w1_all=w1, b1=b1[layer],
                 w2_all=w2_all, b2=b2[layer])
        x = _layer(x, c, layer, depth, p, tiles)
    return x
```

```python
import functools
import math

import numpy as np
import jax
import jax.numpy as jnp
from jax import lax
from jax.experimental import pallas as pl
from jax.experimental.pallas import tpu as pltpu

F32 = jnp.float32
BF16 = jnp.bfloat16
I32 = jnp.int32

HEAD_DIM = 64
GROUP_HEADS = 4
GROUP_W = GROUP_HEADS * HEAD_DIM
DIFF_QK_DIM = HEAD_DIM // 2
DILATED_CONFIGS = ((128, 1), (512, 4), (2048, 16))
IDX_HEADS = 16
IDX_DIM = 64
INDEX_TOPK_MAX = 256
N_EXPERTS = 32
TOP_K = 4
SWIGLU_ALPHA = 1.702
SWIGLU_LIMIT = 7.0
N_BUCKETS = 32
MAX_DISTANCE = 128
LN_EPS = 1e-5
MOE_BLOCK = 512
DMA_ISSUE_CHUNK = 64
PROJ_TILE = 1024
POST_TILE = 1024
COMB_TILE = 512
MIX_TILE = 1024

LANES = 128
SUBLANES = 8
VMEM_LIMIT_BYTES = 56 * 1024 * 1024
NEG_BIG = -1e30
SB_SKIP_LOG = -100.0
INT_MIN = -2 ** 31
NEG_BIG_BITS = int(np.float32(NEG_BIG).view(np.int32))
LOG2E = math.log2(math.e)
ATT_BLOCK = 256
COUNT_ROWS = 4 * SUBLANES
DIL_SUB_BLOCKS = 4
FF_GROUP = 256

COL_A = 0
COL_B = COL_A + 3 * GROUP_W
COL_KC = COL_B + 3 * GROUP_W
COL_KD = COL_KC + GROUP_W
COL_TAIL = COL_KD + GROUP_W
PROJ_COLS = COL_TAIL + LANES
ROW_QX = 0
ROW_QC = ROW_QX + IDX_HEADS * IDX_DIM
ROW_VC = ROW_QC + GROUP_W
ROW_QD = ROW_VC + GROUP_W
ROW_VD = ROW_QD + GROUP_W
ROW_WX = ROW_VD + GROUP_W
PROJ_ROWS = ROW_WX + IDX_HEADS


def _cparams(*sem):
    return pltpu.CompilerParams(dimension_semantics=sem, vmem_limit_bytes=VMEM_LIMIT_BYTES)


def _ln(x):
    mu = jnp.mean(x, axis=-1, keepdims=True)
    xc = x - mu
    return xc * lax.rsqrt(jnp.mean(xc * xc, axis=-1, keepdims=True) + LN_EPS)


def _dot_nt(a, b):
    return lax.dot_general(a, b, (((1,), (1,)), ((), ())), preferred_element_type=F32)


def _dot(a, b):
    return jnp.dot(a, b, preferred_element_type=F32)


def _load_token_tiles(ref, first_tok, n):
    base = first_tok * SUBLANES
    return jnp.concatenate([ref[pl.ds(base + c, n, stride=SUBLANES), :] for c in range(SUBLANES)], axis=1)


def _store_token_tiles(ref, first_tok, val):
    n = val.shape[0]
    base = first_tok * SUBLANES
    for c in range(SUBLANES):
        ref[pl.ds(base + c, n, stride=SUBLANES), :] = val[:, c * LANES:(c + 1) * LANES]


def _ada_kernel(c_ref, w_ref, b_ref, o_ref):
    o_ref[...] = jnp.dot(c_ref[...], w_ref[...], precision=lax.Precision.HIGHEST,
                         preferred_element_type=F32) + b_ref[...]


def _ada_mod(c, w_all, b_all, layer):
    bsz, d = c.shape
    depth, _, n = w_all.shape
    return pl.pallas_call(
        _ada_kernel,
        grid=(n // d,),
        in_specs=[pl.BlockSpec((bsz, d), lambda j: (0, 0)),
                  pl.BlockSpec((None, d, d), lambda j: (layer, 0, j)),
                  pl.BlockSpec((None, 1, d), lambda j: (layer, 0, j))],
        out_specs=pl.BlockSpec((bsz, d), lambda j: (0, j)),
        out_shape=jax.ShapeDtypeStruct((bsz, n), F32),
        compiler_params=_cparams("arbitrary"),
        name="ada",
    )(c, w_all, b_all.reshape(depth, 1, n))


def _proj_kernel(x_ref, mod_ref, w_ref, wt_ref, o_ref, ot_ref, *, chunk):
    h = (_ln(x_ref[...]) * (1.0 + mod_ref[1:2, :]) + mod_ref[0:1, :]).astype(BF16)
    ncol = o_ref.shape[-1]
    for c0 in range(0, ncol, chunk):
        c1 = min(c0 + chunk, ncol)
        o_ref[:, c0:c1] = _dot(h, w_ref[:, c0:c1]).astype(BF16)
    n_sub, nrow, tq = ot_ref.shape
    for r0 in range(0, nrow, chunk):
        r1 = min(r0 + chunk, nrow)
        res = _dot_nt(wt_ref[r0:r1, :], h).astype(BF16)
        for j in range(n_sub):
            ot_ref[j, r0:r1, :] = res[:, j * tq:(j + 1) * tq]


def _ln_mod_proj(x, mod, w, wt):
    bsz, t, d = x.shape
    ncol, nrow = w.shape[1], wt.shape[0]
    tq = min(ATT_BLOCK, t)
    tm = min(PROJ_TILE, t)
    return pl.pallas_call(
        functools.partial(_proj_kernel, chunk=2 * LANES),
        grid=(bsz, t // tm),
        in_specs=[pl.BlockSpec((None, tm, d), lambda b, i: (b, i, 0)),
                  pl.BlockSpec((None, 8, d), lambda b, i: (b, 0, 0)),
                  pl.BlockSpec((d, ncol), lambda b, i: (0, 0)),
                  pl.BlockSpec((nrow, d), lambda b, i: (0, 0))],
        out_specs=[pl.BlockSpec((None, tm, ncol), lambda b, i: (b, i, 0)),
                   pl.BlockSpec((None, tm // tq, nrow, tq), lambda b, i: (b, i, 0, 0))],
        out_shape=[jax.ShapeDtypeStruct((bsz, t, ncol), BF16),
                   jax.ShapeDtypeStruct((bsz, t // tq, nrow, tq), BF16)],
        compiler_params=_cparams("parallel", "parallel"),
        name="proj",
    )(x, mod, w, wt)


def _by_head(x):
    lane = lax.broadcasted_iota(I32, x.shape, 1)
    zero = jnp.zeros_like(x)
    return jnp.concatenate([jnp.where(jnp.logical_and(lane >= h * HEAD_DIM, lane < (h + 1) * HEAD_DIM), x, zero)
                            for h in range(GROUP_HEADS)], axis=0)


def _sb_kernel(q_ref, k_ref, v_ref, o_ref, *, tq, scale):
    qi = pl.program_id(1)
    r = lax.broadcasted_iota(I32, (tq, tq), 0)
    c = lax.broadcasted_iota(I32, (tq, tq), 1)
    strict_lower = c < r
    upper = jnp.where(r > c, 1.0, 0.0).astype(BF16)

    q = q_ref[...]
    valid = jnp.concatenate([strict_lower] * GROUP_HEADS, axis=1)

    def step(kb, carry, acc, masked):
        start = pl.multiple_of(kb * tq, tq)
        z = _dot_nt(q, _by_head(k_ref[pl.ds(start, tq), :])) * scale
        log_sig = jnp.minimum(z, 0.0) - jnp.log(1.0 + jnp.exp(-jnp.abs(z)))
        log_fail = log_sig - z
        if masked:
            log_fail = jnp.where(valid, log_fail, 0.0)
        lf = jnp.concatenate([log_fail[:, h * tq:(h + 1) * tq] for h in range(GROUP_HEADS)], axis=0)
        lf_hi = lf.astype(BF16)
        lf_lo = (lf - lf_hi.astype(F32)).astype(BF16)
        after = _dot(lf_hi, upper) + _dot(lf_lo, upper)
        after = jnp.concatenate([after[h * tq:(h + 1) * tq] + carry[h] for h in range(GROUP_HEADS)], axis=1)
        w = jnp.exp(log_sig + after)
        if masked:
            w = jnp.where(valid, w, 0.0)
        acc = acc + _dot(w.astype(BF16), _by_head(v_ref[pl.ds(start, tq), :]))
        carry = tuple(carry[h] + jnp.sum(log_fail[:, h * tq:(h + 1) * tq], axis=1, keepdims=True)
                      for h in range(GROUP_HEADS))
        return carry, acc

    def live(carry):
        top = carry[0]
        for h in range(1, GROUP_HEADS):
            top = jnp.maximum(top, carry[h])
        return (jnp.max(top) > SB_SKIP_LOG).astype(I32)

    carry, acc = step(qi, tuple(jnp.zeros((tq, 1), F32) for _ in range(GROUP_HEADS)),
                      jnp.zeros((tq, GROUP_W), F32), True)

    def body(s):
        kb, carry, acc, _ = s
        carry, acc = step(kb, carry, acc, False)
        return kb - 1, carry, acc, live(carry)

    _, _, acc, _ = lax.while_loop(lambda s: jnp.logical_and(s[0] >= 0, s[3] > 0), body,
                                  (qi - 1, carry, acc, live(carry)))
    o_ref[...] = acc.astype(BF16)


def _stick_breaking(proj):
    bsz, t, _ = proj.shape
    tq = min(ATT_BLOCK, t)
    cb = COL_A // GROUP_W
    return pl.pallas_call(
        functools.partial(_sb_kernel, tq=tq, scale=HEAD_DIM ** -0.5),
        grid=(bsz, t // tq),
        in_specs=[pl.BlockSpec((None, tq, GROUP_W), lambda b, i: (b, i, cb)),
                  pl.BlockSpec((None, t, GROUP_W), lambda b, i: (b, 0, cb + 1)),
                  pl.BlockSpec((None, t, GROUP_W), lambda b, i: (b, 0, cb + 2))],
        out_specs=pl.BlockSpec((None, tq, GROUP_W), lambda b, i: (b, i, 0)),
        out_shape=jax.ShapeDtypeStruct((bsz, t, GROUP_W), BF16),
        compiler_params=_cparams("parallel", "arbitrary"),
        name="sb",
    )(proj, proj, proj)


ONES_ROWS = 16


def _softmax_init_t(tq):
    return (jnp.full((1, tq), NEG_BIG, F32), jnp.zeros((1, tq), F32), jnp.zeros((HEAD_DIM, tq), F32))


def _chain_logits(k_ref, start, tk, chains, qt):
    outs = [None] * len(chains)
    for half in range(GROUP_W // LANES):
        idx = [i for i, (lo, hi) in enumerate(chains) if lo // LANES == half]
        assert all((chains[i][1] - 1) // LANES == half for i in idx)
        khalf = k_ref[pl.ds(start, tk), half * LANES:(half + 1) * LANES]
        lane = lax.broadcasted_iota(I32, khalf.shape, 1) + half * LANES
        zero = jnp.zeros_like(khalf)
        lhs = jnp.concatenate([jnp.where(jnp.logical_and(lane >= chains[i][0], lane < chains[i][1]), khalf, zero)
                               for i in idx], axis=0)
        z = _dot(lhs, qt[half * LANES:(half + 1) * LANES, :])
        for j, i in enumerate(idx):
            outs[i] = z[j * tk:(j + 1) * tk]
    return outs


def _ones_rows(tk):
    r = lax.broadcasted_iota(I32, (ONES_ROWS, GROUP_HEADS * tk), 0)
    c = lax.broadcasted_iota(I32, (ONES_ROWS, GROUP_HEADS * tk), 1)
    lo = r * tk
    return jnp.where(jnp.logical_and(c >= lo, c < lo + tk), 1.0, 0.0).astype(BF16)


def _value_blockdiag(vblk, ones):
    tk = vblk.shape[1]
    zero = jnp.zeros((HEAD_DIM, tk), BF16)
    rows = [jnp.concatenate([vblk[h * HEAD_DIM:(h + 1) * HEAD_DIM, :] if j == h else zero
                             for j in range(GROUP_HEADS)], axis=1) for h in range(GROUP_HEADS)]
    return jnp.concatenate(rows + [ones], axis=0)


def _softmax_weights(z, m):
    m_new = jnp.maximum(m, jnp.max(z, axis=0, keepdims=True))
    return m_new, jnp.exp2(m - m_new), jnp.exp2(z - m_new).astype(BF16)


def _causal_blocks(qi, logits, step, st):
    st = lax.fori_loop(0, jnp.maximum(qi - 1, 0), lambda kb, s: step(logits(kb), kb, s, None, False), st)
    st = lax.cond(qi >= 1, lambda s: step(logits(qi - 1), qi - 1, s, 1, False), lambda s: s, st)
    return step(logits(qi), qi, st, 0, True)


def _diff_kernel(lam_ref, qt_ref, k_ref, vt_ref, bias_ref, g_ref, o_ref, *, tq, c_scale, out_scale):
    qi = pl.program_id(1)
    r = lax.broadcasted_iota(I32, (tq, tq), 0)
    c = lax.broadcasted_iota(I32, (tq, tq), 1)
    causal = r <= c
    lam = lam_ref[0]
    qt = qt_ref[...]
    ones = _ones_rows(tq)
    chains = [(h * HEAD_DIM + j * DIFF_QK_DIM, h * HEAD_DIM + (j + 1) * DIFF_QK_DIM)
              for h in range(GROUP_HEADS) for j in range(2)]

    def logits(kb):
        return tuple(_chain_logits(k_ref, pl.multiple_of(kb * tq, tq), tq, chains, qt))

    def step(zs, kb, st, which, masked):
        ms, alphas, ps = [], [], []
        for i, z in enumerate(zs):
            z = z * c_scale
            if which is not None:
                z = z + bias_ref[i // 2, which]
            if masked:
                z = jnp.where(causal, z, NEG_BIG)
            m_new, alpha, p = _softmax_weights(z, st[i][0])
            ms.append(m_new)
            alphas.append(alpha)
            ps.append(p)
        pmat = jnp.concatenate([jnp.concatenate([ps[2 * h], ps[2 * h + 1]], axis=1) for h in range(GROUP_HEADS)],
                               axis=0)
        pv = _dot(_value_blockdiag(vt_ref[kb], ones), pmat)
        out = []
        for i in range(len(chains)):
            h, j = divmod(i, 2)
            cols = slice(j * tq, (j + 1) * tq)
            l = alphas[i] * st[i][1] + pv[GROUP_W + h:GROUP_W + h + 1, cols]
            acc = alphas[i] * st[i][2] + pv[h * HEAD_DIM:(h + 1) * HEAD_DIM, cols]
            out.append((ms[i], l, acc))
        return tuple(out)

    st = _causal_blocks(qi, logits, step, tuple(_softmax_init_t(tq) for _ in chains))
    outs = []
    for h in range(GROUP_HEADS):
        (_, l1, a1), (_, l2, a2) = st[2 * h], st[2 * h + 1]
        o = a1 / l1 - lam * (a2 / l2)
        o = o * lax.rsqrt(jnp.mean(o * o, axis=0, keepdims=True) + LN_EPS)
        outs.append(o * g_ref[...] * out_scale)
    o_ref[...] = jnp.concatenate(outs, axis=0).T.astype(BF16)


def _differential(proj, proj_t, lam, bias_tiles, diff_g, lambda_init):
    bsz, t, _ = proj.shape
    nq, tq = proj_t.shape[1], proj_t.shape[3]
    grid_spec = pltpu.PrefetchScalarGridSpec(
        num_scalar_prefetch=1,
        grid=(bsz, nq),
        in_specs=[pl.BlockSpec((None, None, GROUP_W, tq), lambda b, i, lam: (b, i, ROW_QD // GROUP_W, 0)),
                  pl.BlockSpec((None, t, GROUP_W), lambda b, i, lam: (b, 0, COL_KD // GROUP_W)),
                  pl.BlockSpec((None, nq, GROUP_W, tq), lambda b, i, lam: (b, 0, ROW_VD // GROUP_W, 0)),
                  pl.BlockSpec(bias_tiles.shape, lambda b, i, lam: (0, 0, 0, 0)),
                  pl.BlockSpec((HEAD_DIM, 1), lambda b, i, lam: (0, 0))],
        out_specs=pl.BlockSpec((None, tq, GROUP_W), lambda b, i, lam: (b, i, 0)),
    )
    return pl.pallas_call(
        functools.partial(_diff_kernel, tq=tq, c_scale=DIFF_QK_DIM ** -0.5 * LOG2E, out_scale=1.0 - lambda_init),
        grid_spec=grid_spec,
        out_shape=jax.ShapeDtypeStruct((bsz, t, GROUP_W), BF16),
        compiler_params=_cparams("parallel", "arbitrary"),
        name="diff",
    )(lam.reshape(1).astype(F32), proj_t, proj, proj_t, bias_tiles, diff_g.reshape(HEAD_DIM, 1).astype(F32))


def _dsa_kernel(qx_ref, wx_ref, tail_ref, qt_ref, k_ref, vt_ref, bias_ref, o_ref, key_scr, cut_scr,
                *, tq, topk, c_scale, row_bits):
    qi = pl.program_id(1)
    nkb = qi + 1
    r = lax.broadcasted_iota(I32, (tq, tq), 0)
    c = lax.broadcasted_iota(I32, (tq, tq), 1)
    wx = wx_ref[...].astype(F32) * (IDX_HEADS ** -0.5 * IDX_DIM ** -0.5)
    wxb = [jnp.broadcast_to(wx[h:h + 1, :], (SUBLANES, tq)) for h in range(IDX_HEADS)]
    zpad = jnp.zeros((LANES - IDX_DIM, tq), BF16)
    qx = [jnp.concatenate([qx_ref[h * IDX_DIM:(h + 1) * IDX_DIM, :], zpad], axis=0) for h in range(IDX_HEADS)]

    def score_block(kb, _):
        start = pl.multiple_of(kb * tq, tq)
        kt = tail_ref[pl.ds(start, tq), :]
        s = jnp.zeros((tq // SUBLANES, SUBLANES, tq), F32)
        for h in range(IDX_HEADS):
            s = s + wxb[h] * jnp.maximum(_dot(kt, qx[h]), 0.0).reshape(tq // SUBLANES, SUBLANES, tq)
        s = s.reshape(tq, tq)
        s = jnp.where(s == 0.0, 0.0, s)
        s = jnp.where(r + kb * tq <= c + qi * tq, s, -jnp.inf)
        bits = pltpu.bitcast(s, I32)
        key_scr[kb] = bits ^ ((bits >> 31) & 0x7FFFFFFF)
        return 0

    lax.fori_loop(0, nkb, score_block, 0)

    def count(pred):
        def body(kb, acc):
            hit = jnp.where(pred(key_scr[kb], kb), 1.0, 0.0)
            return acc + jnp.sum(hit.reshape(tq // COUNT_ROWS, COUNT_ROWS, tq), axis=0)
        acc = lax.fori_loop(0, nkb, body, jnp.zeros((COUNT_ROWS, tq), F32))
        return jnp.sum(acc, axis=0, keepdims=True)

    def bit_step(i, s):
        thr, c_thr = s
        cand = thr + lax.shift_left(jnp.int32(1), 31 - i)
        cnt = count(lambda key, kb: key >= cand)
        ge = cnt >= topk
        return jnp.where(ge, cand, thr), jnp.where(ge, cnt, c_thr)

    n_keys = (nkb * tq).astype(F32)
    thr, c_thr = lax.fori_loop(0, 32, bit_step, (jnp.full((1, tq), INT_MIN, I32), jnp.full((1, tq), 1.0, F32) * n_keys))
    cut_scr[...] = jnp.full((SUBLANES, tq), 2 ** 30, I32)

    @pl.when(jnp.max(c_thr) > topk)
    def _():
        need = topk - count(lambda key, kb: key > thr)

        def row_step(i, lo):
            cand = lo + lax.shift_left(jnp.int32(1), row_bits - 1 - i)
            cnt = count(lambda key, kb: jnp.logical_and(key == thr, r + kb * tq < cand))
            return jnp.where(cnt < need, cand, lo)
        lo = lax.fori_loop(0, row_bits, row_step, jnp.zeros((1, tq), I32))
        cut_scr[...] = jnp.broadcast_to(lo, (SUBLANES, tq))

    cut = cut_scr[0:1, :]

    def select_block(kb, _):
        key = key_scr[kb]
        pos = r + kb * tq
        sel = jnp.logical_or(key > thr, jnp.logical_and(key == thr, pos <= cut))
        sel = jnp.logical_and(sel, pos <= c + qi * tq)
        key_scr[kb] = jnp.where(sel, 0, NEG_BIG_BITS)
        return 0

    lax.fori_loop(0, nkb, select_block, 0)

    qt = qt_ref[...]
    ones = _ones_rows(tq)
    chains = [(h * HEAD_DIM, (h + 1) * HEAD_DIM) for h in range(GROUP_HEADS)]

    def logits(kb):
        return tuple(_chain_logits(k_ref, pl.multiple_of(kb * tq, tq), tq, chains, qt))

    def step(zs, kb, st, which, masked):
        del masked
        mask = pltpu.bitcast(key_scr[kb], F32)
        ms, alphas, ps = [], [], []
        for h, z in enumerate(zs):
            z = z * c_scale + mask
            if which is not None:
                z = z + bias_ref[h, which]
            m_new, alpha, p = _softmax_weights(z, st[h][0])
            ms.append(m_new)
            alphas.append(alpha)
            ps.append(p)
        pv = _dot(_value_blockdiag(vt_ref[kb], ones), jnp.concatenate(ps, axis=0))
        return tuple((ms[h], alphas[h] * st[h][1] + pv[GROUP_W + h:GROUP_W + h + 1, :],
                      alphas[h] * st[h][2] + pv[h * HEAD_DIM:(h + 1) * HEAD_DIM, :]) for h in range(GROUP_HEADS))

    st = _causal_blocks(qi, logits, step, tuple(_softmax_init_t(tq) for _ in range(GROUP_HEADS)))
    o_ref[...] = jnp.concatenate([acc / l for _, l, acc in st], axis=0).T.astype(BF16)


def _dsa(proj, proj_t, bias_tiles):
    bsz, t, _ = proj.shape
    nq, tq = proj_t.shape[1], proj_t.shape[3]
    topk = min(INDEX_TOPK_MAX, t // 4)
    assert tq >= topk, "the threshold search needs at least topk keys in the first block"
    nqx = IDX_HEADS * IDX_DIM
    return pl.pallas_call(
        functools.partial(_dsa_kernel, tq=tq, topk=float(topk), c_scale=HEAD_DIM ** -0.5 * LOG2E,
                          row_bits=max(1, (t - 1).bit_length())),
        grid=(bsz, nq),
        in_specs=[pl.BlockSpec((None, None, nqx, tq), lambda b, i: (b, i, ROW_QX // nqx, 0)),
                  pl.BlockSpec((None, None, IDX_HEADS, tq), lambda b, i: (b, i, ROW_WX // IDX_HEADS, 0)),
                  pl.BlockSpec((None, t, LANES), lambda b, i: (b, 0, COL_TAIL // LANES)),
                  pl.BlockSpec((None, None, GROUP_W, tq), lambda b, i: (b, i, ROW_QC // GROUP_W, 0)),
                  pl.BlockSpec((None, t, GROUP_W), lambda b, i: (b, 0, COL_KC // GROUP_W)),
                  pl.BlockSpec((None, nq, GROUP_W, tq), lambda b, i: (b, 0, ROW_VC // GROUP_W, 0)),
                  pl.BlockSpec(bias_tiles.shape, lambda b, i: (0, 0, 0, 0))],
        out_specs=pl.BlockSpec((None, tq, GROUP_W), lambda b, i: (b, i, 0)),
        out_shape=jax.ShapeDtypeStruct((bsz, t, GROUP_W), BF16),
        scratch_shapes=[pltpu.VMEM((nq, tq, tq), I32), pltpu.VMEM((SUBLANES, tq), I32)],
        compiler_params=_cparams("parallel", "arbitrary"),
        name="dsa",
    )(proj_t, proj_t, proj, proj_t, proj, proj_t, bias_tiles)


def _dil_kernel(q_ref, kp_ref, kc_ref, vp_ref, vc_ref, bias_ref, ol_ref, *, tq, scale):
    qi = pl.program_id(1)
    r = lax.broadcasted_iota(I32, (tq, tq), 0)
    c = lax.broadcasted_iota(I32, (tq, tq), 1)
    diag_ok = c <= r
    for sb in range(q_ref.shape[0] // tq):
        rows = slice(sb * tq, (sb + 1) * tq)
        if sb == 0:
            kp, vp = kp_ref[...], vp_ref[...]
            prev_ok = jnp.logical_and(r <= c, qi > 0)
        else:
            before = slice((sb - 1) * tq, sb * tq)
            kp, vp = kc_ref[before, :], vc_ref[before, :]
            prev_ok = r <= c
        q = q_ref[rows, :]
        zp_all = _dot_nt(q, _by_head(kp)) * scale
        zd_all = _dot_nt(q, _by_head(kc_ref[rows, :])) * scale
        pps, pds, dens, lses = [], [], [], []
        for h in range(GROUP_HEADS):
            cols = slice(h * tq, (h + 1) * tq)
            zp = jnp.where(prev_ok, zp_all[:, cols] + bias_ref[h, 1], NEG_BIG)
            zd = jnp.where(diag_ok, zd_all[:, cols] + bias_ref[h, 0], NEG_BIG)
            m = jnp.maximum(jnp.max(zp, axis=1, keepdims=True), jnp.max(zd, axis=1, keepdims=True))
            pp = jnp.exp(zp - m)
            pd = jnp.exp(zd - m)
            den = jnp.sum(pp, axis=1, keepdims=True) + jnp.sum(pd, axis=1, keepdims=True)
            pps.append(pp.astype(BF16))
            pds.append(pd.astype(BF16))
            dens.append(jnp.broadcast_to(den, (tq, HEAD_DIM)))
            lses.append(jnp.broadcast_to(m + jnp.log(den), (tq, HEAD_DIM)))
        values = jnp.concatenate([_by_head(vp), _by_head(vc_ref[rows, :])], axis=0)
        ol_ref[rows, :GROUP_W] = _dot(jnp.concatenate(pps + pds, axis=1), values) / jnp.concatenate(dens, axis=-1)
        ol_ref[rows, GROUP_W:] = jnp.concatenate(lses, axis=-1)


def _dilated_one(qkv, bias_tiles):
    n, length, _ = qkv.shape
    tq = bias_tiles.shape[-1]
    nsub = max(n for n in (1, 2, DIL_SUB_BLOCKS) if length % (n * tq) == 0)
    tile = nsub * tq
    wide = lambda col: pl.BlockSpec((None, tile, GROUP_W), lambda b, i: (b, i, col))
    back = lambda col: pl.BlockSpec((None, tq, GROUP_W), lambda b, i: (b, jnp.maximum(i * nsub - 1, 0), col))
    return pl.pallas_call(
        functools.partial(_dil_kernel, tq=tq, scale=HEAD_DIM ** -0.5),
        grid=(n, length // tile),
        in_specs=[wide(0), back(1), wide(1), back(2), wide(2),
                  pl.BlockSpec(bias_tiles.shape, lambda b, i: (0, 0, 0, 0))],
        out_specs=pl.BlockSpec((None, tile, 2 * GROUP_W), lambda b, i: (b, i, 0)),
        out_shape=jax.ShapeDtypeStruct((n, length, 2 * GROUP_W), F32),
        compiler_params=_cparams("parallel", "arbitrary"),
        name="dil",
    )(qkv, qkv, qkv, qkv, qkv, bias_tiles)


def _dilmix_kernel(p0, p1, p2, out_ref):
    a0, a1, a2 = p0[:, GROUP_W:], p1[:, GROUP_W:], p2[:, GROUP_W:]
    m = jnp.maximum(jnp.maximum(a0, a1), a2)
    e0, e1, e2 = jnp.exp(a0 - m), jnp.exp(a1 - m), jnp.exp(a2 - m)
    mixed = e0 * p0[:, :GROUP_W] + e1 * p1[:, :GROUP_W] + e2 * p2[:, :GROUP_W]
    out_ref[...] = (mixed / (e0 + e1 + e2)).astype(BF16)


def _dilated_mix(patterns):
    bsz, t, w = patterns[0].shape
    tm = min(MIX_TILE, t)
    return pl.pallas_call(
        _dilmix_kernel,
        grid=(bsz, t // tm),
        in_specs=[pl.BlockSpec((None, tm, w), lambda b, i: (b, i, 0))] * 3,
        out_specs=pl.BlockSpec((None, tm, w // 2), lambda b, i: (b, i, 0)),
        out_shape=jax.ShapeDtypeStruct((bsz, t, w // 2), BF16),
        compiler_params=_cparams("parallel", "parallel"),
        name="dilmix",
    )(*patterns)


def _dilated(proj, bias_tiles_per_cfg):
    bsz, t, _ = proj.shape
    qkv = proj[:, :, COL_B:COL_B + 3 * GROUP_W]
    patterns = []
    for (_, dil), tiles in zip(DILATED_CONFIGS, bias_tiles_per_cfg):
        def perm(a, dil=dil):
            w = a.shape[-1]
            return a.reshape(bsz, t // dil, dil, w).transpose(0, 2, 1, 3).reshape(bsz * dil, t // dil, w)

        def unperm(a, dil=dil):
            w = a.shape[-1]
            return a.reshape(bsz, dil, t // dil, w).transpose(0, 2, 1, 3).reshape(bsz, t, w)

        patterns.append(unperm(_dilated_one(perm(qkv), tiles)))
    return _dilated_mix(patterns)


def _post_kernel(oa_ref, ob_ref, oc_ref, od_ref, wo_ref, x_ref, mod_ref, ln_ref, wr_ref, br_ref,
                 x1_ref, h2_ref, idx_ref, gate_ref, *, alpha):
    y = jnp.zeros(x_ref.shape, F32)
    for g, o_ref in enumerate((oa_ref, ob_ref, oc_ref, od_ref)):
        y = y + _dot(o_ref[...], wo_ref[g * GROUP_W:(g + 1) * GROUP_W, :])
    u = alpha * x_ref[...] + (1.0 + mod_ref[2:3, :]) * y
    x1 = _ln(u) * ln_ref[0:1, :] + ln_ref[1:2, :]
    x1_ref[...] = x1
    h2 = _ln(x1) * (1.0 + mod_ref[4:5, :]) + mod_ref[3:4, :]
    _store_token_tiles(h2_ref, 0, h2)
    logits = lax.dot_general(wr_ref[...], h2, (((1,), (1,)), ((), ())), precision=lax.Precision.HIGHEST,
                             preferred_element_type=F32) + br_ref[...]
    n_exp, tm = logits.shape
    eid = lax.broadcasted_iota(I32, (n_exp, tm), 0)
    vals, ids = [], []
    for _ in range(TOP_K):
        m = jnp.max(logits, axis=0, keepdims=True)
        first = jnp.min(jnp.where(logits == m, eid, n_exp), axis=0, keepdims=True)
        vals.append(m)
        ids.append(first)
        logits = jnp.where(eid == first, -jnp.inf, logits)
    ex = [jnp.exp(v - vals[0]) for v in vals]
    den = ex[0] + ex[1] + ex[2] + ex[3]
    zero_f = jnp.zeros((8 - TOP_K, tm), F32)
    gate_ref[...] = jnp.concatenate([e / den for e in ex] + [zero_f], axis=0)
    idx_ref[...] = jnp.concatenate(ids + [zero_f.astype(I32)], axis=0)


def _post_mixer(o_groups, w_out, x, mod, ln_rows, w_router_t, b_router, alpha):
    bsz, t, d = x.shape
    tm = min(POST_TILE, t)
    n_exp = w_router_t.shape[0]
    og = pl.BlockSpec((None, tm, GROUP_W), lambda b, i: (b, i, 0))
    row = pl.BlockSpec((None, tm, d), lambda b, i: (b, i, 0))
    small = pl.BlockSpec((None, 8, tm), lambda b, i: (b, 0, i))
    assert d == SUBLANES * LANES, "token-tile layout needs one (8,128) tile per token"
    nt = t // tm
    tiles = pl.BlockSpec((tm * SUBLANES, LANES), lambda b, i: (b * nt + i, 0))
    return pl.pallas_call(
        functools.partial(_post_kernel, alpha=alpha),
        grid=(bsz, nt),
        in_specs=[og, og, og, og,
                  pl.BlockSpec(w_out.shape, lambda b, i: (0, 0)),
                  row,
                  pl.BlockSpec((None, 8, d), lambda b, i: (b, 0, 0)),
                  pl.BlockSpec((4, d), lambda b, i: (0, 0)),
                  pl.BlockSpec((n_exp, d), lambda b, i: (0, 0)),
                  pl.BlockSpec((n_exp, 1), lambda b, i: (0, 0))],
        out_specs=[row, tiles, small, small],
        out_shape=[jax.ShapeDtypeStruct((bsz, t, d), F32), jax.ShapeDtypeStruct((bsz * t * SUBLANES, LANES), F32),
                   jax.ShapeDtypeStruct((bsz, 8, t), I32), jax.ShapeDtypeStruct((bsz, 8, t), F32)],
        compiler_params=_cparams("parallel", "parallel"),
        name="post",
    )(*o_groups, w_out, x, mod, ln_rows, w_router_t, b_router.reshape(n_exp, 1))


def _deint_kernel(w_ref, p_ref, o_ref):
    grp = p_ref.shape[0]
    for g in range(w_ref.shape[1] // grp):
        cols = slice(g * grp, (g + 1) * grp)
        o_ref[:, cols] = _dot(w_ref[:, cols].astype(BF16), p_ref[...]).astype(BF16)


def _deinterleave_w1(w1_all, layer):
    _, n_exp, d, two_f = w1_all.shape
    grp = 2 * FF_GROUP
    j = np.arange(grp)
    src = np.where(j < FF_GROUP, 2 * j, 2 * (j - FF_GROUP) + 1)
    perm = jnp.asarray(np.arange(grp)[:, None] == src[None, :], BF16)
    return pl.pallas_call(
        _deint_kernel,
        grid=(n_exp,),
        in_specs=[pl.BlockSpec((None, None, d, two_f), lambda e: (layer, e, 0, 0)),
                  pl.BlockSpec((grp, grp), lambda e: (0, 0))],
        out_specs=pl.BlockSpec((None, d, two_f), lambda e: (e, 0, 0)),
        out_shape=jax.ShapeDtypeStruct((n_exp, d, two_f), BF16),
        compiler_params=_cparams("parallel"),
        name="deint",
    )(w1_all, perm)


def _moe_kernel(be_ref, nval_ref, nblk_ref, cur_ref, nxt_ref, h_hbm, w1_ref, b1_ref, w2_ref, b2_ref,
                out_hbm, xbuf, ybuf, gsem, ssem, *, bm):
    i = pl.program_id(0)
    nblk = nblk_ref[0]
    slot = i % 2

    def tile_rows(tok):
        if isinstance(tok, int):
            return pl.ds(tok * SUBLANES, SUBLANES)
        return pl.ds(pl.multiple_of(tok * SUBLANES, SUBLANES), SUBLANES)

    def gather_row_copy(tok, r, s):
        return pltpu.make_async_copy(h_hbm.at[tile_rows(tok)], xbuf.at[tile_rows(s * bm + r)], gsem.at[s])

    def scatter_row_copy(row, r, s):
        return pltpu.make_async_copy(ybuf.at[tile_rows(s * bm + r)], out_hbm.at[tile_rows(row)], ssem.at[s])

    def block_rows(s):
        return pl.ds(pl.multiple_of(s * bm * SUBLANES, bm * SUBLANES), bm * SUBLANES)

    def issue_rows(start_row):
        def chunk(ci, _):
            base = ci * DMA_ISSUE_CHUNK
            for j in range(DMA_ISSUE_CHUNK):
                start_row(base + j, j % 2)
            return 0
        lax.fori_loop(0, bm // DMA_ISSUE_CHUNK, chunk, 0)

    def start_gather(idx_ref, s):
        issue_rows(lambda r, prio: gather_row_copy(idx_ref[0, r], r, s).start(priority=prio))

    def per_static_slot(s, fn):
        for p in range(2):
            @pl.when(s == p)
            def _(p=p):
                fn(p)

    def start_gather_unrolled(idx_ref, s):
        def issue(p):
            for r in range(bm):
                gather_row_copy(idx_ref[0, r], r, p).start(priority=r % 2)
        per_static_slot(s, issue)

    def wait_gather(s):
        pltpu.make_async_copy(h_hbm.at[pl.ds(0, bm * SUBLANES)], xbuf.at[block_rows(s)], gsem.at[s]).wait()

    def start_scatter(s, n):
        @pl.when(n == bm)
        def _():
            def issue(p):
                for r in range(bm):
                    scatter_row_copy(cur_ref[0, bm + r], r, p).start(priority=r % 2)
            per_static_slot(s, issue)

        @pl.when(n < bm)
        def _():
            def body(r, _):
                scatter_row_copy(cur_ref[0, bm + r], r, s).start()
                return 0
            lax.fori_loop(0, n, body, 0)

    def wait_scatter(s, n):
        @pl.when(n == bm)
        def _():
            pltpu.make_async_copy(ybuf.at[block_rows(s)], out_hbm.at[pl.ds(0, bm * SUBLANES)], ssem.at[s]).wait()

        @pl.when(n < bm)
        def _():
            def body(r, _):
                scatter_row_copy(0, r, s).wait()
                return 0
            lax.fori_loop(0, n, body, 0)

    @pl.when(jnp.logical_and(i == 0, nblk > 0))
    def _():
        start_gather(cur_ref, 0)

    @pl.when(i + 1 < nblk)
    def _():
        start_gather_unrolled(nxt_ref, 1 - slot)

    @pl.when(i < nblk)
    def _():
        wait_gather(slot)

        @pl.when(i >= 2)
        def _():
            wait_scatter(slot, nval_ref[jnp.maximum(i - 2, 0)])

        x = _load_token_tiles(xbuf, slot * bm, bm).astype(BF16)
        y = jnp.zeros((bm, w2_ref.shape[1]), F32) + b2_ref[...]
        for g in range(w2_ref.shape[0] // FF_GROUP):
            cols = slice(2 * g * FF_GROUP, 2 * (g + 1) * FF_GROUP)
            hh = _dot(x, w1_ref[:, cols]) + b1_ref[:, cols]
            glu = jnp.minimum(hh[:, :FF_GROUP], SWIGLU_LIMIT)
            lin = jnp.clip(hh[:, FF_GROUP:], -SWIGLU_LIMIT, SWIGLU_LIMIT)
            act = glu * jax.nn.sigmoid(SWIGLU_ALPHA * glu) * (lin + 1.0)
            y = y + _dot(act.astype(BF16), w2_ref[g * FF_GROUP:(g + 1) * FF_GROUP, :])
        _store_token_tiles(ybuf, slot * bm, y)
        start_scatter(slot, nval_ref[i])

    last = pl.num_programs(0) - 1

    @pl.when(jnp.logical_and(i == last, nblk >= 2))
    def _():
        wait_scatter(nblk % 2, nval_ref[jnp.maximum(nblk - 2, 0)])

    @pl.when(jnp.logical_and(i == last, nblk >= 1))
    def _():
        wait_scatter((nblk + 1) % 2, nval_ref[jnp.maximum(nblk - 1, 0)])


def _moe_experts(h2, blk_expert, blk_valid, n_used, slot_idx, w1p, b1p, w2_all, b2, layer):
    n_tok = h2.shape[0] // SUBLANES
    d = SUBLANES * LANES
    n_blocks, _, two_bm = slot_idx.shape
    bm = two_bm // 2
    n_exp, _, two_f = w1p.shape
    f = two_f // 2
    idx_spec = lambda f_: pl.BlockSpec((None, 1, two_bm), f_, memory_space=pltpu.SMEM)
    wspec = lambda shp: pl.BlockSpec((None,) + shp, lambda i, be, nv, nb: (be[i], 0, 0))
    grid_spec = pltpu.PrefetchScalarGridSpec(
        num_scalar_prefetch=3,
        grid=(n_blocks,),
        in_specs=[idx_spec(lambda i, be, nv, nb: (i, 0, 0)),
                  idx_spec(lambda i, be, nv, nb: (jnp.minimum(i + 1, n_blocks - 1), 0, 0)),
                  pl.BlockSpec(memory_space=pl.ANY),
                  wspec((d, two_f)), wspec((1, two_f)),
                  pl.BlockSpec((None, None, f, d), lambda i, be, nv, nb: (layer, be[i], 0, 0)),
                  wspec((1, d))],
        out_specs=pl.BlockSpec(memory_space=pl.ANY),
        scratch_shapes=[pltpu.VMEM((2 * bm * SUBLANES, LANES), F32), pltpu.VMEM((2 * bm * SUBLANES, LANES), F32),
                        pltpu.SemaphoreType.DMA((2,)), pltpu.SemaphoreType.DMA((2,))],
    )
    return pl.pallas_call(
        functools.partial(_moe_kernel, bm=bm),
        grid_spec=grid_spec,
        out_shape=jax.ShapeDtypeStruct((n_tok * TOP_K * SUBLANES, LANES), F32),
        compiler_params=_cparams("arbitrary"),
        name="moe",
    )(blk_expert, blk_valid, n_used, slot_idx, slot_idx, h2, w1p, b1p, w2_all, b2)


def _moe_dispatch(top_idx, bm):
    n_tok = top_idx.shape[0]
    m = n_tok * TOP_K
    assert m % bm == 0
    e_flat = top_idx.reshape(-1)
    experts = jnp.arange(N_EXPERTS, dtype=I32)
    counts = jnp.sum((e_flat[:, None] == experts[None, :]).astype(I32), axis=0)
    padded = (counts + bm - 1) // bm * bm
    pend = jnp.cumsum(padded)
    n_blocks = m // bm + N_EXPERTS
    pad_ok = jnp.arange(bm, dtype=I32)[None, :] < (padded - counts)[:, None]
    pad_key = jnp.where(pad_ok, 2 * experts[:, None] + 1, 2 * N_EXPERTS).reshape(-1)
    keys = jnp.concatenate([2 * e_flat, pad_key])
    vals = jnp.concatenate([jnp.arange(m, dtype=I32), jnp.full((N_EXPERTS * bm,), -1, I32)])
    _, asg = lax.sort((keys, vals), num_keys=1, is_stable=True)
    valid = asg >= 0
    tok = jnp.maximum(asg, 0) // TOP_K
    choice = jnp.maximum(asg, 0) % TOP_K
    slot_idx = jnp.concatenate([tok.reshape(n_blocks, bm), (choice * n_tok + tok).reshape(n_blocks, bm)], axis=1)
    blk_valid = jnp.sum(valid.reshape(n_blocks, bm).astype(I32), axis=1)
    blk_start = jnp.arange(n_blocks, dtype=I32) * bm
    blk_expert = jnp.minimum(jnp.sum((blk_start[:, None] >= pend[None, :]).astype(I32), axis=1), N_EXPERTS - 1)
    n_used = (pend[-1] // bm).astype(I32).reshape(1)
    return blk_expert, blk_valid, n_used, slot_idx.reshape(n_blocks, 1, 2 * bm)


def _comb_kernel(y4_ref, gate_ref, x_ref, mod_ref, ln_ref, o_ref, *, alpha):
    y = jnp.zeros(x_ref.shape, F32)
    for k in range(TOP_K):
        y = y + gate_ref[:, k:k + 1] * _load_token_tiles(y4_ref.at[k], 0, x_ref.shape[0])
    u = alpha * x_ref[...] + (1.0 + mod_ref[5:6, :]) * y
    o_ref[...] = _ln(u) * ln_ref[2:3, :] + ln_ref[3:4, :]


def _combine(y4, gate, x1, mod, ln_rows, alpha):
    bsz, t, d = x1.shape
    tm = min(COMB_TILE, t)
    nt = t // tm
    return pl.pallas_call(
        functools.partial(_comb_kernel, alpha=alpha),
        grid=(bsz, nt),
        in_specs=[pl.BlockSpec((TOP_K, tm * SUBLANES, LANES), lambda b, i: (0, b * nt + i, 0)),
                  pl.BlockSpec((None, tm, 8), lambda b, i: (b, i, 0)),
                  pl.BlockSpec((None, tm, d), lambda b, i: (b, i, 0)),
                  pl.BlockSpec((None, 8, d), lambda b, i: (b, 0, 0)),
                  pl.BlockSpec((4, d), lambda b, i: (0, 0))],
        out_specs=pl.BlockSpec((None, tm, d), lambda b, i: (b, i, 0)),
        out_shape=jax.ShapeDtypeStruct((bsz, t, d), F32),
        compiler_params=_cparams("parallel", "parallel"),
        name="comb",
    )(y4, gate, x1, mod, ln_rows)


def _rel_bucket(dist):
    n = jnp.maximum(dist, 0)
    max_exact = N_BUCKETS // 2
    nf = jnp.maximum(n, 1).astype(F32)
    large = max_exact + (jnp.log(nf / max_exact) / math.log(MAX_DISTANCE / max_exact)
                         * (N_BUCKETS - max_exact)).astype(I32)
    large = jnp.minimum(large, N_BUCKETS - 1)
    return jnp.where(n < max_exact, n, large)


def _bias_tiles(bias_tab, tq, dil, key_major):
    period = 2 * tq
    k = np.arange(period)
    d = np.where(k < tq, -k, period - k)
    dist = np.stack([np.maximum(d, 0), d + tq]) * dil
    line = bias_tab.astype(F32)[_rel_bucket(jnp.asarray(dist, I32))]
    line = jnp.moveaxis(line, -1, 0)
    flat = jnp.tile(line, (1, 1, tq))[..., :tq * (period - 1)]
    tiles = flat.reshape(line.shape[0], 2, tq, period - 1)[..., :tq]
    if key_major:
        tiles = (tiles - bias_tab.astype(F32)[N_BUCKETS - 1][:, None, None, None]) * LOG2E
        tiles = jnp.swapaxes(tiles, -1, -2)
    return tiles


def _split_w_in(w_in):
    d = w_in.shape[0]
    g = GROUP_W
    segs = {}
    o = 0
    for name, width in (("qa", g), ("ka", g), ("va", g), ("qb", g), ("kb", g), ("vb", g),
                        ("qc", g), ("kc", g), ("vc", g), ("qx", IDX_HEADS * IDX_DIM), ("kx", IDX_DIM),
                        ("wx", IDX_HEADS), ("qd", g), ("kd", g), ("vd", g)):
        segs[name] = w_in[:, o:o + width]
        o += width
    pad = jnp.zeros((d, LANES - IDX_DIM - IDX_HEADS), w_in.dtype)
    w = jnp.concatenate([segs[n] for n in ("qa", "ka", "va", "qb", "kb", "vb", "kc", "kd", "kx", "wx")] + [pad],
                        axis=1)
    wt = jnp.concatenate([segs[n] for n in ("qx", "qc", "vc", "qd", "vd", "wx")], axis=1).T
    return w.astype(BF16), wt.astype(BF16)


def _layer(x, c, layer, depth, p, tiles):
    bsz, t, d = x.shape
    alpha = (2 * depth) ** 0.25
    mod = _ada_mod(c, p["w_ada_all"], p["b_ada_all"], layer).reshape(bsz, 6, d)
    mod = jnp.concatenate([mod, jnp.zeros((bsz, 2, d), F32)], axis=1)
    ln_rows = jnp.concatenate([p["ln_g"][0:1], p["ln_b"][0:1], p["ln_g"][1:2], p["ln_b"][1:2]], axis=0)

    proj, proj_t = _ln_mod_proj(x, mod, *_split_w_in(p["w_in"]))

    o_a = _stick_breaking(proj)
    o_b = _dilated(proj, tiles["b"])
    o_c = _dsa(proj, proj_t, tiles["c"])
    lamp = p["diff_lam"].astype(F32)
    lambda_init = 0.8 - 0.6 * math.exp(-0.3 * layer)
    lam = jnp.exp(jnp.sum(lamp[0] * lamp[1])) - jnp.exp(jnp.sum(lamp[2] * lamp[3])) + lambda_init
    o_d = _differential(proj, proj_t, lam, tiles["d"], p["diff_g"], lambda_init)

    x1, h2, top_idx, gate = _post_mixer((o_a, o_b, o_c, o_d), p["w_out"].astype(BF16), x, mod, ln_rows,
                                        p["w_router"].T, p["b_router"], alpha)

    top_idx = top_idx[:, :TOP_K, :].transpose(0, 2, 1).reshape(bsz * t, TOP_K)
    blk_expert, blk_valid, n_used, slot_idx = _moe_dispatch(top_idx, MOE_BLOCK)
    n_grp = p["b1"].shape[-1] // (2 * FF_GROUP)
    b1p = p["b1"].reshape(N_EXPERTS, n_grp, FF_GROUP, 2).transpose(0, 1, 3, 2).reshape(N_EXPERTS, 1, -1)
    y_rows = _moe_experts(h2, blk_expert, blk_valid, n_used, slot_idx,
                          _deinterleave_w1(p["w1_all"], layer), b1p, p["w2_all"], p["b2"][:, None, :], layer)
    return _combine(y_rows.reshape(TOP_K, bsz * t * SUBLANES, LANES), gate.transpose(0, 2, 1), x1, mod, ln_rows,
                    alpha)


def kernel(x, c, w_ada, b_ada, w_in, w_out, diff_lam, diff_g, ln_g, ln_b, w_router, b_router, w1, b1, w2, b2,
           rel_bias):
    depth = w_in.shape[0]
    t = x.shape[1]
    tq = min(ATT_BLOCK, t)
    tiles = dict(
        b=[_bias_tiles(rel_bias[:, :GROUP_HEADS], min(128, t // dil), dil, False) for _, dil in DILATED_CONFIGS],
        c=_bias_tiles(rel_bias[:, GROUP_HEADS:2 * GROUP_HEADS], tq, 1, True),
        d=_bias_tiles(rel_bias[:, 2 * GROUP_HEADS:], tq, 1, True))
    w2_all = w2.astype(BF16)
    for layer in range(depth):
        p = dict(w_ada_all=w_ada, b_ada_all=b_ada, w_in=w_in[layer], w_out=w_out[layer],
                 diff_lam=diff_lam[layer], diff_g=diff_g[layer], ln_g=ln_g[layer], ln_b=ln_b[layer],
                 w_router=w_router[layer], b_router=b_router[layer], w1_all=w1, b1=b1[layer],
                 w2_all=w2_all, b2=b2[layer])
        x = _layer(x, c, layer, depth, p, tiles)
    return x
```

```python
import functools
import math

import numpy as np
import jax
import jax.numpy as jnp
from jax import lax
from jax.experimental import pallas as pl
from jax.experimental.pallas import tpu as pltpu

F32 = jnp.float32
BF16 = jnp.bfloat16
I32 = jnp.int32

HEAD_DIM = 64
GROUP_HEADS = 4
GROUP_W = GROUP_HEADS * HEAD_DIM
DIFF_QK_DIM = HEAD_DIM // 2
DILATED_CONFIGS = ((128, 1), (512, 4), (2048, 16))
IDX_HEADS = 16
IDX_DIM = 64
INDEX_TOPK_MAX = 256
N_EXPERTS = 32
TOP_K = 4
SWIGLU_ALPHA = 1.702
SWIGLU_LIMIT = 7.0
N_BUCKETS = 32
MAX_DISTANCE = 128
LN_EPS = 1e-5
MOE_BLOCK = 512
DMA_ISSUE_CHUNK = 64
PROJ_TILE = 1024
POST_TILE = 1024
COMB_TILE = 512
MIX_TILE = 1024

LANES = 128
SUBLANES = 8
VMEM_LIMIT_BYTES = 56 * 1024 * 1024
NEG_BIG = -1e30
SB_SKIP_LOG = -100.0
INT_MIN = -2 ** 31
NEG_BIG_BITS = int(np.float32(NEG_BIG).view(np.int32))
LOG2E = math.log2(math.e)
ATT_BLOCK = 256
COUNT_ROWS = 4 * SUBLANES
DIL_SUB_BLOCKS = 4
FF_GROUP = 256

COL_A = 0
COL_B = COL_A + 3 * GROUP_W
COL_KC = COL_B + 3 * GROUP_W
COL_KD = COL_KC + GROUP_W
COL_TAIL = COL_KD + GROUP_W
PROJ_COLS = COL_TAIL + LANES
ROW_QX = 0
ROW_QC = ROW_QX + IDX_HEADS * IDX_DIM
ROW_VC = ROW_QC + GROUP_W
ROW_QD = ROW_VC + GROUP_W
ROW_VD = ROW_QD + GROUP_W
ROW_WX = ROW_VD + GROUP_W
PROJ_ROWS = ROW_WX + IDX_HEADS


def _cparams(*sem):
    return pltpu.CompilerParams(dimension_semantics=sem, vmem_limit_bytes=VMEM_LIMIT_BYTES)


def _ln(x):
    mu = jnp.mean(x, axis=-1, keepdims=True)
    xc = x - mu
    return xc * lax.rsqrt(jnp.mean(xc * xc, axis=-1, keepdims=True) + LN_EPS)


def _dot_nt(a, b):
    return lax.dot_general(a, b, (((1,), (1,)), ((), ())), preferred_element_type=F32)


def _dot(a, b):
    return jnp.dot(a, b, preferred_element_type=F32)


def _load_token_tiles(ref, first_tok, n):
    base = first_tok * SUBLANES
    return jnp.concatenate([ref[pl.ds(base + c, n, stride=SUBLANES), :] for c in range(SUBLANES)], axis=1)


def _store_token_tiles(ref, first_tok, val):
    n = val.shape[0]
    base = first_tok * SUBLANES
    for c in range(SUBLANES):
        ref[pl.ds(base + c, n, stride=SUBLANES), :] = val[:, c * LANES:(c + 1) * LANES]


def _ada_kernel(c_ref, w_ref, b_ref, o_ref):
    o_ref[...] = jnp.dot(c_ref[...], w_ref[...], precision=lax.Precision.HIGHEST,
                         preferred_element_type=F32) + b_ref[...]


def _ada_mod(c, w_all, b_all, layer):
    bsz, d = c.shape
    depth, _, n = w_all.shape
    return pl.pallas_call(
        _ada_kernel,
        grid=(n // d,),
        in_specs=[pl.BlockSpec((bsz, d), lambda j: (0, 0)),
                  pl.BlockSpec((None, d, d), lambda j: (layer, 0, j)),
                  pl.BlockSpec((None, 1, d), lambda j: (layer, 0, j))],
        out_specs=pl.BlockSpec((bsz, d), lambda j: (0, j)),
        out_shape=jax.ShapeDtypeStruct((bsz, n), F32),
        compiler_params=_cparams("arbitrary"),
        name="ada",
    )(c, w_all, b_all.reshape(depth, 1, n))


def _proj_kernel(x_ref, mod_ref, w_ref, wt_ref, o_ref, ot_ref, *, chunk):
    h = (_ln(x_ref[...]) * (1.0 + mod_ref[1:2, :]) + mod_ref[0:1, :]).astype(BF16)
    ncol = o_ref.shape[-1]
    for c0 in range(0, ncol, chunk):
        c1 = min(c0 + chunk, ncol)
        o_ref[:, c0:c1] = _dot(h, w_ref[:, c0:c1]).astype(BF16)
    n_sub, nrow, tq = ot_ref.shape
    for r0 in range(0, nrow, chunk):
        r1 = min(r0 + chunk, nrow)
        res = _dot_nt(wt_ref[r0:r1, :], h).astype(BF16)
        for j in range(n_sub):
            ot_ref[j, r0:r1, :] = res[:, j * tq:(j + 1) * tq]


def _ln_mod_proj(x, mod, w, wt):
    bsz, t, d = x.shape
    ncol, nrow = w.shape[1], wt.shape[0]
    tq = min(ATT_BLOCK, t)
    tm = min(PROJ_TILE, t)
    return pl.pallas_call(
        functools.partial(_proj_kernel, chunk=2 * LANES),
        grid=(bsz, t // tm),
        in_specs=[pl.BlockSpec((None, tm, d), lambda b, i: (b, i, 0)),
                  pl.BlockSpec((None, 8, d), lambda b, i: (b, 0, 0)),
                  pl.BlockSpec((d, ncol), lambda b, i: (0, 0)),
                  pl.BlockSpec((nrow, d), lambda b, i: (0, 0))],
        out_specs=[pl.BlockSpec((None, tm, ncol), lambda b, i: (b, i, 0)),
                   pl.BlockSpec((None, tm // tq, nrow, tq), lambda b, i: (b, i, 0, 0))],
        out_shape=[jax.ShapeDtypeStruct((bsz, t, ncol), BF16),
                   jax.ShapeDtypeStruct((bsz, t // tq, nrow, tq), BF16)],
        compiler_params=_cparams("parallel", "parallel"),
        name="proj",
    )(x, mod, w, wt)


def _by_head(x):
    lane = lax.broadcasted_iota(I32, x.shape, 1)
    zero = jnp.zeros_like(x)
    return jnp.concatenate([jnp.where(jnp.logical_and(lane >= h * HEAD_DIM, lane < (h + 1) * HEAD_DIM), x, zero)
                            for h in range(GROUP_HEADS)], axis=0)


def _sb_kernel(q_ref, k_ref, v_ref, o_ref, *, tq, scale):
    qi = pl.program_id(1)
    r = lax.broadcasted_iota(I32, (tq, tq), 0)
    c = lax.broadcasted_iota(I32, (tq, tq), 1)
    strict_lower = c < r
    upper = jnp.where(r > c, 1.0, 0.0).astype(BF16)

    q = q_ref[...]
    valid = jnp.concatenate([strict_lower] * GROUP_HEADS, axis=1)

    def step(kb, carry, acc, masked):
        start = pl.multiple_of(kb * tq, tq)
        z = _dot_nt(q, _by_head(k_ref[pl.ds(start, tq), :])) * scale
        log_sig = jnp.minimum(z, 0.0) - jnp.log(1.0 + jnp.exp(-jnp.abs(z)))
        log_fail = log_sig - z
        if masked:
            log_fail = jnp.where(valid, log_fail, 0.0)
        lf = jnp.concatenate([log_fail[:, h * tq:(h + 1) * tq] for h in range(GROUP_HEADS)], axis=0)
        lf_hi = lf.astype(BF16)
        lf_lo = (lf - lf_hi.astype(F32)).astype(BF16)
        after = _dot(lf_hi, upper) + _dot(lf_lo, upper)
        after = jnp.concatenate([after[h * tq:(h + 1) * tq] + carry[h] for h in range(GROUP_HEADS)], axis=1)
        w = jnp.exp(log_sig + after)
        if masked:
            w = jnp.where(valid, w, 0.0)
        acc = acc + _dot(w.astype(BF16), _by_head(v_ref[pl.ds(start, tq), :]))
        carry = tuple(carry[h] + jnp.sum(log_fail[:, h * tq:(h + 1) * tq], axis=1, keepdims=True)
                      for h in range(GROUP_HEADS))
        return carry, acc

    def live(carry):
        top = carry[0]
        for h in range(1, GROUP_HEADS):
            top = jnp.maximum(top, carry[h])
        return (jnp.max(top) > SB_SKIP_LOG).astype(I32)

    carry, acc = step(qi, tuple(jnp.zeros((tq, 1), F32) for _ in range(GROUP_HEADS)),
                      jnp.zeros((tq, GROUP_W), F32), True)

    def body(s):
        kb, carry, acc, _ = s
        carry, acc = step(kb, carry, acc, False)
        return kb - 1, carry, acc, live(carry)

    _, _, acc, _ = lax.while_loop(lambda s: jnp.logical_and(s[0] >= 0, s[3] > 0), body,
                                  (qi - 1, carry, acc, live(carry)))
    o_ref[...] = acc.astype(BF16)


def _stick_breaking(proj):
    bsz, t, _ = proj.shape
    tq = min(ATT_BLOCK, t)
    cb = COL_A // GROUP_W
    return pl.pallas_call(
        functools.partial(_sb_kernel, tq=tq, scale=HEAD_DIM ** -0.5),
        grid=(bsz, t // tq),
        in_specs=[pl.BlockSpec((None, tq, GROUP_W), lambda b, i: (b, i, cb)),
                  pl.BlockSpec((None, t, GROUP_W), lambda b, i: (b, 0, cb + 1)),
                  pl.BlockSpec((None, t, GROUP_W), lambda b, i: (b, 0, cb + 2))],
        out_specs=pl.BlockSpec((None, tq, GROUP_W), lambda b, i: (b, i, 0)),
        out_shape=jax.ShapeDtypeStruct((bsz, t, GROUP_W), BF16),
        compiler_params=_cparams("parallel", "arbitrary"),
        name="sb",
    )(proj, proj, proj)


ONES_ROWS = 16


def _softmax_init_t(tq):
    return (jnp.full((1, tq), NEG_BIG, F32), jnp.zeros((1, tq), F32), jnp.zeros((HEAD_DIM, tq), F32))


def _chain_logits(k_ref, start, tk, chains, qt):
    outs = [None] * len(chains)
    for half in range(GROUP_W // LANES):
        idx = [i for i, (lo, hi) in enumerate(chains) if lo // LANES == half]
        assert all((chains[i][1] - 1) // LANES == half for i in idx)
        khalf = k_ref[pl.ds(start, tk), half * LANES:(half + 1) * LANES]
        lane = lax.broadcasted_iota(I32, khalf.shape, 1) + half * LANES
        zero = jnp.zeros_like(khalf)
        lhs = jnp.concatenate([jnp.where(jnp.logical_and(lane >= chains[i][0], lane < chains[i][1]), khalf, zero)
                               for i in idx], axis=0)
        z = _dot(lhs, qt[half * LANES:(half + 1) * LANES, :])
        for j, i in enumerate(idx):
            outs[i] = z[j * tk:(j + 1) * tk]
    return outs


def _ones_rows(tk):
    r = lax.broadcasted_iota(I32, (ONES_ROWS, GROUP_HEADS * tk), 0)
    c = lax.broadcasted_iota(I32, (ONES_ROWS, GROUP_HEADS * tk), 1)
    lo = r * tk
    return jnp.where(jnp.logical_and(c >= lo, c < lo + tk), 1.0, 0.0).astype(BF16)


def _value_blockdiag(vblk, ones):
    tk = vblk.shape[1]
    zero = jnp.zeros((HEAD_DIM, tk), BF16)
    rows = [jnp.concatenate([vblk[h * HEAD_DIM:(h + 1) * HEAD_DIM, :] if j == h else zero
                             for j in range(GROUP_HEADS)], axis=1) for h in range(GROUP_HEADS)]
    return jnp.concatenate(rows + [ones], axis=0)


def _softmax_weights(z, m):
    m_new = jnp.maximum(m, jnp.max(z, axis=0, keepdims=True))
    return m_new, jnp.exp2(m - m_new), jnp.exp2(z - m_new).astype(BF16)


def _causal_blocks(qi, logits, step, st):
    st = lax.fori_loop(0, jnp.maximum(qi - 1, 0), lambda kb, s: step(logits(kb), kb, s, None, False), st)
    st = lax.cond(qi >= 1, lambda s: step(logits(qi - 1), qi - 1, s, 1, False), lambda s: s, st)
    return step(logits(qi), qi, st, 0, True)


def _diff_kernel(lam_ref, qt_ref, k_ref, vt_ref, bias_ref, g_ref, o_ref, *, tq, c_scale, out_scale):
    qi = pl.program_id(1)
    r = lax.broadcasted_iota(I32, (tq, tq), 0)
    c = lax.broadcasted_iota(I32, (tq, tq), 1)
    causal = r <= c
    lam = lam_ref[0]
    qt = qt_ref[...]
    ones = _ones_rows(tq)
    chains = [(h * HEAD_DIM + j * DIFF_QK_DIM, h * HEAD_DIM + (j + 1) * DIFF_QK_DIM)
              for h in range(GROUP_HEADS) for j in range(2)]

    def logits(kb):
        return tuple(_chain_logits(k_ref, pl.multiple_of(kb * tq, tq), tq, chains, qt))

    def step(zs, kb, st, which, masked):
        ms, alphas, ps = [], [], []
        for i, z in enumerate(zs):
            z = z * c_scale
            if which is not None:
                z = z + bias_ref[i // 2, which]
            if masked:
                z = jnp.where(causal, z, NEG_BIG)
            m_new, alpha, p = _softmax_weights(z, st[i][0])
            ms.append(m_new)
            alphas.append(alpha)
            ps.append(p)
        pmat = jnp.concatenate([jnp.concatenate([ps[2 * h], ps[2 * h + 1]], axis=1) for h in range(GROUP_HEADS)],
                               axis=0)
        pv = _dot(_value_blockdiag(vt_ref[kb], ones), pmat)
        out = []
        for i in range(len(chains)):
            h, j = divmod(i, 2)
            cols = slice(j * tq, (j + 1) * tq)
            l = alphas[i] * st[i][1] + pv[GROUP_W + h:GROUP_W + h + 1, cols]
            acc = alphas[i] * st[i][2] + pv[h * HEAD_DIM:(h + 1) * HEAD_DIM, cols]
            out.append((ms[i], l, acc))
        return tuple(out)

    st = _causal_blocks(qi, logits, step, tuple(_softmax_init_t(tq) for _ in chains))
    outs = []
    for h in range(GROUP_HEADS):
        (_, l1, a1), (_, l2, a2) = st[2 * h], st[2 * h + 1]
        o = a1 / l1 - lam * (a2 / l2)
        o = o * lax.rsqrt(jnp.mean(o * o, axis=0, keepdims=True) + LN_EPS)
        outs.append(o * g_ref[...] * out_scale)
    o_ref[...] = jnp.concatenate(outs, axis=0).T.astype(BF16)


def _differential(proj, proj_t, lam, bias_tiles, diff_g, lambda_init):
    bsz, t, _ = proj.shape
    nq, tq = proj_t.shape[1], proj_t.shape[3]
    grid_spec = pltpu.PrefetchScalarGridSpec(
        num_scalar_prefetch=1,
        grid=(bsz, nq),
        in_specs=[pl.BlockSpec((None, None, GROUP_W, tq), lambda b, i, lam: (b, i, ROW_QD // GROUP_W, 0)),
                  pl.BlockSpec((None, t, GROUP_W), lambda b, i, lam: (b, 0, COL_KD // GROUP_W)),
                  pl.BlockSpec((None, nq, GROUP_W, tq), lambda b, i, lam: (b, 0, ROW_VD // GROUP_W, 0)),
                  pl.BlockSpec(bias_tiles.shape, lambda b, i, lam: (0, 0, 0, 0)),
                  pl.BlockSpec((HEAD_DIM, 1), lambda b, i, lam: (0, 0))],
        out_specs=pl.BlockSpec((None, tq, GROUP_W), lambda b, i, lam: (b, i, 0)),
    )
    return pl.pallas_call(
        functools.partial(_diff_kernel, tq=tq, c_scale=DIFF_QK_DIM ** -0.5 * LOG2E, out_scale=1.0 - lambda_init),
        grid_spec=grid_spec,
        out_shape=jax.ShapeDtypeStruct((bsz, t, GROUP_W), BF16),
        compiler_params=_cparams("parallel", "arbitrary"),
        name="diff",
    )(lam.reshape(1).astype(F32), proj_t, proj, proj_t, bias_tiles, diff_g.reshape(HEAD_DIM, 1).astype(F32))


def _dsa_kernel(qx_ref, wx_ref, tail_ref, qt_ref, k_ref, vt_ref, bias_ref, o_ref, key_scr, cut_scr,
                *, tq, topk, c_scale, row_bits):
    qi = pl.program_id(1)
    nkb = qi + 1
    r = lax.broadcasted_iota(I32, (tq, tq), 0)
    c = lax.broadcasted_iota(I32, (tq, tq), 1)
    wx = wx_ref[...].astype(F32) * (IDX_HEADS ** -0.5 * IDX_DIM ** -0.5)
    wxb = [jnp.broadcast_to(wx[h:h + 1, :], (SUBLANES, tq)) for h in range(IDX_HEADS)]
    zpad = jnp.zeros((LANES - IDX_DIM, tq), BF16)
    qx = [jnp.concatenate([qx_ref[h * IDX_DIM:(h + 1) * IDX_DIM, :], zpad], axis=0) for h in range(IDX_HEADS)]

    def score_block(kb, _):
        start = pl.multiple_of(kb * tq, tq)
        kt = tail_ref[pl.ds(start, tq), :]
        s = jnp.zeros((tq // SUBLANES, SUBLANES, tq), F32)
        for h in range(IDX_HEADS):
            s = s + wxb[h] * jnp.maximum(_dot(kt, qx[h]), 0.0).reshape(tq // SUBLANES, SUBLANES, tq)
        s = s.reshape(tq, tq)
        s = jnp.where(s == 0.0, 0.0, s)
        s = jnp.where(r + kb * tq <= c + qi * tq, s, -jnp.inf)
        bits = pltpu.bitcast(s, I32)
        key_scr[kb] = bits ^ ((bits >> 31) & 0x7FFFFFFF)
        return 0

    lax.fori_loop(0, nkb, score_block, 0)

    ones_lhs = jnp.ones((SUBLANES, tq), BF16)

    def count(pred):
        def body(kb, acc):
            hit = jnp.where(pred(key_scr[kb], kb), 1.0, 0.0).astype(BF16)
            return acc + _dot(ones_lhs, hit)
        acc = lax.fori_loop(0, nkb, body, jnp.zeros((SUBLANES, tq), F32))
        return acc[0:1, :]

    def bit_step(i, s):
        thr, c_thr = s
        cand = thr + lax.shift_left(jnp.int32(1), 31 - i)
        cnt = count(lambda key, kb: key >= cand)
        ge = cnt >= topk
        return jnp.where(ge, cand, thr), jnp.where(ge, cnt, c_thr)

    n_keys = (nkb * tq).astype(F32)
    thr, c_thr = lax.fori_loop(0, 32, bit_step, (jnp.full((1, tq), INT_MIN, I32), jnp.full((1, tq), 1.0, F32) * n_keys))
    cut_scr[...] = jnp.full((SUBLANES, tq), 2 ** 30, I32)

    @pl.when(jnp.max(c_thr) > topk)
    def _():
        need = topk - count(lambda key, kb: key > thr)

        def row_step(i, lo):
            cand = lo + lax.shift_left(jnp.int32(1), row_bits - 1 - i)
            cnt = count(lambda key, kb: jnp.logical_and(key == thr, r + kb * tq < cand))
            return jnp.where(cnt < need, cand, lo)
        lo = lax.fori_loop(0, row_bits, row_step, jnp.zeros((1, tq), I32))
        cut_scr[...] = jnp.broadcast_to(lo, (SUBLANES, tq))

    cut = cut_scr[0:1, :]

    def select_block(kb, _):
        key = key_scr[kb]
        pos = r + kb * tq
        sel = jnp.logical_or(key > thr, jnp.logical_and(key == thr, pos <= cut))
        sel = jnp.logical_and(sel, pos <= c + qi * tq)
        key_scr[kb] = jnp.where(sel, 0, NEG_BIG_BITS)
        return 0

    lax.fori_loop(0, nkb, select_block, 0)

    qt = qt_ref[...]
    ones = _ones_rows(tq)
    chains = [(h * HEAD_DIM, (h + 1) * HEAD_DIM) for h in range(GROUP_HEADS)]

    def logits(kb):
        return tuple(_chain_logits(k_ref, pl.multiple_of(kb * tq, tq), tq, chains, qt))

    def step(zs, kb, st, which, masked):
        del masked
        mask = pltpu.bitcast(key_scr[kb], F32)
        ms, alphas, ps = [], [], []
        for h, z in enumerate(zs):
            z = z * c_scale + mask
            if which is not None:
                z = z + bias_ref[h, which]
            m_new, alpha, p = _softmax_weights(z, st[h][0])
            ms.append(m_new)
            alphas.append(alpha)
            ps.append(p)
        pv = _dot(_value_blockdiag(vt_ref[kb], ones), jnp.concatenate(ps, axis=0))
        return tuple((ms[h], alphas[h] * st[h][1] + pv[GROUP_W + h:GROUP_W + h + 1, :],
                      alphas[h] * st[h][2] + pv[h * HEAD_DIM:(h + 1) * HEAD_DIM, :]) for h in range(GROUP_HEADS))

    st = _causal_blocks(qi, logits, step, tuple(_softmax_init_t(tq) for _ in range(GROUP_HEADS)))
    o_ref[...] = jnp.concatenate([acc / l for _, l, acc in st], axis=0).T.astype(BF16)


def _dsa(proj, proj_t, bias_tiles):
    bsz, t, _ = proj.shape
    nq, tq = proj_t.shape[1], proj_t.shape[3]
    topk = min(INDEX_TOPK_MAX, t // 4)
    assert tq >= topk, "the threshold search needs at least topk keys in the first block"
    nqx = IDX_HEADS * IDX_DIM
    return pl.pallas_call(
        functools.partial(_dsa_kernel, tq=tq, topk=float(topk), c_scale=HEAD_DIM ** -0.5 * LOG2E,
                          row_bits=max(1, (t - 1).bit_length())),
        grid=(bsz, nq),
        in_specs=[pl.BlockSpec((None, None, nqx, tq), lambda b, i: (b, i, ROW_QX // nqx, 0)),
                  pl.BlockSpec((None, None, IDX_HEADS, tq), lambda b, i: (b, i, ROW_WX // IDX_HEADS, 0)),
                  pl.BlockSpec((None, t, LANES), lambda b, i: (b, 0, COL_TAIL // LANES)),
                  pl.BlockSpec((None, None, GROUP_W, tq), lambda b, i: (b, i, ROW_QC // GROUP_W, 0)),
                  pl.BlockSpec((None, t, GROUP_W), lambda b, i: (b, 0, COL_KC // GROUP_W)),
                  pl.BlockSpec((None, nq, GROUP_W, tq), lambda b, i: (b, 0, ROW_VC // GROUP_W, 0)),
                  pl.BlockSpec(bias_tiles.shape, lambda b, i: (0, 0, 0, 0))],
        out_specs=pl.BlockSpec((None, tq, GROUP_W), lambda b, i: (b, i, 0)),
        out_shape=jax.ShapeDtypeStruct((bsz, t, GROUP_W), BF16),
        scratch_shapes=[pltpu.VMEM((nq, tq, tq), I32), pltpu.VMEM((SUBLANES, tq), I32)],
        compiler_params=_cparams("parallel", "arbitrary"),
        name="dsa",
    )(proj_t, proj_t, proj, proj_t, proj, proj_t, bias_tiles)


def _dil_kernel(q_ref, kp_ref, kc_ref, vp_ref, vc_ref, bias_ref, ol_ref, *, tq, scale):
    qi = pl.program_id(1)
    r = lax.broadcasted_iota(I32, (tq, tq), 0)
    c = lax.broadcasted_iota(I32, (tq, tq), 1)
    diag_ok = c <= r
    for sb in range(q_ref.shape[0] // tq):
        rows = slice(sb * tq, (sb + 1) * tq)
        if sb == 0:
            kp, vp = kp_ref[...], vp_ref[...]
            prev_ok = jnp.logical_and(r <= c, qi > 0)
        else:
            before = slice((sb - 1) * tq, sb * tq)
            kp, vp = kc_ref[before, :], vc_ref[before, :]
            prev_ok = r <= c
        q = q_ref[rows, :]
        zp_all = _dot_nt(q, _by_head(kp)) * scale
        zd_all = _dot_nt(q, _by_head(kc_ref[rows, :])) * scale
        pps, pds, dens, lses = [], [], [], []
        for h in range(GROUP_HEADS):
            cols = slice(h * tq, (h + 1) * tq)
            zp = jnp.where(prev_ok, zp_all[:, cols] + bias_ref[h, 1], NEG_BIG)
            zd = jnp.where(diag_ok, zd_all[:, cols] + bias_ref[h, 0], NEG_BIG)
            m = jnp.maximum(jnp.max(zp, axis=1, keepdims=True), jnp.max(zd, axis=1, keepdims=True))
            pp = jnp.exp(zp - m)
            pd = jnp.exp(zd - m)
            den = jnp.sum(pp, axis=1, keepdims=True) + jnp.sum(pd, axis=1, keepdims=True)
            pps.append(pp.astype(BF16))
            pds.append(pd.astype(BF16))
            dens.append(jnp.broadcast_to(den, (tq, HEAD_DIM)))
            lses.append(jnp.broadcast_to(m + jnp.log(den), (tq, HEAD_DIM)))
        values = jnp.concatenate([_by_head(vp), _by_head(vc_ref[rows, :])], axis=0)
        ol_ref[rows, :GROUP_W] = _dot(jnp.concatenate(pps + pds, axis=1), values) / jnp.concatenate(dens, axis=-1)
        ol_ref[rows, GROUP_W:] = jnp.concatenate(lses, axis=-1)


def _dilated_one(qkv, bias_tiles):
    n, length, _ = qkv.shape
    tq = bias_tiles.shape[-1]
    nsub = max(n for n in (1, 2, DIL_SUB_BLOCKS) if length % (n * tq) == 0)
    tile = nsub * tq
    wide = lambda col: pl.BlockSpec((None, tile, GROUP_W), lambda b, i: (b, i, col))
    back = lambda col: pl.BlockSpec((None, tq, GROUP_W), lambda b, i: (b, jnp.maximum(i * nsub - 1, 0), col))
    return pl.pallas_call(
        functools.partial(_dil_kernel, tq=tq, scale=HEAD_DIM ** -0.5),
        grid=(n, length // tile),
        in_specs=[wide(0), back(1), wide(1), back(2), wide(2),
                  pl.BlockSpec(bias_tiles.shape, lambda b, i: (0, 0, 0, 0))],
        out_specs=pl.BlockSpec((None, tile, 2 * GROUP_W), lambda b, i: (b, i, 0)),
        out_shape=jax.ShapeDtypeStruct((n, length, 2 * GROUP_W), F32),
        compiler_params=_cparams("parallel", "arbitrary"),
        name="dil",
    )(qkv, qkv, qkv, qkv, qkv, bias_tiles)


def _dilmix_kernel(p0, p1, p2, out_ref):
    a0, a1, a2 = p0[:, GROUP_W:], p1[:, GROUP_W:], p2[:, GROUP_W:]
    m = jnp.maximum(jnp.maximum(a0, a1), a2)
    e0, e1, e2 = jnp.exp(a0 - m), jnp.exp(a1 - m), jnp.exp(a2 - m)
    mixed = e0 * p0[:, :GROUP_W] + e1 * p1[:, :GROUP_W] + e2 * p2[:, :GROUP_W]
    out_ref[...] = (mixed / (e0 + e1 + e2)).astype(BF16)


def _dilated_mix(patterns):
    bsz, t, w = patterns[0].shape
    tm = min(MIX_TILE, t)
    return pl.pallas_call(
        _dilmix_kernel,
        grid=(bsz, t // tm),
        in_specs=[pl.BlockSpec((None, tm, w), lambda b, i: (b, i, 0))] * 3,
        out_specs=pl.BlockSpec((None, tm, w // 2), lambda b, i: (b, i, 0)),
        out_shape=jax.ShapeDtypeStruct((bsz, t, w // 2), BF16),
        compiler_params=_cparams("parallel", "parallel"),
        name="dilmix",
    )(*patterns)


def _dilated(proj, bias_tiles_per_cfg):
    bsz, t, _ = proj.shape
    qkv = proj[:, :, COL_B:COL_B + 3 * GROUP_W]
    patterns = []
    for (_, dil), tiles in zip(DILATED_CONFIGS, bias_tiles_per_cfg):
        def perm(a, dil=dil):
            w = a.shape[-1]
            return a.reshape(bsz, t // dil, dil, w).transpose(0, 2, 1, 3).reshape(bsz * dil, t // dil, w)

        def unperm(a, dil=dil):
            w = a.shape[-1]
            return a.reshape(bsz, dil, t // dil, w).transpose(0, 2, 1, 3).reshape(bsz, t, w)

        patterns.append(unperm(_dilated_one(perm(qkv), tiles)))
    return _dilated_mix(patterns)


def _post_kernel(oa_ref, ob_ref, oc_ref, od_ref, wo_ref, x_ref, mod_ref, ln_ref, wr_ref, br_ref,
                 x1_ref, h2_ref, idx_ref, gate_ref, *, alpha):
    y = jnp.zeros(x_ref.shape, F32)
    for g, o_ref in enumerate((oa_ref, ob_ref, oc_ref, od_ref)):
        y = y + _dot(o_ref[...], wo_ref[g * GROUP_W:(g + 1) * GROUP_W, :])
    u = alpha * x_ref[...] + (1.0 + mod_ref[2:3, :]) * y
    x1 = _ln(u) * ln_ref[0:1, :] + ln_ref[1:2, :]
    x1_ref[...] = x1
    h2 = _ln(x1) * (1.0 + mod_ref[4:5, :]) + mod_ref[3:4, :]
    _store_token_tiles(h2_ref, 0, h2)
    logits = lax.dot_general(wr_ref[...], h2, (((1,), (1,)), ((), ())), precision=lax.Precision.HIGHEST,
                             preferred_element_type=F32) + br_ref[...]
    n_exp, tm = logits.shape
    eid = lax.broadcasted_iota(I32, (n_exp, tm), 0)
    vals, ids = [], []
    for _ in range(TOP_K):
        m = jnp.max(logits, axis=0, keepdims=True)
        first = jnp.min(jnp.where(logits == m, eid, n_exp), axis=0, keepdims=True)
        vals.append(m)
        ids.append(first)
        logits = jnp.where(eid == first, -jnp.inf, logits)
    ex = [jnp.exp(v - vals[0]) for v in vals]
    den = ex[0] + ex[1] + ex[2] + ex[3]
    zero_f = jnp.zeros((8 - TOP_K, tm), F32)
    gate_ref[...] = jnp.concatenate([e / den for e in ex] + [zero_f], axis=0)
    idx_ref[...] = jnp.concatenate(ids + [zero_f.astype(I32)], axis=0)


def _post_mixer(o_groups, w_out, x, mod, ln_rows, w_router_t, b_router, alpha):
    bsz, t, d = x.shape
    tm = min(POST_TILE, t)
    n_exp = w_router_t.shape[0]
    og = pl.BlockSpec((None, tm, GROUP_W), lambda b, i: (b, i, 0))
    row = pl.BlockSpec((None, tm, d), lambda b, i: (b, i, 0))
    small = pl.BlockSpec((None, 8, tm), lambda b, i: (b, 0, i))
    assert d == SUBLANES * LANES, "token-tile layout needs one (8,128) tile per token"
    nt = t // tm
    tiles = pl.BlockSpec((tm * SUBLANES, LANES), lambda b, i: (b * nt + i, 0))
    return pl.pallas_call(
        functools.partial(_post_kernel, alpha=alpha),
        grid=(bsz, nt),
        in_specs=[og, og, og, og,
                  pl.BlockSpec(w_out.shape, lambda b, i: (0, 0)),
                  row,
                  pl.BlockSpec((None, 8, d), lambda b, i: (b, 0, 0)),
                  pl.BlockSpec((4, d), lambda b, i: (0, 0)),
                  pl.BlockSpec((n_exp, d), lambda b, i: (0, 0)),
                  pl.BlockSpec((n_exp, 1), lambda b, i: (0, 0))],
        out_specs=[row, tiles, small, small],
        out_shape=[jax.ShapeDtypeStruct((bsz, t, d), F32), jax.ShapeDtypeStruct((bsz * t * SUBLANES, LANES), F32),
                   jax.ShapeDtypeStruct((bsz, 8, t), I32), jax.ShapeDtypeStruct((bsz, 8, t), F32)],
        compiler_params=_cparams("parallel", "parallel"),
        name="post",
    )(*o_groups, w_out, x, mod, ln_rows, w_router_t, b_router.reshape(n_exp, 1))


def _deint_kernel(w_ref, p_ref, o_ref):
    grp = p_ref.shape[0]
    for g in range(w_ref.shape[1] // grp):
        cols = slice(g * grp, (g + 1) * grp)
        o_ref[:, cols] = _dot(w_ref[:, cols].astype(BF16), p_ref[...]).astype(BF16)


def _deinterleave_w1(w1_all, layer):
    _, n_exp, d, two_f = w1_all.shape
    grp = 2 * FF_GROUP
    j = np.arange(grp)
    src = np.where(j < FF_GROUP, 2 * j, 2 * (j - FF_GROUP) + 1)
    perm = jnp.asarray(np.arange(grp)[:, None] == src[None, :], BF16)
    return pl.pallas_call(
        _deint_kernel,
        grid=(n_exp,),
        in_specs=[pl.BlockSpec((None, None, d, two_f), lambda e: (layer, e, 0, 0)),
                  pl.BlockSpec((grp, grp), lambda e: (0, 0))],
        out_specs=pl.BlockSpec((None, d, two_f), lambda e: (e, 0, 0)),
        out_shape=jax.ShapeDtypeStruct((n_exp, d, two_f), BF16),
        compiler_params=_cparams("parallel"),
        name="deint",
    )(w1_all, perm)


def _moe_kernel(be_ref, nval_ref, nblk_ref, cur_ref, nxt_ref, h_hbm, w1_ref, b1_ref, w2_ref, b2_ref,
                out_hbm, xbuf, ybuf, gsem, ssem, *, bm):
    i = pl.program_id(0)
    nblk = nblk_ref[0]
    slot = i % 2

    def tile_rows(tok):
        return pl.ds(pl.multiple_of(tok * SUBLANES, SUBLANES), SUBLANES)

    def gather_row_copy(tok, r, s):
        return pltpu.make_async_copy(h_hbm.at[tile_rows(tok)], xbuf.at[tile_rows(s * bm + r)], gsem.at[s])

    def scatter_row_copy(row, r, s):
        return pltpu.make_async_copy(ybuf.at[tile_rows(s * bm + r)], out_hbm.at[tile_rows(row)], ssem.at[s])

    def block_rows(s):
        return pl.ds(pl.multiple_of(s * bm * SUBLANES, bm * SUBLANES), bm * SUBLANES)

    def issue_rows(start_row):
        def chunk(ci, _):
            base = ci * DMA_ISSUE_CHUNK
            for j in range(DMA_ISSUE_CHUNK):
                start_row(base + j, j % 2)
            return 0
        lax.fori_loop(0, bm // DMA_ISSUE_CHUNK, chunk, 0)

    def start_gather(idx_ref, s):
        issue_rows(lambda r, prio: gather_row_copy(idx_ref[0, r], r, s).start(priority=prio))

    def wait_gather(s):
        pltpu.make_async_copy(h_hbm.at[pl.ds(0, bm * SUBLANES)], xbuf.at[block_rows(s)], gsem.at[s]).wait()

    def start_scatter(s, n):
        @pl.when(n == bm)
        def _():
            issue_rows(lambda r, prio: scatter_row_copy(cur_ref[0, bm + r], r, s).start(priority=prio))

        @pl.when(n < bm)
        def _():
            def body(r, _):
                scatter_row_copy(cur_ref[0, bm + r], r, s).start()
                return 0
            lax.fori_loop(0, n, body, 0)

    def wait_scatter(s, n):
        @pl.when(n == bm)
        def _():
            pltpu.make_async_copy(ybuf.at[block_rows(s)], out_hbm.at[pl.ds(0, bm * SUBLANES)], ssem.at[s]).wait()

        @pl.when(n < bm)
        def _():
            def body(r, _):
                scatter_row_copy(0, r, s).wait()
                return 0
            lax.fori_loop(0, n, body, 0)

    @pl.when(jnp.logical_and(i == 0, nblk > 0))
    def _():
        start_gather(cur_ref, 0)

    @pl.when(i + 1 < nblk)
    def _():
        start_gather(nxt_ref, 1 - slot)

    @pl.when(i < nblk)
    def _():
        wait_gather(slot)

        @pl.when(i >= 2)
        def _():
            wait_scatter(slot, nval_ref[jnp.maximum(i - 2, 0)])

        x = _load_token_tiles(xbuf, slot * bm, bm).astype(BF16)
        y = jnp.zeros((bm, w2_ref.shape[1]), F32) + b2_ref[...]
        for g in range(w2_ref.shape[0] // FF_GROUP):
            cols = slice(2 * g * FF_GROUP, 2 * (g + 1) * FF_GROUP)
            hh = _dot(x, w1_ref[:, cols]) + b1_ref[:, cols]
            glu = jnp.minimum(hh[:, :FF_GROUP], SWIGLU_LIMIT)
            lin = jnp.clip(hh[:, FF_GROUP:], -SWIGLU_LIMIT, SWIGLU_LIMIT)
            act = glu * jax.nn.sigmoid(SWIGLU_ALPHA * glu) * (lin + 1.0)
            y = y + _dot(act.astype(BF16), w2_ref[g * FF_GROUP:(g + 1) * FF_GROUP, :])
        _store_token_tiles(ybuf, slot * bm, y)
        start_scatter(slot, nval_ref[i])

    last = pl.num_programs(0) - 1

    @pl.when(jnp.logical_and(i == last, nblk >= 2))
    def _():
        wait_scatter(nblk % 2, nval_ref[jnp.maximum(nblk - 2, 0)])

    @pl.when(jnp.logical_and(i == last, nblk >= 1))
    def _():
        wait_scatter((nblk + 1) % 2, nval_ref[jnp.maximum(nblk - 1, 0)])


def _moe_experts(h2, blk_expert, blk_valid, n_used, slot_idx, w1p, b1p, w2_all, b2, layer):
    n_tok = h2.shape[0] // SUBLANES
    d = SUBLANES * LANES
    n_blocks, _, two_bm = slot_idx.shape
    bm = two_bm // 2
    n_exp, _, two_f = w1p.shape
    f = two_f // 2
    idx_spec = lambda f_: pl.BlockSpec((None, 1, two_bm), f_, memory_space=pltpu.SMEM)
    wspec = lambda shp: pl.BlockSpec((None,) + shp, lambda i, be, nv, nb: (be[i], 0, 0))
    grid_spec = pltpu.PrefetchScalarGridSpec(
        num_scalar_prefetch=3,
        grid=(n_blocks,),
        in_specs=[idx_spec(lambda i, be, nv, nb: (i, 0, 0)),
                  idx_spec(lambda i, be, nv, nb: (jnp.minimum(i + 1, n_blocks - 1), 0, 0)),
                  pl.BlockSpec(memory_space=pl.ANY),
                  wspec((d, two_f)), wspec((1, two_f)),
                  pl.BlockSpec((None, None, f, d), lambda i, be, nv, nb: (layer, be[i], 0, 0)),
                  wspec((1, d))],
        out_specs=pl.BlockSpec(memory_space=pl.ANY),
        scratch_shapes=[pltpu.VMEM((2 * bm * SUBLANES, LANES), F32), pltpu.VMEM((2 * bm * SUBLANES, LANES), F32),
                        pltpu.SemaphoreType.DMA((2,)), pltpu.SemaphoreType.DMA((2,))],
    )
    return pl.pallas_call(
        functools.partial(_moe_kernel, bm=bm),
        grid_spec=grid_spec,
        out_shape=jax.ShapeDtypeStruct((n_tok * TOP_K * SUBLANES, LANES), F32),
        compiler_params=_cparams("arbitrary"),
        name="moe",
    )(blk_expert, blk_valid, n_used, slot_idx, slot_idx, h2, w1p, b1p, w2_all, b2)


def _moe_dispatch(top_idx, bm):
    n_tok = top_idx.shape[0]
    m = n_tok * TOP_K
    assert m % bm == 0
    e_flat = top_idx.reshape(-1)
    experts = jnp.arange(N_EXPERTS, dtype=I32)
    counts = jnp.sum((e_flat[:, None] == experts[None, :]).astype(I32), axis=0)
    padded = (counts + bm - 1) // bm * bm
    pend = jnp.cumsum(padded)
    n_blocks = m // bm + N_EXPERTS
    pad_ok = jnp.arange(bm, dtype=I32)[None, :] < (padded - counts)[:, None]
    pad_key = jnp.where(pad_ok, 2 * experts[:, None] + 1, 2 * N_EXPERTS).reshape(-1)
    keys = jnp.concatenate([2 * e_flat, pad_key])
    vals = jnp.concatenate([jnp.arange(m, dtype=I32), jnp.full((N_EXPERTS * bm,), -1, I32)])
    _, asg = lax.sort((keys, vals), num_keys=1, is_stable=True)
    valid = asg >= 0
    tok = jnp.maximum(asg, 0) // TOP_K
    choice = jnp.maximum(asg, 0) % TOP_K
    slot_idx = jnp.concatenate([tok.reshape(n_blocks, bm), (choice * n_tok + tok).reshape(n_blocks, bm)], axis=1)
    blk_valid = jnp.sum(valid.reshape(n_blocks, bm).astype(I32), axis=1)
    blk_start = jnp.arange(n_blocks, dtype=I32) * bm
    blk_expert = jnp.minimum(jnp.sum((blk_start[:, None] >= pend[None, :]).astype(I32), axis=1), N_EXPERTS - 1)
    n_used = (pend[-1] // bm).astype(I32).reshape(1)
    return blk_expert, blk_valid, n_used, slot_idx.reshape(n_blocks, 1, 2 * bm)


def _comb_kernel(y4_ref, gate_ref, x_ref, mod_ref, ln_ref, o_ref, *, alpha):
    y = jnp.zeros(x_ref.shape, F32)
    for k in range(TOP_K):
        y = y + gate_ref[:, k:k + 1] * _load_token_tiles(y4_ref.at[k], 0, x_ref.shape[0])
    u = alpha * x_ref[...] + (1.0 + mod_ref[5:6, :]) * y
    o_ref[...] = _ln(u) * ln_ref[2:3, :] + ln_ref[3:4, :]


def _combine(y4, gate, x1, mod, ln_rows, alpha):
    bsz, t, d = x1.shape
    tm = min(COMB_TILE, t)
    nt = t // tm
    return pl.pallas_call(
        functools.partial(_comb_kernel, alpha=alpha),
        grid=(bsz, nt),
        in_specs=[pl.BlockSpec((TOP_K, tm * SUBLANES, LANES), lambda b, i: (0, b * nt + i, 0)),
                  pl.BlockSpec((None, tm, 8), lambda b, i: (b, i, 0)),
                  pl.BlockSpec((None, tm, d), lambda b, i: (b, i, 0)),
                  pl.BlockSpec((None, 8, d), lambda b, i: (b, 0, 0)),
                  pl.BlockSpec((4, d), lambda b, i: (0, 0))],
        out_specs=pl.BlockSpec((None, tm, d), lambda b, i: (b, i, 0)),
        out_shape=jax.ShapeDtypeStruct((bsz, t, d), F32),
        compiler_params=_cparams("parallel", "parallel"),
        name="comb",
    )(y4, gate, x1, mod, ln_rows)


def _rel_bucket(dist):
    n = jnp.maximum(dist, 0)
    max_exact = N_BUCKETS // 2
    nf = jnp.maximum(n, 1).astype(F32)
    large = max_exact + (jnp.log(nf / max_exact) / math.log(MAX_DISTANCE / max_exact)
                         * (N_BUCKETS - max_exact)).astype(I32)
    large = jnp.minimum(large, N_BUCKETS - 1)
    return jnp.where(n < max_exact, n, large)


def _bias_tiles(bias_tab, tq, dil, key_major):
    period = 2 * tq
    k = np.arange(period)
    d = np.where(k < tq, -k, period - k)
    dist = np.stack([np.maximum(d, 0), d + tq]) * dil
    line = bias_tab.astype(F32)[_rel_bucket(jnp.asarray(dist, I32))]
    line = jnp.moveaxis(line, -1, 0)
    flat = jnp.tile(line, (1, 1, tq))[..., :tq * (period - 1)]
    tiles = flat.reshape(line.shape[0], 2, tq, period - 1)[..., :tq]
    if key_major:
        tiles = (tiles - bias_tab.astype(F32)[N_BUCKETS - 1][:, None, None, None]) * LOG2E
        tiles = jnp.swapaxes(tiles, -1, -2)
    return tiles


def _split_w_in(w_in):
    d = w_in.shape[0]
    g = GROUP_W
    segs = {}
    o = 0
    for name, width in (("qa", g), ("ka", g), ("va", g), ("qb", g), ("kb", g), ("vb", g),
                        ("qc", g), ("kc", g), ("vc", g), ("qx", IDX_HEADS * IDX_DIM), ("kx", IDX_DIM),
                        ("wx", IDX_HEADS), ("qd", g), ("kd", g), ("vd", g)):
        segs[name] = w_in[:, o:o + width]
        o += width
    pad = jnp.zeros((d, LANES - IDX_DIM - IDX_HEADS), w_in.dtype)
    w = jnp.concatenate([segs[n] for n in ("qa", "ka", "va", "qb", "kb", "vb", "kc", "kd", "kx", "wx")] + [pad],
                        axis=1)
    wt = jnp.concatenate([segs[n] for n in ("qx", "qc", "vc", "qd", "vd", "wx")], axis=1).T
    return w.astype(BF16), wt.astype(BF16)


def _layer(x, c, layer, depth, p, tiles):
    bsz, t, d = x.shape
    alpha = (2 * depth) ** 0.25
    mod = _ada_mod(c, p["w_ada_all"], p["b_ada_all"], layer).reshape(bsz, 6, d)
    mod = jnp.concatenate([mod, jnp.zeros((bsz, 2, d), F32)], axis=1)
    ln_rows = jnp.concatenate([p["ln_g"][0:1], p["ln_b"][0:1], p["ln_g"][1:2], p["ln_b"][1:2]], axis=0)

    proj, proj_t = _ln_mod_proj(x, mod, *_split_w_in(p["w_in"]))

    o_a = _stick_breaking(proj)
    o_b = _dilated(proj, tiles["b"])
    o_c = _dsa(proj, proj_t, tiles["c"])
    lamp = p["diff_lam"].astype(F32)
    lambda_init = 0.8 - 0.6 * math.exp(-0.3 * layer)
    lam = jnp.exp(jnp.sum(lamp[0] * lamp[1])) - jnp.exp(jnp.sum(lamp[2] * lamp[3])) + lambda_init
    o_d = _differential(proj, proj_t, lam, tiles["d"], p["diff_g"], lambda_init)

    x1, h2, top_idx, gate = _post_mixer((o_a, o_b, o_c, o_d), p["w_out"].astype(BF16), x, mod, ln_rows,
                                        p["w_router"].T, p["b_router"], alpha)

    top_idx = top_idx[:, :TOP_K, :].transpose(0, 2, 1).reshape(bsz * t, TOP_K)
    blk_expert, blk_valid, n_used, slot_idx = _moe_dispatch(top_idx, MOE_BLOCK)
    n_grp = p["b1"].shape[-1] // (2 * FF_GROUP)
    b1p = p["b1"].reshape(N_EXPERTS, n_grp, FF_GROUP, 2).transpose(0, 1, 3, 2).reshape(N_EXPERTS, 1, -1)
    y_rows = _moe_experts(h2, blk_expert, blk_valid, n_used, slot_idx,
                          _deinterleave_w1(p["w1_all"], layer), b1p, p["w2_all"], p["b2"][:, None, :], layer)
    return _combine(y_rows.reshape(TOP_K, bsz * t * SUBLANES, LANES), gate.transpose(0, 2, 1), x1, mod, ln_rows,
                    alpha)


def kernel(x, c, w_ada, b_ada, w_in, w_out, diff_lam, diff_g, ln_g, ln_b, w_router, b_router, w1, b1, w2, b2,
           rel_bias):
    depth = w_in.shape[0]
    t = x.shape[1]
    tq = min(ATT_BLOCK, t)
    tiles = dict(
        b=[_bias_tiles(rel_bias[:, :GROUP_HEADS], min(128, t // dil), dil, False) for _, dil in DILATED_CONFIGS],
        c=_bias_tiles(rel_bias[:, GROUP_HEADS:2 * GROUP_HEADS], tq, 1, True),
        d=_bias_tiles(rel_bias[:, 2 * GROUP_HEADS:], tq, 1, True))
    w2_all = w2.astype(BF16)
    for layer in range(depth):
        p = dict(w_ada_all=w_ada, b_ada_all=b_ada, w_in=w_in[layer], w_out=w_out[layer],
                 diff_lam=diff_lam[layer], diff_g=diff_g[layer], ln_g=ln_g[layer], ln_b=ln_b[layer],
                 w_router=w_router[layer], b_router=b_router[layer], w1_all=w1, b1=b1[layer],
                 w2_all=w2_all, b2=b2[layer])
        x = _layer(x, c, layer, depth, p, tiles)
    return x
```
